```python
import math
import jax, jax.numpy as jnp
from jax import lax
import numpy as np

D_MODEL = 2048
BATCH = 8
SEQ = 2048
DEPTH = 4

N_MIXERS = 3
HEAD_DIM = 64
N_HEADS = D_MODEL // HEAD_DIM
BRANCH = N_HEADS * HEAD_DIM
N_KV_A = N_HEADS // 8
KV_A = N_KV_A * HEAD_DIM
WINDOW = 128
BLOCK = 128
NORM_EPS = 1e-6
NEG = -1e30

A_COLS = BRANCH + 2 * KV_A + BRANCH
B_COLS = 4 * BRANCH
C_COLS = 4 * BRANCH + N_HEADS

kernel_name = "hybrid_swa_stickbreak_fox_trunk"


def _n_layers_of(kind):
    return sum(1 for i in range(DEPTH) if i % N_MIXERS == kind)


def _alibi_slopes(n):
    return jnp.asarray(2.0 ** (-8.0 * np.arange(1, n + 1, dtype=np.float32) / n), dtype=jnp.float32)


def rmsnorm(x, g):
    xf = x.astype(jnp.float32)
    r = lax.rsqrt(jnp.mean(xf * xf, axis=-1, keepdims=True) + NORM_EPS)
    return (xf * r * g.astype(jnp.float32)).astype(x.dtype)


def swa_sink_mixer(h, w_in, sinks):
    B, S, _ = h.shape
    nb = S // BLOCK
    G = N_HEADS // N_KV_A
    q, k, v, z = jnp.split(h @ w_in, [BRANCH, BRANCH + KV_A, BRANCH + 2 * KV_A], axis=-1)
    q = q.reshape(B, nb, BLOCK, N_KV_A, G, HEAD_DIM).astype(jnp.float32)
    k = k.reshape(B, nb, BLOCK, N_KV_A, HEAD_DIM).astype(jnp.float32)
    v = v.reshape(B, nb, BLOCK, N_KV_A, HEAD_DIM).astype(jnp.float32)

    def band(a):
        prev = jnp.concatenate([jnp.zeros_like(a[:, :1]), a[:, :-1]], axis=1)
        return jnp.concatenate([prev, a], axis=2)

    kb, vb = band(k), band(v)
    scores = jnp.einsum('bnqhgd,bnkhd->bnhgqk', q, kb) * (HEAD_DIM ** -0.5)
    qi = jnp.arange(BLOCK)[:, None]
    kj = jnp.arange(2 * BLOCK)[None, :]
    dist = (qi + BLOCK - kj).astype(jnp.float32)
    s_pos = jnp.arange(nb)[:, None, None] * BLOCK - BLOCK + kj[None]
    valid = (dist >= 0) & (dist < WINDOW) & (s_pos >= 0)
    slopes = _alibi_slopes(N_HEADS).reshape(N_KV_A, G)
    scores = scores - slopes[:, :, None, None] * dist
    scores = jnp.where(valid[None, :, None, None], scores, NEG)
    sink = sinks.astype(jnp.float32).reshape(N_KV_A, G)
    sink_col = jnp.broadcast_to(sink[None, None, :, :, None, None], scores.shape[:-1] + (1,))
    p = jax.nn.softmax(jnp.concatenate([scores, sink_col], axis=-1), axis=-1)[..., :-1]
    o = jnp.einsum('bnhgqk,bnkhd->bnqhgd', p, vb).reshape(B, S, BRANCH)
    return o, z


def stick_breaking_mixer(h, w_in):
    B, S, _ = h.shape
    nb = S // BLOCK
    q, k, v, z = jnp.split(h @ w_in, 4, axis=-1)
    q = q.reshape(B, nb, BLOCK, N_HEADS, HEAD_DIM).transpose(1, 0, 3, 2, 4).astype(jnp.float32)
    k = k.reshape(B, S, N_HEADS, HEAD_DIM).astype(jnp.float32)
    v = v.reshape(B, S, N_HEADS, HEAD_DIM).astype(jnp.float32)
    s_pos = jnp.arange(S)
    scale = HEAD_DIM ** -0.5

    def block(args):
        qb, n = args
        t_pos = n * BLOCK + jnp.arange(BLOCK)
        logits = jnp.einsum('bhqd,bshd->bhqs', qb, k) * scale
        before = s_pos[None, :] < t_pos[:, None]
        log_fail = jnp.where(before, jax.nn.log_sigmoid(-logits), 0.0)
        incl = lax.cumsum(log_fail, axis=3, reverse=True)
        suffix = jnp.concatenate([incl[..., 1:], jnp.zeros_like(incl[..., :1])], axis=-1)
        a = jnp.where(before, jnp.exp(jax.nn.log_sigmoid(logits) + suffix), 0.0)
        return jnp.einsum('bhqs,bshd->bqhd', a, v)

    o = lax.map(block, (q, jnp.arange(nb)))
    o = o.transpose(1, 0, 2, 3, 4).reshape(B, S, BRANCH)
    return o, z


def forgetting_mixer(h, w_in, b_f):
    B, S, _ = h.shape
    nb = S // BLOCK
    q, k, v, z, f_logit = jnp.split(h @ w_in, [BRANCH, 2 * BRANCH, 3 * BRANCH, 4 * BRANCH], axis=-1)
    log_f = jax.nn.log_sigmoid(f_logit.astype(jnp.float32) + b_f.astype(jnp.float32))
    cum = lax.cumsum(log_f, axis=1).transpose(0, 2, 1)
    q = q.reshape(B, nb, BLOCK, N_HEADS, HEAD_DIM).transpose(1, 0, 3, 2, 4).astype(jnp.float32)
    cq = cum.reshape(B, N_HEADS, nb, BLOCK).transpose(2, 0, 1, 3)
    k = k.reshape(B, S, N_HEADS, HEAD_DIM).astype(jnp.float32)
    v = v.reshape(B, S, N_HEADS, HEAD_DIM).astype(jnp.float32)
    s_pos = jnp.arange(S)
    scale = HEAD_DIM ** -0.5

    def block(args):
        qb, cqb, n = args
        t_pos = n * BLOCK + jnp.arange(BLOCK)
        logits = jnp.einsum('bhqd,bshd->bhqs', qb, k) * scale + cqb[..., :, None] - cum[:, :, None, :]
        causal = s_pos[None, :] <= t_pos[:, None]
        p = jax.nn.softmax(jnp.where(causal, logits, NEG), axis=-1)
        return jnp.einsum('bhqs,bshd->bqhd', p, v)

    o = lax.map(block, (q, cq, jnp.arange(nb)))
    o = o.transpose(1, 0, 2, 3, 4).reshape(B, S, BRANCH)
    return o, z


def _fwd_setup_inputs(seed: int = 0) -> dict:
    key = jax.random.key(seed)
    ks = jax.random.split(key, 12)
    n_a, n_b, n_c = _n_layers_of(0), _n_layers_of(1), _n_layers_of(2)
    f32 = jnp.float32
    nrm = jax.random.normal
    return {
        "x": nrm(ks[0], (BATCH, SEQ, D_MODEL), f32),
        "g_pre": 1.0 + 0.02 * nrm(ks[1], (DEPTH, D_MODEL), f32),
        "g_post": 1.0 + 0.02 * nrm(ks[2], (DEPTH, D_MODEL), f32),
        "w_in_a": nrm(ks[3], (n_a, D_MODEL, A_COLS), f32) * D_MODEL ** -0.5,
        "w_out_a": nrm(ks[4], (n_a, BRANCH, D_MODEL), f32) * BRANCH ** -0.5,
        "sinks_a": 0.5 * nrm(ks[5], (n_a, N_HEADS), f32),
        "w_in_b": nrm(ks[6], (n_b, D_MODEL, B_COLS), f32) * D_MODEL ** -0.5,
        "w_out_b": nrm(ks[7], (n_b, BRANCH, D_MODEL), f32) * BRANCH ** -0.5,
        "w_in_c": nrm(ks[8], (n_c, D_MODEL, C_COLS), f32) * D_MODEL ** -0.5,
        "b_f_c": 1.0 + 5.0 * jax.random.uniform(ks[9], (n_c, N_HEADS), f32),
        "w_out_c": nrm(ks[10], (n_c, BRANCH, D_MODEL), f32) * BRANCH ** -0.5,
    }


def _fwd_reference(x, g_pre, g_post, w_in_a, w_out_a, sinks_a, w_in_b, w_out_b, w_in_c, b_f_c, w_out_c):
    for i in range(DEPTH):
        kind, j = i % N_MIXERS, i // N_MIXERS
        h = rmsnorm(x, g_pre[i])
        if kind == 0:
            o, z = swa_sink_mixer(h, w_in_a[j], sinks_a[j])
            w_out = w_out_a[j]
        elif kind == 1:
            o, z = stick_breaking_mixer(h, w_in_b[j])
            w_out = w_out_b[j]
        else:
            o, z = forgetting_mixer(h, w_in_c[j], b_f_c[j])
            w_out = w_out_c[j]
        y = (o.astype(z.dtype) * jax.nn.silu(z)) @ w_out
        x = x + rmsnorm(y, g_post[i])
    return x


import jax as _jax
import jax.numpy as _jnp

TWIN_FORMAT = 'train_step'
FWD_PARAMS = ['x', 'g_pre', 'g_post', 'w_in_a', 'w_out_a', 'sinks_a', 'w_in_b', 'w_out_b', 'w_in_c', 'b_f_c', 'w_out_c']
TWIN_WEIGHTS = ['g_pre', 'g_post', 'w_in_a', 'w_out_a', 'sinks_a', 'w_in_b', 'w_out_b', 'w_in_c', 'b_f_c', 'w_out_c']
TWIN_DIFF_INPUT = 'x'
TWIN_INPUTS = ['x', 'g_pre', 'g_post', 'w_in_a', 'w_out_a', 'sinks_a', 'w_in_b', 'w_out_b', 'w_in_c', 'b_f_c', 'w_out_c', 'loss_target', 'm_g_pre', 'm_g_post', 'm_w_in_a', 'm_w_out_a', 'm_sinks_a', 'm_w_in_b', 'm_w_out_b', 'm_w_in_c', 'm_b_f_c', 'm_w_out_c', 'v_g_pre', 'v_g_post', 'v_w_in_a', 'v_w_out_a', 'v_sinks_a', 'v_w_in_b', 'v_w_out_b', 'v_w_in_c', 'v_b_f_c', 'v_w_out_c']
TWIN_OUTPUTS = ['loss', 'grad_x', 'grad_g_pre', 'grad_g_post', 'grad_w_in_a', 'grad_w_out_a', 'grad_sinks_a', 'grad_w_in_b', 'grad_w_out_b', 'grad_w_in_c', 'grad_b_f_c', 'grad_w_out_c', 'delta_g_pre', 'delta_g_post', 'delta_w_in_a', 'delta_w_out_a', 'delta_sinks_a', 'delta_w_in_b', 'delta_w_out_b', 'delta_w_in_c', 'delta_b_f_c', 'delta_w_out_c', 'new_m_g_pre', 'new_m_g_post', 'new_m_w_in_a', 'new_m_w_out_a', 'new_m_sinks_a', 'new_m_w_in_b', 'new_m_w_out_b', 'new_m_w_in_c', 'new_m_b_f_c', 'new_m_w_out_c', 'new_v_g_pre', 'new_v_g_post', 'new_v_w_in_a', 'new_v_w_out_a', 'new_v_sinks_a', 'new_v_w_in_b', 'new_v_w_out_b', 'new_v_w_in_c', 'new_v_b_f_c', 'new_v_w_out_c']
TWIN_LEAF_KINDS = {'loss': 'loss', 'grad_x': 'grad_x', 'grad_g_pre': 'grad_w', 'grad_g_post': 'grad_w', 'grad_w_in_a': 'grad_w', 'grad_w_out_a': 'grad_w', 'grad_sinks_a': 'grad_w', 'grad_w_in_b': 'grad_w', 'grad_w_out_b': 'grad_w', 'grad_w_in_c': 'grad_w', 'grad_b_f_c': 'grad_w', 'grad_w_out_c': 'grad_w', 'delta_g_pre': 'delta_w', 'delta_g_post': 'delta_w', 'delta_w_in_a': 'delta_w', 'delta_w_out_a': 'delta_w', 'delta_sinks_a': 'delta_w', 'delta_w_in_b': 'delta_w', 'delta_w_out_b': 'delta_w', 'delta_w_in_c': 'delta_w', 'delta_b_f_c': 'delta_w', 'delta_w_out_c': 'delta_w', 'new_m_g_pre': 'new_m', 'new_m_g_post': 'new_m', 'new_m_w_in_a': 'new_m', 'new_m_w_out_a': 'new_m', 'new_m_sinks_a': 'new_m', 'new_m_w_in_b': 'new_m', 'new_m_w_out_b': 'new_m', 'new_m_w_in_c': 'new_m', 'new_m_b_f_c': 'new_m', 'new_m_w_out_c': 'new_m', 'new_v_g_pre': 'new_v', 'new_v_g_post': 'new_v', 'new_v_w_in_a': 'new_v', 'new_v_w_out_a': 'new_v', 'new_v_sinks_a': 'new_v', 'new_v_w_in_b': 'new_v', 'new_v_w_out_b': 'new_v', 'new_v_w_in_c': 'new_v', 'new_v_b_f_c': 'new_v', 'new_v_w_out_c': 'new_v'}


def _forward(args):
    return _fwd_reference(*[args[k] for k in FWD_PARAMS])


def _output_shape():
    out = _jax.eval_shape(lambda: _forward(_fwd_setup_inputs(0)))
    return out.shape, out.dtype

N_MICROBATCH = 1
ADAM_LR = 0.001
ADAM_B1 = 0.9
ADAM_B2 = 0.999
ADAM_EPS = 1e-08
ADAM_WD = 0.01
ADAM_STEP = 10
PER_EXAMPLE_BATCH_AXIS = {'x': 0, 'loss_target': 0}
SHARED_INPUTS = []
_WEIGHT_DTYPES = {'g_pre': _jnp.float32, 'g_post': _jnp.float32, 'w_in_a': _jnp.float32, 'w_out_a': _jnp.float32, 'sinks_a': _jnp.float32, 'w_in_b': _jnp.float32, 'w_out_b': _jnp.float32, 'w_in_c': _jnp.float32, 'b_f_c': _jnp.float32, 'w_out_c': _jnp.float32}
MOMENT_SCALE = {'g_pre': 4.864801e-01, 'g_post': 7.983895e+00, 'w_in_a': 3.651296e-01, 'w_out_a': 3.075149e-01, 'sinks_a': 6.127894e-01, 'w_in_b': 2.120783e-01, 'w_out_b': 2.719090e-01, 'w_in_c': 1.694146e-01, 'b_f_c': 1.298862e+00, 'w_out_c': 1.938131e-01}


def _to_microbatches(a, axis):
    t = _jnp.moveaxis(a, axis, 0)
    t = t.reshape((N_MICROBATCH, t.shape[0] // N_MICROBATCH) + t.shape[1:])
    return _jnp.moveaxis(t, 1, axis + 1)


def setup_inputs(seed: int = 0) -> dict:
    inp = _fwd_setup_inputs(seed)
    key = _jax.random.fold_in(_jax.random.key(seed), 7919)
    shape, _ = _output_shape()
    out = dict(inp)
    out["loss_target"] = _jax.random.normal(_jax.random.fold_in(key, 0), shape, _jnp.float32)
    for i, name in enumerate(TWIN_WEIGHTS):
        w = inp[name].astype(_jnp.float32)
        if MOMENT_SCALE is None:
            s = _jnp.sqrt(_jnp.mean(_jnp.square(w)) + 1e-30)
        else:
            s = MOMENT_SCALE[name]
        km, kv = _jax.random.split(_jax.random.fold_in(key, i + 1))
        out[name] = w
        out["m_" + name] = s * _jax.random.normal(km, w.shape, _jnp.float32)
        out["v_" + name] = (s * s) * _jax.random.uniform(kv, w.shape, _jnp.float32, 0.5, 1.5)
    if N_MICROBATCH > 1:
        for name, axis in PER_EXAMPLE_BATCH_AXIS.items():
            out[name] = _to_microbatches(out[name], axis)
    return {'x': out['x'], 'g_pre': out['g_pre'], 'g_post': out['g_post'], 'w_in_a': out['w_in_a'], 'w_out_a': out['w_out_a'], 'sinks_a': out['sinks_a'], 'w_in_b': out['w_in_b'], 'w_out_b': out['w_out_b'], 'w_in_c': out['w_in_c'], 'b_f_c': out['b_f_c'], 'w_out_c': out['w_out_c'], 'loss_target': out['loss_target'], 'm_g_pre': out['m_g_pre'], 'm_g_post': out['m_g_post'], 'm_w_in_a': out['m_w_in_a'], 'm_w_out_a': out['m_w_out_a'], 'm_sinks_a': out['m_sinks_a'], 'm_w_in_b': out['m_w_in_b'], 'm_w_out_b': out['m_w_out_b'], 'm_w_in_c': out['m_w_in_c'], 'm_b_f_c': out['m_b_f_c'], 'm_w_out_c': out['m_w_out_c'], 'v_g_pre': out['v_g_pre'], 'v_g_post': out['v_g_post'], 'v_w_in_a': out['v_w_in_a'], 'v_w_out_a': out['v_w_out_a'], 'v_sinks_a': out['v_sinks_a'], 'v_w_in_b': out['v_w_in_b'], 'v_w_out_b': out['v_w_out_b'], 'v_w_in_c': out['v_w_in_c'], 'v_b_f_c': out['v_b_f_c'], 'v_w_out_c': out['v_w_out_c']}


def _loss(weights, diff, rest, loss_target):
    with _jax.named_scope("forward"):
        args = {**rest, TWIN_DIFF_INPUT: diff, **{k: w.astype(_WEIGHT_DTYPES[k]) for k, w in weights.items()}}
        y = _forward(args)
    with _jax.named_scope("loss_head"):
        err = _jnp.square(y.astype(_jnp.float32) - loss_target)
        return 0.5 * _jnp.sum(_jnp.mean(err, axis=-1)) if err.ndim else 0.5 * err


def _adamw(w, g, m, v):
    m = ADAM_B1 * m + (1.0 - ADAM_B1) * g
    v = ADAM_B2 * v + (1.0 - ADAM_B2) * _jnp.square(g)
    m_hat = m / (1.0 - ADAM_B1 ** ADAM_STEP)
    v_hat = v / (1.0 - ADAM_B2 ** ADAM_STEP)
    delta = -ADAM_LR * (m_hat / (_jnp.sqrt(v_hat) + ADAM_EPS) + ADAM_WD * w)
    return delta, m, v


def reference(x, g_pre, g_post, w_in_a, w_out_a, sinks_a, w_in_b, w_out_b, w_in_c, b_f_c, w_out_c, loss_target, m_g_pre, m_g_post, m_w_in_a, m_w_out_a, m_sinks_a, m_w_in_b, m_w_out_b, m_w_in_c, m_b_f_c, m_w_out_c, v_g_pre, v_g_post, v_w_in_a, v_w_out_a, v_sinks_a, v_w_in_b, v_w_out_b, v_w_in_c, v_b_f_c, v_w_out_c):
    given = dict(x=x, g_pre=g_pre, g_post=g_post, w_in_a=w_in_a, w_out_a=w_out_a, sinks_a=sinks_a, w_in_b=w_in_b, w_out_b=w_out_b, w_in_c=w_in_c, b_f_c=b_f_c, w_out_c=w_out_c, loss_target=loss_target, m_g_pre=m_g_pre, m_g_post=m_g_post, m_w_in_a=m_w_in_a, m_w_out_a=m_w_out_a, m_sinks_a=m_sinks_a, m_w_in_b=m_w_in_b, m_w_out_b=m_w_out_b, m_w_in_c=m_w_in_c, m_b_f_c=m_b_f_c, m_w_out_c=m_w_out_c, v_g_pre=v_g_pre, v_g_post=v_g_post, v_w_in_a=v_w_in_a, v_w_out_a=v_w_out_a, v_sinks_a=v_sinks_a, v_w_in_b=v_w_in_b, v_w_out_b=v_w_out_b, v_w_in_c=v_w_in_c, v_b_f_c=v_b_f_c, v_w_out_c=v_w_out_c)
    weights = {n: given[n] for n in TWIN_WEIGHTS}
    shared = {n: given[n] for n in SHARED_INPUTS}
    per_example = {n: given[n] for n in ['x']}
    grad_fn = _jax.value_and_grad(_loss, argnums=(0, 1))

    def one_microbatch(ex, loss_target):
        ex = dict(ex)
        diff = ex.pop(TWIN_DIFF_INPUT)
        return grad_fn(weights, diff, {**shared, **ex}, loss_target)

    if N_MICROBATCH == 1:
        loss, (grad_w, grad_x) = one_microbatch(per_example, given["loss_target"])
    else:
        def body(carry, xs):
            loss_sum, grad_sum = carry
            l_k, (gw_k, gx_k) = one_microbatch(xs[0], xs[1])
            with _jax.named_scope("update"):
                return (loss_sum + l_k, _jax.tree.map(_jnp.add, grad_sum, gw_k)), gx_k

        init = (_jnp.zeros((), _jnp.float32), _jax.tree.map(_jnp.zeros_like, weights))
        (loss, grad_w), grad_x = _jax.lax.scan(body, init, (per_example, given["loss_target"]))
    with _jax.named_scope("update"):
        delta_w, new_m, new_v = {}, {}, {}
        for n in TWIN_WEIGHTS:
            delta_w[n], new_m[n], new_v[n] = _adamw(weights[n], grad_w[n], given["m_" + n], given["v_" + n])
    return (loss, grad_x, *[grad_w[n] for n in TWIN_WEIGHTS], *[delta_w[n] for n in TWIN_WEIGHTS],
            *[new_m[n] for n in TWIN_WEIGHTS], *[new_v[n] for n in TWIN_WEIGHTS])
```

```python
import functools
import math

import numpy as np
import jax
import jax.numpy as jnp
from jax import lax
from jax.experimental import pallas as pl
from jax.experimental.pallas import tpu as pltpu

HEAD_DIM = 64
BLOCK = 128
NORM_EPS = 1e-6
NEG = -1e30
N_DEV = 8
DEPTH = 4
ADAM_LR, ADAM_B1, ADAM_B2, ADAM_EPS, ADAM_WD, ADAM_STEP = 0.001, 0.9, 0.999, 1e-8, 0.01, 10
VMEM_LIMIT = 56 * 1024 * 1024

bf16 = jnp.bfloat16
f32 = jnp.float32
MESH = pl.DeviceIdType.MESH
ANY = pl.BlockSpec(memory_space=pl.ANY)
SMEM = pl.BlockSpec(memory_space=pltpu.SMEM)

NN = (((1,), (0,)), ((), ()))
NT = (((1,), (1,)), ((), ()))
TN = (((0,), (0,)), ((), ()))


def _dot(a, b, dims=NN):
    return lax.dot_general(a, b, dims, preferred_element_type=f32)


def _params(sem):
    return pltpu.CompilerParams(dimension_semantics=sem, vmem_limit_bytes=VMEM_LIMIT)


def _attn_tile(S):
    return 512 if S % 512 == 0 and S >= 1024 else 128


def _pick(n, pref):
    for t in pref:
        if n % t == 0:
            return t
    return n


def _matmul(a, b, mode, out_dtype, name):
    if mode == "nn":
        (M, K), (K2, N) = a.shape, b.shape
    elif mode == "nt":
        (M, K), (N, K2) = a.shape, b.shape
    else:
        (K, M), (K2, N) = a.shape, b.shape
    assert K == K2, (a.shape, b.shape, mode)
    tm = _pick(M, (1024, 512, 256, 128))
    tn = _pick(N, (512, 640, 256, 128))
    tk = _pick(K, (512, 640, 256, 128))
    nk = K // tk
    dims = {"nn": NN, "nt": NT, "tn": TN}[mode]

    def body(a_ref, b_ref, o_ref, acc_ref):
        k = pl.program_id(2)

        @pl.when(k == 0)
        def _():
            acc_ref[...] = jnp.zeros_like(acc_ref)

        acc_ref[...] += _dot(a_ref[...], b_ref[...], dims)

        @pl.when(k == nk - 1)
        def _():
            o_ref[...] = acc_ref[...].astype(o_ref.dtype)

    if mode == "tn":
        a_spec = pl.BlockSpec((tk, tm), lambda i, j, k: (k, i))
    else:
        a_spec = pl.BlockSpec((tm, tk), lambda i, j, k: (i, k))
    if mode == "nt":
        b_spec = pl.BlockSpec((tn, tk), lambda i, j, k: (j, k))
    else:
        b_spec = pl.BlockSpec((tk, tn), lambda i, j, k: (k, j))
    return pl.pallas_call(
        body, name=name,
        grid=(M // tm, N // tn, nk),
        in_specs=[a_spec, b_spec],
        out_specs=pl.BlockSpec((tm, tn), lambda i, j, k: (i, j)),
        out_shape=jax.ShapeDtypeStruct((M, N), out_dtype),
        scratch_shapes=[pltpu.VMEM((tm, tn), f32)],
        compiler_params=_params(("parallel", "parallel", "arbitrary")),
    )(a, b)


ROWS = 256


def _rows(S):
    return ROWS if S % ROWS == 0 else S


def _rmsnorm_fwd(x, g, name):
    S, D = x.shape
    tr = _rows(S)

    def body(x_ref, g_ref, h_ref):
        xv = x_ref[...]
        r = lax.rsqrt(jnp.mean(xv * xv, axis=-1, keepdims=True) + NORM_EPS)
        h_ref[...] = (xv * r * g_ref[...]).astype(h_ref.dtype)

    return pl.pallas_call(
        body, name=name, grid=(S // tr,),
        in_specs=[pl.BlockSpec((tr, D), lambda i: (i, 0)), pl.BlockSpec((1, D), lambda i: (0, 0))],
        out_specs=pl.BlockSpec((tr, D), lambda i: (i, 0)),
        out_shape=jax.ShapeDtypeStruct((S, D), bf16),
        compiler_params=_params(("parallel",)),
    )(x, g)


def _post_fwd(x, y, g, name):
    S, D = x.shape
    tr = _rows(S)

    def body(x_ref, y_ref, g_ref, o_ref):
        yv = y_ref[...]
        r = lax.rsqrt(jnp.mean(yv * yv, axis=-1, keepdims=True) + NORM_EPS)
        o_ref[...] = x_ref[...] + yv * r * g_ref[...]

    row = pl.BlockSpec((tr, D), lambda i: (i, 0))
    return pl.pallas_call(
        body, name=name, grid=(S // tr,),
        in_specs=[row, row, pl.BlockSpec((1, D), lambda i: (0, 0))],
        out_specs=row,
        out_shape=jax.ShapeDtypeStruct((S, D), f32),
        compiler_params=_params(("parallel",)),
    )(x, y, g)


def _loss_fwd_bwd(y, t, name):
    S, D = y.shape
    tr = _rows(S)

    def body(y_ref, t_ref, l_ref, d_ref):
        @pl.when(pl.program_id(0) == 0)
        def _():
            l_ref[...] = jnp.zeros_like(l_ref)

        e = y_ref[...] - t_ref[...]
        d_ref[...] = e * (1.0 / D)
        part = 0.5 * jnp.sum(jnp.sum(e * e, axis=-1, keepdims=True) * (1.0 / D), axis=0, keepdims=True)
        l_ref[...] += jnp.broadcast_to(part, l_ref.shape)

    row = pl.BlockSpec((tr, D), lambda i: (i, 0))
    return pl.pallas_call(
        body, name=name, grid=(S // tr,),
        in_specs=[row, row],
        out_specs=[pl.BlockSpec((8, 128), lambda i: (0, 0)), row],
        out_shape=[jax.ShapeDtypeStruct((8, 128), f32), jax.ShapeDtypeStruct((S, D), f32)],
        compiler_params=_params(("arbitrary",)),
    )(y, t)


def _post_bwd(dxn, y, g, name):
    S, D = y.shape
    tr = _rows(S)

    def body(d_ref, y_ref, g_ref, dy_ref, dg_ref):
        @pl.when(pl.program_id(0) == 0)
        def _():
            dg_ref[...] = jnp.zeros_like(dg_ref)

        yv = y_ref[...]
        d = d_ref[...]
        r = lax.rsqrt(jnp.mean(yv * yv, axis=-1, keepdims=True) + NORM_EPS)
        n = yv * r
        dn = d * g_ref[...]
        dg_ref[...] += jnp.sum(d * n, axis=0, keepdims=True)
        dy_ref[...] = (r * (dn - n * jnp.mean(dn * n, axis=-1, keepdims=True))).astype(dy_ref.dtype)

    row = pl.BlockSpec((tr, D), lambda i: (i, 0))
    vec = pl.BlockSpec((1, D), lambda i: (0, 0))
    return pl.pallas_call(
        body, name=name, grid=(S // tr,),
        in_specs=[row, row, vec],
        out_specs=[row, vec],
        out_shape=[jax.ShapeDtypeStruct((S, D), bf16), jax.ShapeDtypeStruct((1, D), f32)],
        compiler_params=_params(("arbitrary",)),
    )(dxn, y, g)


def _pre_bwd(dh, x, g, dres, name):
    S, D = x.shape
    tr = _rows(S)

    def body(dh_ref, x_ref, g_ref, dres_ref, dx_ref, dg_ref):
        @pl.when(pl.program_id(0) == 0)
        def _():
            dg_ref[...] = jnp.zeros_like(dg_ref)

        xv = x_ref[...]
        d = dh_ref[...]
        r = lax.rsqrt(jnp.mean(xv * xv, axis=-1, keepdims=True) + NORM_EPS)
        n = xv * r
        dn = d * g_ref[...]
        dg_ref[...] += jnp.sum(d * n, axis=0, keepdims=True)
        dx_ref[...] = dres_ref[...] + r * (dn - n * jnp.mean(dn * n, axis=-1, keepdims=True))

    row = pl.BlockSpec((tr, D), lambda i: (i, 0))
    vec = pl.BlockSpec((1, D), lambda i: (0, 0))
    return pl.pallas_call(
        body, name=name, grid=(S // tr,),
        in_specs=[row, row, vec, row],
        out_specs=[row, vec],
        out_shape=[jax.ShapeDtypeStruct((S, D), f32), jax.ShapeDtypeStruct((1, D), f32)],
        compiler_params=_params(("arbitrary",)),
    )(dh, x, g, dres)


def _sigmoid(x):
    return 1.0 / (1.0 + jnp.exp(-x))


def _gate_fwd(o, z):
    zf = z.astype(f32)
    return o * (zf * _sigmoid(zf))


def _gate_bwd(du, o, z):
    zf = z.astype(f32)
    sig = _sigmoid(zf)
    do = du * (zf * sig)
    dz = du * o * (sig * (1.0 + zf * (1.0 - sig)))
    return do, dz


def _iota2(shape, dim):
    return lax.broadcasted_iota(jnp.int32, shape, dim)


def _swa_scores(qh, kc, kp, slope, sink, n, scale):
    qi = _iota2((BLOCK, BLOCK), 0)
    kj = _iota2((BLOCK, BLOCK), 1)
    mask_c = kj <= qi
    mask_p = kj > qi + jnp.where(n > 0, 0, BLOCK)
    dist_c = (qi - kj).astype(f32)
    dist_p = (qi + BLOCK - kj).astype(f32)
    sc = jnp.where(mask_c, _dot(qh, kc, NT) * scale - slope * dist_c, NEG)
    sp = jnp.where(mask_p, _dot(qh, kp, NT) * scale - slope * dist_p, NEG)
    m = jnp.maximum(jnp.maximum(jnp.max(sc, axis=-1, keepdims=True), jnp.max(sp, axis=-1, keepdims=True)), sink)
    pc = jnp.exp(sc - m)
    pp = jnp.exp(sp - m)
    ps = jnp.exp(sink - m)
    den = jnp.sum(pc, axis=-1, keepdims=True) + jnp.sum(pp, axis=-1, keepdims=True) + ps
    return pc, pp, ps, den


def _swa_specs(S, BR, KV, G):
    gw = G * HEAD_DIM
    qspec = pl.BlockSpec((BLOCK, gw), lambda h, n: (n, h))
    zoff = (BR + 2 * KV) // gw
    zspec = pl.BlockSpec((BLOCK, gw), lambda h, n: (n, zoff + h))
    cur = pl.BlockSpec((1, BLOCK, HEAD_DIM), lambda h, n: (h, n, 0))
    prev = pl.BlockSpec((1, BLOCK, HEAD_DIM), lambda h, n: (h, jnp.maximum(n - 1, 0), 0))
    return qspec, zspec, cur, prev


def _swa_fwd(P, kh, vh, sinks, slopes, name):
    S = P.shape[0]
    n_kv = kh.shape[0]
    H = sinks.shape[0]
    G = H // n_kv
    BR, KV = H * HEAD_DIM, n_kv * HEAD_DIM
    scale = HEAD_DIM ** -0.5
    assert (BR + 2 * KV) % (G * HEAD_DIM) == 0

    def body(q_ref, z_ref, kc_ref, kp_ref, vc_ref, vp_ref, sink_ref, slope_ref, o_ref, u_ref):
        kvh, n = pl.program_id(0), pl.program_id(1)
        kc, kp, vc, vp = kc_ref[0], kp_ref[0], vc_ref[0], vp_ref[0]
        for g in range(G):
            sl = slice(g * HEAD_DIM, (g + 1) * HEAD_DIM)
            h = kvh * G + g
            pc, pp, _, den = _swa_scores(q_ref[:, sl], kc, kp, slope_ref[h], sink_ref[h], n, scale)
            oh = (_dot(pc.astype(bf16), vc) + _dot(pp.astype(bf16), vp)) / den
            o_ref[:, sl] = oh.astype(o_ref.dtype)
            u_ref[:, sl] = _gate_fwd(oh, z_ref[:, sl]).astype(u_ref.dtype)

    qspec, zspec, cur, prev = _swa_specs(S, BR, KV, G)
    ospec = pl.BlockSpec((BLOCK, G * HEAD_DIM), lambda h, n: (n, h))
    return pl.pallas_call(
        body, name=name, grid=(n_kv, S // BLOCK),
        in_specs=[qspec, zspec, cur, prev, cur, prev, SMEM, SMEM],
        out_specs=[ospec, ospec],
        out_shape=[jax.ShapeDtypeStruct((S, BR), bf16)] * 2,
        compiler_params=_params(("parallel", "parallel")),
    )(P, P, kh, kh, vh, vh, sinks, slopes)


def _swa_bwd(P, kh, vh, o, du, sinks, slopes, name):
    S = P.shape[0]
    n_kv = kh.shape[0]
    H = sinks.shape[0]
    G = H // n_kv
    BR, KV = H * HEAD_DIM, n_kv * HEAD_DIM
    scale = HEAD_DIM ** -0.5

    def body(q_ref, z_ref, kc_ref, kp_ref, vc_ref, vp_ref, o_ref, du_ref, sink_ref, slope_ref,
             dq_ref, dz_ref, dk_ref, dv_ref, ds_ref):
        kvh, n = pl.program_id(0), pl.program_id(1)

        @pl.when(n == 0)
        def _():
            dk_ref[...] = jnp.zeros_like(dk_ref)
            dv_ref[...] = jnp.zeros_like(dv_ref)
            ds_ref[...] = jnp.zeros_like(ds_ref)

        kc, kp, vc, vp = kc_ref[0], kp_ref[0], vc_ref[0], vp_ref[0]
        dkc = jnp.zeros((BLOCK, HEAD_DIM), f32)
        dkp = jnp.zeros((BLOCK, HEAD_DIM), f32)
        dvc = jnp.zeros((BLOCK, HEAD_DIM), f32)
        dvp = jnp.zeros((BLOCK, HEAD_DIM), f32)
        dsink = jnp.zeros((8, 128), f32)
        lane = _iota2((8, 128), 1)
        for g in range(G):
            sl = slice(g * HEAD_DIM, (g + 1) * HEAD_DIM)
            h = kvh * G + g
            qh = q_ref[:, sl]
            pc, pp, ps, den = _swa_scores(qh, kc, kp, slope_ref[h], sink_ref[h], n, scale)
            inv = 1.0 / den
            pc, pp, ps = pc * inv, pp * inv, ps * inv
            oh = o_ref[:, sl].astype(f32)
            doh, dzh = _gate_bwd(du_ref[:, sl].astype(f32), oh, z_ref[:, sl])
            dz_ref[:, sl] = dzh.astype(dz_ref.dtype)
            delta = jnp.sum(doh * oh, axis=-1, keepdims=True)
            dob = doh.astype(bf16)
            dsc = (pc * (_dot(dob, vc, NT) - delta)).astype(bf16)
            dsp = (pp * (_dot(dob, vp, NT) - delta)).astype(bf16)
            dq_ref[:, sl] = ((_dot(dsc, kc) + _dot(dsp, kp)) * scale).astype(dq_ref.dtype)
            dkc += _dot(dsc, qh, TN)
            dkp += _dot(dsp, qh, TN)
            dvc += _dot(pc.astype(bf16), dob, TN)
            dvp += _dot(pp.astype(bf16), dob, TN)
            dsh = -jnp.sum(ps * delta, axis=0, keepdims=True)
            dsink += jnp.where(lane == g, jnp.broadcast_to(dsh, (8, 128)), 0.0)
        ds_ref[0] += dsink
        row_c = pl.multiple_of(n * BLOCK, BLOCK)
        dk_ref[0, pl.ds(row_c, BLOCK), :] += dkc * scale
        dv_ref[0, pl.ds(row_c, BLOCK), :] += dvc

        @pl.when(n > 0)
        def _():
            row_p = pl.multiple_of((n - 1) * BLOCK, BLOCK)
            dk_ref[0, pl.ds(row_p, BLOCK), :] += dkp * scale
            dv_ref[0, pl.ds(row_p, BLOCK), :] += dvp

    qspec, zspec, cur, prev = _swa_specs(S, BR, KV, G)
    ospec = pl.BlockSpec((BLOCK, G * HEAD_DIM), lambda h, n: (n, h))
    full = pl.BlockSpec((1, S, HEAD_DIM), lambda h, n: (h, 0, 0))
    return pl.pallas_call(
        body, name=name, grid=(n_kv, S // BLOCK),
        in_specs=[qspec, zspec, cur, prev, cur, prev, ospec, ospec, SMEM, SMEM],
        out_specs=[ospec, ospec, full, full, pl.BlockSpec((1, 8, 128), lambda h, n: (h, 0, 0))],
        out_shape=[jax.ShapeDtypeStruct((S, BR), bf16)] * 2
        + [jax.ShapeDtypeStruct((n_kv, S, HEAD_DIM), f32)] * 2
        + [jax.ShapeDtypeStruct((n_kv, 8, 128), f32)],
        compiler_params=_params(("parallel", "arbitrary")),
    )(P, P, kh, kh, vh, vh, o, du, sinks, slopes)


def _split3(x):
    x1 = x.astype(bf16)
    r1 = x - x1.astype(f32)
    x2 = r1.astype(bf16)
    x3 = (r1 - x2.astype(f32)).astype(bf16)
    return x1, x2, x3


def _split2(x):
    x1 = x.astype(bf16)
    return x1, (x - x1.astype(f32)).astype(bf16)


def _pair_specs(S, BR, T, kmap):
    nb = BR // 128
    q = pl.BlockSpec((T, 128), lambda p, i, j: (i, p))
    k = pl.BlockSpec((T, 128), lambda p, i, j: (kmap(i, j), nb + p))
    v = pl.BlockSpec((T, 128), lambda p, i, j: (kmap(i, j), 2 * nb + p))
    z = pl.BlockSpec((T, 128), lambda p, i, j: (i, 3 * nb + p))
    return q, k, v, z


def _fox_fwd(P, cq, ck, BR, name):
    S = P.shape[0]
    H = BR // HEAD_DIM
    T = _attn_tile(S)
    nq = S // T
    scale = HEAD_DIM ** -0.5
    kmap = lambda i, j: jnp.minimum(i, j)

    def body(q_ref, k_ref, v_ref, z_ref, cq_ref, ck_ref, o_ref, u_ref, lse_ref, m_s, l_s, acc_s):
        qb, kb = pl.program_id(1), pl.program_id(2)

        @pl.when(kb == 0)
        def _():
            m_s[...] = jnp.full_like(m_s, NEG)
            l_s[...] = jnp.zeros_like(l_s)
            acc_s[...] = jnp.zeros_like(acc_s)

        @pl.when(kb <= qb)
        def _():
            tpos = qb * T + _iota2((T, T), 0)
            spos = kb * T + _iota2((T, T), 1)
            causal = spos <= tpos
            for hh in range(2):
                sl = slice(hh * HEAD_DIM, (hh + 1) * HEAD_DIM)
                s = _dot(q_ref[:, sl], k_ref[:, sl], NT) * scale + cq_ref[hh] - ck_ref[hh]
                s = jnp.where(causal, s, NEG)
                m_old = m_s[hh]
                m_new = jnp.maximum(m_old, jnp.max(s, axis=-1, keepdims=True))
                alpha = jnp.exp(m_old - m_new)
                p = jnp.exp(s - m_new)
                l_s[hh] = alpha * l_s[hh] + jnp.sum(p, axis=-1, keepdims=True)
                acc_s[hh] = alpha * acc_s[hh] + _dot(p.astype(bf16), v_ref[:, sl])
                m_s[hh] = m_new

        @pl.when(kb == qb)
        def _():
            for hh in range(2):
                sl = slice(hh * HEAD_DIM, (hh + 1) * HEAD_DIM)
                oh = acc_s[hh] / l_s[hh]
                o_ref[:, sl] = oh.astype(o_ref.dtype)
                u_ref[:, sl] = _gate_fwd(oh, z_ref[:, sl]).astype(u_ref.dtype)
                lse_ref[hh] = m_s[hh] + jnp.log(l_s[hh])

    q, k, v, z = _pair_specs(S, BR, T, kmap)
    col = pl.BlockSpec((2, T, 1), lambda p, i, j: (p, i, 0))
    rowk = pl.BlockSpec((2, 1, T), lambda p, i, j: (p, 0, kmap(i, j)))
    ospec = pl.BlockSpec((T, 128), lambda p, i, j: (i, p))
    return pl.pallas_call(
        body, name=name, grid=(H // 2, nq, nq),
        in_specs=[q, k, v, z, col, rowk],
        out_specs=[ospec, ospec, col],
        out_shape=[jax.ShapeDtypeStruct((S, BR), bf16)] * 2 + [jax.ShapeDtypeStruct((H, S, 1), f32)],
        scratch_shapes=[pltpu.VMEM((2, T, 1), f32), pltpu.VMEM((2, T, 1), f32), pltpu.VMEM((2, T, HEAD_DIM), f32)],
        compiler_params=_params(("parallel", "arbitrary", "arbitrary")),
    )(P, P, P, P, cq, ck)


def _fox_bwd(P, o, du, lse, cq, ck, BR, name):
    S = P.shape[0]
    H = BR // HEAD_DIM
    T = _attn_tile(S)
    nq = S // T
    scale = HEAD_DIM ** -0.5
    kmap = lambda i, j: jnp.minimum(i, j)

    def body(q_ref, k_ref, v_ref, z_ref, o_ref, du_ref, lse_ref, cq_ref, ck_ref,
             dq_ref, dz_ref, dk_ref, dv_ref, dcq_ref, dck_ref, do_s, delta_s, dq_s, dcq_s):
        qb, kb = pl.program_id(1), pl.program_id(2)

        @pl.when(jnp.logical_and(qb == 0, kb == 0))
        def _():
            dk_ref[...] = jnp.zeros_like(dk_ref)
            dv_ref[...] = jnp.zeros_like(dv_ref)
            dck_ref[...] = jnp.zeros_like(dck_ref)

        @pl.when(kb == 0)
        def _():
            for hh in range(2):
                sl = slice(hh * HEAD_DIM, (hh + 1) * HEAD_DIM)
                oh = o_ref[:, sl].astype(f32)
                doh, dzh = _gate_bwd(du_ref[:, sl].astype(f32), oh, z_ref[:, sl])
                dz_ref[:, sl] = dzh.astype(dz_ref.dtype)
                do_s[hh] = doh.astype(bf16)
                delta_s[hh] = jnp.sum(doh * oh, axis=-1, keepdims=True)
            dq_s[...] = jnp.zeros_like(dq_s)
            dcq_s[...] = jnp.zeros_like(dcq_s)

        @pl.when(kb <= qb)
        def _():
            tpos = qb * T + _iota2((T, T), 0)
            spos = kb * T + _iota2((T, T), 1)
            causal = spos <= tpos
            rows = pl.ds(pl.multiple_of(kb * T, T), T)
            for hh in range(2):
                sl = slice(hh * HEAD_DIM, (hh + 1) * HEAD_DIM)
                qh, kh, vh, dob = q_ref[:, sl], k_ref[:, sl], v_ref[:, sl], do_s[hh]
                s = _dot(qh, kh, NT) * scale + cq_ref[hh] - ck_ref[hh]
                p = jnp.where(causal, jnp.exp(s - lse_ref[hh]), 0.0)
                ds = p * (_dot(dob, vh, NT) - delta_s[hh])
                dsb = ds.astype(bf16)
                dq_s[hh] += _dot(dsb, kh)
                dk_ref[rows, sl] += _dot(dsb, qh, TN) * scale
                dv_ref[rows, sl] += _dot(p.astype(bf16), dob, TN)
                dcq_s[hh] += jnp.sum(ds, axis=-1, keepdims=True)
                dck_ref[hh, kb] += jnp.sum(ds, axis=0, keepdims=True)

        @pl.when(kb == qb)
        def _():
            for hh in range(2):
                sl = slice(hh * HEAD_DIM, (hh + 1) * HEAD_DIM)
                dq_ref[:, sl] = (dq_s[hh] * scale).astype(dq_ref.dtype)
                dcq_ref[hh] = dcq_s[hh]

    q, k, v, z = _pair_specs(S, BR, T, kmap)
    col = pl.BlockSpec((2, T, 1), lambda p, i, j: (p, i, 0))
    rowk = pl.BlockSpec((2, 1, T), lambda p, i, j: (p, 0, kmap(i, j)))
    ospec = pl.BlockSpec((T, 128), lambda p, i, j: (i, p))
    full = pl.BlockSpec((S, 128), lambda p, i, j: (0, p))
    return pl.pallas_call(
        body, name=name, grid=(H // 2, nq, nq),
        in_specs=[q, k, v, z, ospec, ospec, col, col, rowk],
        out_specs=[ospec, ospec, full, full, col, pl.BlockSpec((2, nq, 1, T), lambda p, i, j: (p, 0, 0, 0))],
        out_shape=[jax.ShapeDtypeStruct((S, BR), bf16)] * 2 + [jax.ShapeDtypeStruct((S, BR), f32)] * 2
        + [jax.ShapeDtypeStruct((H, S, 1), f32), jax.ShapeDtypeStruct((H, nq, 1, T), f32)],
        scratch_shapes=[pltpu.VMEM((2, T, HEAD_DIM), bf16), pltpu.VMEM((2, T, 1), f32),
                        pltpu.VMEM((2, T, HEAD_DIM), f32), pltpu.VMEM((2, T, 1), f32)],
        compiler_params=_params(("parallel", "arbitrary", "arbitrary")),
    )(P, P, P, P, o, du, lse, cq, ck)


def _fox_cum(flT, b, name):
    H, S = flT.shape
    tj = _pick(S, (256, 128))

    def body(fl_ref, b_ref, c_ref):
        j = pl.program_id(0)
        x = fl_ref[...] + b_ref[...]
        logf = jnp.minimum(x, 0.0) - jnp.log(1.0 + jnp.exp(-jnp.abs(x)))
        tri = (_iota2((S, tj), 0) <= j * tj + _iota2((S, tj), 1)).astype(bf16)
        c_ref[...] = sum(_dot(part, tri) for part in _split3(logf))

    return pl.pallas_call(
        body, name=name, grid=(S // tj,),
        in_specs=[pl.BlockSpec((H, S), lambda j: (0, 0)), pl.BlockSpec((H, 1), lambda j: (0, 0))],
        out_specs=pl.BlockSpec((H, tj), lambda j: (0, j)),
        out_shape=jax.ShapeDtypeStruct((H, S), f32),
        compiler_params=_params(("parallel",)),
    )(flT, b)


def _fox_cum_bwd(dcq, dck, flT, b, name):
    H, S = flT.shape
    tj = _pick(S, (256, 128))

    def body(dcq_ref, dck_ref, fl_ref, b_ref, o_ref, db_ref):
        j = pl.program_id(0)

        @pl.when(j == 0)
        def _():
            db_ref[...] = jnp.zeros_like(db_ref)

        tri = (_iota2((S, tj), 0) >= j * tj + _iota2((S, tj), 1)).astype(bf16)
        dlogf = sum(_dot(part, tri) for part in _split3(dcq_ref[...] - dck_ref[...]))
        x = fl_ref[...] + b_ref[...]
        dfl = dlogf * _sigmoid(-x)
        o_ref[...] = dfl
        db_ref[...] += jnp.sum(dfl, axis=-1, keepdims=True)

    blk = pl.BlockSpec((H, tj), lambda j: (0, j))
    whole = pl.BlockSpec((H, S), lambda j: (0, 0))
    col = pl.BlockSpec((H, 1), lambda j: (0, 0))
    return pl.pallas_call(
        body, name=name, grid=(S // tj,),
        in_specs=[whole, whole, blk, col],
        out_specs=[blk, col],
        out_shape=[jax.ShapeDtypeStruct((H, S), f32), jax.ShapeDtypeStruct((H, 1), f32)],
        compiler_params=_params(("arbitrary",)),
    )(dcq, dck, flT, b)


def _sb_logs(qh, kc, before, scale):
    l = _dot(qh, kc, NT) * scale
    lsig = jnp.minimum(l, 0.0) - jnp.log(1.0 + jnp.exp(-jnp.abs(l)))
    lf = jnp.where(before, lsig - l, 0.0)
    return lsig, lf


def _sb_suffix(lf, tri):
    hi, lo = _split2(lf)
    return _dot(hi, tri) + _dot(lo, tri)


def _sb_fwd(P, BR, name):
    S = P.shape[0]
    H = BR // HEAD_DIM
    T = _attn_tile(S)
    nq = S // T
    nc = T // BLOCK
    scale = HEAD_DIM ** -0.5
    kmap = lambda i, j: jnp.maximum(i - j, 0)

    def body(q_ref, k_ref, v_ref, z_ref, o_ref, u_ref, rc_ref, r_s, acc_s):
        qb, j = pl.program_id(1), pl.program_id(2)
        kb = qb - j

        @pl.when(j == 0)
        def _():
            r_s[...] = jnp.zeros_like(r_s)
            acc_s[...] = jnp.zeros_like(acc_s)

        @pl.when(j <= qb)
        def _():
            tpos = qb * T + _iota2((T, BLOCK), 0)
            col = _iota2((T, BLOCK), 1)
            tri = (_iota2((BLOCK, BLOCK), 0) > _iota2((BLOCK, BLOCK), 1)).astype(bf16)
            for hh in range(2):
                sl = slice(hh * HEAD_DIM, (hh + 1) * HEAD_DIM)
                qh = q_ref[:, sl]
                r = r_s[hh]
                rc_ref[hh, 0] = r
                acc = acc_s[hh]
                for c in reversed(range(nc)):
                    rows = slice(c * BLOCK, (c + 1) * BLOCK)
                    before = kb * T + c * BLOCK + col < tpos
                    lsig, lf = _sb_logs(qh, k_ref[rows, sl], before, scale)
                    a = jnp.where(before, jnp.exp(lsig + _sb_suffix(lf, tri) + r), 0.0)
                    acc = acc + _dot(a.astype(bf16), v_ref[rows, sl])
                    r = r + jnp.sum(lf, axis=-1, keepdims=True)
                r_s[hh] = r
                acc_s[hh] = acc

        @pl.when(j == qb)
        def _():
            for hh in range(2):
                sl = slice(hh * HEAD_DIM, (hh + 1) * HEAD_DIM)
                oh = acc_s[hh]
                o_ref[:, sl] = oh.astype(o_ref.dtype)
                u_ref[:, sl] = _gate_fwd(oh, z_ref[:, sl]).astype(u_ref.dtype)

    q, k, v, z = _pair_specs(S, BR, T, kmap)
    ospec = pl.BlockSpec((T, 128), lambda p, i, j: (i, p))
    rc = pl.BlockSpec((2, 1, T, 1), lambda p, i, j: (p, kmap(i, j), i, 0))
    return pl.pallas_call(
        body, name=name, grid=(H // 2, nq, nq),
        in_specs=[q, k, v, z],
        out_specs=[ospec, ospec, rc],
        out_shape=[jax.ShapeDtypeStruct((S, BR), bf16)] * 2 + [jax.ShapeDtypeStruct((H, nq, S, 1), f32)],
        scratch_shapes=[pltpu.VMEM((2, T, 1), f32), pltpu.VMEM((2, T, HEAD_DIM), f32)],
        compiler_params=_params(("parallel", "arbitrary", "arbitrary")),
    )(P, P, P, P)


def _sb_bwd(P, o, du, rc, BR, name):
    S = P.shape[0]
    H = BR // HEAD_DIM
    T = _attn_tile(S)
    nq = S // T
    nc = T // BLOCK
    scale = HEAD_DIM ** -0.5
    kmap = lambda i, j: jnp.minimum(i, j)

    def body(q_ref, k_ref, v_ref, z_ref, o_ref, du_ref, rc_ref,
             dq_ref, dz_ref, dk_ref, dv_ref, do_s, dq_s, g_s):
        qb, kb = pl.program_id(1), pl.program_id(2)

        @pl.when(jnp.logical_and(qb == 0, kb == 0))
        def _():
            dk_ref[...] = jnp.zeros_like(dk_ref)
            dv_ref[...] = jnp.zeros_like(dv_ref)

        @pl.when(kb == 0)
        def _():
            for hh in range(2):
                sl = slice(hh * HEAD_DIM, (hh + 1) * HEAD_DIM)
                doh, dzh = _gate_bwd(du_ref[:, sl].astype(f32), o_ref[:, sl].astype(f32), z_ref[:, sl])
                dz_ref[:, sl] = dzh.astype(dz_ref.dtype)
                do_s[hh] = doh.astype(bf16)
            dq_s[...] = jnp.zeros_like(dq_s)
            g_s[...] = jnp.zeros_like(g_s)

        @pl.when(kb <= qb)
        def _():
            tpos = qb * T + _iota2((T, BLOCK), 0)
            col = _iota2((T, BLOCK), 1)
            ii, jj = _iota2((BLOCK, BLOCK), 0), _iota2((BLOCK, BLOCK), 1)
            tri_suffix = (ii > jj).astype(bf16)
            tri_prefix = (ii < jj).astype(bf16)
            for hh in range(2):
                sl = slice(hh * HEAD_DIM, (hh + 1) * HEAD_DIM)
                qh, dob = q_ref[:, sl], do_s[hh]
                befores, lsigs, lfs = [], [], []
                for c in range(nc):
                    before = kb * T + c * BLOCK + col < tpos
                    lsig, lf = _sb_logs(qh, k_ref[c * BLOCK:(c + 1) * BLOCK, sl], before, scale)
                    befores.append(before)
                    lsigs.append(lsig)
                    lfs.append(lf)
                carries = [None] * nc
                r = rc_ref[hh, 0]
                for c in reversed(range(nc)):
                    carries[c] = r
                    r = r + jnp.sum(lfs[c], axis=-1, keepdims=True)
                gsum = g_s[hh]
                dq = dq_s[hh]
                for c in range(nc):
                    rows = slice(c * BLOCK, (c + 1) * BLOCK)
                    out_rows = pl.ds(pl.multiple_of(kb * T + c * BLOCK, BLOCK), BLOCK)
                    kc, vc = k_ref[rows, sl], v_ref[rows, sl]
                    a = jnp.where(befores[c], jnp.exp(lsigs[c] + _sb_suffix(lfs[c], tri_suffix) + carries[c]), 0.0)
                    g = a * _dot(dob, vc, NT)
                    gex = _dot(g.astype(bf16), tri_prefix) + gsum
                    sig = jnp.exp(lsigs[c])
                    dl = jnp.where(befores[c], g * (1.0 - sig) - gex * sig, 0.0).astype(bf16)
                    dq = dq + _dot(dl, kc)
                    dk_ref[out_rows, sl] += _dot(dl, qh, TN) * scale
                    dv_ref[out_rows, sl] += _dot(a.astype(bf16), dob, TN)
                    gsum = gsum + jnp.sum(g, axis=-1, keepdims=True)
                g_s[hh] = gsum
                dq_s[hh] = dq

        @pl.when(kb == qb)
        def _():
            for hh in range(2):
                sl = slice(hh * HEAD_DIM, (hh + 1) * HEAD_DIM)
                dq_ref[:, sl] = (dq_s[hh] * scale).astype(dq_ref.dtype)

    q, k, v, z = _pair_specs(S, BR, T, kmap)
    ospec = pl.BlockSpec((T, 128), lambda p, i, j: (i, p))
    full = pl.BlockSpec((S, 128), lambda p, i, j: (0, p))
    rcs = pl.BlockSpec((2, 1, T, 1), lambda p, i, j: (p, kmap(i, j), i, 0))
    return pl.pallas_call(
        body, name=name, grid=(H // 2, nq, nq),
        in_specs=[q, k, v, z, ospec, ospec, rcs],
        out_specs=[ospec, ospec, full, full],
        out_shape=[jax.ShapeDtypeStruct((S, BR), bf16)] * 2 + [jax.ShapeDtypeStruct((S, BR), f32)] * 2,
        scratch_shapes=[pltpu.VMEM((2, T, HEAD_DIM), bf16), pltpu.VMEM((2, T, HEAD_DIM), f32),
                        pltpu.VMEM((2, T, 1), f32)],
        compiler_params=_params(("parallel", "arbitrary", "arbitrary")),
    )(P, P, P, P, o, du, rc)


def _adam_math(w, g, m, v):
    m = ADAM_B1 * m + (1.0 - ADAM_B1) * g
    v = ADAM_B2 * v + (1.0 - ADAM_B2) * (g * g)
    m_hat = m / (1.0 - ADAM_B1 ** ADAM_STEP)
    v_hat = v / (1.0 - ADAM_B2 ** ADAM_STEP)
    delta = -ADAM_LR * (m_hat / (jnp.sqrt(v_hat) + ADAM_EPS) + ADAM_WD * w)
    return delta, m, v


def _adamw_sum(parts, w, m, v, name):
    R, C = w.shape
    tr = _pick(R, (256, 128))

    def body(p_ref, w_ref, m_ref, v_ref, g_ref, d_ref, nm_ref, nv_ref):
        g = p_ref[0].astype(f32)
        for i in range(1, N_DEV):
            g = g + p_ref[i].astype(f32)
        g_ref[...] = g
        d_ref[...], nm_ref[...], nv_ref[...] = _adam_math(w_ref[...], g, m_ref[...], v_ref[...])

    blk = pl.BlockSpec((tr, C), lambda i: (i, 0))
    return pl.pallas_call(
        body, name=name, grid=(R // tr,),
        in_specs=[pl.BlockSpec((N_DEV, tr, C), lambda i: (0, i, 0)), blk, blk, blk],
        out_specs=[blk] * 4,
        out_shape=[jax.ShapeDtypeStruct((R, C), f32)] * 4,
        compiler_params=_params(("parallel",)),
    )(parts, w, m, v)


def _coords():
    return lax.axis_index("x"), lax.axis_index("y"), lax.axis_index("c")


def _all_gather(shards, name):
    n = len(shards)

    def body(*refs):
        ins, outs = refs[:n], refs[n:2 * n]
        send_sems, recv_sems, local_sems = refs[2 * n:]
        x, y, c = _coords()
        me, sibling = (x, y, c), (x, y, 1 - c)
        chips = [(1 - x, y), (x, 1 - y), (1 - x, 1 - y)]

        def slot(out, dev):
            return out.at[4 * dev[0] + 2 * dev[1] + dev[2]]

        def copy(a, k, block, to, src=None):
            return pltpu.make_async_remote_copy(
                src_ref=slot(outs[a], block) if src is None else src, dst_ref=slot(outs[a], block),
                send_sem=send_sems.at[a, k], recv_sem=recv_sems.at[a, k], device_id=to, device_id_type=MESH)

        mine = [pltpu.make_async_copy(ins[a], slot(outs[a], me), local_sems.at[a]) for a in range(n)]
        for cp in mine:
            cp.start()
        first = []
        for a in range(n):
            first.append(copy(a, 0, me, sibling, src=ins[a]))
            first += [copy(a, 1 + j, me, (*chip, c), src=ins[a]) for j, chip in enumerate(chips)]
        for cp in first:
            cp.start()
        passed = []
        for j, chip in enumerate(chips):
            for a in range(n):
                copy(a, 1 + j, (*chip, c), me).wait_recv()
                passed.append(copy(a, 4 + j, (*chip, c), sibling))
                passed[-1].start()
        for a in range(n):
            copy(a, 0, sibling, me).wait_recv()
            for j, chip in enumerate(chips):
                copy(a, 4 + j, (*chip, 1 - c), me).wait_recv()
        for cp in first + passed:
            cp.wait_send()
        for cp in mine:
            cp.wait()

    return pl.pallas_call(
        body, name=name,
        in_specs=[ANY] * n, out_specs=[ANY] * n,
        out_shape=[jax.ShapeDtypeStruct((N_DEV,) + s.shape, s.dtype) for s in shards],
        scratch_shapes=[pltpu.SemaphoreType.DMA((n, 7)), pltpu.SemaphoreType.DMA((n, 7)),
                        pltpu.SemaphoreType.DMA((n,))],
    )(*shards)


def _all_to_all(blocks, name):
    n = len(blocks)

    def body(*refs):
        ins, outs = refs[:n], refs[n:2 * n]
        send_sems, recv_sems, local_sems = refs[2 * n:]
        x, y, c = _coords()
        me = 4 * x + 2 * y + c
        mine = [pltpu.make_async_copy(ins[a].at[me], outs[a].at[me], local_sems.at[a]) for a in range(n)]
        for cp in mine:
            cp.start()

        def peer(k):
            return (x ^ ((k >> 2) & 1), y ^ ((k >> 1) & 1), c ^ (k & 1))

        def copy(a, k):
            px, py, pc = peer(k)
            them = 4 * px + 2 * py + pc
            return pltpu.make_async_remote_copy(
                src_ref=ins[a].at[them], dst_ref=outs[a].at[me],
                send_sem=send_sems.at[a, k - 1], recv_sem=recv_sems.at[a, k - 1],
                device_id=(px, py, pc), device_id_type=MESH)

        def arrival(a, k):
            px, py, pc = peer(k)
            them = 4 * px + 2 * py + pc
            return pltpu.make_async_remote_copy(
                src_ref=ins[a].at[them], dst_ref=outs[a].at[them],
                send_sem=send_sems.at[a, k - 1], recv_sem=recv_sems.at[a, k - 1],
                device_id=(px, py, pc), device_id_type=MESH)

        sent = [copy(a, k) for k in range(1, N_DEV) for a in range(n)]
        for cp in sent:
            cp.start()
        for k in range(1, N_DEV):
            for a in range(n):
                arrival(a, k).wait_recv()
        for cp in sent:
            cp.wait_send()
        for cp in mine:
            cp.wait()

    return pl.pallas_call(
        body, name=name,
        in_specs=[ANY] * n, out_specs=[ANY] * n,
        out_shape=[jax.ShapeDtypeStruct(b.shape, b.dtype) for b in blocks],
        scratch_shapes=[pltpu.SemaphoreType.DMA((n, 7)), pltpu.SemaphoreType.DMA((n, 7)),
                        pltpu.SemaphoreType.DMA((n,))],
    )(*blocks)


def _all_reduce_small(vec, name):
    R, C = vec.shape

    def body(v_ref, o_ref, gath, send_sems, recv_sems):
        x, y, c = _coords()
        me = 4 * x + 2 * y + c
        gath[me] = v_ref[...]

        def copy(k):
            px, py, pc = x ^ ((k >> 2) & 1), y ^ ((k >> 1) & 1), c ^ (k & 1)
            return pltpu.make_async_remote_copy(
                src_ref=v_ref, dst_ref=gath.at[me], send_sem=send_sems.at[k - 1], recv_sem=recv_sems.at[k - 1],
                device_id=(px, py, pc), device_id_type=MESH)

        sent = [copy(k) for k in range(1, N_DEV)]
        for cp in sent:
            cp.start()
        for cp in sent:
            cp.wait()
        total = gath[0]
        for i in range(1, N_DEV):
            total = total + gath[i]
        o_ref[...] = total

    return pl.pallas_call(
        body, name=name,
        in_specs=[pl.BlockSpec(memory_space=pltpu.VMEM)], out_specs=pl.BlockSpec(memory_space=pltpu.VMEM),
        out_shape=jax.ShapeDtypeStruct((R, C), f32),
        scratch_shapes=[pltpu.VMEM((N_DEV, R, C), f32), pltpu.SemaphoreType.DMA((7,)), pltpu.SemaphoreType.DMA((7,))],
    )(vec)


def _gathered_in(g):
    return jnp.transpose(g, (1, 0, 2)).reshape(g.shape[1], -1)


def _col_blocks(dw):
    D, N = dw.shape
    return jnp.transpose(dw.reshape(D, N_DEV, N // N_DEV), (1, 0, 2)).astype(bf16)


def _alibi_slopes(n):
    return jnp.asarray(2.0 ** (-8.0 * np.arange(1, n + 1, dtype=np.float32) / n), dtype=f32)


def _heads_major(a, n):
    return jnp.transpose(a.reshape(a.shape[0], n, HEAD_DIM), (1, 0, 2))


def _heads_minor(a):
    return jnp.transpose(a, (1, 0, 2)).reshape(a.shape[1], -1)


def kernel(x, g_pre, g_post, w_in_a, w_out_a, sinks_a, w_in_b, w_out_b, w_in_c, b_f_c, w_out_c, loss_target, m_g_pre, m_g_post, m_w_in_a, m_w_out_a, m_sinks_a, m_w_in_b, m_w_out_b, m_w_in_c, m_b_f_c, m_w_out_c, v_g_pre, v_g_post, v_w_in_a, v_w_out_a, v_sinks_a, v_w_in_b, v_w_out_b, v_w_in_c, v_b_f_c, v_w_out_c):
    S, D = x.shape[1], x.shape[2]
    H = D // HEAD_DIM
    BR = H * HEAD_DIM
    n_kv = H // 8
    KV = n_kv * HEAD_DIM
    x0 = x[0]
    target = loss_target[0]
    slopes = _alibi_slopes(H)
    w_in = {0: w_in_a, 1: w_in_b, 2: w_in_c}
    w_out = {0: w_out_a, 1: w_out_b, 2: w_out_c}

    saved = []
    xi = x0
    for i in range(DEPTH):
        kind, j = i % 3, i // 3
        win_g, wout_g = _all_gather([w_in[kind][j].astype(bf16), w_out[kind][j].astype(bf16)], name=f"gather_w{i}")
        W_in = _gathered_in(win_g)
        W_out = wout_g.reshape(BR, D)
        h = _rmsnorm_fwd(xi, g_pre[i:i + 1], name=f"pre_norm{i}")
        st = dict(x=xi, h=h, W_out=W_out)
        if kind == 2:
            n_main = 4 * BR
            pad = (-W_in.shape[1]) % 128
            W_in = jnp.pad(W_in, ((0, 0), (0, pad)))
            W_f = W_in[:, n_main:n_main + 128]
            fl = _matmul(h, W_f, "nn", f32, name=f"f_proj{i}")
            flT = jnp.transpose(fl[:, :H])
            bcol = b_f_c[j].reshape(H, 1)
            cumT = _fox_cum(flT, bcol, name=f"fox_cum{i}")
            cq, ck = cumT[:, :, None], cumT[:, None, :]
            st.update(flT=flT, bcol=bcol, cq=cq, ck=ck)
        st["W_in"] = W_in
        P = _matmul(h, W_in, "nn", bf16, name=f"in_proj{i}")
        st["P"] = P
        if kind == 0:
            kh = _heads_major(P[:, BR:BR + KV], n_kv)
            vh = _heads_major(P[:, BR + KV:BR + 2 * KV], n_kv)
            o, u = _swa_fwd(P, kh, vh, sinks_a[j], slopes, name=f"swa_fwd{i}")
            st.update(kh=kh, vh=vh)
        elif kind == 1:
            o, u, rc = _sb_fwd(P, BR, name=f"sb_fwd{i}")
            st.update(rc=rc)
        else:
            o, u, lse = _fox_fwd(P, cq, ck, BR, name=f"fox_fwd{i}")
            st.update(lse=lse)
        st.update(o=o, u=u)
        y = _matmul(u, W_out, "nn", f32, name=f"out_proj{i}")
        st["y"] = y
        xi = _post_fwd(xi, y, g_post[i:i + 1], name=f"post_norm{i}")
        saved.append(st)

    loss_part, dx = _loss_fwd_bwd(xi, target, name="loss")

    dg_pre, dg_post = [None] * DEPTH, [None] * DEPTH
    dsinks = [None, None]
    db_f = None
    recv = [None] * DEPTH
    for i in reversed(range(DEPTH)):
        kind, j = i % 3, i // 3
        st = saved[i]
        dy, dg_post[i] = _post_bwd(dx, st["y"], g_post[i:i + 1], name=f"post_bwd{i}")
        du = _matmul(dy, st["W_out"], "nt", bf16, name=f"du{i}")
        dW_out = _matmul(st["u"], dy, "tn", f32, name=f"dw_out{i}")
        P = st["P"]
        if kind == 0:
            dq, dz, dkh, dvh, dsk = _swa_bwd(P, st["kh"], st["vh"], st["o"], du, sinks_a[j], slopes, name=f"swa_bwd{i}")
            dsinks[j] = dsk[:, 0, :H // n_kv].reshape(H)
            dP = jnp.concatenate([dq, _heads_minor(dkh).astype(bf16), _heads_minor(dvh).astype(bf16), dz], axis=1)
        elif kind == 1:
            dq, dz, dk, dv = _sb_bwd(P, st["o"], du, st["rc"], BR, name=f"sb_bwd{i}")
            dP = jnp.concatenate([dq, dk.astype(bf16), dv.astype(bf16), dz], axis=1)
        else:
            dq, dz, dk, dv, dcq, dck = _fox_bwd(P, st["o"], du, st["lse"], st["cq"], st["ck"], BR, name=f"fox_bwd{i}")
            dflT, db_col = _fox_cum_bwd(dcq.reshape(H, S), dck.reshape(H, S), st["flT"], st["bcol"],
                                        name=f"fox_cum_bwd{i}")
            db_f = db_col.reshape(H)
            dfl = jnp.pad(jnp.transpose(dflT), ((0, 0), (0, 128 - H))).astype(bf16)
            dP = jnp.concatenate([dq, dk.astype(bf16), dv.astype(bf16), dz, dfl], axis=1)
        dh = _matmul(dP, st["W_in"], "nt", f32, name=f"dh{i}")
        dW_in = _matmul(st["h"], dP, "tn", f32, name=f"dw_in{i}")
        n_cols = w_in[kind].shape[2] * N_DEV
        recv[i] = _all_to_all([_col_blocks(dW_in[:, :n_cols]), dW_out.astype(bf16).reshape(N_DEV, BR // N_DEV, D)],
                              name=f"scatter_dw{i}")
        dx, dg_pre[i] = _pre_bwd(dh, st["x"], g_pre[i:i + 1], dx, name=f"pre_bwd{i}")

    small = jnp.concatenate(
        [jnp.concatenate(dg_pre, axis=0).reshape(-1), jnp.concatenate(dg_post, axis=0).reshape(-1),
         jnp.concatenate(dsinks), db_f, loss_part[0, :1]])
    n_small = small.shape[0]
    rows = -(-n_small // 128)
    rows = -(-rows // 8) * 8
    small = jnp.pad(small, (0, rows * 128 - n_small)).reshape(rows, 128)
    total = _all_reduce_small(small, name="reduce_small").reshape(-1)
    o0 = DEPTH * D
    grad_g_pre = total[:o0].reshape(DEPTH, D)
    grad_g_post = total[o0:2 * o0].reshape(DEPTH, D)
    grad_sinks = total[2 * o0:2 * o0 + 2 * H].reshape(2, H)
    grad_b_f = total[2 * o0 + 2 * H:2 * o0 + 3 * H].reshape(1, H)
    loss = total[2 * o0 + 3 * H]

    def small_adam(w, g, m, v, name):
        def body(w_ref, g_ref, m_ref, v_ref, d_ref, nm_ref, nv_ref):
            d_ref[...], nm_ref[...], nv_ref[...] = _adam_math(w_ref[...], g_ref[...], m_ref[...], v_ref[...])
        vm = pl.BlockSpec(memory_space=pltpu.VMEM)
        return pl.pallas_call(body, name=name, in_specs=[vm] * 4, out_specs=[vm] * 3,
                              out_shape=[jax.ShapeDtypeStruct(w.shape, f32)] * 3)(w, g, m, v)

    upd = {}
    upd["g_pre"] = (grad_g_pre,) + tuple(small_adam(g_pre, grad_g_pre, m_g_pre, v_g_pre, "adam_g_pre"))
    upd["g_post"] = (grad_g_post,) + tuple(small_adam(g_post, grad_g_post, m_g_post, v_g_post, "adam_g_post"))
    upd["sinks_a"] = (grad_sinks,) + tuple(small_adam(sinks_a, grad_sinks, m_sinks_a, v_sinks_a, "adam_sinks"))
    upd["b_f_c"] = (grad_b_f,) + tuple(small_adam(b_f_c, grad_b_f, m_b_f_c, v_b_f_c, "adam_b_f"))

    def big(i, which, w, m, v):
        return _adamw_sum(recv[i][which], w, m, v, name=f"adam_{'in' if which == 0 else 'out'}{i}")

    a_in = [big(i, 0, w_in_a[jj], m_w_in_a[jj], v_w_in_a[jj]) for jj, i in enumerate((0, 3))]
    a_out = [big(i, 1, w_out_a[jj], m_w_out_a[jj], v_w_out_a[jj]) for jj, i in enumerate((0, 3))]
    upd["w_in_a"] = tuple(jnp.stack([a_in[0][t], a_in[1][t]]) for t in range(4))
    upd["w_out_a"] = tuple(jnp.stack([a_out[0][t], a_out[1][t]]) for t in range(4))
    upd["w_in_b"] = tuple(t[None] for t in big(1, 0, w_in_b[0], m_w_in_b[0], v_w_in_b[0]))
    upd["w_out_b"] = tuple(t[None] for t in big(1, 1, w_out_b[0], m_w_out_b[0], v_w_out_b[0]))
    upd["w_in_c"] = tuple(t[None] for t in big(2, 0, w_in_c[0], m_w_in_c[0], v_w_in_c[0]))
    upd["w_out_c"] = tuple(t[None] for t in big(2, 1, w_out_c[0], m_w_out_c[0], v_w_out_c[0]))

    names = ["g_pre", "g_post", "w_in_a", "w_out_a", "sinks_a", "w_in_b", "w_out_b", "w_in_c", "b_f_c", "w_out_c"]
    return (loss, dx[None], *[upd[k][0] for k in names], *[upd[k][1] for k in names],
            *[upd[k][2] for k in names], *[upd[k][3] for k in names])
```

```python
import functools
import math

import numpy as np
import jax
import jax.numpy as jnp
from jax import lax
from jax.experimental import pallas as pl
from jax.experimental.pallas import tpu as pltpu

HEAD_DIM = 64
BLOCK = 128
NORM_EPS = 1e-6
NEG = -1e30
N_DEV = 8
DEPTH = 4
ADAM_LR, ADAM_B1, ADAM_B2, ADAM_EPS, ADAM_WD, ADAM_STEP = 0.001, 0.9, 0.999, 1e-8, 0.01, 10
VMEM_LIMIT = 56 * 1024 * 1024

bf16 = jnp.bfloat16
f32 = jnp.float32
MESH = pl.DeviceIdType.MESH
ANY = pl.BlockSpec(memory_space=pl.ANY)
SMEM = pl.BlockSpec(memory_space=pltpu.SMEM)

NN = (((1,), (0,)), ((), ()))
NT = (((1,), (1,)), ((), ()))
TN = (((0,), (0,)), ((), ()))


def _dot(a, b, dims=NN):
    return lax.dot_general(a, b, dims, preferred_element_type=f32)


def _params(sem):
    return pltpu.CompilerParams(dimension_semantics=sem, vmem_limit_bytes=VMEM_LIMIT)


def _attn_tile(S):
    return 512 if S % 512 == 0 and S >= 1024 else 128


def _pick(n, pref):
    for t in pref:
        if n % t == 0:
            return t
    return n


def _matmul(a, b, mode, out_dtype, name):
    if mode == "nn":
        (M, K), (K2, N) = a.shape, b.shape
    elif mode == "nt":
        (M, K), (N, K2) = a.shape, b.shape
    else:
        (K, M), (K2, N) = a.shape, b.shape
    assert K == K2, (a.shape, b.shape, mode)
    tm = _pick(M, (1024, 512, 256, 128))
    tn = _pick(N, (1024, 768, 640, 512, 256, 128))
    tk = _pick(K, (2048, 1664, 1536, 1024, 512, 640, 256, 128))
    nk = K // tk
    dims = {"nn": NN, "nt": NT, "tn": TN}[mode]

    def body(a_ref, b_ref, o_ref, acc_ref):
        if nk == 1:
            o_ref[...] = _dot(a_ref[...], b_ref[...], dims).astype(o_ref.dtype)
            return
        k = pl.program_id(2)

        @pl.when(k == 0)
        def _():
            acc_ref[...] = jnp.zeros_like(acc_ref)

        acc_ref[...] += _dot(a_ref[...], b_ref[...], dims)

        @pl.when(k == nk - 1)
        def _():
            o_ref[...] = acc_ref[...].astype(o_ref.dtype)

    if mode == "tn":
        a_spec = pl.BlockSpec((tk, tm), lambda i, j, k: (k, i))
    else:
        a_spec = pl.BlockSpec((tm, tk), lambda i, j, k: (i, k))
    if mode == "nt":
        b_spec = pl.BlockSpec((tn, tk), lambda i, j, k: (j, k))
    else:
        b_spec = pl.BlockSpec((tk, tn), lambda i, j, k: (k, j))
    return pl.pallas_call(
        body, name=name,
        grid=(M // tm, N // tn, nk),
        in_specs=[a_spec, b_spec],
        out_specs=pl.BlockSpec((tm, tn), lambda i, j, k: (i, j)),
        out_shape=jax.ShapeDtypeStruct((M, N), out_dtype),
        scratch_shapes=[pltpu.VMEM((tm, tn) if nk > 1 else (8, 128), f32)],
        compiler_params=_params(("parallel", "parallel", "arbitrary")),
    )(a, b)


ROWS = 256


def _rows(S):
    return ROWS if S % ROWS == 0 else S


def _rmsnorm_fwd(x, g, name):
    S, D = x.shape
    tr = _rows(S)

    def body(x_ref, g_ref, h_ref):
        xv = x_ref[...]
        r = lax.rsqrt(jnp.mean(xv * xv, axis=-1, keepdims=True) + NORM_EPS)
        h_ref[...] = (xv * r * g_ref[...]).astype(h_ref.dtype)

    return pl.pallas_call(
        body, name=name, grid=(S // tr,),
        in_specs=[pl.BlockSpec((tr, D), lambda i: (i, 0)), pl.BlockSpec((1, D), lambda i: (0, 0))],
        out_specs=pl.BlockSpec((tr, D), lambda i: (i, 0)),
        out_shape=jax.ShapeDtypeStruct((S, D), bf16),
        compiler_params=_params(("parallel",)),
    )(x, g)


def _post_fwd(x, y, g, name):
    S, D = x.shape
    tr = _rows(S)

    def body(x_ref, y_ref, g_ref, o_ref):
        yv = y_ref[...]
        r = lax.rsqrt(jnp.mean(yv * yv, axis=-1, keepdims=True) + NORM_EPS)
        o_ref[...] = x_ref[...] + yv * r * g_ref[...]

    row = pl.BlockSpec((tr, D), lambda i: (i, 0))
    return pl.pallas_call(
        body, name=name, grid=(S // tr,),
        in_specs=[row, row, pl.BlockSpec((1, D), lambda i: (0, 0))],
        out_specs=row,
        out_shape=jax.ShapeDtypeStruct((S, D), f32),
        compiler_params=_params(("parallel",)),
    )(x, y, g)


def _loss_fwd_bwd(y, t, name):
    S, D = y.shape
    tr = _rows(S)

    def body(y_ref, t_ref, l_ref, d_ref):
        @pl.when(pl.program_id(0) == 0)
        def _():
            l_ref[...] = jnp.zeros_like(l_ref)

        e = y_ref[...] - t_ref[...]
        d_ref[...] = e * (1.0 / D)
        part = 0.5 * jnp.sum(jnp.sum(e * e, axis=-1, keepdims=True) * (1.0 / D), axis=0, keepdims=True)
        l_ref[...] += jnp.broadcast_to(part, l_ref.shape)

    row = pl.BlockSpec((tr, D), lambda i: (i, 0))
    return pl.pallas_call(
        body, name=name, grid=(S // tr,),
        in_specs=[row, row],
        out_specs=[pl.BlockSpec((8, 128), lambda i: (0, 0)), row],
        out_shape=[jax.ShapeDtypeStruct((8, 128), f32), jax.ShapeDtypeStruct((S, D), f32)],
        compiler_params=_params(("arbitrary",)),
    )(y, t)


def _post_bwd(dxn, y, g, name):
    S, D = y.shape
    tr = _rows(S)

    def body(d_ref, y_ref, g_ref, dy_ref, dg_ref):
        @pl.when(pl.program_id(0) == 0)
        def _():
            dg_ref[...] = jnp.zeros_like(dg_ref)

        yv = y_ref[...]
        d = d_ref[...]
        r = lax.rsqrt(jnp.mean(yv * yv, axis=-1, keepdims=True) + NORM_EPS)
        n = yv * r
        dn = d * g_ref[...]
        dg_ref[...] += jnp.sum(d * n, axis=0, keepdims=True)
        dy_ref[...] = (r * (dn - n * jnp.mean(dn * n, axis=-1, keepdims=True))).astype(dy_ref.dtype)

    row = pl.BlockSpec((tr, D), lambda i: (i, 0))
    vec = pl.BlockSpec((1, D), lambda i: (0, 0))
    return pl.pallas_call(
        body, name=name, grid=(S // tr,),
        in_specs=[row, row, vec],
        out_specs=[row, vec],
        out_shape=[jax.ShapeDtypeStruct((S, D), bf16), jax.ShapeDtypeStruct((1, D), f32)],
        compiler_params=_params(("arbitrary",)),
    )(dxn, y, g)


def _pre_bwd(dh, x, g, dres, name):
    S, D = x.shape
    tr = _rows(S)

    def body(dh_ref, x_ref, g_ref, dres_ref, dx_ref, dg_ref):
        @pl.when(pl.program_id(0) == 0)
        def _():
            dg_ref[...] = jnp.zeros_like(dg_ref)

        xv = x_ref[...]
        d = dh_ref[...]
        r = lax.rsqrt(jnp.mean(xv * xv, axis=-1, keepdims=True) + NORM_EPS)
        n = xv * r
        dn = d * g_ref[...]
        dg_ref[...] += jnp.sum(d * n, axis=0, keepdims=True)
        dx_ref[...] = dres_ref[...] + r * (dn - n * jnp.mean(dn * n, axis=-1, keepdims=True))

    row = pl.BlockSpec((tr, D), lambda i: (i, 0))
    vec = pl.BlockSpec((1, D), lambda i: (0, 0))
    return pl.pallas_call(
        body, name=name, grid=(S // tr,),
        in_specs=[row, row, vec, row],
        out_specs=[row, vec],
        out_shape=[jax.ShapeDtypeStruct((S, D), f32), jax.ShapeDtypeStruct((1, D), f32)],
        compiler_params=_params(("arbitrary",)),
    )(dh, x, g, dres)


def _sigmoid(x):
    return 1.0 / (1.0 + jnp.exp(-x))


def _gate_fwd(o, z):
    zf = z.astype(f32)
    return o * (zf * _sigmoid(zf))


def _gate_bwd(du, o, z):
    zf = z.astype(f32)
    sig = _sigmoid(zf)
    do = du * (zf * sig)
    dz = du * o * (sig * (1.0 + zf * (1.0 - sig)))
    return do, dz


def _iota2(shape, dim):
    return lax.broadcasted_iota(jnp.int32, shape, dim)


def _swa_scores(qh, kc, kp, slope, sink, n, scale):
    qi = _iota2((BLOCK, BLOCK), 0)
    kj = _iota2((BLOCK, BLOCK), 1)
    mask_c = kj <= qi
    mask_p = kj > qi + jnp.where(n > 0, 0, BLOCK)
    dist_c = (qi - kj).astype(f32)
    dist_p = (qi + BLOCK - kj).astype(f32)
    sc = jnp.where(mask_c, _dot(qh, kc, NT) * scale - slope * dist_c, NEG)
    sp = jnp.where(mask_p, _dot(qh, kp, NT) * scale - slope * dist_p, NEG)
    m = jnp.maximum(jnp.maximum(jnp.max(sc, axis=-1, keepdims=True), jnp.max(sp, axis=-1, keepdims=True)), sink)
    pc = jnp.exp(sc - m)
    pp = jnp.exp(sp - m)
    ps = jnp.exp(sink - m)
    den = jnp.sum(pc, axis=-1, keepdims=True) + jnp.sum(pp, axis=-1, keepdims=True) + ps
    return pc, pp, ps, den


def _swa_specs(S, BR, KV, G):
    gw = G * HEAD_DIM
    qspec = pl.BlockSpec((BLOCK, gw), lambda h, n: (n, h))
    zoff = (BR + 2 * KV) // gw
    zspec = pl.BlockSpec((BLOCK, gw), lambda h, n: (n, zoff + h))
    cur = pl.BlockSpec((1, BLOCK, HEAD_DIM), lambda h, n: (h, n, 0))
    prev = pl.BlockSpec((1, BLOCK, HEAD_DIM), lambda h, n: (h, jnp.maximum(n - 1, 0), 0))
    return qspec, zspec, cur, prev


def _swa_fwd(P, kh, vh, sinks, slopes, name, comm=None):
    S = P.shape[0]
    n_kv = kh.shape[0]
    H = sinks.shape[0]
    G = H // n_kv
    BR, KV = H * HEAD_DIM, n_kv * HEAD_DIM
    scale = HEAD_DIM ** -0.5
    assert (BR + 2 * KV) % (G * HEAD_DIM) == 0

    def body(q_ref, z_ref, kc_ref, kp_ref, vc_ref, vp_ref, sink_ref, slope_ref, o_ref, u_ref):
        kvh, n = pl.program_id(0), pl.program_id(1)
        kc, kp, vc, vp = kc_ref[0], kp_ref[0], vc_ref[0], vp_ref[0]
        for g in range(G):
            sl = slice(g * HEAD_DIM, (g + 1) * HEAD_DIM)
            h = kvh * G + g
            pc, pp, _, den = _swa_scores(q_ref[:, sl], kc, kp, slope_ref[h], sink_ref[h], n, scale)
            oh = (_dot(pc.astype(bf16), vc) + _dot(pp.astype(bf16), vp)) / den
            o_ref[:, sl] = oh.astype(o_ref.dtype)
            u_ref[:, sl] = _gate_fwd(oh, z_ref[:, sl]).astype(u_ref.dtype)

    qspec, zspec, cur, prev = _swa_specs(S, BR, KV, G)
    ospec = pl.BlockSpec((BLOCK, G * HEAD_DIM), lambda h, n: (n, h))
    return _pallas(
        body, (P, P, kh, kh, vh, vh, sinks, slopes), name=name, grid=(n_kv, S // BLOCK),
        in_specs=[qspec, zspec, cur, prev, cur, prev, SMEM, SMEM],
        out_specs=[ospec, ospec],
        out_shape=[jax.ShapeDtypeStruct((S, BR), bf16)] * 2,
        semantics=("parallel", "parallel"), comm=comm)


def _swa_bwd(P, kh, vh, o, du, sinks, slopes, name, comm=None):
    S = P.shape[0]
    n_kv = kh.shape[0]
    H = sinks.shape[0]
    G = H // n_kv
    BR, KV = H * HEAD_DIM, n_kv * HEAD_DIM
    scale = HEAD_DIM ** -0.5

    def body(q_ref, z_ref, kc_ref, kp_ref, vc_ref, vp_ref, o_ref, du_ref, sink_ref, slope_ref,
             dq_ref, dz_ref, dk_ref, dv_ref, ds_ref):
        kvh, n = pl.program_id(0), pl.program_id(1)

        @pl.when(n == 0)
        def _():
            dk_ref[...] = jnp.zeros_like(dk_ref)
            dv_ref[...] = jnp.zeros_like(dv_ref)
            ds_ref[...] = jnp.zeros_like(ds_ref)

        kc, kp, vc, vp = kc_ref[0], kp_ref[0], vc_ref[0], vp_ref[0]
        dkc = jnp.zeros((BLOCK, HEAD_DIM), f32)
        dkp = jnp.zeros((BLOCK, HEAD_DIM), f32)
        dvc = jnp.zeros((BLOCK, HEAD_DIM), f32)
        dvp = jnp.zeros((BLOCK, HEAD_DIM), f32)
        dsink = jnp.zeros((8, 128), f32)
        lane = _iota2((8, 128), 1)
        for g in range(G):
            sl = slice(g * HEAD_DIM, (g + 1) * HEAD_DIM)
            h = kvh * G + g
            qh = q_ref[:, sl]
            pc, pp, ps, den = _swa_scores(qh, kc, kp, slope_ref[h], sink_ref[h], n, scale)
            inv = 1.0 / den
            pc, pp, ps = pc * inv, pp * inv, ps * inv
            oh = o_ref[:, sl].astype(f32)
            doh, dzh = _gate_bwd(du_ref[:, sl].astype(f32), oh, z_ref[:, sl])
            dz_ref[:, sl] = dzh.astype(dz_ref.dtype)
            delta = jnp.sum(doh * oh, axis=-1, keepdims=True)
            dob = doh.astype(bf16)
            dsc = (pc * (_dot(dob, vc, NT) - delta)).astype(bf16)
            dsp = (pp * (_dot(dob, vp, NT) - delta)).astype(bf16)
            dq_ref[:, sl] = ((_dot(dsc, kc) + _dot(dsp, kp)) * scale).astype(dq_ref.dtype)
            dkc += _dot(dsc, qh, TN)
            dkp += _dot(dsp, qh, TN)
            dvc += _dot(pc.astype(bf16), dob, TN)
            dvp += _dot(pp.astype(bf16), dob, TN)
            dsh = -jnp.sum(ps * delta, axis=0, keepdims=True)
            dsink += jnp.where(lane == g, jnp.broadcast_to(dsh, (8, 128)), 0.0)
        ds_ref[0] += dsink
        row_c = pl.multiple_of(n * BLOCK, BLOCK)
        dk_ref[0, pl.ds(row_c, BLOCK), :] += dkc * scale
        dv_ref[0, pl.ds(row_c, BLOCK), :] += dvc

        @pl.when(n > 0)
        def _():
            row_p = pl.multiple_of((n - 1) * BLOCK, BLOCK)
            dk_ref[0, pl.ds(row_p, BLOCK), :] += dkp * scale
            dv_ref[0, pl.ds(row_p, BLOCK), :] += dvp

    qspec, zspec, cur, prev = _swa_specs(S, BR, KV, G)
    ospec = pl.BlockSpec((BLOCK, G * HEAD_DIM), lambda h, n: (n, h))
    full = pl.BlockSpec((1, S, HEAD_DIM), lambda h, n: (h, 0, 0))
    return _pallas(
        body, (P, P, kh, kh, vh, vh, o, du, sinks, slopes), name=name, grid=(n_kv, S // BLOCK),
        in_specs=[qspec, zspec, cur, prev, cur, prev, ospec, ospec, SMEM, SMEM],
        out_specs=[ospec, ospec, full, full, pl.BlockSpec((1, 8, 128), lambda h, n: (h, 0, 0))],
        out_shape=[jax.ShapeDtypeStruct((S, BR), bf16)] * 2
        + [jax.ShapeDtypeStruct((n_kv, S, HEAD_DIM), f32)] * 2
        + [jax.ShapeDtypeStruct((n_kv, 8, 128), f32)],
        semantics=("parallel", "arbitrary"), comm=comm)


def _split3(x):
    x1 = x.astype(bf16)
    r1 = x - x1.astype(f32)
    x2 = r1.astype(bf16)
    x3 = (r1 - x2.astype(f32)).astype(bf16)
    return x1, x2, x3


def _split2(x):
    x1 = x.astype(bf16)
    return x1, (x - x1.astype(f32)).astype(bf16)


def _pair_specs(S, BR, T, kmap):
    nb = BR // 128
    q = pl.BlockSpec((T, 128), lambda p, i, j: (i, p))
    k = pl.BlockSpec((T, 128), lambda p, i, j: (kmap(i, j), nb + p))
    v = pl.BlockSpec((T, 128), lambda p, i, j: (kmap(i, j), 2 * nb + p))
    z = pl.BlockSpec((T, 128), lambda p, i, j: (i, 3 * nb + p))
    return q, k, v, z


def _fox_fwd(P, cq, ck, BR, name, comm=None):
    S = P.shape[0]
    H = BR // HEAD_DIM
    T = _attn_tile(S)
    nq = S // T
    scale = HEAD_DIM ** -0.5
    kmap = lambda i, j: jnp.minimum(i, j)

    def body(q_ref, k_ref, v_ref, z_ref, cq_ref, ck_ref, o_ref, u_ref, lse_ref, m_s, l_s, acc_s):
        qb, kb = pl.program_id(1), pl.program_id(2)

        @pl.when(kb == 0)
        def _():
            m_s[...] = jnp.full_like(m_s, NEG)
            l_s[...] = jnp.zeros_like(l_s)
            acc_s[...] = jnp.zeros_like(acc_s)

        @pl.when(kb <= qb)
        def _():
            tpos = qb * T + _iota2((T, T), 0)
            spos = kb * T + _iota2((T, T), 1)
            causal = spos <= tpos
            for hh in range(2):
                sl = slice(hh * HEAD_DIM, (hh + 1) * HEAD_DIM)
                s = _dot(q_ref[:, sl], k_ref[:, sl], NT) * scale + cq_ref[hh] - ck_ref[hh]
                s = jnp.where(causal, s, NEG)
                m_old = m_s[hh]
                m_new = jnp.maximum(m_old, jnp.max(s, axis=-1, keepdims=True))
                alpha = jnp.exp(m_old - m_new)
                p = jnp.exp(s - m_new)
                l_s[hh] = alpha * l_s[hh] + jnp.sum(p, axis=-1, keepdims=True)
                acc_s[hh] = alpha * acc_s[hh] + _dot(p.astype(bf16), v_ref[:, sl])
                m_s[hh] = m_new

        @pl.when(kb == qb)
        def _():
            for hh in range(2):
                sl = slice(hh * HEAD_DIM, (hh + 1) * HEAD_DIM)
                oh = acc_s[hh] / l_s[hh]
                o_ref[:, sl] = oh.astype(o_ref.dtype)
                u_ref[:, sl] = _gate_fwd(oh, z_ref[:, sl]).astype(u_ref.dtype)
                lse_ref[hh] = m_s[hh] + jnp.log(l_s[hh])

    q, k, v, z = _pair_specs(S, BR, T, kmap)
    col = pl.BlockSpec((2, T, 1), lambda p, i, j: (p, i, 0))
    rowk = pl.BlockSpec((2, 1, T), lambda p, i, j: (p, 0, kmap(i, j)))
    ospec = pl.BlockSpec((T, 128), lambda p, i, j: (i, p))
    return _pallas(
        body, (P, P, P, P, cq, ck), name=name, grid=(H // 2, nq, nq),
        in_specs=[q, k, v, z, col, rowk],
        out_specs=[ospec, ospec, col],
        out_shape=[jax.ShapeDtypeStruct((S, BR), bf16)] * 2 + [jax.ShapeDtypeStruct((H, S, 1), f32)],
        scratch_shapes=[pltpu.VMEM((2, T, 1), f32), pltpu.VMEM((2, T, 1), f32), pltpu.VMEM((2, T, HEAD_DIM), f32)],
        semantics=("parallel", "arbitrary", "arbitrary"), comm=comm)


def _fox_bwd(P, o, du, lse, cq, ck, BR, name, comm=None):
    S = P.shape[0]
    H = BR // HEAD_DIM
    T = _attn_tile(S)
    nq = S // T
    scale = HEAD_DIM ** -0.5
    kmap = lambda i, j: jnp.minimum(i, j)

    def body(q_ref, k_ref, v_ref, z_ref, o_ref, du_ref, lse_ref, cq_ref, ck_ref,
             dq_ref, dz_ref, dk_ref, dv_ref, dcq_ref, dck_ref, do_s, delta_s, dq_s, dcq_s):
        qb, kb = pl.program_id(1), pl.program_id(2)

        @pl.when(jnp.logical_and(qb == 0, kb == 0))
        def _():
            dk_ref[...] = jnp.zeros_like(dk_ref)
            dv_ref[...] = jnp.zeros_like(dv_ref)
            dck_ref[...] = jnp.zeros_like(dck_ref)

        @pl.when(kb == 0)
        def _():
            for hh in range(2):
                sl = slice(hh * HEAD_DIM, (hh + 1) * HEAD_DIM)
                oh = o_ref[:, sl].astype(f32)
                doh, dzh = _gate_bwd(du_ref[:, sl].astype(f32), oh, z_ref[:, sl])
                dz_ref[:, sl] = dzh.astype(dz_ref.dtype)
                do_s[hh] = doh.astype(bf16)
                delta_s[hh] = jnp.sum(doh * oh, axis=-1, keepdims=True)
            dq_s[...] = jnp.zeros_like(dq_s)
            dcq_s[...] = jnp.zeros_like(dcq_s)

        @pl.when(kb <= qb)
        def _():
            tpos = qb * T + _iota2((T, T), 0)
            spos = kb * T + _iota2((T, T), 1)
            causal = spos <= tpos
            rows = pl.ds(pl.multiple_of(kb * T, T), T)
            for hh in range(2):
                sl = slice(hh * HEAD_DIM, (hh + 1) * HEAD_DIM)
                qh, kh, vh, dob = q_ref[:, sl], k_ref[:, sl], v_ref[:, sl], do_s[hh]
                s = _dot(qh, kh, NT) * scale + cq_ref[hh] - ck_ref[hh]
                p = jnp.where(causal, jnp.exp(s - lse_ref[hh]), 0.0)
                ds = p * (_dot(dob, vh, NT) - delta_s[hh])
                dsb = ds.astype(bf16)
                dq_s[hh] += _dot(dsb, kh)
                dk_ref[rows, sl] += _dot(dsb, qh, TN) * scale
                dv_ref[rows, sl] += _dot(p.astype(bf16), dob, TN)
                dcq_s[hh] += jnp.sum(ds, axis=-1, keepdims=True)
                dck_ref[hh, kb] += jnp.sum(ds, axis=0, keepdims=True)

        @pl.when(kb == qb)
        def _():
            for hh in range(2):
                sl = slice(hh * HEAD_DIM, (hh + 1) * HEAD_DIM)
                dq_ref[:, sl] = (dq_s[hh] * scale).astype(dq_ref.dtype)
                dcq_ref[hh] = dcq_s[hh]

    q, k, v, z = _pair_specs(S, BR, T, kmap)
    col = pl.BlockSpec((2, T, 1), lambda p, i, j: (p, i, 0))
    rowk = pl.BlockSpec((2, 1, T), lambda p, i, j: (p, 0, kmap(i, j)))
    ospec = pl.BlockSpec((T, 128), lambda p, i, j: (i, p))
    full = pl.BlockSpec((S, 128), lambda p, i, j: (0, p))
    return _pallas(
        body, (P, P, P, P, o, du, lse, cq, ck), name=name, grid=(H // 2, nq, nq),
        in_specs=[q, k, v, z, ospec, ospec, col, col, rowk],
        out_specs=[ospec, ospec, full, full, col, pl.BlockSpec((2, nq, 1, T), lambda p, i, j: (p, 0, 0, 0))],
        out_shape=[jax.ShapeDtypeStruct((S, BR), bf16)] * 2 + [jax.ShapeDtypeStruct((S, BR), f32)] * 2
        + [jax.ShapeDtypeStruct((H, S, 1), f32), jax.ShapeDtypeStruct((H, nq, 1, T), f32)],
        scratch_shapes=[pltpu.VMEM((2, T, HEAD_DIM), bf16), pltpu.VMEM((2, T, 1), f32),
                        pltpu.VMEM((2, T, HEAD_DIM), f32), pltpu.VMEM((2, T, 1), f32)],
        semantics=("parallel", "arbitrary", "arbitrary"), comm=comm)


def _fox_cum(flT, b, name):
    H, S = flT.shape
    tj = _pick(S, (256, 128))

    def body(fl_ref, b_ref, c_ref):
        j = pl.program_id(0)
        x = fl_ref[...] + b_ref[...]
        logf = jnp.minimum(x, 0.0) - jnp.log(1.0 + jnp.exp(-jnp.abs(x)))
        tri = (_iota2((S, tj), 0) <= j * tj + _iota2((S, tj), 1)).astype(bf16)
        c_ref[...] = sum(_dot(part, tri) for part in _split3(logf))

    return pl.pallas_call(
        body, name=name, grid=(S // tj,),
        in_specs=[pl.BlockSpec((H, S), lambda j: (0, 0)), pl.BlockSpec((H, 1), lambda j: (0, 0))],
        out_specs=pl.BlockSpec((H, tj), lambda j: (0, j)),
        out_shape=jax.ShapeDtypeStruct((H, S), f32),
        compiler_params=_params(("parallel",)),
    )(flT, b)


def _fox_cum_bwd(dcq, dck, flT, b, name):
    H, S = flT.shape
    tj = _pick(S, (256, 128))

    def body(dcq_ref, dck_ref, fl_ref, b_ref, o_ref, db_ref):
        j = pl.program_id(0)

        @pl.when(j == 0)
        def _():
            db_ref[...] = jnp.zeros_like(db_ref)

        tri = (_iota2((S, tj), 0) >= j * tj + _iota2((S, tj), 1)).astype(bf16)
        dlogf = sum(_dot(part, tri) for part in _split3(dcq_ref[...] - dck_ref[...]))
        x = fl_ref[...] + b_ref[...]
        dfl = dlogf * _sigmoid(-x)
        o_ref[...] = dfl
        db_ref[...] += jnp.sum(dfl, axis=-1, keepdims=True)

    blk = pl.BlockSpec((H, tj), lambda j: (0, j))
    whole = pl.BlockSpec((H, S), lambda j: (0, 0))
    col = pl.BlockSpec((H, 1), lambda j: (0, 0))
    return pl.pallas_call(
        body, name=name, grid=(S // tj,),
        in_specs=[whole, whole, blk, col],
        out_specs=[blk, col],
        out_shape=[jax.ShapeDtypeStruct((H, S), f32), jax.ShapeDtypeStruct((H, 1), f32)],
        compiler_params=_params(("arbitrary",)),
    )(dcq, dck, flT, b)


def _sb_logs(qh, kc, before, scale):
    l = _dot(qh, kc, NT) * scale
    lsig = jnp.minimum(l, 0.0) - jnp.log(1.0 + jnp.exp(-jnp.abs(l)))
    lf = jnp.where(before, lsig - l, 0.0)
    return lsig, lf


def _sb_suffix(lf, tri):
    hi, lo = _split2(lf)
    return _dot(hi, tri) + _dot(lo, tri)


def _sb_fwd(P, BR, name, comm=None):
    S = P.shape[0]
    H = BR // HEAD_DIM
    T = _attn_tile(S)
    nq = S // T
    nc = T // BLOCK
    scale = HEAD_DIM ** -0.5
    kmap = lambda i, j: jnp.maximum(i - j, 0)

    def body(q_ref, k_ref, v_ref, z_ref, o_ref, u_ref, rc_ref, r_s, acc_s):
        qb, j = pl.program_id(1), pl.program_id(2)
        kb = qb - j

        @pl.when(j == 0)
        def _():
            r_s[...] = jnp.zeros_like(r_s)
            acc_s[...] = jnp.zeros_like(acc_s)

        @pl.when(j <= qb)
        def _():
            tpos = qb * T + _iota2((T, BLOCK), 0)
            col = _iota2((T, BLOCK), 1)
            tri = (_iota2((BLOCK, BLOCK), 0) > _iota2((BLOCK, BLOCK), 1)).astype(bf16)
            for hh in range(2):
                sl = slice(hh * HEAD_DIM, (hh + 1) * HEAD_DIM)
                qh = q_ref[:, sl]
                r = r_s[hh]
                rc_ref[hh, 0] = r
                acc = acc_s[hh]
                for c in reversed(range(nc)):
                    rows = slice(c * BLOCK, (c + 1) * BLOCK)
                    before = kb * T + c * BLOCK + col < tpos
                    lsig, lf = _sb_logs(qh, k_ref[rows, sl], before, scale)
                    a = jnp.where(before, jnp.exp(lsig + _sb_suffix(lf, tri) + r), 0.0)
                    acc = acc + _dot(a.astype(bf16), v_ref[rows, sl])
                    r = r + jnp.sum(lf, axis=-1, keepdims=True)
                r_s[hh] = r
                acc_s[hh] = acc

        @pl.when(j == qb)
        def _():
            for hh in range(2):
                sl = slice(hh * HEAD_DIM, (hh + 1) * HEAD_DIM)
                oh = acc_s[hh]
                o_ref[:, sl] = oh.astype(o_ref.dtype)
                u_ref[:, sl] = _gate_fwd(oh, z_ref[:, sl]).astype(u_ref.dtype)

    q, k, v, z = _pair_specs(S, BR, T, kmap)
    ospec = pl.BlockSpec((T, 128), lambda p, i, j: (i, p))
    rc = pl.BlockSpec((2, 1, T, 1), lambda p, i, j: (p, kmap(i, j), i, 0))
    return _pallas(
        body, (P, P, P, P), name=name, grid=(H // 2, nq, nq),
        in_specs=[q, k, v, z],
        out_specs=[ospec, ospec, rc],
        out_shape=[jax.ShapeDtypeStruct((S, BR), bf16)] * 2 + [jax.ShapeDtypeStruct((H, nq, S, 1), f32)],
        scratch_shapes=[pltpu.VMEM((2, T, 1), f32), pltpu.VMEM((2, T, HEAD_DIM), f32)],
        semantics=("parallel", "arbitrary", "arbitrary"), comm=comm)


def _sb_bwd(P, o, du, rc, BR, name, comm=None):
    S = P.shape[0]
    H = BR // HEAD_DIM
    T = _attn_tile(S)
    nq = S // T
    nc = T // BLOCK
    scale = HEAD_DIM ** -0.5
    kmap = lambda i, j: jnp.minimum(i, j)

    def body(q_ref, k_ref, v_ref, z_ref, o_ref, du_ref, rc_ref,
             dq_ref, dz_ref, dk_ref, dv_ref, do_s, dq_s, g_s):
        qb, kb = pl.program_id(1), pl.program_id(2)

        @pl.when(jnp.logical_and(qb == 0, kb == 0))
        def _():
            dk_ref[...] = jnp.zeros_like(dk_ref)
            dv_ref[...] = jnp.zeros_like(dv_ref)

        @pl.when(kb == 0)
        def _():
            for hh in range(2):
                sl = slice(hh * HEAD_DIM, (hh + 1) * HEAD_DIM)
                doh, dzh = _gate_bwd(du_ref[:, sl].astype(f32), o_ref[:, sl].astype(f32), z_ref[:, sl])
                dz_ref[:, sl] = dzh.astype(dz_ref.dtype)
                do_s[hh] = doh.astype(bf16)
            dq_s[...] = jnp.zeros_like(dq_s)
            g_s[...] = jnp.zeros_like(g_s)

        @pl.when(kb <= qb)
        def _():
            tpos = qb * T + _iota2((T, BLOCK), 0)
            col = _iota2((T, BLOCK), 1)
            ii, jj = _iota2((BLOCK, BLOCK), 0), _iota2((BLOCK, BLOCK), 1)
            tri_suffix = (ii > jj).astype(bf16)
            tri_prefix = (ii < jj).astype(bf16)
            for hh in range(2):
                sl = slice(hh * HEAD_DIM, (hh + 1) * HEAD_DIM)
                qh, dob = q_ref[:, sl], do_s[hh]
                befores, lsigs, lfs = [], [], []
                for c in range(nc):
                    before = kb * T + c * BLOCK + col < tpos
                    lsig, lf = _sb_logs(qh, k_ref[c * BLOCK:(c + 1) * BLOCK, sl], before, scale)
                    befores.append(before)
                    lsigs.append(lsig)
                    lfs.append(lf)
                carries = [None] * nc
                r = rc_ref[hh, 0]
                for c in reversed(range(nc)):
                    carries[c] = r
                    r = r + jnp.sum(lfs[c], axis=-1, keepdims=True)
                gsum = g_s[hh]
                dq = dq_s[hh]
                for c in range(nc):
                    rows = slice(c * BLOCK, (c + 1) * BLOCK)
                    out_rows = pl.ds(pl.multiple_of(kb * T + c * BLOCK, BLOCK), BLOCK)
                    kc, vc = k_ref[rows, sl], v_ref[rows, sl]
                    a = jnp.where(befores[c], jnp.exp(lsigs[c] + _sb_suffix(lfs[c], tri_suffix) + carries[c]), 0.0)
                    g = a * _dot(dob, vc, NT)
                    gex = _dot(g.astype(bf16), tri_prefix) + gsum
                    sig = jnp.exp(lsigs[c])
                    dl = jnp.where(befores[c], g * (1.0 - sig) - gex * sig, 0.0).astype(bf16)
                    dq = dq + _dot(dl, kc)
                    dk_ref[out_rows, sl] += _dot(dl, qh, TN) * scale
                    dv_ref[out_rows, sl] += _dot(a.astype(bf16), dob, TN)
                    gsum = gsum + jnp.sum(g, axis=-1, keepdims=True)
                g_s[hh] = gsum
                dq_s[hh] = dq

        @pl.when(kb == qb)
        def _():
            for hh in range(2):
                sl = slice(hh * HEAD_DIM, (hh + 1) * HEAD_DIM)
                dq_ref[:, sl] = (dq_s[hh] * scale).astype(dq_ref.dtype)

    q, k, v, z = _pair_specs(S, BR, T, kmap)
    ospec = pl.BlockSpec((T, 128), lambda p, i, j: (i, p))
    full = pl.BlockSpec((S, 128), lambda p, i, j: (0, p))
    rcs = pl.BlockSpec((2, 1, T, 1), lambda p, i, j: (p, kmap(i, j), i, 0))
    return _pallas(
        body, (P, P, P, P, o, du, rc), name=name, grid=(H // 2, nq, nq),
        in_specs=[q, k, v, z, ospec, ospec, rcs],
        out_specs=[ospec, ospec, full, full],
        out_shape=[jax.ShapeDtypeStruct((S, BR), bf16)] * 2 + [jax.ShapeDtypeStruct((S, BR), f32)] * 2,
        scratch_shapes=[pltpu.VMEM((2, T, HEAD_DIM), bf16), pltpu.VMEM((2, T, HEAD_DIM), f32),
                        pltpu.VMEM((2, T, 1), f32)],
        semantics=("parallel", "arbitrary", "arbitrary"), comm=comm)


def _adam_math(w, g, m, v):
    m = ADAM_B1 * m + (1.0 - ADAM_B1) * g
    v = ADAM_B2 * v + (1.0 - ADAM_B2) * (g * g)
    m_hat = m / (1.0 - ADAM_B1 ** ADAM_STEP)
    v_hat = v / (1.0 - ADAM_B2 ** ADAM_STEP)
    delta = -ADAM_LR * (m_hat / (jnp.sqrt(v_hat) + ADAM_EPS) + ADAM_WD * w)
    return delta, m, v


def _adamw_sum(parts, w, m, v, name):
    R, C = w.shape
    tr = _pick(R, (256, 128))

    def body(p_ref, w_ref, m_ref, v_ref, g_ref, d_ref, nm_ref, nv_ref):
        g = p_ref[0].astype(f32)
        for i in range(1, N_DEV):
            g = g + p_ref[i].astype(f32)
        g_ref[...] = g
        d_ref[...], nm_ref[...], nv_ref[...] = _adam_math(w_ref[...], g, m_ref[...], v_ref[...])

    blk = pl.BlockSpec((tr, C), lambda i: (i, 0))
    return pl.pallas_call(
        body, name=name, grid=(R // tr,),
        in_specs=[pl.BlockSpec((N_DEV, tr, C), lambda i: (0, i, 0)), blk, blk, blk],
        out_specs=[blk] * 4,
        out_shape=[jax.ShapeDtypeStruct((R, C), f32)] * 4,
        compiler_params=_params(("parallel",)),
    )(parts, w, m, v)


def _coords():
    return lax.axis_index("x"), lax.axis_index("y"), lax.axis_index("c")


class _Gather:
    def __init__(self, shards):
        self.args = list(shards)
        self.n = len(shards)
        self.out_shape = [jax.ShapeDtypeStruct((N_DEV,) + s.shape, s.dtype) for s in shards]
        self.scratch = [pltpu.SemaphoreType.DMA((self.n, 7)), pltpu.SemaphoreType.DMA((self.n, 7)),
                        pltpu.SemaphoreType.DMA((self.n,))]

    def _ctx(self, ins, outs, sems):
        send_sems, recv_sems, local_sems = sems
        x, y, c = _coords()
        me, sibling = (x, y, c), (x, y, 1 - c)
        chips = [(1 - x, y), (x, 1 - y), (1 - x, 1 - y)]

        def slot(out, dev):
            return out.at[4 * dev[0] + 2 * dev[1] + dev[2]]

        def copy(a, k, block, to, src=None):
            return pltpu.make_async_remote_copy(
                src_ref=slot(outs[a], block) if src is None else src, dst_ref=slot(outs[a], block),
                send_sem=send_sems.at[a, k], recv_sem=recv_sems.at[a, k], device_id=to, device_id_type=MESH)

        mine = [pltpu.make_async_copy(ins[a], slot(outs[a], me), local_sems.at[a]) for a in range(self.n)]
        first = []
        for a in range(self.n):
            first.append(copy(a, 0, me, sibling, src=ins[a]))
            first += [copy(a, 1 + j, me, (*chip, c), src=ins[a]) for j, chip in enumerate(chips)]
        passed = [copy(a, 4 + j, (*chip, c), sibling) for j, chip in enumerate(chips) for a in range(self.n)]
        return c, me, sibling, chips, copy, mine, first, passed

    def start(self, ins, outs, sems):
        *_, mine, first, _ = self._ctx(ins, outs, sems)
        for cp in mine + first:
            cp.start()

    def mid(self, ins, outs, sems):
        c, me, _, chips, copy, _, _, passed = self._ctx(ins, outs, sems)
        i = 0
        for j, chip in enumerate(chips):
            for a in range(self.n):
                copy(a, 1 + j, (*chip, c), me).wait_recv()
                passed[i].start()
                i += 1

    def finish(self, ins, outs, sems):
        c, me, sibling, chips, copy, mine, first, passed = self._ctx(ins, outs, sems)
        for a in range(self.n):
            copy(a, 0, sibling, me).wait_recv()
            for j, chip in enumerate(chips):
                copy(a, 4 + j, (*chip, 1 - c), me).wait_recv()
        for cp in first + passed:
            cp.wait_send()
        for cp in mine:
            cp.wait()


class _Scatter:
    mid = None

    def __init__(self, blocks):
        self.args = list(blocks)
        self.n = len(blocks)
        self.out_shape = [jax.ShapeDtypeStruct(b.shape, b.dtype) for b in blocks]
        self.scratch = [pltpu.SemaphoreType.DMA((self.n, 7)), pltpu.SemaphoreType.DMA((self.n, 7)),
                        pltpu.SemaphoreType.DMA((self.n,))]

    def _ctx(self, ins, outs, sems):
        send_sems, recv_sems, local_sems = sems
        x, y, c = _coords()
        me = 4 * x + 2 * y + c
        mine = [pltpu.make_async_copy(ins[a].at[me], outs[a].at[me], local_sems.at[a]) for a in range(self.n)]

        def copy(a, k, landing_here):
            px, py, pc = x ^ ((k >> 2) & 1), y ^ ((k >> 1) & 1), c ^ (k & 1)
            them = 4 * px + 2 * py + pc
            return pltpu.make_async_remote_copy(
                src_ref=ins[a].at[them], dst_ref=outs[a].at[them if landing_here else me],
                send_sem=send_sems.at[a, k - 1], recv_sem=recv_sems.at[a, k - 1],
                device_id=(px, py, pc), device_id_type=MESH)

        sent = [copy(a, k, False) for k in range(1, N_DEV) for a in range(self.n)]
        arrivals = [copy(a, k, True) for k in range(1, N_DEV) for a in range(self.n)]
        return mine, sent, arrivals

    def start(self, ins, outs, sems):
        mine, sent, _ = self._ctx(ins, outs, sems)
        for cp in mine + sent:
            cp.start()

    def finish(self, ins, outs, sems):
        mine, sent, arrivals = self._ctx(ins, outs, sems)
        for cp in arrivals:
            cp.wait_recv()
        for cp in sent:
            cp.wait_send()
        for cp in mine:
            cp.wait()


def _run_comm(comm, name):
    n = comm.n

    def body(*refs):
        ins, outs, sems = refs[:n], refs[n:2 * n], refs[2 * n:]
        comm.start(ins, outs, sems)
        if comm.mid is not None:
            comm.mid(ins, outs, sems)
        comm.finish(ins, outs, sems)

    return pl.pallas_call(
        body, name=name, in_specs=[ANY] * n, out_specs=[ANY] * n,
        out_shape=comm.out_shape, scratch_shapes=comm.scratch,
    )(*comm.args)


def _pallas(body, args, *, name, grid, in_specs, out_specs, out_shape, scratch_shapes=(), semantics, comm=None):
    in_specs, out_specs, out_shape, scratch_shapes = list(in_specs), list(out_specs), list(out_shape), list(scratch_shapes)
    if comm is None:
        return pl.pallas_call(
            body, name=name, grid=grid, in_specs=in_specs, out_specs=out_specs, out_shape=out_shape,
            scratch_shapes=scratch_shapes, compiler_params=_params(semantics))(*args)
    a = len(in_specs)
    b = a + comm.n
    c = b + len(out_specs)
    d = c + comm.n
    e = d + len(scratch_shapes)
    total = math.prod(grid)
    mid_step = (3 * total) // 4

    def hosted(*refs):
        step = pl.program_id(0)
        for axis in range(1, len(grid)):
            step = step * grid[axis] + pl.program_id(axis)
        ins, outs, sems = refs[a:b], refs[c:d], refs[e:]

        @pl.when(step == 0)
        def _():
            comm.start(ins, outs, sems)

        body(*refs[:a], *refs[b:c], *refs[d:e])

        if comm.mid is not None:
            @pl.when(step == mid_step)
            def _():
                comm.mid(ins, outs, sems)

        @pl.when(step == total - 1)
        def _():
            comm.finish(ins, outs, sems)

    res = pl.pallas_call(
        hosted, name=name, grid=grid, in_specs=in_specs + [ANY] * comm.n, out_specs=out_specs + [ANY] * comm.n,
        out_shape=out_shape + comm.out_shape, scratch_shapes=scratch_shapes + comm.scratch,
        compiler_params=_params(("arbitrary",) * len(grid)))(*args, *comm.args)
    return list(res[:len(out_specs)]) + [list(res[len(out_specs):])]


def _all_reduce_small(vec, name):
    R, C = vec.shape

    def body(v_ref, o_ref, gath, send_sems, recv_sems):
        x, y, c = _coords()
        me = 4 * x + 2 * y + c
        gath[me] = v_ref[...]

        def copy(k):
            px, py, pc = x ^ ((k >> 2) & 1), y ^ ((k >> 1) & 1), c ^ (k & 1)
            return pltpu.make_async_remote_copy(
                src_ref=v_ref, dst_ref=gath.at[me], send_sem=send_sems.at[k - 1], recv_sem=recv_sems.at[k - 1],
                device_id=(px, py, pc), device_id_type=MESH)

        sent = [copy(k) for k in range(1, N_DEV)]
        for cp in sent:
            cp.start()
        for cp in sent:
            cp.wait()
        total = gath[0]
        for i in range(1, N_DEV):
            total = total + gath[i]
        o_ref[...] = total

    return pl.pallas_call(
        body, name=name,
        in_specs=[pl.BlockSpec(memory_space=pltpu.VMEM)], out_specs=pl.BlockSpec(memory_space=pltpu.VMEM),
        out_shape=jax.ShapeDtypeStruct((R, C), f32),
        scratch_shapes=[pltpu.VMEM((N_DEV, R, C), f32), pltpu.SemaphoreType.DMA((7,)), pltpu.SemaphoreType.DMA((7,))],
    )(vec)


def _gathered_in(g):
    return jnp.transpose(g, (1, 0, 2)).reshape(g.shape[1], -1)


def _col_blocks(dw):
    D, N = dw.shape
    return jnp.transpose(dw.reshape(D, N_DEV, N // N_DEV), (1, 0, 2)).astype(bf16)


def _alibi_slopes(n):
    return jnp.asarray(2.0 ** (-8.0 * np.arange(1, n + 1, dtype=np.float32) / n), dtype=f32)


def _heads_major(a, n):
    return jnp.transpose(a.reshape(a.shape[0], n, HEAD_DIM), (1, 0, 2))


def _heads_minor(a):
    return jnp.transpose(a, (1, 0, 2)).reshape(a.shape[1], -1)


def kernel(x, g_pre, g_post, w_in_a, w_out_a, sinks_a, w_in_b, w_out_b, w_in_c, b_f_c, w_out_c, loss_target, m_g_pre, m_g_post, m_w_in_a, m_w_out_a, m_sinks_a, m_w_in_b, m_w_out_b, m_w_in_c, m_b_f_c, m_w_out_c, v_g_pre, v_g_post, v_w_in_a, v_w_out_a, v_sinks_a, v_w_in_b, v_w_out_b, v_w_in_c, v_b_f_c, v_w_out_c):
    S, D = x.shape[1], x.shape[2]
    H = D // HEAD_DIM
    BR = H * HEAD_DIM
    n_kv = H // 8
    KV = n_kv * HEAD_DIM
    x0 = x[0]
    target = loss_target[0]
    slopes = _alibi_slopes(H)
    w_in = {0: w_in_a, 1: w_in_b, 2: w_in_c}
    w_out = {0: w_out_a, 1: w_out_b, 2: w_out_c}

    saved = []
    xi = x0
    def weight_gather(i):
        kind, j = i % 3, i // 3
        return _Gather([w_in[kind][j].astype(bf16), w_out[kind][j].astype(bf16)])

    gathered = _run_comm(weight_gather(0), name="gather_w0")
    for i in range(DEPTH):
        kind, j = i % 3, i // 3
        win_g, wout_g = gathered
        nxt = weight_gather(i + 1) if i + 1 < DEPTH else None
        W_in = _gathered_in(win_g)
        W_out = wout_g.reshape(BR, D)
        h = _rmsnorm_fwd(xi, g_pre[i:i + 1], name=f"pre_norm{i}")
        st = dict(x=xi, h=h, W_out=W_out)
        if kind == 2:
            n_main = 4 * BR
            pad = (-W_in.shape[1]) % 128
            W_in = jnp.pad(W_in, ((0, 0), (0, pad)))
            W_f = W_in[:, n_main:n_main + 128]
            fl = _matmul(h, W_f, "nn", f32, name=f"f_proj{i}")
            flT = jnp.transpose(fl[:, :H])
            bcol = b_f_c[j].reshape(H, 1)
            cumT = _fox_cum(flT, bcol, name=f"fox_cum{i}")
            cq, ck = cumT[:, :, None], cumT[:, None, :]
            st.update(flT=flT, bcol=bcol, cq=cq, ck=ck)
        st["W_in"] = W_in
        P = _matmul(h, W_in, "nn", bf16, name=f"in_proj{i}")
        st["P"] = P
        if kind == 0:
            kh = _heads_major(P[:, BR:BR + KV], n_kv)
            vh = _heads_major(P[:, BR + KV:BR + 2 * KV], n_kv)
            o, u, *rest = _swa_fwd(P, kh, vh, sinks_a[j], slopes, name=f"swa_fwd{i}", comm=nxt)
            st.update(kh=kh, vh=vh)
        elif kind == 1:
            o, u, rc, *rest = _sb_fwd(P, BR, name=f"sb_fwd{i}", comm=nxt)
            st.update(rc=rc)
        else:
            o, u, lse, *rest = _fox_fwd(P, cq, ck, BR, name=f"fox_fwd{i}", comm=nxt)
            st.update(lse=lse)
        if nxt is not None:
            gathered = rest[0]
        st.update(o=o, u=u)
        y = _matmul(u, W_out, "nn", f32, name=f"out_proj{i}")
        st["y"] = y
        xi = _post_fwd(xi, y, g_post[i:i + 1], name=f"post_norm{i}")
        saved.append(st)

    loss_part, dx = _loss_fwd_bwd(xi, target, name="loss")

    dg_pre, dg_post = [None] * DEPTH, [None] * DEPTH
    dsinks = [None, None]
    db_f = None
    recv = [None] * DEPTH
    pending = None
    for i in reversed(range(DEPTH)):
        kind, j = i % 3, i // 3
        st = saved[i]
        dy, dg_post[i] = _post_bwd(dx, st["y"], g_post[i:i + 1], name=f"post_bwd{i}")
        du = _matmul(dy, st["W_out"], "nt", bf16, name=f"du{i}")
        dW_out = _matmul(st["u"], dy, "tn", f32, name=f"dw_out{i}")
        P = st["P"]
        if kind == 0:
            dq, dz, dkh, dvh, dsk, *rest = _swa_bwd(P, st["kh"], st["vh"], st["o"], du, sinks_a[j], slopes,
                                                    name=f"swa_bwd{i}", comm=pending)
            dsinks[j] = dsk[:, 0, :H // n_kv].reshape(H)
            dP = jnp.concatenate([dq, _heads_minor(dkh).astype(bf16), _heads_minor(dvh).astype(bf16), dz], axis=1)
        elif kind == 1:
            dq, dz, dk, dv, *rest = _sb_bwd(P, st["o"], du, st["rc"], BR, name=f"sb_bwd{i}", comm=pending)
            dP = jnp.concatenate([dq, dk.astype(bf16), dv.astype(bf16), dz], axis=1)
        else:
            dq, dz, dk, dv, dcq, dck, *rest = _fox_bwd(P, st["o"], du, st["lse"], st["cq"], st["ck"], BR,
                                                       name=f"fox_bwd{i}", comm=pending)
        if pending is not None:
            recv[i + 1] = rest[0]
        if kind == 2:
            dflT, db_col = _fox_cum_bwd(dcq.reshape(H, S), dck.reshape(H, S), st["flT"], st["bcol"],
                                        name=f"fox_cum_bwd{i}")
            db_f = db_col.reshape(H)
            dfl = jnp.pad(jnp.transpose(dflT), ((0, 0), (0, 128 - H))).astype(bf16)
            dP = jnp.concatenate([dq, dk.astype(bf16), dv.astype(bf16), dz, dfl], axis=1)
        dh = _matmul(dP, st["W_in"], "nt", f32, name=f"dh{i}")
        dW_in = _matmul(st["h"], dP, "tn", f32, name=f"dw_in{i}")
        n_cols = w_in[kind].shape[2] * N_DEV
        pending = _Scatter([_col_blocks(dW_in[:, :n_cols]), dW_out.astype(bf16).reshape(N_DEV, BR // N_DEV, D)])
        dx, dg_pre[i] = _pre_bwd(dh, st["x"], g_pre[i:i + 1], dx, name=f"pre_bwd{i}")
    recv[0] = _run_comm(pending, name="scatter_dw0")

    small = jnp.concatenate(
        [jnp.concatenate(dg_pre, axis=0).reshape(-1), jnp.concatenate(dg_post, axis=0).reshape(-1),
         jnp.concatenate(dsinks), db_f, loss_part[0, :1]])
    n_small = small.shape[0]
    rows = -(-n_small // 128)
    rows = -(-rows // 8) * 8
    small = jnp.pad(small, (0, rows * 128 - n_small)).reshape(rows, 128)
    total = _all_reduce_small(small, name="reduce_small").reshape(-1)
    o0 = DEPTH * D
    grad_g_pre = total[:o0].reshape(DEPTH, D)
    grad_g_post = total[o0:2 * o0].reshape(DEPTH, D)
    grad_sinks = total[2 * o0:2 * o0 + 2 * H].reshape(2, H)
    grad_b_f = total[2 * o0 + 2 * H:2 * o0 + 3 * H].reshape(1, H)
    loss = total[2 * o0 + 3 * H]

    def small_adam(w, g, m, v, name):
        def body(w_ref, g_ref, m_ref, v_ref, d_ref, nm_ref, nv_ref):
            d_ref[...], nm_ref[...], nv_ref[...] = _adam_math(w_ref[...], g_ref[...], m_ref[...], v_ref[...])
        vm = pl.BlockSpec(memory_space=pltpu.VMEM)
        return pl.pallas_call(body, name=name, in_specs=[vm] * 4, out_specs=[vm] * 3,
                              out_shape=[jax.ShapeDtypeStruct(w.shape, f32)] * 3)(w, g, m, v)

    upd = {}
    upd["g_pre"] = (grad_g_pre,) + tuple(small_adam(g_pre, grad_g_pre, m_g_pre, v_g_pre, "adam_g_pre"))
    upd["g_post"] = (grad_g_post,) + tuple(small_adam(g_post, grad_g_post, m_g_post, v_g_post, "adam_g_post"))
    upd["sinks_a"] = (grad_sinks,) + tuple(small_adam(sinks_a, grad_sinks, m_sinks_a, v_sinks_a, "adam_sinks"))
    upd["b_f_c"] = (grad_b_f,) + tuple(small_adam(b_f_c, grad_b_f, m_b_f_c, v_b_f_c, "adam_b_f"))

    def big(i, which, w, m, v):
        return _adamw_sum(recv[i][which], w, m, v, name=f"adam_{'in' if which == 0 else 'out'}{i}")

    a_in = [big(i, 0, w_in_a[jj], m_w_in_a[jj], v_w_in_a[jj]) for jj, i in enumerate((0, 3))]
    a_out = [big(i, 1, w_out_a[jj], m_w_out_a[jj], v_w_out_a[jj]) for jj, i in enumerate((0, 3))]
    upd["w_in_a"] = tuple(jnp.stack([a_in[0][t], a_in[1][t]]) for t in range(4))
    upd["w_out_a"] = tuple(jnp.stack([a_out[0][t], a_out[1][t]]) for t in range(4))
    upd["w_in_b"] = tuple(t[None] for t in big(1, 0, w_in_b[0], m_w_in_b[0], v_w_in_b[0]))
    upd["w_out_b"] = tuple(t[None] for t in big(1, 1, w_out_b[0], m_w_out_b[0], v_w_out_b[0]))
    upd["w_in_c"] = tuple(t[None] for t in big(2, 0, w_in_c[0], m_w_in_c[0], v_w_in_c[0]))
    upd["w_out_c"] = tuple(t[None] for t in big(2, 1, w_out_c[0], m_w_out_c[0], v_w_out_c[0]))

    names = ["g_pre", "g_post", "w_in_a", "w_out_a", "sinks_a", "w_in_b", "w_out_b", "w_in_c", "b_f_c", "w_out_c"]
    return (loss, dx[None], *[upd[k][0] for k in names], *[upd[k][1] for k in names],
            *[upd[k][2] for k in names], *[upd[k][3] for k in names])
```

```python
import functools
import math

import numpy as np
import jax
import jax.numpy as jnp
from jax import lax
from jax.experimental import pallas as pl
from jax.experimental.pallas import tpu as pltpu

HEAD_DIM = 64
BLOCK = 128
NORM_EPS = 1e-6
NEG = -1e30
N_DEV = 8
DEPTH = 4
ADAM_LR, ADAM_B1, ADAM_B2, ADAM_EPS, ADAM_WD, ADAM_STEP = 0.001, 0.9, 0.999, 1e-8, 0.01, 10
LOG2E = 1.4426950408889634
VMEM_LIMIT = 56 * 1024 * 1024

bf16 = jnp.bfloat16
f32 = jnp.float32
MESH = pl.DeviceIdType.MESH
ANY = pl.BlockSpec(memory_space=pl.ANY)
SMEM = pl.BlockSpec(memory_space=pltpu.SMEM)

NN = (((1,), (0,)), ((), ()))
NT = (((1,), (1,)), ((), ()))
TN = (((0,), (0,)), ((), ()))


def _dot(a, b, dims=NN):
    return lax.dot_general(a, b, dims, preferred_element_type=f32)


def _params(sem):
    return pltpu.CompilerParams(dimension_semantics=sem, vmem_limit_bytes=VMEM_LIMIT)


def _attn_tile(S):
    return 512 if S % 512 == 0 and S >= 1024 else 128


def _pick(n, pref):
    for t in pref:
        if n % t == 0:
            return t
    return n


def _matmul(a, b, mode, out_dtype, name):
    if mode == "nn":
        (M, K), (K2, N) = a.shape, b.shape
    elif mode == "nt":
        (M, K), (N, K2) = a.shape, b.shape
    else:
        (K, M), (K2, N) = a.shape, b.shape
    assert K == K2, (a.shape, b.shape, mode)
    tm = _pick(M, (1024, 512, 256, 128))
    tn = _pick(N, (1024, 768, 640, 512, 256, 128))
    tk = _pick(K, (2048, 1664, 1536, 1024, 512, 640, 256, 128))
    nk = K // tk
    dims = {"nn": NN, "nt": NT, "tn": TN}[mode]

    def body(a_ref, b_ref, o_ref, acc_ref):
        if nk == 1:
            o_ref[...] = _dot(a_ref[...], b_ref[...], dims).astype(o_ref.dtype)
            return
        k = pl.program_id(2)

        @pl.when(k == 0)
        def _():
            acc_ref[...] = jnp.zeros_like(acc_ref)

        acc_ref[...] += _dot(a_ref[...], b_ref[...], dims)

        @pl.when(k == nk - 1)
        def _():
            o_ref[...] = acc_ref[...].astype(o_ref.dtype)

    if mode == "tn":
        a_spec = pl.BlockSpec((tk, tm), lambda i, j, k: (k, i))
    else:
        a_spec = pl.BlockSpec((tm, tk), lambda i, j, k: (i, k))
    if mode == "nt":
        b_spec = pl.BlockSpec((tn, tk), lambda i, j, k: (j, k))
    else:
        b_spec = pl.BlockSpec((tk, tn), lambda i, j, k: (k, j))
    return pl.pallas_call(
        body, name=name,
        grid=(M // tm, N // tn, nk),
        in_specs=[a_spec, b_spec],
        out_specs=pl.BlockSpec((tm, tn), lambda i, j, k: (i, j)),
        out_shape=jax.ShapeDtypeStruct((M, N), out_dtype),
        scratch_shapes=[pltpu.VMEM((tm, tn) if nk > 1 else (8, 128), f32)],
        compiler_params=_params(("parallel", "parallel", "arbitrary")),
    )(a, b)


ROWS = 256


def _rows(S):
    return ROWS if S % ROWS == 0 else S


def _rmsnorm_fwd(x, g, name):
    S, D = x.shape
    tr = _rows(S)

    def body(x_ref, g_ref, h_ref):
        xv = x_ref[...]
        r = lax.rsqrt(jnp.mean(xv * xv, axis=-1, keepdims=True) + NORM_EPS)
        h_ref[...] = (xv * r * g_ref[...]).astype(h_ref.dtype)

    return pl.pallas_call(
        body, name=name, grid=(S // tr,),
        in_specs=[pl.BlockSpec((tr, D), lambda i: (i, 0)), pl.BlockSpec((1, D), lambda i: (0, 0))],
        out_specs=pl.BlockSpec((tr, D), lambda i: (i, 0)),
        out_shape=jax.ShapeDtypeStruct((S, D), bf16),
        compiler_params=_params(("parallel",)),
    )(x, g)


def _post_fwd(x, y, g, name):
    S, D = x.shape
    tr = _rows(S)

    def body(x_ref, y_ref, g_ref, o_ref):
        yv = y_ref[...]
        r = lax.rsqrt(jnp.mean(yv * yv, axis=-1, keepdims=True) + NORM_EPS)
        o_ref[...] = x_ref[...] + yv * r * g_ref[...]

    row = pl.BlockSpec((tr, D), lambda i: (i, 0))
    return pl.pallas_call(
        body, name=name, grid=(S // tr,),
        in_specs=[row, row, pl.BlockSpec((1, D), lambda i: (0, 0))],
        out_specs=row,
        out_shape=jax.ShapeDtypeStruct((S, D), f32),
        compiler_params=_params(("parallel",)),
    )(x, y, g)


def _loss_fwd_bwd(y, t, name):
    S, D = y.shape
    tr = _rows(S)

    def body(y_ref, t_ref, l_ref, d_ref):
        @pl.when(pl.program_id(0) == 0)
        def _():
            l_ref[...] = jnp.zeros_like(l_ref)

        e = y_ref[...] - t_ref[...]
        d_ref[...] = e * (1.0 / D)
        part = 0.5 * jnp.sum(jnp.sum(e * e, axis=-1, keepdims=True) * (1.0 / D), axis=0, keepdims=True)
        l_ref[...] += jnp.broadcast_to(part, l_ref.shape)

    row = pl.BlockSpec((tr, D), lambda i: (i, 0))
    return pl.pallas_call(
        body, name=name, grid=(S // tr,),
        in_specs=[row, row],
        out_specs=[pl.BlockSpec((8, 128), lambda i: (0, 0)), row],
        out_shape=[jax.ShapeDtypeStruct((8, 128), f32), jax.ShapeDtypeStruct((S, D), f32)],
        compiler_params=_params(("arbitrary",)),
    )(y, t)


def _post_bwd(dxn, y, g, name):
    S, D = y.shape
    tr = _rows(S)

    def body(d_ref, y_ref, g_ref, dy_ref, dg_ref):
        @pl.when(pl.program_id(0) == 0)
        def _():
            dg_ref[...] = jnp.zeros_like(dg_ref)

        yv = y_ref[...]
        d = d_ref[...]
        r = lax.rsqrt(jnp.mean(yv * yv, axis=-1, keepdims=True) + NORM_EPS)
        n = yv * r
        dn = d * g_ref[...]
        dg_ref[...] += jnp.sum(d * n, axis=0, keepdims=True)
        dy_ref[...] = (r * (dn - n * jnp.mean(dn * n, axis=-1, keepdims=True))).astype(dy_ref.dtype)

    row = pl.BlockSpec((tr, D), lambda i: (i, 0))
    vec = pl.BlockSpec((1, D), lambda i: (0, 0))
    return pl.pallas_call(
        body, name=name, grid=(S // tr,),
        in_specs=[row, row, vec],
        out_specs=[row, vec],
        out_shape=[jax.ShapeDtypeStruct((S, D), bf16), jax.ShapeDtypeStruct((1, D), f32)],
        compiler_params=_params(("arbitrary",)),
    )(dxn, y, g)


def _pre_bwd(dh, x, g, dres, name):
    S, D = x.shape
    tr = _rows(S)

    def body(dh_ref, x_ref, g_ref, dres_ref, dx_ref, dg_ref):
        @pl.when(pl.program_id(0) == 0)
        def _():
            dg_ref[...] = jnp.zeros_like(dg_ref)

        xv = x_ref[...]
        d = dh_ref[...]
        r = lax.rsqrt(jnp.mean(xv * xv, axis=-1, keepdims=True) + NORM_EPS)
        n = xv * r
        dn = d * g_ref[...]
        dg_ref[...] += jnp.sum(d * n, axis=0, keepdims=True)
        dx_ref[...] = dres_ref[...] + r * (dn - n * jnp.mean(dn * n, axis=-1, keepdims=True))

    row = pl.BlockSpec((tr, D), lambda i: (i, 0))
    vec = pl.BlockSpec((1, D), lambda i: (0, 0))
    return pl.pallas_call(
        body, name=name, grid=(S // tr,),
        in_specs=[row, row, vec, row],
        out_specs=[row, vec],
        out_shape=[jax.ShapeDtypeStruct((S, D), f32), jax.ShapeDtypeStruct((1, D), f32)],
        compiler_params=_params(("arbitrary",)),
    )(dh, x, g, dres)


def _sigmoid(x):
    return 1.0 / (1.0 + jnp.exp(-x))


def _gate_fwd(o, z):
    zf = z.astype(f32)
    return o * (zf * _sigmoid(zf))


def _gate_bwd(du, o, z):
    zf = z.astype(f32)
    sig = _sigmoid(zf)
    do = du * (zf * sig)
    dz = du * o * (sig * (1.0 + zf * (1.0 - sig)))
    return do, dz


def _iota2(shape, dim):
    return lax.broadcasted_iota(jnp.int32, shape, dim)


STRIP = 16


def _strips(n_rows, fn, unroll=4):
    def step(i, carry):
        fn(i, pl.ds(pl.multiple_of(i * STRIP, STRIP), STRIP))
        return carry

    lax.fori_loop(0, n_rows // STRIP, step, 0, unroll=unroll)


def _swa_scores(qh, kc, kp, slope, sink, n, scale):
    qi = _iota2((BLOCK, BLOCK), 0)
    kj = _iota2((BLOCK, BLOCK), 1)
    mask_c = kj <= qi
    mask_p = kj > qi + jnp.where(n > 0, 0, BLOCK)
    dist_c = (qi - kj).astype(f32)
    dist_p = (qi + BLOCK - kj).astype(f32)
    sc = jnp.where(mask_c, _dot(qh, kc, NT) * scale - slope * dist_c, NEG)
    sp = jnp.where(mask_p, _dot(qh, kp, NT) * scale - slope * dist_p, NEG)
    m = jnp.maximum(jnp.maximum(jnp.max(sc, axis=-1, keepdims=True), jnp.max(sp, axis=-1, keepdims=True)), sink)
    pc = jnp.exp(sc - m)
    pp = jnp.exp(sp - m)
    ps = jnp.exp(sink - m)
    den = jnp.sum(pc, axis=-1, keepdims=True) + jnp.sum(pp, axis=-1, keepdims=True) + ps
    return pc, pp, ps, den


def _swa_specs(S, BR, KV, G):
    gw = G * HEAD_DIM
    qspec = pl.BlockSpec((BLOCK, gw), lambda h, n: (n, h))
    zoff = (BR + 2 * KV) // gw
    zspec = pl.BlockSpec((BLOCK, gw), lambda h, n: (n, zoff + h))
    cur = pl.BlockSpec((1, BLOCK, HEAD_DIM), lambda h, n: (h, n, 0))
    prev = pl.BlockSpec((1, BLOCK, HEAD_DIM), lambda h, n: (h, jnp.maximum(n - 1, 0), 0))
    return qspec, zspec, cur, prev


def _swa_fwd(P, kh, vh, sinks, slopes, name, comm=None):
    S = P.shape[0]
    n_kv = kh.shape[0]
    H = sinks.shape[0]
    G = H // n_kv
    BR, KV = H * HEAD_DIM, n_kv * HEAD_DIM
    scale = HEAD_DIM ** -0.5
    assert (BR + 2 * KV) % (G * HEAD_DIM) == 0

    def body(q_ref, z_ref, kc_ref, kp_ref, vc_ref, vp_ref, sink_ref, slope_ref, o_ref, u_ref):
        kvh, n = pl.program_id(0), pl.program_id(1)
        kc, kp, vc, vp = kc_ref[0], kp_ref[0], vc_ref[0], vp_ref[0]
        for g in range(G):
            sl = slice(g * HEAD_DIM, (g + 1) * HEAD_DIM)
            h = kvh * G + g
            pc, pp, _, den = _swa_scores(q_ref[:, sl], kc, kp, slope_ref[h], sink_ref[h], n, scale)
            oh = (_dot(pc.astype(bf16), vc) + _dot(pp.astype(bf16), vp)) / den
            o_ref[:, sl] = oh.astype(o_ref.dtype)
            u_ref[:, sl] = _gate_fwd(oh, z_ref[:, sl]).astype(u_ref.dtype)

    qspec, zspec, cur, prev = _swa_specs(S, BR, KV, G)
    ospec = pl.BlockSpec((BLOCK, G * HEAD_DIM), lambda h, n: (n, h))
    return _pallas(
        body, (P, P, kh, kh, vh, vh, sinks, slopes), name=name, grid=(n_kv, S // BLOCK),
        in_specs=[qspec, zspec, cur, prev, cur, prev, SMEM, SMEM],
        out_specs=[ospec, ospec],
        out_shape=[jax.ShapeDtypeStruct((S, BR), bf16)] * 2,
        semantics=("parallel", "parallel"), comm=comm)


def _swa_bwd(P, kh, vh, o, du, sinks, slopes, name, comm=None):
    S = P.shape[0]
    n_kv = kh.shape[0]
    H = sinks.shape[0]
    G = H // n_kv
    BR, KV = H * HEAD_DIM, n_kv * HEAD_DIM
    scale = HEAD_DIM ** -0.5

    def body(q_ref, z_ref, kc_ref, kp_ref, vc_ref, vp_ref, o_ref, du_ref, sink_ref, slope_ref,
             dq_ref, dz_ref, dk_ref, dv_ref, ds_ref):
        kvh, n = pl.program_id(0), pl.program_id(1)

        @pl.when(n == 0)
        def _():
            dk_ref[...] = jnp.zeros_like(dk_ref)
            dv_ref[...] = jnp.zeros_like(dv_ref)
            ds_ref[...] = jnp.zeros_like(ds_ref)

        kc, kp, vc, vp = kc_ref[0], kp_ref[0], vc_ref[0], vp_ref[0]
        dkc = jnp.zeros((BLOCK, HEAD_DIM), f32)
        dkp = jnp.zeros((BLOCK, HEAD_DIM), f32)
        dvc = jnp.zeros((BLOCK, HEAD_DIM), f32)
        dvp = jnp.zeros((BLOCK, HEAD_DIM), f32)
        dsink = jnp.zeros((8, 128), f32)
        lane = _iota2((8, 128), 1)
        for g in range(G):
            sl = slice(g * HEAD_DIM, (g + 1) * HEAD_DIM)
            h = kvh * G + g
            qh = q_ref[:, sl]
            pc, pp, ps, den = _swa_scores(qh, kc, kp, slope_ref[h], sink_ref[h], n, scale)
            inv = 1.0 / den
            pc, pp, ps = pc * inv, pp * inv, ps * inv
            oh = o_ref[:, sl].astype(f32)
            doh, dzh = _gate_bwd(du_ref[:, sl].astype(f32), oh, z_ref[:, sl])
            dz_ref[:, sl] = dzh.astype(dz_ref.dtype)
            delta = jnp.sum(doh * oh, axis=-1, keepdims=True)
            dob = doh.astype(bf16)
            dsc = (pc * (_dot(dob, vc, NT) - delta)).astype(bf16)
            dsp = (pp * (_dot(dob, vp, NT) - delta)).astype(bf16)
            dq_ref[:, sl] = ((_dot(dsc, kc) + _dot(dsp, kp)) * scale).astype(dq_ref.dtype)
            dkc += _dot(dsc, qh, TN)
            dkp += _dot(dsp, qh, TN)
            dvc += _dot(pc.astype(bf16), dob, TN)
            dvp += _dot(pp.astype(bf16), dob, TN)
            dsh = -jnp.sum(ps * delta, axis=0, keepdims=True)
            dsink += jnp.where(lane == g, jnp.broadcast_to(dsh, (8, 128)), 0.0)
        ds_ref[0] += dsink
        row_c = pl.multiple_of(n * BLOCK, BLOCK)
        dk_ref[0, pl.ds(row_c, BLOCK), :] += dkc * scale
        dv_ref[0, pl.ds(row_c, BLOCK), :] += dvc

        @pl.when(n > 0)
        def _():
            row_p = pl.multiple_of((n - 1) * BLOCK, BLOCK)
            dk_ref[0, pl.ds(row_p, BLOCK), :] += dkp * scale
            dv_ref[0, pl.ds(row_p, BLOCK), :] += dvp

    qspec, zspec, cur, prev = _swa_specs(S, BR, KV, G)
    ospec = pl.BlockSpec((BLOCK, G * HEAD_DIM), lambda h, n: (n, h))
    full = pl.BlockSpec((1, S, HEAD_DIM), lambda h, n: (h, 0, 0))
    return _pallas(
        body, (P, P, kh, kh, vh, vh, o, du, sinks, slopes), name=name, grid=(n_kv, S // BLOCK),
        in_specs=[qspec, zspec, cur, prev, cur, prev, ospec, ospec, SMEM, SMEM],
        out_specs=[ospec, ospec, full, full, pl.BlockSpec((1, 8, 128), lambda h, n: (h, 0, 0))],
        out_shape=[jax.ShapeDtypeStruct((S, BR), bf16)] * 2
        + [jax.ShapeDtypeStruct((n_kv, S, HEAD_DIM), f32)] * 2
        + [jax.ShapeDtypeStruct((n_kv, 8, 128), f32)],
        semantics=("parallel", "arbitrary"), comm=comm)


def _split3(x):
    x1 = x.astype(bf16)
    r1 = x - x1.astype(f32)
    x2 = r1.astype(bf16)
    x3 = (r1 - x2.astype(f32)).astype(bf16)
    return x1, x2, x3


def _split2(x):
    x1 = x.astype(bf16)
    return x1, (x - x1.astype(f32)).astype(bf16)


def _pair_specs(S, BR, T, kmap):
    nb = BR // 128
    q = pl.BlockSpec((T, 128), lambda p, i, j: (i, p))
    k = pl.BlockSpec((T, 128), lambda p, i, j: (kmap(i, j), nb + p))
    v = pl.BlockSpec((T, 128), lambda p, i, j: (kmap(i, j), 2 * nb + p))
    z = pl.BlockSpec((T, 128), lambda p, i, j: (i, 3 * nb + p))
    return q, k, v, z


def _fox_fwd(P, cq, ck, BR, name, comm=None):
    S = P.shape[0]
    H = BR // HEAD_DIM
    T = _attn_tile(S)
    nq = S // T
    scale = HEAD_DIM ** -0.5
    kmap = lambda i, j: jnp.minimum(i, j)

    def body(q_ref, k_ref, v_ref, z_ref, cq_ref, ck_ref, o_ref, u_ref, lse_ref, m_s, l_s, acc_s, s_s, p_s, al_s):
        qb, kb = pl.program_id(1), pl.program_id(2)

        @pl.when(kb == 0)
        def _():
            m_s[...] = jnp.full_like(m_s, NEG)
            l_s[...] = jnp.zeros_like(l_s)
            acc_s[...] = jnp.zeros_like(acc_s)

        def tile(diagonal):
            for hh in range(2):
                sl = slice(hh * HEAD_DIM, (hh + 1) * HEAD_DIM)
                s_s[...] = _dot(q_ref[:, sl], k_ref[:, sl], NT)
                ck_row = ck_ref[hh]

                def strip(i, r):
                    s = s_s[r, :] * scale + cq_ref[hh, r, :] - ck_row
                    if diagonal:
                        s = jnp.where(_iota2((STRIP, T), 1) <= i * STRIP + _iota2((STRIP, T), 0), s, NEG)
                    m_old = m_s[hh, r, :]
                    m_new = jnp.maximum(m_old, jnp.max(s, axis=-1, keepdims=True))
                    alpha = jnp.exp(m_old - m_new)
                    p = jnp.exp(s - m_new)
                    l_s[hh, r, :] = alpha * l_s[hh, r, :] + jnp.sum(p, axis=-1, keepdims=True)
                    m_s[hh, r, :] = m_new
                    al_s[r, :] = alpha
                    p_s[r, :] = p.astype(bf16)

                _strips(T, strip)
                acc_s[hh] = al_s[...] * acc_s[hh] + _dot(p_s[...], v_ref[:, sl])

        @pl.when(kb < qb)
        def _():
            tile(False)

        @pl.when(kb == qb)
        def _():
            tile(True)

        @pl.when(kb == qb)
        def _():
            for hh in range(2):
                sl = slice(hh * HEAD_DIM, (hh + 1) * HEAD_DIM)
                oh = acc_s[hh] / l_s[hh]
                o_ref[:, sl] = oh.astype(o_ref.dtype)
                u_ref[:, sl] = _gate_fwd(oh, z_ref[:, sl]).astype(u_ref.dtype)
                lse_ref[hh] = m_s[hh] + jnp.log(l_s[hh])

    q, k, v, z = _pair_specs(S, BR, T, kmap)
    col = pl.BlockSpec((2, T, 1), lambda p, i, j: (p, i, 0))
    rowk = pl.BlockSpec((2, 1, T), lambda p, i, j: (p, 0, kmap(i, j)))
    ospec = pl.BlockSpec((T, 128), lambda p, i, j: (i, p))
    return _pallas(
        body, (P, P, P, P, cq, ck), name=name, grid=(H // 2, nq, nq),
        in_specs=[q, k, v, z, col, rowk],
        out_specs=[ospec, ospec, col],
        out_shape=[jax.ShapeDtypeStruct((S, BR), bf16)] * 2 + [jax.ShapeDtypeStruct((H, S, 1), f32)],
        scratch_shapes=[pltpu.VMEM((2, T, 1), f32), pltpu.VMEM((2, T, 1), f32), pltpu.VMEM((2, T, HEAD_DIM), f32),
                        pltpu.VMEM((T, T), f32), pltpu.VMEM((T, T), bf16), pltpu.VMEM((T, 1), f32)],
        semantics=("parallel", "arbitrary", "arbitrary"), comm=comm)


def _fox_bwd(P, o, du, lse, cq, ck, BR, name, comm=None):
    S = P.shape[0]
    H = BR // HEAD_DIM
    T = _attn_tile(S)
    nq = S // T
    scale = HEAD_DIM ** -0.5
    kmap = lambda i, j: jnp.minimum(i, j)

    def body(q_ref, k_ref, v_ref, z_ref, o_ref, du_ref, lse_ref, cq_ref, ck_ref,
             dq_ref, dz_ref, dk_ref, dv_ref, dcq_ref, dck_ref, do_s, delta_s, dq_s, dcq_s):
        qb, kb = pl.program_id(1), pl.program_id(2)

        @pl.when(jnp.logical_and(qb == 0, kb == 0))
        def _():
            dk_ref[...] = jnp.zeros_like(dk_ref)
            dv_ref[...] = jnp.zeros_like(dv_ref)
            dck_ref[...] = jnp.zeros_like(dck_ref)

        @pl.when(kb == 0)
        def _():
            for hh in range(2):
                sl = slice(hh * HEAD_DIM, (hh + 1) * HEAD_DIM)
                oh = o_ref[:, sl].astype(f32)
                doh, dzh = _gate_bwd(du_ref[:, sl].astype(f32), oh, z_ref[:, sl])
                dz_ref[:, sl] = dzh.astype(dz_ref.dtype)
                do_s[hh] = doh.astype(bf16)
                delta_s[hh] = jnp.sum(doh * oh, axis=-1, keepdims=True)
            dq_s[...] = jnp.zeros_like(dq_s)
            dcq_s[...] = jnp.zeros_like(dcq_s)

        @pl.when(kb <= qb)
        def _():
            tpos = qb * T + _iota2((T, T), 0)
            spos = kb * T + _iota2((T, T), 1)
            causal = spos <= tpos
            rows = pl.ds(pl.multiple_of(kb * T, T), T)
            for hh in range(2):
                sl = slice(hh * HEAD_DIM, (hh + 1) * HEAD_DIM)
                qh, kh, vh, dob = q_ref[:, sl], k_ref[:, sl], v_ref[:, sl], do_s[hh]
                s = _dot(qh, kh, NT) * scale + cq_ref[hh] - ck_ref[hh]
                p = jnp.where(causal, jnp.exp(s - lse_ref[hh]), 0.0)
                ds = p * (_dot(dob, vh, NT) - delta_s[hh])
                dsb = ds.astype(bf16)
                dq_s[hh] += _dot(dsb, kh)
                dk_ref[rows, sl] += _dot(dsb, qh, TN) * scale
                dv_ref[rows, sl] += _dot(p.astype(bf16), dob, TN)
                dcq_s[hh] += jnp.sum(ds, axis=-1, keepdims=True)
                dck_ref[hh, kb] += jnp.sum(ds, axis=0, keepdims=True)

        @pl.when(kb == qb)
        def _():
            for hh in range(2):
                sl = slice(hh * HEAD_DIM, (hh + 1) * HEAD_DIM)
                dq_ref[:, sl] = (dq_s[hh] * scale).astype(dq_ref.dtype)
                dcq_ref[hh] = dcq_s[hh]

    q, k, v, z = _pair_specs(S, BR, T, kmap)
    col = pl.BlockSpec((2, T, 1), lambda p, i, j: (p, i, 0))
    rowk = pl.BlockSpec((2, 1, T), lambda p, i, j: (p, 0, kmap(i, j)))
    ospec = pl.BlockSpec((T, 128), lambda p, i, j: (i, p))
    full = pl.BlockSpec((S, 128), lambda p, i, j: (0, p))
    return _pallas(
        body, (P, P, P, P, o, du, lse, cq, ck), name=name, grid=(H // 2, nq, nq),
        in_specs=[q, k, v, z, ospec, ospec, col, col, rowk],
        out_specs=[ospec, ospec, full, full, col, pl.BlockSpec((2, nq, 1, T), lambda p, i, j: (p, 0, 0, 0))],
        out_shape=[jax.ShapeDtypeStruct((S, BR), bf16)] * 2 + [jax.ShapeDtypeStruct((S, BR), f32)] * 2
        + [jax.ShapeDtypeStruct((H, S, 1), f32), jax.ShapeDtypeStruct((H, nq, 1, T), f32)],
        scratch_shapes=[pltpu.VMEM((2, T, HEAD_DIM), bf16), pltpu.VMEM((2, T, 1), f32),
                        pltpu.VMEM((2, T, HEAD_DIM), f32), pltpu.VMEM((2, T, 1), f32)],
        semantics=("parallel", "arbitrary", "arbitrary"), comm=comm)


def _fox_cum(flT, b, name):
    H, S = flT.shape
    tj = _pick(S, (256, 128))

    def body(fl_ref, b_ref, c_ref):
        j = pl.program_id(0)
        x = fl_ref[...] + b_ref[...]
        logf = jnp.minimum(x, 0.0) - jnp.log(1.0 + jnp.exp(-jnp.abs(x)))
        tri = (_iota2((S, tj), 0) <= j * tj + _iota2((S, tj), 1)).astype(bf16)
        c_ref[...] = sum(_dot(part, tri) for part in _split3(logf))

    return pl.pallas_call(
        body, name=name, grid=(S // tj,),
        in_specs=[pl.BlockSpec((H, S), lambda j: (0, 0)), pl.BlockSpec((H, 1), lambda j: (0, 0))],
        out_specs=pl.BlockSpec((H, tj), lambda j: (0, j)),
        out_shape=jax.ShapeDtypeStruct((H, S), f32),
        compiler_params=_params(("parallel",)),
    )(flT, b)


def _fox_cum_bwd(dcq, dck, flT, b, name):
    H, S = flT.shape
    tj = _pick(S, (256, 128))

    def body(dcq_ref, dck_ref, fl_ref, b_ref, o_ref, db_ref):
        j = pl.program_id(0)

        @pl.when(j == 0)
        def _():
            db_ref[...] = jnp.zeros_like(db_ref)

        tri = (_iota2((S, tj), 0) >= j * tj + _iota2((S, tj), 1)).astype(bf16)
        dlogf = sum(_dot(part, tri) for part in _split3(dcq_ref[...] - dck_ref[...]))
        x = fl_ref[...] + b_ref[...]
        dfl = dlogf * _sigmoid(-x)
        o_ref[...] = dfl
        db_ref[...] += jnp.sum(dfl, axis=-1, keepdims=True)

    blk = pl.BlockSpec((H, tj), lambda j: (0, j))
    whole = pl.BlockSpec((H, S), lambda j: (0, 0))
    col = pl.BlockSpec((H, 1), lambda j: (0, 0))
    return pl.pallas_call(
        body, name=name, grid=(S // tj,),
        in_specs=[whole, whole, blk, col],
        out_specs=[blk, col],
        out_shape=[jax.ShapeDtypeStruct((H, S), f32), jax.ShapeDtypeStruct((H, 1), f32)],
        compiler_params=_params(("arbitrary",)),
    )(dcq, dck, flT, b)


def _sb_logs(qh, kc, before, scale):
    l = _dot(qh, kc, NT) * scale
    lsig = jnp.minimum(l, 0.0) - jnp.log(1.0 + jnp.exp(-jnp.abs(l)))
    lf = jnp.where(before, lsig - l, 0.0)
    return lsig, lf


def _sb_suffix(lf, tri):
    hi, lo = _split2(lf)
    return _dot(hi, tri) + _dot(lo, tri)


def _sb_fwd(P, BR, name, comm=None):
    S = P.shape[0]
    H = BR // HEAD_DIM
    T = _attn_tile(S)
    nq = S // T
    nc = T // BLOCK
    scale = HEAD_DIM ** -0.5
    kmap = lambda i, j: jnp.maximum(i - j, 0)

    def body(q_ref, k_ref, v_ref, z_ref, o_ref, u_ref, rc_ref, r_s, acc_s):
        qb, j = pl.program_id(1), pl.program_id(2)
        kb = qb - j

        @pl.when(j == 0)
        def _():
            r_s[...] = jnp.zeros_like(r_s)
            acc_s[...] = jnp.zeros_like(acc_s)

        @pl.when(j <= qb)
        def _():
            tpos = qb * T + _iota2((T, BLOCK), 0)
            col = _iota2((T, BLOCK), 1)
            tri = (_iota2((BLOCK, BLOCK), 0) > _iota2((BLOCK, BLOCK), 1)).astype(bf16)
            for hh in range(2):
                sl = slice(hh * HEAD_DIM, (hh + 1) * HEAD_DIM)
                qh = q_ref[:, sl]
                r = r_s[hh]
                rc_ref[hh, 0] = r
                acc = acc_s[hh]
                for c in reversed(range(nc)):
                    rows = slice(c * BLOCK, (c + 1) * BLOCK)
                    before = kb * T + c * BLOCK + col < tpos
                    lsig, lf = _sb_logs(qh, k_ref[rows, sl], before, scale)
                    a = jnp.where(before, jnp.exp(lsig + _sb_suffix(lf, tri) + r), 0.0)
                    acc = acc + _dot(a.astype(bf16), v_ref[rows, sl])
                    r = r + jnp.sum(lf, axis=-1, keepdims=True)
                r_s[hh] = r
                acc_s[hh] = acc

        @pl.when(j == qb)
        def _():
            for hh in range(2):
                sl = slice(hh * HEAD_DIM, (hh + 1) * HEAD_DIM)
                oh = acc_s[hh]
                o_ref[:, sl] = oh.astype(o_ref.dtype)
                u_ref[:, sl] = _gate_fwd(oh, z_ref[:, sl]).astype(u_ref.dtype)

    q, k, v, z = _pair_specs(S, BR, T, kmap)
    ospec = pl.BlockSpec((T, 128), lambda p, i, j: (i, p))
    rc = pl.BlockSpec((2, 1, T, 1), lambda p, i, j: (p, kmap(i, j), i, 0))
    return _pallas(
        body, (P, P, P, P), name=name, grid=(H // 2, nq, nq),
        in_specs=[q, k, v, z],
        out_specs=[ospec, ospec, rc],
        out_shape=[jax.ShapeDtypeStruct((S, BR), bf16)] * 2 + [jax.ShapeDtypeStruct((H, nq, S, 1), f32)],
        scratch_shapes=[pltpu.VMEM((2, T, 1), f32), pltpu.VMEM((2, T, HEAD_DIM), f32)],
        semantics=("parallel", "arbitrary", "arbitrary"), comm=comm)


def _sb_bwd(P, o, du, rc, BR, name, comm=None):
    S = P.shape[0]
    H = BR // HEAD_DIM
    T = _attn_tile(S)
    nq = S // T
    nc = T // BLOCK
    scale = HEAD_DIM ** -0.5
    kmap = lambda i, j: jnp.minimum(i, j)

    def body(q_ref, k_ref, v_ref, z_ref, o_ref, du_ref, rc_ref,
             dq_ref, dz_ref, dk_ref, dv_ref, do_s, dq_s, g_s):
        qb, kb = pl.program_id(1), pl.program_id(2)

        @pl.when(jnp.logical_and(qb == 0, kb == 0))
        def _():
            dk_ref[...] = jnp.zeros_like(dk_ref)
            dv_ref[...] = jnp.zeros_like(dv_ref)

        @pl.when(kb == 0)
        def _():
            for hh in range(2):
                sl = slice(hh * HEAD_DIM, (hh + 1) * HEAD_DIM)
                doh, dzh = _gate_bwd(du_ref[:, sl].astype(f32), o_ref[:, sl].astype(f32), z_ref[:, sl])
                dz_ref[:, sl] = dzh.astype(dz_ref.dtype)
                do_s[hh] = doh.astype(bf16)
            dq_s[...] = jnp.zeros_like(dq_s)
            g_s[...] = jnp.zeros_like(g_s)

        @pl.when(kb <= qb)
        def _():
            tpos = qb * T + _iota2((T, BLOCK), 0)
            col = _iota2((T, BLOCK), 1)
            ii, jj = _iota2((BLOCK, BLOCK), 0), _iota2((BLOCK, BLOCK), 1)
            tri_suffix = (ii > jj).astype(bf16)
            tri_prefix = (ii < jj).astype(bf16)
            for hh in range(2):
                sl = slice(hh * HEAD_DIM, (hh + 1) * HEAD_DIM)
                qh, dob = q_ref[:, sl], do_s[hh]
                befores, lsigs, lfs = [], [], []
                for c in range(nc):
                    before = kb * T + c * BLOCK + col < tpos
                    lsig, lf = _sb_logs(qh, k_ref[c * BLOCK:(c + 1) * BLOCK, sl], before, scale)
                    befores.append(before)
                    lsigs.append(lsig)
                    lfs.append(lf)
                carries = [None] * nc
                r = rc_ref[hh, 0]
                for c in reversed(range(nc)):
                    carries[c] = r
                    r = r + jnp.sum(lfs[c], axis=-1, keepdims=True)
                gsum = g_s[hh]
                dq = dq_s[hh]
                for c in range(nc):
                    rows = slice(c * BLOCK, (c + 1) * BLOCK)
                    out_rows = pl.ds(pl.multiple_of(kb * T + c * BLOCK, BLOCK), BLOCK)
                    kc, vc = k_ref[rows, sl], v_ref[rows, sl]
                    a = jnp.where(befores[c], jnp.exp(lsigs[c] + _sb_suffix(lfs[c], tri_suffix) + carries[c]), 0.0)
                    g = a * _dot(dob, vc, NT)
                    gex = _dot(g.astype(bf16), tri_prefix) + gsum
                    sig = jnp.exp(lsigs[c])
                    dl = jnp.where(befores[c], g * (1.0 - sig) - gex * sig, 0.0).astype(bf16)
                    dq = dq + _dot(dl, kc)
                    dk_ref[out_rows, sl] += _dot(dl, qh, TN) * scale
                    dv_ref[out_rows, sl] += _dot(a.astype(bf16), dob, TN)
                    gsum = gsum + jnp.sum(g, axis=-1, keepdims=True)
                g_s[hh] = gsum
                dq_s[hh] = dq

        @pl.when(kb == qb)
        def _():
            for hh in range(2):
                sl = slice(hh * HEAD_DIM, (hh + 1) * HEAD_DIM)
                dq_ref[:, sl] = (dq_s[hh] * scale).astype(dq_ref.dtype)

    q, k, v, z = _pair_specs(S, BR, T, kmap)
    ospec = pl.BlockSpec((T, 128), lambda p, i, j: (i, p))
    full = pl.BlockSpec((S, 128), lambda p, i, j: (0, p))
    rcs = pl.BlockSpec((2, 1, T, 1), lambda p, i, j: (p, kmap(i, j), i, 0))
    return _pallas(
        body, (P, P, P, P, o, du, rc), name=name, grid=(H // 2, nq, nq),
        in_specs=[q, k, v, z, ospec, ospec, rcs],
        out_specs=[ospec, ospec, full, full],
        out_shape=[jax.ShapeDtypeStruct((S, BR), bf16)] * 2 + [jax.ShapeDtypeStruct((S, BR), f32)] * 2,
        scratch_shapes=[pltpu.VMEM((2, T, HEAD_DIM), bf16), pltpu.VMEM((2, T, HEAD_DIM), f32),
                        pltpu.VMEM((2, T, 1), f32)],
        semantics=("parallel", "arbitrary", "arbitrary"), comm=comm)


def _hsl(hh):
    return slice(hh * HEAD_DIM, (hh + 1) * HEAD_DIM)


def _foxT_fwd(P, cq, ck, BR, name, comm=None):
    S = P.shape[0]
    H = BR // HEAD_DIM
    T = _attn_tile(S)
    nq = S // T
    scale = HEAD_DIM ** -0.5
    kmap = lambda i, j: jnp.minimum(i, j)

    def body(q_ref, k_ref, v_ref, z_ref, cq_ref, ck_ref, o_ref, u_ref, lse_ref, m_s, l_s, acc_s):
        qb, kb = pl.program_id(1), pl.program_id(2)

        @pl.when(kb == 0)
        def _():
            m_s[...] = jnp.full_like(m_s, NEG)
            l_s[...] = jnp.zeros_like(l_s)
            acc_s[...] = jnp.zeros_like(acc_s)

        def tile(diagonal):
            for hh in range(2):
                sl = _hsl(hh)
                s = _dot(k_ref[:, sl], q_ref[:, sl], NT) * scale + cq_ref[hh] - ck_ref[hh]
                if diagonal:
                    s = jnp.where(_iota2((T, T), 0) <= _iota2((T, T), 1), s, NEG)
                m_old = m_s[hh]
                m_new = jnp.maximum(m_old, jnp.max(s, axis=0, keepdims=True))
                alpha = jnp.exp(m_old - m_new)
                p = jnp.exp(s - m_new)
                l_s[hh] = alpha * l_s[hh] + jnp.sum(p, axis=0, keepdims=True)
                acc_s[hh] = alpha * acc_s[hh] + _dot(v_ref[:, sl], p.astype(bf16), TN)
                m_s[hh] = m_new

        @pl.when(kb < qb)
        def _():
            tile(False)

        @pl.when(kb == qb)
        def _():
            tile(True)
            for hh in range(2):
                sl = _hsl(hh)
                oh = jnp.transpose(acc_s[hh] / l_s[hh])
                o_ref[:, sl] = oh.astype(o_ref.dtype)
                u_ref[:, sl] = _gate_fwd(oh, z_ref[:, sl]).astype(u_ref.dtype)
                lse_ref[hh] = m_s[hh] + jnp.log(l_s[hh])

    q, k, v, z = _pair_specs(S, BR, T, kmap)
    rowq = pl.BlockSpec((2, 1, T), lambda p, i, j: (p, 0, i))
    colk = pl.BlockSpec((2, T, 1), lambda p, i, j: (p, kmap(i, j), 0))
    ospec = pl.BlockSpec((T, 128), lambda p, i, j: (i, p))
    return _pallas(
        body, (P, P, P, P, cq, ck), name=name, grid=(H // 2, nq, nq),
        in_specs=[q, k, v, z, rowq, colk],
        out_specs=[ospec, ospec, rowq],
        out_shape=[jax.ShapeDtypeStruct((S, BR), bf16)] * 2 + [jax.ShapeDtypeStruct((H, 1, S), f32)],
        scratch_shapes=[pltpu.VMEM((2, 1, T), f32), pltpu.VMEM((2, 1, T), f32), pltpu.VMEM((2, HEAD_DIM, T), f32)],
        semantics=("parallel", "arbitrary", "arbitrary"), comm=comm)


def _foxT_bwd(P, o, du, lse, cq, ck, BR, name, comm=None):
    S = P.shape[0]
    H = BR // HEAD_DIM
    T = _attn_tile(S)
    nq = S // T
    scale = HEAD_DIM ** -0.5
    kmap = lambda i, j: jnp.minimum(i, j)

    def body(q_ref, k_ref, v_ref, z_ref, o_ref, du_ref, lse_ref, cq_ref, ck_ref,
             dq_ref, dz_ref, dk_ref, dv_ref, dcq_ref, dck_ref, do_s, delta_s, dq_s, dcq_s):
        qb, kb = pl.program_id(1), pl.program_id(2)

        @pl.when(jnp.logical_and(qb == 0, kb == 0))
        def _():
            dk_ref[...] = jnp.zeros_like(dk_ref)
            dv_ref[...] = jnp.zeros_like(dv_ref)
            dck_ref[...] = jnp.zeros_like(dck_ref)

        @pl.when(kb == 0)
        def _():
            for hh in range(2):
                sl = _hsl(hh)
                oh = o_ref[:, sl].astype(f32)
                doh, dzh = _gate_bwd(du_ref[:, sl].astype(f32), oh, z_ref[:, sl])
                dz_ref[:, sl] = dzh.astype(dz_ref.dtype)
                do_s[hh] = doh.astype(bf16)
                delta_s[hh] = jnp.sum(jnp.transpose(doh * oh), axis=0, keepdims=True)
            dq_s[...] = jnp.zeros_like(dq_s)
            dcq_s[...] = jnp.zeros_like(dcq_s)

        def tile(diagonal):
            rows = pl.ds(pl.multiple_of(kb * T, T), T)
            for hh in range(2):
                sl = _hsl(hh)
                qh, kh, vh, dob = q_ref[:, sl], k_ref[:, sl], v_ref[:, sl], do_s[hh]
                s = _dot(kh, qh, NT) * scale + cq_ref[hh] - ck_ref[hh]
                p = jnp.exp(s - lse_ref[hh])
                if diagonal:
                    p = jnp.where(_iota2((T, T), 0) <= _iota2((T, T), 1), p, 0.0)
                ds = p * (_dot(vh, dob, NT) - delta_s[hh])
                dsb = ds.astype(bf16)
                dq_s[hh] += _dot(kh, dsb, TN)
                dk_ref[rows, sl] += _dot(dsb, qh) * scale
                dv_ref[rows, sl] += _dot(p.astype(bf16), dob)
                dcq_s[hh] += jnp.sum(ds, axis=0, keepdims=True)
                dck_ref[hh, rows, :] += jnp.sum(ds, axis=1, keepdims=True)

        @pl.when(kb < qb)
        def _():
            tile(False)

        @pl.when(kb == qb)
        def _():
            tile(True)
            for hh in range(2):
                dq_ref[:, _hsl(hh)] = (jnp.transpose(dq_s[hh]) * scale).astype(dq_ref.dtype)
                dcq_ref[hh] = dcq_s[hh]

    q, k, v, z = _pair_specs(S, BR, T, kmap)
    rowq = pl.BlockSpec((2, 1, T), lambda p, i, j: (p, 0, i))
    colk = pl.BlockSpec((2, T, 1), lambda p, i, j: (p, kmap(i, j), 0))
    ospec = pl.BlockSpec((T, 128), lambda p, i, j: (i, p))
    full = pl.BlockSpec((S, 128), lambda p, i, j: (0, p))
    return _pallas(
        body, (P, P, P, P, o, du, lse, cq, ck), name=name, grid=(H // 2, nq, nq),
        in_specs=[q, k, v, z, ospec, ospec, rowq, rowq, colk],
        out_specs=[ospec, ospec, full, full, rowq, pl.BlockSpec((2, S, 1), lambda p, i, j: (p, 0, 0))],
        out_shape=[jax.ShapeDtypeStruct((S, BR), bf16)] * 2 + [jax.ShapeDtypeStruct((S, BR), f32)] * 2
        + [jax.ShapeDtypeStruct((H, 1, S), f32), jax.ShapeDtypeStruct((H, S, 1), f32)],
        scratch_shapes=[pltpu.VMEM((2, T, HEAD_DIM), bf16), pltpu.VMEM((2, 1, T), f32),
                        pltpu.VMEM((2, HEAD_DIM, T), f32), pltpu.VMEM((2, 1, T), f32)],
        semantics=("parallel", "arbitrary", "arbitrary"), comm=comm)


def _sbT_logs(kc, qh, before, scale):
    l = _dot(kc, qh, NT) * scale
    minus_abs = lax.bitcast_convert_type(lax.bitcast_convert_type(l, jnp.int32) | jnp.int32(-2 ** 31), f32)
    lsig = jnp.minimum(l, 0.0) - jnp.log(1.0 + jnp.exp(minus_abs))
    lf = lsig - l
    if before is not None:
        lf = jnp.where(before, lf, 0.0)
    return lsig, lf


def _chunk_rows(c):
    return slice(c * BLOCK, (c + 1) * BLOCK)


def _sbT_suffix_tile(lf, tri2, nc):
    hi, lo = _split2(lf)
    return jnp.concatenate(
        [_dot(tri2, jnp.concatenate([hi[_chunk_rows(c)], lo[_chunk_rows(c)]], axis=0)) for c in range(nc)], axis=0)


def _sbT_suffix(lf, tri2):
    hi, lo = _split2(lf)
    return _dot(tri2, jnp.concatenate([hi, lo], axis=0))


def _sbT_fwd(P, BR, name, comm=None):
    S = P.shape[0]
    H = BR // HEAD_DIM
    T = _attn_tile(S)
    nq = S // T
    nc = T // BLOCK
    scale = HEAD_DIM ** -0.5
    kmap = lambda i, j: jnp.maximum(i - j, 0)

    def body(q_ref, k_ref, v_ref, z_ref, o_ref, u_ref, rc_ref, r_s, acc_s):
        qb, j = pl.program_id(1), pl.program_id(2)

        @pl.when(j == 0)
        def _():
            r_s[...] = jnp.zeros_like(r_s)
            acc_s[...] = jnp.zeros_like(acc_s)

        def tile(diagonal):
            ii, jj = _iota2((BLOCK, BLOCK), 0), _iota2((BLOCK, BLOCK), 1)
            tri = (jj > ii).astype(bf16)
            tri2 = jnp.concatenate([tri, tri], axis=1)
            before = (_iota2((T, T), 0) < _iota2((T, T), 1)) if diagonal else None
            for hh in range(2):
                sl = _hsl(hh)
                lsig, lf = _sbT_logs(k_ref[:, sl], q_ref[:, sl], before, scale)
                x = lsig + _sbT_suffix_tile(lf, tri2, nc)
                r = r_s[hh]
                rc_ref[hh, 0] = r
                parts = [None] * nc
                for c in reversed(range(nc)):
                    parts[c] = jnp.exp(x[_chunk_rows(c)] + r)
                    r = r + jnp.sum(lf[_chunk_rows(c)], axis=0, keepdims=True)
                r_s[hh] = r
                a = jnp.concatenate(parts, axis=0)
                if diagonal:
                    a = jnp.where(before, a, 0.0)
                acc_s[hh] += _dot(v_ref[:, sl], a.astype(bf16), TN)

        @pl.when(j > 0)
        def _():
            @pl.when(j <= qb)
            def _():
                tile(False)

        @pl.when(j == 0)
        def _():
            tile(True)

        @pl.when(j == qb)
        def _():
            for hh in range(2):
                sl = _hsl(hh)
                oh = jnp.transpose(acc_s[hh])
                o_ref[:, sl] = oh.astype(o_ref.dtype)
                u_ref[:, sl] = _gate_fwd(oh, z_ref[:, sl]).astype(u_ref.dtype)

    q, k, v, z = _pair_specs(S, BR, T, kmap)
    ospec = pl.BlockSpec((T, 128), lambda p, i, j: (i, p))
    rc = pl.BlockSpec((2, 1, 1, T), lambda p, i, j: (p, kmap(i, j), 0, i))
    return _pallas(
        body, (P, P, P, P), name=name, grid=(H // 2, nq, nq),
        in_specs=[q, k, v, z],
        out_specs=[ospec, ospec, rc],
        out_shape=[jax.ShapeDtypeStruct((S, BR), bf16)] * 2 + [jax.ShapeDtypeStruct((H, nq, 1, S), f32)],
        scratch_shapes=[pltpu.VMEM((2, 1, T), f32), pltpu.VMEM((2, HEAD_DIM, T), f32)],
        semantics=("parallel", "arbitrary", "arbitrary"), comm=comm)


def _sbT_bwd(P, o, du, rc, BR, name, comm=None):
    S = P.shape[0]
    H = BR // HEAD_DIM
    T = _attn_tile(S)
    nq = S // T
    nc = T // BLOCK
    scale = HEAD_DIM ** -0.5
    kmap = lambda i, j: jnp.minimum(i, j)

    def body(q_ref, k_ref, v_ref, z_ref, o_ref, du_ref, rc_ref,
             dq_ref, dz_ref, dk_ref, dv_ref, do_s, dq_s, g_s):
        qb, kb = pl.program_id(1), pl.program_id(2)

        @pl.when(jnp.logical_and(qb == 0, kb == 0))
        def _():
            dk_ref[...] = jnp.zeros_like(dk_ref)
            dv_ref[...] = jnp.zeros_like(dv_ref)

        @pl.when(kb == 0)
        def _():
            for hh in range(2):
                sl = _hsl(hh)
                doh, dzh = _gate_bwd(du_ref[:, sl].astype(f32), o_ref[:, sl].astype(f32), z_ref[:, sl])
                dz_ref[:, sl] = dzh.astype(dz_ref.dtype)
                do_s[hh] = doh.astype(bf16)
            dq_s[...] = jnp.zeros_like(dq_s)
            g_s[...] = jnp.zeros_like(g_s)

        def tile(diagonal):
            ii, jj = _iota2((BLOCK, BLOCK), 0), _iota2((BLOCK, BLOCK), 1)
            tri_suffix = (jj > ii).astype(bf16)
            tri2 = jnp.concatenate([tri_suffix, tri_suffix], axis=1)
            tri_prefix = (jj < ii).astype(bf16)
            before = (_iota2((T, T), 0) < _iota2((T, T), 1)) if diagonal else None
            out_rows = pl.ds(pl.multiple_of(kb * T, T), T)
            for hh in range(2):
                sl = _hsl(hh)
                qh, kh, dob = q_ref[:, sl], k_ref[:, sl], do_s[hh]
                lsig, lf = _sbT_logs(kh, qh, before, scale)
                x = lsig + _sbT_suffix_tile(lf, tri2, nc)
                r = rc_ref[hh, 0]
                parts = [None] * nc
                for c in reversed(range(nc)):
                    parts[c] = jnp.exp(x[_chunk_rows(c)] + r)
                    r = r + jnp.sum(lf[_chunk_rows(c)], axis=0, keepdims=True)
                a = jnp.concatenate(parts, axis=0)
                if diagonal:
                    a = jnp.where(before, a, 0.0)
                g = a * _dot(v_ref[:, sl], dob, NT)
                gb = g.astype(bf16)
                gsum = g_s[hh]
                for c in range(nc):
                    parts[c] = _dot(tri_prefix, gb[_chunk_rows(c)]) + gsum
                    gsum = gsum + jnp.sum(g[_chunk_rows(c)], axis=0, keepdims=True)
                g_s[hh] = gsum
                dl = g - (g + jnp.concatenate(parts, axis=0)) * jnp.exp(lsig)
                if diagonal:
                    dl = jnp.where(before, dl, 0.0)
                dl = dl.astype(bf16)
                dq_s[hh] += _dot(kh, dl, TN)
                dk_ref[out_rows, sl] += _dot(dl, qh) * scale
                dv_ref[out_rows, sl] += _dot(a.astype(bf16), dob)

        @pl.when(kb < qb)
        def _():
            tile(False)

        @pl.when(kb == qb)
        def _():
            tile(True)
            for hh in range(2):
                dq_ref[:, _hsl(hh)] = (jnp.transpose(dq_s[hh]) * scale).astype(dq_ref.dtype)

    q, k, v, z = _pair_specs(S, BR, T, kmap)
    ospec = pl.BlockSpec((T, 128), lambda p, i, j: (i, p))
    full = pl.BlockSpec((S, 128), lambda p, i, j: (0, p))
    rcs = pl.BlockSpec((2, 1, 1, T), lambda p, i, j: (p, kmap(i, j), 0, i))
    return _pallas(
        body, (P, P, P, P, o, du, rc), name=name, grid=(H // 2, nq, nq),
        in_specs=[q, k, v, z, ospec, ospec, rcs],
        out_specs=[ospec, ospec, full, full],
        out_shape=[jax.ShapeDtypeStruct((S, BR), bf16)] * 2 + [jax.ShapeDtypeStruct((S, BR), f32)] * 2,
        scratch_shapes=[pltpu.VMEM((2, T, HEAD_DIM), bf16), pltpu.VMEM((2, HEAD_DIM, T), f32),
                        pltpu.VMEM((2, 1, T), f32)],
        semantics=("parallel", "arbitrary", "arbitrary"), comm=comm)


def _adam_math(w, g, m, v):
    m = ADAM_B1 * m + (1.0 - ADAM_B1) * g
    v = ADAM_B2 * v + (1.0 - ADAM_B2) * (g * g)
    m_hat = m / (1.0 - ADAM_B1 ** ADAM_STEP)
    v_hat = v / (1.0 - ADAM_B2 ** ADAM_STEP)
    delta = -ADAM_LR * (m_hat / (jnp.sqrt(v_hat) + ADAM_EPS) + ADAM_WD * w)
    return delta, m, v


def _adamw_sum(parts, w, m, v, name):
    R, C = w.shape
    tr = _pick(R, (256, 128))

    def body(p_ref, w_ref, m_ref, v_ref, g_ref, d_ref, nm_ref, nv_ref):
        g = p_ref[0].astype(f32)
        for i in range(1, N_DEV):
            g = g + p_ref[i].astype(f32)
        g_ref[...] = g
        d_ref[...], nm_ref[...], nv_ref[...] = _adam_math(w_ref[...], g, m_ref[...], v_ref[...])

    blk = pl.BlockSpec((tr, C), lambda i: (i, 0))
    return pl.pallas_call(
        body, name=name, grid=(R // tr,),
        in_specs=[pl.BlockSpec((N_DEV, tr, C), lambda i: (0, i, 0)), blk, blk, blk],
        out_specs=[blk] * 4,
        out_shape=[jax.ShapeDtypeStruct((R, C), f32)] * 4,
        compiler_params=_params(("parallel",)),
    )(parts, w, m, v)


def _coords():
    return lax.axis_index("x"), lax.axis_index("y"), lax.axis_index("c")


class _Gather:
    def __init__(self, shards):
        self.args = list(shards)
        self.n = len(shards)
        self.out_shape = [jax.ShapeDtypeStruct((N_DEV,) + s.shape, s.dtype) for s in shards]
        self.scratch = [pltpu.SemaphoreType.DMA((self.n, 7)), pltpu.SemaphoreType.DMA((self.n, 7)),
                        pltpu.SemaphoreType.DMA((self.n,))]

    def _ctx(self, ins, outs, sems):
        send_sems, recv_sems, local_sems = sems
        x, y, c = _coords()
        me, sibling = (x, y, c), (x, y, 1 - c)
        chips = [(1 - x, y), (x, 1 - y), (1 - x, 1 - y)]

        def slot(out, dev):
            return out.at[4 * dev[0] + 2 * dev[1] + dev[2]]

        def copy(a, k, block, to, src=None):
            return pltpu.make_async_remote_copy(
                src_ref=slot(outs[a], block) if src is None else src, dst_ref=slot(outs[a], block),
                send_sem=send_sems.at[a, k], recv_sem=recv_sems.at[a, k], device_id=to, device_id_type=MESH)

        mine = [pltpu.make_async_copy(ins[a], slot(outs[a], me), local_sems.at[a]) for a in range(self.n)]
        first = []
        for a in range(self.n):
            first.append(copy(a, 0, me, sibling, src=ins[a]))
            first += [copy(a, 1 + j, me, (*chip, c), src=ins[a]) for j, chip in enumerate(chips)]
        passed = [copy(a, 4 + j, (*chip, c), sibling) for j, chip in enumerate(chips) for a in range(self.n)]
        return c, me, sibling, chips, copy, mine, first, passed

    def start(self, ins, outs, sems):
        *_, mine, first, _ = self._ctx(ins, outs, sems)
        for cp in mine + first:
            cp.start()

    def mid(self, ins, outs, sems):
        c, me, _, chips, copy, _, _, passed = self._ctx(ins, outs, sems)
        i = 0
        for j, chip in enumerate(chips):
            for a in range(self.n):
                copy(a, 1 + j, (*chip, c), me).wait_recv()
                passed[i].start()
                i += 1

    def finish(self, ins, outs, sems):
        c, me, sibling, chips, copy, mine, first, passed = self._ctx(ins, outs, sems)
        for a in range(self.n):
            copy(a, 0, sibling, me).wait_recv()
            for j, chip in enumerate(chips):
                copy(a, 4 + j, (*chip, 1 - c), me).wait_recv()
        for cp in first + passed:
            cp.wait_send()
        for cp in mine:
            cp.wait()


class _Scatter:
    mid = None

    def __init__(self, blocks):
        self.args = list(blocks)
        self.n = len(blocks)
        self.out_shape = [jax.ShapeDtypeStruct(b.shape, b.dtype) for b in blocks]
        self.scratch = [pltpu.SemaphoreType.DMA((self.n, 7)), pltpu.SemaphoreType.DMA((self.n, 7)),
                        pltpu.SemaphoreType.DMA((self.n,))]

    def _ctx(self, ins, outs, sems):
        send_sems, recv_sems, local_sems = sems
        x, y, c = _coords()
        me = 4 * x + 2 * y + c
        mine = [pltpu.make_async_copy(ins[a].at[me], outs[a].at[me], local_sems.at[a]) for a in range(self.n)]

        def copy(a, k, landing_here):
            px, py, pc = x ^ ((k >> 2) & 1), y ^ ((k >> 1) & 1), c ^ (k & 1)
            them = 4 * px + 2 * py + pc
            return pltpu.make_async_remote_copy(
                src_ref=ins[a].at[them], dst_ref=outs[a].at[them if landing_here else me],
                send_sem=send_sems.at[a, k - 1], recv_sem=recv_sems.at[a, k - 1],
                device_id=(px, py, pc), device_id_type=MESH)

        sent = [copy(a, k, False) for k in range(1, N_DEV) for a in range(self.n)]
        arrivals = [copy(a, k, True) for k in range(1, N_DEV) for a in range(self.n)]
        return mine, sent, arrivals

    def start(self, ins, outs, sems):
        mine, sent, _ = self._ctx(ins, outs, sems)
        for cp in mine + sent:
            cp.start()

    def finish(self, ins, outs, sems):
        mine, sent, arrivals = self._ctx(ins, outs, sems)
        for cp in arrivals:
            cp.wait_recv()
        for cp in sent:
            cp.wait_send()
        for cp in mine:
            cp.wait()


def _run_comm(comm, name):
    n = comm.n

    def body(*refs):
        ins, outs, sems = refs[:n], refs[n:2 * n], refs[2 * n:]
        comm.start(ins, outs, sems)
        if comm.mid is not None:
            comm.mid(ins, outs, sems)
        comm.finish(ins, outs, sems)

    return pl.pallas_call(
        body, name=name, in_specs=[ANY] * n, out_specs=[ANY] * n,
        out_shape=comm.out_shape, scratch_shapes=comm.scratch,
    )(*comm.args)


def _pallas(body, args, *, name, grid, in_specs, out_specs, out_shape, scratch_shapes=(), semantics, comm=None):
    in_specs, out_specs, out_shape, scratch_shapes = list(in_specs), list(out_specs), list(out_shape), list(scratch_shapes)
    if comm is None:
        return pl.pallas_call(
            body, name=name, grid=grid, in_specs=in_specs, out_specs=out_specs, out_shape=out_shape,
            scratch_shapes=scratch_shapes, compiler_params=_params(semantics))(*args)
    a = len(in_specs)
    b = a + comm.n
    c = b + len(out_specs)
    d = c + comm.n
    e = d + len(scratch_shapes)
    total = math.prod(grid)
    mid_step = (3 * total) // 4

    def hosted(*refs):
        step = pl.program_id(0)
        for axis in range(1, len(grid)):
            step = step * grid[axis] + pl.program_id(axis)
        ins, outs, sems = refs[a:b], refs[c:d], refs[e:]

        @pl.when(step == 0)
        def _():
            comm.start(ins, outs, sems)

        body(*refs[:a], *refs[b:c], *refs[d:e])

        if comm.mid is not None:
            @pl.when(step == mid_step)
            def _():
                comm.mid(ins, outs, sems)

        @pl.when(step == total - 1)
        def _():
            comm.finish(ins, outs, sems)

    res = pl.pallas_call(
        hosted, name=name, grid=grid, in_specs=in_specs + [ANY] * comm.n, out_specs=out_specs + [ANY] * comm.n,
        out_shape=out_shape + comm.out_shape, scratch_shapes=scratch_shapes + comm.scratch,
        compiler_params=_params(("arbitrary",) * len(grid)))(*args, *comm.args)
    return list(res[:len(out_specs)]) + [list(res[len(out_specs):])]


def _all_reduce_small(vec, name):
    R, C = vec.shape

    def body(v_ref, o_ref, gath, send_sems, recv_sems):
        x, y, c = _coords()
        me = 4 * x + 2 * y + c
        gath[me] = v_ref[...]

        def copy(k):
            px, py, pc = x ^ ((k >> 2) & 1), y ^ ((k >> 1) & 1), c ^ (k & 1)
            return pltpu.make_async_remote_copy(
                src_ref=v_ref, dst_ref=gath.at[me], send_sem=send_sems.at[k - 1], recv_sem=recv_sems.at[k - 1],
                device_id=(px, py, pc), device_id_type=MESH)

        sent = [copy(k) for k in range(1, N_DEV)]
        for cp in sent:
            cp.start()
        for cp in sent:
            cp.wait()
        total = gath[0]
        for i in range(1, N_DEV):
            total = total + gath[i]
        o_ref[...] = total

    return pl.pallas_call(
        body, name=name,
        in_specs=[pl.BlockSpec(memory_space=pltpu.VMEM)], out_specs=pl.BlockSpec(memory_space=pltpu.VMEM),
        out_shape=jax.ShapeDtypeStruct((R, C), f32),
        scratch_shapes=[pltpu.VMEM((N_DEV, R, C), f32), pltpu.SemaphoreType.DMA((7,)), pltpu.SemaphoreType.DMA((7,))],
    )(vec)


def _gathered_in(g):
    return jnp.transpose(g, (1, 0, 2)).reshape(g.shape[1], -1)


def _col_blocks(dw):
    D, N = dw.shape
    return jnp.transpose(dw.reshape(D, N_DEV, N // N_DEV), (1, 0, 2)).astype(bf16)


def _alibi_slopes(n):
    return jnp.asarray(2.0 ** (-8.0 * np.arange(1, n + 1, dtype=np.float32) / n), dtype=f32)


def _heads_major(a, n):
    return jnp.transpose(a.reshape(a.shape[0], n, HEAD_DIM), (1, 0, 2))


def _heads_minor(a):
    return jnp.transpose(a, (1, 0, 2)).reshape(a.shape[1], -1)


def kernel(x, g_pre, g_post, w_in_a, w_out_a, sinks_a, w_in_b, w_out_b, w_in_c, b_f_c, w_out_c, loss_target, m_g_pre, m_g_post, m_w_in_a, m_w_out_a, m_sinks_a, m_w_in_b, m_w_out_b, m_w_in_c, m_b_f_c, m_w_out_c, v_g_pre, v_g_post, v_w_in_a, v_w_out_a, v_sinks_a, v_w_in_b, v_w_out_b, v_w_in_c, v_b_f_c, v_w_out_c):
    S, D = x.shape[1], x.shape[2]
    H = D // HEAD_DIM
    BR = H * HEAD_DIM
    n_kv = H // 8
    KV = n_kv * HEAD_DIM
    x0 = x[0]
    target = loss_target[0]
    slopes = _alibi_slopes(H)
    w_in = {0: w_in_a, 1: w_in_b, 2: w_in_c}
    w_out = {0: w_out_a, 1: w_out_b, 2: w_out_c}

    saved = []
    xi = x0
    def weight_gather(i):
        kind, j = i % 3, i // 3
        return _Gather([w_in[kind][j].astype(bf16), w_out[kind][j].astype(bf16)])

    gathered = _run_comm(weight_gather(0), name="gather_w0")
    for i in range(DEPTH):
        kind, j = i % 3, i // 3
        win_g, wout_g = gathered
        nxt = weight_gather(i + 1) if i + 1 < DEPTH else None
        W_in = _gathered_in(win_g)
        W_out = wout_g.reshape(BR, D)
        h = _rmsnorm_fwd(xi, g_pre[i:i + 1], name=f"pre_norm{i}")
        st = dict(x=xi, h=h, W_out=W_out)
        if kind == 2:
            n_main = 4 * BR
            pad = (-W_in.shape[1]) % 128
            W_in = jnp.pad(W_in, ((0, 0), (0, pad)))
            W_f = W_in[:, n_main:n_main + 128]
            fl = _matmul(h, W_f, "nn", f32, name=f"f_proj{i}")
            flT = jnp.transpose(fl[:, :H])
            bcol = b_f_c[j].reshape(H, 1)
            cumT = _fox_cum(flT, bcol, name=f"fox_cum{i}")
            cq, ck = cumT[:, None, :], cumT[:, :, None]
            st.update(flT=flT, bcol=bcol, cq=cq, ck=ck)
        st["W_in"] = W_in
        P = _matmul(h, W_in, "nn", bf16, name=f"in_proj{i}")
        st["P"] = P
        if kind == 0:
            kh = _heads_major(P[:, BR:BR + KV], n_kv)
            vh = _heads_major(P[:, BR + KV:BR + 2 * KV], n_kv)
            o, u, *rest = _swa_fwd(P, kh, vh, sinks_a[j], slopes, name=f"swa_fwd{i}", comm=nxt)
            st.update(kh=kh, vh=vh)
        elif kind == 1:
            o, u, rc, *rest = _sbT_fwd(P, BR, name=f"sb_fwd{i}", comm=nxt)
            st.update(rc=rc)
        else:
            o, u, lse, *rest = _foxT_fwd(P, cq, ck, BR, name=f"fox_fwd{i}", comm=nxt)
            st.update(lse=lse)
        if nxt is not None:
            gathered = rest[0]
        st.update(o=o, u=u)
        y = _matmul(u, W_out, "nn", f32, name=f"out_proj{i}")
        st["y"] = y
        xi = _post_fwd(xi, y, g_post[i:i + 1], name=f"post_norm{i}")
        saved.append(st)

    loss_part, dx = _loss_fwd_bwd(xi, target, name="loss")

    dg_pre, dg_post = [None] * DEPTH, [None] * DEPTH
    dsinks = [None, None]
    db_f = None
    recv = [None] * DEPTH
    pending = None
    for i in reversed(range(DEPTH)):
        kind, j = i % 3, i // 3
        st = saved[i]
        dy, dg_post[i] = _post_bwd(dx, st["y"], g_post[i:i + 1], name=f"post_bwd{i}")
        du = _matmul(dy, st["W_out"], "nt", bf16, name=f"du{i}")
        dW_out = _matmul(st["u"], dy, "tn", f32, name=f"dw_out{i}")
        P = st["P"]
        if kind == 0:
            dq, dz, dkh, dvh, dsk, *rest = _swa_bwd(P, st["kh"], st["vh"], st["o"], du, sinks_a[j], slopes,
                                                    name=f"swa_bwd{i}", comm=pending)
            dsinks[j] = dsk[:, 0, :H // n_kv].reshape(H)
            dP = jnp.concatenate([dq, _heads_minor(dkh).astype(bf16), _heads_minor(dvh).astype(bf16), dz], axis=1)
        elif kind == 1:
            dq, dz, dk, dv, *rest = _sbT_bwd(P, st["o"], du, st["rc"], BR, name=f"sb_bwd{i}", comm=pending)
            dP = jnp.concatenate([dq, dk.astype(bf16), dv.astype(bf16), dz], axis=1)
        else:
            dq, dz, dk, dv, dcq, dck, *rest = _foxT_bwd(P, st["o"], du, st["lse"], st["cq"], st["ck"], BR,
                                                       name=f"fox_bwd{i}", comm=pending)
        if pending is not None:
            recv[i + 1] = rest[0]
        if kind == 2:
            dflT, db_col = _fox_cum_bwd(dcq.reshape(H, S), dck.reshape(H, S), st["flT"], st["bcol"],
                                        name=f"fox_cum_bwd{i}")
            db_f = db_col.reshape(H)
            dfl = jnp.pad(jnp.transpose(dflT), ((0, 0), (0, 128 - H))).astype(bf16)
            dP = jnp.concatenate([dq, dk.astype(bf16), dv.astype(bf16), dz, dfl], axis=1)
        dh = _matmul(dP, st["W_in"], "nt", f32, name=f"dh{i}")
        dW_in = _matmul(st["h"], dP, "tn", f32, name=f"dw_in{i}")
        n_cols = w_in[kind].shape[2] * N_DEV
        pending = _Scatter([_col_blocks(dW_in[:, :n_cols]), dW_out.astype(bf16).reshape(N_DEV, BR // N_DEV, D)])
        dx, dg_pre[i] = _pre_bwd(dh, st["x"], g_pre[i:i + 1], dx, name=f"pre_bwd{i}")
    recv[0] = _run_comm(pending, name="scatter_dw0")

    small = jnp.concatenate(
        [jnp.concatenate(dg_pre, axis=0).reshape(-1), jnp.concatenate(dg_post, axis=0).reshape(-1),
         jnp.concatenate(dsinks), db_f, loss_part[0, :1]])
    n_small = small.shape[0]
    rows = -(-n_small // 128)
    rows = -(-rows // 8) * 8
    small = jnp.pad(small, (0, rows * 128 - n_small)).reshape(rows, 128)
    total = _all_reduce_small(small, name="reduce_small").reshape(-1)
    o0 = DEPTH * D
    grad_g_pre = total[:o0].reshape(DEPTH, D)
    grad_g_post = total[o0:2 * o0].reshape(DEPTH, D)
    grad_sinks = total[2 * o0:2 * o0 + 2 * H].reshape(2, H)
    grad_b_f = total[2 * o0 + 2 * H:2 * o0 + 3 * H].reshape(1, H)
    loss = total[2 * o0 + 3 * H]

    def small_adam(w, g, m, v, name):
        def body(w_ref, g_ref, m_ref, v_ref, d_ref, nm_ref, nv_ref):
            d_ref[...], nm_ref[...], nv_ref[...] = _adam_math(w_ref[...], g_ref[...], m_ref[...], v_ref[...])
        vm = pl.BlockSpec(memory_space=pltpu.VMEM)
        return pl.pallas_call(body, name=name, in_specs=[vm] * 4, out_specs=[vm] * 3,
                              out_shape=[jax.ShapeDtypeStruct(w.shape, f32)] * 3)(w, g, m, v)

    upd = {}
    upd["g_pre"] = (grad_g_pre,) + tuple(small_adam(g_pre, grad_g_pre, m_g_pre, v_g_pre, "adam_g_pre"))
    upd["g_post"] = (grad_g_post,) + tuple(small_adam(g_post, grad_g_post, m_g_post, v_g_post, "adam_g_post"))
    upd["sinks_a"] = (grad_sinks,) + tuple(small_adam(sinks_a, grad_sinks, m_sinks_a, v_sinks_a, "adam_sinks"))
    upd["b_f_c"] = (grad_b_f,) + tuple(small_adam(b_f_c, grad_b_f, m_b_f_c, v_b_f_c, "adam_b_f"))

    def big(i, which, w, m, v):
        return _adamw_sum(recv[i][which], w, m, v, name=f"adam_{'in' if which == 0 else 'out'}{i}")

    a_in = [big(i, 0, w_in_a[jj], m_w_in_a[jj], v_w_in_a[jj]) for jj, i in enumerate((0, 3))]
    a_out = [big(i, 1, w_out_a[jj], m_w_out_a[jj], v_w_out_a[jj]) for jj, i in enumerate((0, 3))]
    upd["w_in_a"] = tuple(jnp.stack([a_in[0][t], a_in[1][t]]) for t in range(4))
    upd["w_out_a"] = tuple(jnp.stack([a_out[0][t], a_out[1][t]]) for t in range(4))
    upd["w_in_b"] = tuple(t[None] for t in big(1, 0, w_in_b[0], m_w_in_b[0], v_w_in_b[0]))
    upd["w_out_b"] = tuple(t[None] for t in big(1, 1, w_out_b[0], m_w_out_b[0], v_w_out_b[0]))
    upd["w_in_c"] = tuple(t[None] for t in big(2, 0, w_in_c[0], m_w_in_c[0], v_w_in_c[0]))
    upd["w_out_c"] = tuple(t[None] for t in big(2, 1, w_out_c[0], m_w_out_c[0], v_w_out_c[0]))

    names = ["g_pre", "g_post", "w_in_a", "w_out_a", "sinks_a", "w_in_b", "w_out_b", "w_in_c", "b_f_c", "w_out_c"]
    return (loss, dx[None], *[upd[k][0] for k in names], *[upd[k][1] for k in names],
            *[upd[k][2] for k in names], *[upd[k][3] for k in names])
```

```python
import functools
import math

import numpy as np
import jax
import jax.numpy as jnp
from jax import lax
from jax.experimental import pallas as pl
from jax.experimental.pallas import tpu as pltpu

HEAD_DIM = 64
BLOCK = 128
NORM_EPS = 1e-6
NEG = -1e30
N_DEV = 8
DEPTH = 4
ADAM_LR, ADAM_B1, ADAM_B2, ADAM_EPS, ADAM_WD, ADAM_STEP = 0.001, 0.9, 0.999, 1e-8, 0.01, 10
VMEM_LIMIT = 56 * 1024 * 1024

bf16 = jnp.bfloat16
f32 = jnp.float32
MESH = pl.DeviceIdType.MESH
ANY = pl.BlockSpec(memory_space=pl.ANY)
SMEM = pl.BlockSpec(memory_space=pltpu.SMEM)

NN = (((1,), (0,)), ((), ()))
NT = (((1,), (1,)), ((), ()))
TN = (((0,), (0,)), ((), ()))


def _dot(a, b, dims=NN):
    return lax.dot_general(a, b, dims, preferred_element_type=f32)


def _params(sem):
    return pltpu.CompilerParams(dimension_semantics=sem, vmem_limit_bytes=VMEM_LIMIT)


def _attn_tile(S):
    return 512 if S % 512 == 0 and S >= 1024 else 128


def _pick(n, pref):
    for t in pref:
        if n % t == 0:
            return t
    return n


def _matmul(a, b, mode, out_dtype, name, comm=None):
    if mode == "nn":
        (M, K), (K2, N) = a.shape, b.shape
    elif mode == "nt":
        (M, K), (N, K2) = a.shape, b.shape
    else:
        (K, M), (K2, N) = a.shape, b.shape
    assert K == K2, (a.shape, b.shape, mode)
    tm = _pick(M, (1024, 512, 256, 128))
    tn = _pick(N, (1024, 768, 640, 512, 256, 128))
    tk = _pick(K, (2048, 1664, 1536, 1024, 512, 640, 256, 128))
    nk = K // tk
    dims = {"nn": NN, "nt": NT, "tn": TN}[mode]

    def body(a_ref, b_ref, o_ref, acc_ref):
        if nk == 1:
            o_ref[...] = _dot(a_ref[...], b_ref[...], dims).astype(o_ref.dtype)
            return
        k = pl.program_id(2)

        @pl.when(k == 0)
        def _():
            acc_ref[...] = jnp.zeros_like(acc_ref)

        acc_ref[...] += _dot(a_ref[...], b_ref[...], dims)

        @pl.when(k == nk - 1)
        def _():
            o_ref[...] = acc_ref[...].astype(o_ref.dtype)

    if mode == "tn":
        a_spec = pl.BlockSpec((tk, tm), lambda i, j, k: (k, i))
    else:
        a_spec = pl.BlockSpec((tm, tk), lambda i, j, k: (i, k))
    if mode == "nt":
        b_spec = pl.BlockSpec((tn, tk), lambda i, j, k: (j, k))
    else:
        b_spec = pl.BlockSpec((tk, tn), lambda i, j, k: (k, j))
    res = _pallas(
        body, (a, b), name=name,
        grid=(M // tm, N // tn, nk),
        in_specs=[a_spec, b_spec],
        out_specs=[pl.BlockSpec((tm, tn), lambda i, j, k: (i, j))],
        out_shape=[jax.ShapeDtypeStruct((M, N), out_dtype)],
        scratch_shapes=[pltpu.VMEM((tm, tn) if nk > 1 else (8, 128), f32)],
        semantics=("parallel", "parallel", "arbitrary"), comm=comm)
    return res[0] if comm is None else (res[0], res[1])


ROWS = 256


def _rows(S):
    return ROWS if S % ROWS == 0 else S


def _rmsnorm_fwd(x, g, name):
    S, D = x.shape
    tr = _rows(S)

    def body(x_ref, g_ref, h_ref):
        xv = x_ref[...]
        r = lax.rsqrt(jnp.mean(xv * xv, axis=-1, keepdims=True) + NORM_EPS)
        h_ref[...] = (xv * r * g_ref[...]).astype(h_ref.dtype)

    return pl.pallas_call(
        body, name=name, grid=(S // tr,),
        in_specs=[pl.BlockSpec((tr, D), lambda i: (i, 0)), pl.BlockSpec((1, D), lambda i: (0, 0))],
        out_specs=pl.BlockSpec((tr, D), lambda i: (i, 0)),
        out_shape=jax.ShapeDtypeStruct((S, D), bf16),
        compiler_params=_params(("parallel",)),
    )(x, g)


def _post_fwd(x, y, g, name):
    S, D = x.shape
    tr = _rows(S)

    def body(x_ref, y_ref, g_ref, o_ref):
        yv = y_ref[...]
        r = lax.rsqrt(jnp.mean(yv * yv, axis=-1, keepdims=True) + NORM_EPS)
        o_ref[...] = x_ref[...] + yv * r * g_ref[...]

    row = pl.BlockSpec((tr, D), lambda i: (i, 0))
    return pl.pallas_call(
        body, name=name, grid=(S // tr,),
        in_specs=[row, row, pl.BlockSpec((1, D), lambda i: (0, 0))],
        out_specs=row,
        out_shape=jax.ShapeDtypeStruct((S, D), f32),
        compiler_params=_params(("parallel",)),
    )(x, y, g)


def _loss_fwd_bwd(y, t, name):
    S, D = y.shape
    tr = _rows(S)

    def body(y_ref, t_ref, l_ref, d_ref):
        @pl.when(pl.program_id(0) == 0)
        def _():
            l_ref[...] = jnp.zeros_like(l_ref)

        e = y_ref[...] - t_ref[...]
        d_ref[...] = e * (1.0 / D)
        part = 0.5 * jnp.sum(jnp.sum(e * e, axis=-1, keepdims=True) * (1.0 / D), axis=0, keepdims=True)
        l_ref[...] += jnp.broadcast_to(part, l_ref.shape)

    row = pl.BlockSpec((tr, D), lambda i: (i, 0))
    return pl.pallas_call(
        body, name=name, grid=(S // tr,),
        in_specs=[row, row],
        out_specs=[pl.BlockSpec((8, 128), lambda i: (0, 0)), row],
        out_shape=[jax.ShapeDtypeStruct((8, 128), f32), jax.ShapeDtypeStruct((S, D), f32)],
        compiler_params=_params(("arbitrary",)),
    )(y, t)


def _post_bwd(dxn, y, g, name):
    S, D = y.shape
    tr = _rows(S)

    def body(d_ref, y_ref, g_ref, dy_ref, dg_ref):
        @pl.when(pl.program_id(0) == 0)
        def _():
            dg_ref[...] = jnp.zeros_like(dg_ref)

        yv = y_ref[...]
        d = d_ref[...]
        r = lax.rsqrt(jnp.mean(yv * yv, axis=-1, keepdims=True) + NORM_EPS)
        n = yv * r
        dn = d * g_ref[...]
        dg_ref[...] += jnp.sum(d * n, axis=0, keepdims=True)
        dy_ref[...] = (r * (dn - n * jnp.mean(dn * n, axis=-1, keepdims=True))).astype(dy_ref.dtype)

    row = pl.BlockSpec((tr, D), lambda i: (i, 0))
    vec = pl.BlockSpec((1, D), lambda i: (0, 0))
    return pl.pallas_call(
        body, name=name, grid=(S // tr,),
        in_specs=[row, row, vec],
        out_specs=[row, vec],
        out_shape=[jax.ShapeDtypeStruct((S, D), bf16), jax.ShapeDtypeStruct((1, D), f32)],
        compiler_params=_params(("arbitrary",)),
    )(dxn, y, g)


def _pre_bwd(dh, x, g, dres, name, comm=None):
    S, D = x.shape
    tr = _rows(S)

    def body(dh_ref, x_ref, g_ref, dres_ref, dx_ref, dg_ref):
        @pl.when(pl.program_id(0) == 0)
        def _():
            dg_ref[...] = jnp.zeros_like(dg_ref)

        xv = x_ref[...]
        d = dh_ref[...]
        r = lax.rsqrt(jnp.mean(xv * xv, axis=-1, keepdims=True) + NORM_EPS)
        n = xv * r
        dn = d * g_ref[...]
        dg_ref[...] += jnp.sum(d * n, axis=0, keepdims=True)
        dx_ref[...] = dres_ref[...] + r * (dn - n * jnp.mean(dn * n, axis=-1, keepdims=True))

    row = pl.BlockSpec((tr, D), lambda i: (i, 0))
    vec = pl.BlockSpec((1, D), lambda i: (0, 0))
    return _pallas(
        body, (dh, x, g, dres), name=name, grid=(S // tr,),
        in_specs=[row, row, vec, row],
        out_specs=[row, vec],
        out_shape=[jax.ShapeDtypeStruct((S, D), f32), jax.ShapeDtypeStruct((1, D), f32)],
        semantics=("arbitrary",), comm=comm)


def _sigmoid(x):
    return 1.0 / (1.0 + jnp.exp(-x))


def _gate_fwd(o, z):
    zf = z.astype(f32)
    return o * (zf * _sigmoid(zf))


def _gate_bwd(du, o, z):
    zf = z.astype(f32)
    sig = _sigmoid(zf)
    do = du * (zf * sig)
    dz = du * o * (sig * (1.0 + zf * (1.0 - sig)))
    return do, dz


def _iota2(shape, dim):
    return lax.broadcasted_iota(jnp.int32, shape, dim)


def _swa_scores(qh, kc, kp, slope, sink, n, scale):
    qi = _iota2((BLOCK, BLOCK), 0)
    kj = _iota2((BLOCK, BLOCK), 1)
    mask_c = kj <= qi
    mask_p = kj > qi + jnp.where(n > 0, 0, BLOCK)
    dist_c = (qi - kj).astype(f32)
    dist_p = (qi + BLOCK - kj).astype(f32)
    sc = jnp.where(mask_c, _dot(qh, kc, NT) * scale - slope * dist_c, NEG)
    sp = jnp.where(mask_p, _dot(qh, kp, NT) * scale - slope * dist_p, NEG)
    m = jnp.maximum(jnp.maximum(jnp.max(sc, axis=-1, keepdims=True), jnp.max(sp, axis=-1, keepdims=True)), sink)
    pc = jnp.exp(sc - m)
    pp = jnp.exp(sp - m)
    ps = jnp.exp(sink - m)
    den = jnp.sum(pc, axis=-1, keepdims=True) + jnp.sum(pp, axis=-1, keepdims=True) + ps
    return pc, pp, ps, den


def _swa_specs(S, BR, KV, G):
    gw = G * HEAD_DIM
    qspec = pl.BlockSpec((BLOCK, gw), lambda h, n: (n, h))
    zoff = (BR + 2 * KV) // gw
    zspec = pl.BlockSpec((BLOCK, gw), lambda h, n: (n, zoff + h))
    cur = pl.BlockSpec((1, BLOCK, HEAD_DIM), lambda h, n: (h, n, 0))
    prev = pl.BlockSpec((1, BLOCK, HEAD_DIM), lambda h, n: (h, jnp.maximum(n - 1, 0), 0))
    return qspec, zspec, cur, prev


def _swa_fwd(P, kh, vh, sinks, slopes, name, comm=None):
    S = P.shape[0]
    n_kv = kh.shape[0]
    H = sinks.shape[0]
    G = H // n_kv
    BR, KV = H * HEAD_DIM, n_kv * HEAD_DIM
    scale = HEAD_DIM ** -0.5
    assert (BR + 2 * KV) % (G * HEAD_DIM) == 0

    def body(q_ref, z_ref, kc_ref, kp_ref, vc_ref, vp_ref, sink_ref, slope_ref, o_ref, u_ref):
        kvh, n = pl.program_id(0), pl.program_id(1)
        kc, kp, vc, vp = kc_ref[0], kp_ref[0], vc_ref[0], vp_ref[0]
        for g in range(G):
            sl = slice(g * HEAD_DIM, (g + 1) * HEAD_DIM)
            h = kvh * G + g
            pc, pp, _, den = _swa_scores(q_ref[:, sl], kc, kp, slope_ref[h], sink_ref[h], n, scale)
            oh = (_dot(pc.astype(bf16), vc) + _dot(pp.astype(bf16), vp)) / den
            o_ref[:, sl] = oh.astype(o_ref.dtype)
            u_ref[:, sl] = _gate_fwd(oh, z_ref[:, sl]).astype(u_ref.dtype)

    qspec, zspec, cur, prev = _swa_specs(S, BR, KV, G)
    ospec = pl.BlockSpec((BLOCK, G * HEAD_DIM), lambda h, n: (n, h))
    return _pallas(
        body, (P, P, kh, kh, vh, vh, sinks, slopes), name=name, grid=(n_kv, S // BLOCK),
        in_specs=[qspec, zspec, cur, prev, cur, prev, SMEM, SMEM],
        out_specs=[ospec, ospec],
        out_shape=[jax.ShapeDtypeStruct((S, BR), bf16)] * 2,
        semantics=("parallel", "parallel"), comm=comm)


def _swa_bwd(P, kh, vh, o, du, sinks, slopes, name, comm=None):
    S = P.shape[0]
    n_kv = kh.shape[0]
    H = sinks.shape[0]
    G = H // n_kv
    BR, KV = H * HEAD_DIM, n_kv * HEAD_DIM
    scale = HEAD_DIM ** -0.5

    def body(q_ref, z_ref, kc_ref, kp_ref, vc_ref, vp_ref, o_ref, du_ref, sink_ref, slope_ref,
             dq_ref, dz_ref, dk_ref, dv_ref, ds_ref):
        kvh, n = pl.program_id(0), pl.program_id(1)

        @pl.when(n == 0)
        def _():
            dk_ref[...] = jnp.zeros_like(dk_ref)
            dv_ref[...] = jnp.zeros_like(dv_ref)
            ds_ref[...] = jnp.zeros_like(ds_ref)

        kc, kp, vc, vp = kc_ref[0], kp_ref[0], vc_ref[0], vp_ref[0]
        dkc = jnp.zeros((BLOCK, HEAD_DIM), f32)
        dkp = jnp.zeros((BLOCK, HEAD_DIM), f32)
        dvc = jnp.zeros((BLOCK, HEAD_DIM), f32)
        dvp = jnp.zeros((BLOCK, HEAD_DIM), f32)
        dsink = jnp.zeros((8, 128), f32)
        lane = _iota2((8, 128), 1)
        for g in range(G):
            sl = slice(g * HEAD_DIM, (g + 1) * HEAD_DIM)
            h = kvh * G + g
            qh = q_ref[:, sl]
            pc, pp, ps, den = _swa_scores(qh, kc, kp, slope_ref[h], sink_ref[h], n, scale)
            inv = 1.0 / den
            pc, pp, ps = pc * inv, pp * inv, ps * inv
            oh = o_ref[:, sl].astype(f32)
            doh, dzh = _gate_bwd(du_ref[:, sl].astype(f32), oh, z_ref[:, sl])
            dz_ref[:, sl] = dzh.astype(dz_ref.dtype)
            delta = jnp.sum(doh * oh, axis=-1, keepdims=True)
            dob = doh.astype(bf16)
            dsc = (pc * (_dot(dob, vc, NT) - delta)).astype(bf16)
            dsp = (pp * (_dot(dob, vp, NT) - delta)).astype(bf16)
            dq_ref[:, sl] = ((_dot(dsc, kc) + _dot(dsp, kp)) * scale).astype(dq_ref.dtype)
            dkc += _dot(dsc, qh, TN)
            dkp += _dot(dsp, qh, TN)
            dvc += _dot(pc.astype(bf16), dob, TN)
            dvp += _dot(pp.astype(bf16), dob, TN)
            dsh = -jnp.sum(ps * delta, axis=0, keepdims=True)
            dsink += jnp.where(lane == g, jnp.broadcast_to(dsh, (8, 128)), 0.0)
        ds_ref[0] += dsink
        row_c = pl.multiple_of(n * BLOCK, BLOCK)
        dk_ref[0, pl.ds(row_c, BLOCK), :] += dkc * scale
        dv_ref[0, pl.ds(row_c, BLOCK), :] += dvc

        @pl.when(n > 0)
        def _():
            row_p = pl.multiple_of((n - 1) * BLOCK, BLOCK)
            dk_ref[0, pl.ds(row_p, BLOCK), :] += dkp * scale
            dv_ref[0, pl.ds(row_p, BLOCK), :] += dvp

    qspec, zspec, cur, prev = _swa_specs(S, BR, KV, G)
    ospec = pl.BlockSpec((BLOCK, G * HEAD_DIM), lambda h, n: (n, h))
    full = pl.BlockSpec((1, S, HEAD_DIM), lambda h, n: (h, 0, 0))
    return _pallas(
        body, (P, P, kh, kh, vh, vh, o, du, sinks, slopes), name=name, grid=(n_kv, S // BLOCK),
        in_specs=[qspec, zspec, cur, prev, cur, prev, ospec, ospec, SMEM, SMEM],
        out_specs=[ospec, ospec, full, full, pl.BlockSpec((1, 8, 128), lambda h, n: (h, 0, 0))],
        out_shape=[jax.ShapeDtypeStruct((S, BR), bf16)] * 2
        + [jax.ShapeDtypeStruct((n_kv, S, HEAD_DIM), f32)] * 2
        + [jax.ShapeDtypeStruct((n_kv, 8, 128), f32)],
        semantics=("parallel", "arbitrary"), comm=comm)


def _split3(x):
    x1 = x.astype(bf16)
    r1 = x - x1.astype(f32)
    x2 = r1.astype(bf16)
    x3 = (r1 - x2.astype(f32)).astype(bf16)
    return x1, x2, x3


def _split2(x):
    x1 = x.astype(bf16)
    return x1, (x - x1.astype(f32)).astype(bf16)


def _pair_specs(S, BR, T, kmap):
    nb = BR // 128
    q = pl.BlockSpec((T, 128), lambda p, i, j: (i, p))
    k = pl.BlockSpec((T, 128), lambda p, i, j: (kmap(i, j), nb + p))
    v = pl.BlockSpec((T, 128), lambda p, i, j: (kmap(i, j), 2 * nb + p))
    z = pl.BlockSpec((T, 128), lambda p, i, j: (i, 3 * nb + p))
    return q, k, v, z


def _fox_cum(flT, b, name):
    H, S = flT.shape
    tj = _pick(S, (256, 128))

    def body(fl_ref, b_ref, c_ref):
        j = pl.program_id(0)
        x = fl_ref[...] + b_ref[...]
        logf = jnp.minimum(x, 0.0) - jnp.log(1.0 + jnp.exp(-jnp.abs(x)))
        tri = (_iota2((S, tj), 0) <= j * tj + _iota2((S, tj), 1)).astype(bf16)
        c_ref[...] = sum(_dot(part, tri) for part in _split3(logf))

    return pl.pallas_call(
        body, name=name, grid=(S // tj,),
        in_specs=[pl.BlockSpec((H, S), lambda j: (0, 0)), pl.BlockSpec((H, 1), lambda j: (0, 0))],
        out_specs=pl.BlockSpec((H, tj), lambda j: (0, j)),
        out_shape=jax.ShapeDtypeStruct((H, S), f32),
        compiler_params=_params(("parallel",)),
    )(flT, b)


def _fox_cum_bwd(dcq, dck, flT, b, name):
    H, S = flT.shape
    tj = _pick(S, (256, 128))

    def body(dcq_ref, dck_ref, fl_ref, b_ref, o_ref, db_ref):
        j = pl.program_id(0)

        @pl.when(j == 0)
        def _():
            db_ref[...] = jnp.zeros_like(db_ref)

        tri = (_iota2((S, tj), 0) >= j * tj + _iota2((S, tj), 1)).astype(bf16)
        dlogf = sum(_dot(part, tri) for part in _split3(dcq_ref[...] - dck_ref[...]))
        x = fl_ref[...] + b_ref[...]
        dfl = dlogf * _sigmoid(-x)
        o_ref[...] = dfl
        db_ref[...] += jnp.sum(dfl, axis=-1, keepdims=True)

    blk = pl.BlockSpec((H, tj), lambda j: (0, j))
    whole = pl.BlockSpec((H, S), lambda j: (0, 0))
    col = pl.BlockSpec((H, 1), lambda j: (0, 0))
    return pl.pallas_call(
        body, name=name, grid=(S // tj,),
        in_specs=[whole, whole, blk, col],
        out_specs=[blk, col],
        out_shape=[jax.ShapeDtypeStruct((H, S), f32), jax.ShapeDtypeStruct((H, 1), f32)],
        compiler_params=_params(("arbitrary",)),
    )(dcq, dck, flT, b)


def _hsl(hh):
    return slice(hh * HEAD_DIM, (hh + 1) * HEAD_DIM)


def _foxT_fwd(P, cq, ck, BR, name, comm=None):
    S = P.shape[0]
    H = BR // HEAD_DIM
    T = _attn_tile(S)
    nq = S // T
    scale = HEAD_DIM ** -0.5
    kmap = lambda i, j: jnp.minimum(i, j)

    def body(q_ref, k_ref, v_ref, z_ref, cq_ref, ck_ref, o_ref, u_ref, lse_ref, m_s, l_s, acc_s):
        qb, kb = pl.program_id(1), pl.program_id(2)

        @pl.when(kb == 0)
        def _():
            m_s[...] = jnp.full_like(m_s, NEG)
            l_s[...] = jnp.zeros_like(l_s)
            acc_s[...] = jnp.zeros_like(acc_s)

        def tile(diagonal):
            for hh in range(2):
                sl = _hsl(hh)
                s = _dot(k_ref[:, sl], q_ref[:, sl], NT) * scale + cq_ref[hh] - ck_ref[hh]
                if diagonal:
                    s = jnp.where(_iota2((T, T), 0) <= _iota2((T, T), 1), s, NEG)
                m_old = m_s[hh]
                m_new = jnp.maximum(m_old, jnp.max(s, axis=0, keepdims=True))
                alpha = jnp.exp(m_old - m_new)
                p = jnp.exp(s - m_new)
                l_s[hh] = alpha * l_s[hh] + jnp.sum(p, axis=0, keepdims=True)
                acc_s[hh] = alpha * acc_s[hh] + _dot(v_ref[:, sl], p.astype(bf16), TN)
                m_s[hh] = m_new

        @pl.when(kb < qb)
        def _():
            tile(False)

        @pl.when(kb == qb)
        def _():
            tile(True)
            for hh in range(2):
                sl = _hsl(hh)
                oh = jnp.transpose(acc_s[hh] / l_s[hh])
                o_ref[:, sl] = oh.astype(o_ref.dtype)
                u_ref[:, sl] = _gate_fwd(oh, z_ref[:, sl]).astype(u_ref.dtype)
                lse_ref[hh] = m_s[hh] + jnp.log(l_s[hh])

    q, k, v, z = _pair_specs(S, BR, T, kmap)
    rowq = pl.BlockSpec((2, 1, T), lambda p, i, j: (p, 0, i))
    colk = pl.BlockSpec((2, T, 1), lambda p, i, j: (p, kmap(i, j), 0))
    ospec = pl.BlockSpec((T, 128), lambda p, i, j: (i, p))
    return _pallas(
        body, (P, P, P, P, cq, ck), name=name, grid=(H // 2, nq, nq),
        in_specs=[q, k, v, z, rowq, colk],
        out_specs=[ospec, ospec, rowq],
        out_shape=[jax.ShapeDtypeStruct((S, BR), bf16)] * 2 + [jax.ShapeDtypeStruct((H, 1, S), f32)],
        scratch_shapes=[pltpu.VMEM((2, 1, T), f32), pltpu.VMEM((2, 1, T), f32), pltpu.VMEM((2, HEAD_DIM, T), f32)],
        semantics=("parallel", "arbitrary", "arbitrary"), comm=comm)


def _foxT_bwd(P, o, du, lse, cq, ck, BR, name, comm=None):
    S = P.shape[0]
    H = BR // HEAD_DIM
    T = _attn_tile(S)
    nq = S // T
    scale = HEAD_DIM ** -0.5
    kmap = lambda i, j: jnp.minimum(i, j)

    def body(q_ref, k_ref, v_ref, z_ref, o_ref, du_ref, lse_ref, cq_ref, ck_ref,
             dq_ref, dz_ref, dk_ref, dv_ref, dcq_ref, dck_ref, do_s, delta_s, dq_s, dcq_s):
        qb, kb = pl.program_id(1), pl.program_id(2)

        @pl.when(jnp.logical_and(qb == 0, kb == 0))
        def _():
            dk_ref[...] = jnp.zeros_like(dk_ref)
            dv_ref[...] = jnp.zeros_like(dv_ref)
            dck_ref[...] = jnp.zeros_like(dck_ref)

        @pl.when(kb == 0)
        def _():
            for hh in range(2):
                sl = _hsl(hh)
                oh = o_ref[:, sl].astype(f32)
                doh, dzh = _gate_bwd(du_ref[:, sl].astype(f32), oh, z_ref[:, sl])
                dz_ref[:, sl] = dzh.astype(dz_ref.dtype)
                do_s[hh] = doh.astype(bf16)
                delta_s[hh] = jnp.sum(jnp.transpose(doh * oh), axis=0, keepdims=True)
            dq_s[...] = jnp.zeros_like(dq_s)
            dcq_s[...] = jnp.zeros_like(dcq_s)

        def tile(diagonal):
            rows = pl.ds(pl.multiple_of(kb * T, T), T)
            for hh in range(2):
                sl = _hsl(hh)
                qh, kh, vh, dob = q_ref[:, sl], k_ref[:, sl], v_ref[:, sl], do_s[hh]
                s = _dot(kh, qh, NT) * scale + cq_ref[hh] - ck_ref[hh]
                p = jnp.exp(s - lse_ref[hh])
                if diagonal:
                    p = jnp.where(_iota2((T, T), 0) <= _iota2((T, T), 1), p, 0.0)
                ds = p * (_dot(vh, dob, NT) - delta_s[hh])
                dsb = ds.astype(bf16)
                dq_s[hh] += _dot(kh, dsb, TN)
                dk_ref[rows, sl] += _dot(dsb, qh) * scale
                dv_ref[rows, sl] += _dot(p.astype(bf16), dob)
                dcq_s[hh] += jnp.sum(ds, axis=0, keepdims=True)
                dck_ref[hh, rows, :] += jnp.sum(ds, axis=1, keepdims=True)

        @pl.when(kb < qb)
        def _():
            tile(False)

        @pl.when(kb == qb)
        def _():
            tile(True)
            for hh in range(2):
                dq_ref[:, _hsl(hh)] = (jnp.transpose(dq_s[hh]) * scale).astype(dq_ref.dtype)
                dcq_ref[hh] = dcq_s[hh]

    q, k, v, z = _pair_specs(S, BR, T, kmap)
    rowq = pl.BlockSpec((2, 1, T), lambda p, i, j: (p, 0, i))
    colk = pl.BlockSpec((2, T, 1), lambda p, i, j: (p, kmap(i, j), 0))
    ospec = pl.BlockSpec((T, 128), lambda p, i, j: (i, p))
    full = pl.BlockSpec((S, 128), lambda p, i, j: (0, p))
    return _pallas(
        body, (P, P, P, P, o, du, lse, cq, ck), name=name, grid=(H // 2, nq, nq),
        in_specs=[q, k, v, z, ospec, ospec, rowq, rowq, colk],
        out_specs=[ospec, ospec, full, full, rowq, pl.BlockSpec((2, S, 1), lambda p, i, j: (p, 0, 0))],
        out_shape=[jax.ShapeDtypeStruct((S, BR), bf16)] * 2 + [jax.ShapeDtypeStruct((S, BR), f32)] * 2
        + [jax.ShapeDtypeStruct((H, 1, S), f32), jax.ShapeDtypeStruct((H, S, 1), f32)],
        scratch_shapes=[pltpu.VMEM((2, T, HEAD_DIM), bf16), pltpu.VMEM((2, 1, T), f32),
                        pltpu.VMEM((2, HEAD_DIM, T), f32), pltpu.VMEM((2, 1, T), f32)],
        semantics=("parallel", "arbitrary", "arbitrary"), comm=comm)


def _sbT_logs(kc, qh, before, scale):
    l = _dot(kc, qh, NT) * scale
    minus_abs = lax.bitcast_convert_type(lax.bitcast_convert_type(l, jnp.int32) | jnp.int32(-2 ** 31), f32)
    lsig = jnp.minimum(l, 0.0) - jnp.log(1.0 + jnp.exp(minus_abs))
    lf = lsig - l
    if before is not None:
        lf = jnp.where(before, lf, 0.0)
    return lsig, lf


def _chunk_rows(c):
    return slice(c * BLOCK, (c + 1) * BLOCK)


def _sbT_suffix_tile(lf, tri2, nc):
    hi, lo = _split2(lf)
    return jnp.concatenate(
        [_dot(tri2, jnp.concatenate([hi[_chunk_rows(c)], lo[_chunk_rows(c)]], axis=0)) for c in range(nc)], axis=0)


def _sbT_fwd(P, BR, name, comm=None):
    S = P.shape[0]
    H = BR // HEAD_DIM
    T = _attn_tile(S)
    nq = S // T
    nc = T // BLOCK
    scale = HEAD_DIM ** -0.5
    kmap = lambda i, j: jnp.maximum(i - j, 0)

    def body(q_ref, k_ref, v_ref, z_ref, o_ref, u_ref, rc_ref, r_s, acc_s):
        qb, j = pl.program_id(1), pl.program_id(2)

        @pl.when(j == 0)
        def _():
            r_s[...] = jnp.zeros_like(r_s)
            acc_s[...] = jnp.zeros_like(acc_s)

        def tile(diagonal):
            ii, jj = _iota2((BLOCK, BLOCK), 0), _iota2((BLOCK, BLOCK), 1)
            tri = (jj > ii).astype(bf16)
            tri2 = jnp.concatenate([tri, tri], axis=1)
            before = (_iota2((T, T), 0) < _iota2((T, T), 1)) if diagonal else None
            for hh in range(2):
                sl = _hsl(hh)
                lsig, lf = _sbT_logs(k_ref[:, sl], q_ref[:, sl], before, scale)
                x = lsig + _sbT_suffix_tile(lf, tri2, nc)
                r = r_s[hh]
                rc_ref[hh, 0] = r
                parts = [None] * nc
                for c in reversed(range(nc)):
                    parts[c] = jnp.exp(x[_chunk_rows(c)] + r)
                    r = r + jnp.sum(lf[_chunk_rows(c)], axis=0, keepdims=True)
                r_s[hh] = r
                a = jnp.concatenate(parts, axis=0)
                if diagonal:
                    a = jnp.where(before, a, 0.0)
                acc_s[hh] += _dot(v_ref[:, sl], a.astype(bf16), TN)

        @pl.when(j > 0)
        def _():
            @pl.when(j <= qb)
            def _():
                tile(False)

        @pl.when(j == 0)
        def _():
            tile(True)

        @pl.when(j == qb)
        def _():
            for hh in range(2):
                sl = _hsl(hh)
                oh = jnp.transpose(acc_s[hh])
                o_ref[:, sl] = oh.astype(o_ref.dtype)
                u_ref[:, sl] = _gate_fwd(oh, z_ref[:, sl]).astype(u_ref.dtype)

    q, k, v, z = _pair_specs(S, BR, T, kmap)
    ospec = pl.BlockSpec((T, 128), lambda p, i, j: (i, p))
    rc = pl.BlockSpec((2, 1, 1, T), lambda p, i, j: (p, kmap(i, j), 0, i))
    return _pallas(
        body, (P, P, P, P), name=name, grid=(H // 2, nq, nq),
        in_specs=[q, k, v, z],
        out_specs=[ospec, ospec, rc],
        out_shape=[jax.ShapeDtypeStruct((S, BR), bf16)] * 2 + [jax.ShapeDtypeStruct((H, nq, 1, S), f32)],
        scratch_shapes=[pltpu.VMEM((2, 1, T), f32), pltpu.VMEM((2, HEAD_DIM, T), f32)],
        semantics=("parallel", "arbitrary", "arbitrary"), comm=comm)


def _sbT_bwd(P, o, du, rc, BR, name, comm=None):
    S = P.shape[0]
    H = BR // HEAD_DIM
    T = _attn_tile(S)
    nq = S // T
    nc = T // BLOCK
    scale = HEAD_DIM ** -0.5
    kmap = lambda i, j: jnp.minimum(i, j)

    def body(q_ref, k_ref, v_ref, z_ref, o_ref, du_ref, rc_ref,
             dq_ref, dz_ref, dk_ref, dv_ref, do_s, dq_s, g_s):
        qb, kb = pl.program_id(1), pl.program_id(2)

        @pl.when(jnp.logical_and(qb == 0, kb == 0))
        def _():
            dk_ref[...] = jnp.zeros_like(dk_ref)
            dv_ref[...] = jnp.zeros_like(dv_ref)

        @pl.when(kb == 0)
        def _():
            for hh in range(2):
                sl = _hsl(hh)
                doh, dzh = _gate_bwd(du_ref[:, sl].astype(f32), o_ref[:, sl].astype(f32), z_ref[:, sl])
                dz_ref[:, sl] = dzh.astype(dz_ref.dtype)
                do_s[hh] = doh.astype(bf16)
            dq_s[...] = jnp.zeros_like(dq_s)
            g_s[...] = jnp.zeros_like(g_s)

        def tile(diagonal):
            ii, jj = _iota2((BLOCK, BLOCK), 0), _iota2((BLOCK, BLOCK), 1)
            tri_suffix = (jj > ii).astype(bf16)
            tri2 = jnp.concatenate([tri_suffix, tri_suffix], axis=1)
            tri_prefix = (jj < ii).astype(bf16)
            before = (_iota2((T, T), 0) < _iota2((T, T), 1)) if diagonal else None
            out_rows = pl.ds(pl.multiple_of(kb * T, T), T)
            for hh in range(2):
                sl = _hsl(hh)
                qh, kh, dob = q_ref[:, sl], k_ref[:, sl], do_s[hh]
                lsig, lf = _sbT_logs(kh, qh, before, scale)
                x = lsig + _sbT_suffix_tile(lf, tri2, nc)
                r = rc_ref[hh, 0]
                parts = [None] * nc
                for c in reversed(range(nc)):
                    parts[c] = jnp.exp(x[_chunk_rows(c)] + r)
                    r = r + jnp.sum(lf[_chunk_rows(c)], axis=0, keepdims=True)
                a = jnp.concatenate(parts, axis=0)
                if diagonal:
                    a = jnp.where(before, a, 0.0)
                g = a * _dot(v_ref[:, sl], dob, NT)
                gb = g.astype(bf16)
                gsum = g_s[hh]
                for c in range(nc):
                    parts[c] = _dot(tri_prefix, gb[_chunk_rows(c)]) + gsum
                    gsum = gsum + jnp.sum(g[_chunk_rows(c)], axis=0, keepdims=True)
                g_s[hh] = gsum
                dl = g - (g + jnp.concatenate(parts, axis=0)) * jnp.exp(lsig)
                if diagonal:
                    dl = jnp.where(before, dl, 0.0)
                dl = dl.astype(bf16)
                dq_s[hh] += _dot(kh, dl, TN)
                dk_ref[out_rows, sl] += _dot(dl, qh) * scale
                dv_ref[out_rows, sl] += _dot(a.astype(bf16), dob)

        @pl.when(kb < qb)
        def _():
            tile(False)

        @pl.when(kb == qb)
        def _():
            tile(True)
            for hh in range(2):
                dq_ref[:, _hsl(hh)] = (jnp.transpose(dq_s[hh]) * scale).astype(dq_ref.dtype)

    q, k, v, z = _pair_specs(S, BR, T, kmap)
    ospec = pl.BlockSpec((T, 128), lambda p, i, j: (i, p))
    full = pl.BlockSpec((S, 128), lambda p, i, j: (0, p))
    rcs = pl.BlockSpec((2, 1, 1, T), lambda p, i, j: (p, kmap(i, j), 0, i))
    return _pallas(
        body, (P, P, P, P, o, du, rc), name=name, grid=(H // 2, nq, nq),
        in_specs=[q, k, v, z, ospec, ospec, rcs],
        out_specs=[ospec, ospec, full, full],
        out_shape=[jax.ShapeDtypeStruct((S, BR), bf16)] * 2 + [jax.ShapeDtypeStruct((S, BR), f32)] * 2,
        scratch_shapes=[pltpu.VMEM((2, T, HEAD_DIM), bf16), pltpu.VMEM((2, HEAD_DIM, T), f32),
                        pltpu.VMEM((2, 1, T), f32)],
        semantics=("parallel", "arbitrary", "arbitrary"), comm=comm)


def _adam_math(w, g, m, v):
    m = ADAM_B1 * m + (1.0 - ADAM_B1) * g
    v = ADAM_B2 * v + (1.0 - ADAM_B2) * (g * g)
    m_hat = m / (1.0 - ADAM_B1 ** ADAM_STEP)
    v_hat = v / (1.0 - ADAM_B2 ** ADAM_STEP)
    delta = -ADAM_LR * (m_hat / (jnp.sqrt(v_hat) + ADAM_EPS) + ADAM_WD * w)
    return delta, m, v


def _adamw_sum(parts, w, m, v, name):
    R, C = w.shape
    n_parts = parts.shape[0]
    tr = _pick(R, (256, 128))

    def body(p_ref, w_ref, m_ref, v_ref, g_ref, d_ref, nm_ref, nv_ref):
        g = p_ref[0].astype(f32)
        for i in range(1, n_parts):
            g = g + p_ref[i].astype(f32)
        g_ref[...] = g
        d_ref[...], nm_ref[...], nv_ref[...] = _adam_math(w_ref[...], g, m_ref[...], v_ref[...])

    blk = pl.BlockSpec((tr, C), lambda i: (i, 0))
    return pl.pallas_call(
        body, name=name, grid=(R // tr,),
        in_specs=[pl.BlockSpec((n_parts, tr, C), lambda i: (0, i, 0)), blk, blk, blk],
        out_specs=[blk] * 4,
        out_shape=[jax.ShapeDtypeStruct((R, C), f32)] * 4,
        compiler_params=_params(("parallel",)),
    )(parts, w, m, v)


def _coords():
    return lax.axis_index("x"), lax.axis_index("y"), lax.axis_index("c")


class _Gather:
    def __init__(self, shards):
        self.args = list(shards)
        self.n = len(shards)
        self.out_shape = [jax.ShapeDtypeStruct((N_DEV,) + s.shape, s.dtype) for s in shards]
        self.scratch = [pltpu.SemaphoreType.DMA((self.n, 7)), pltpu.SemaphoreType.DMA((self.n, 7)),
                        pltpu.SemaphoreType.DMA((self.n,))]

    def _ctx(self, ins, outs, sems):
        send_sems, recv_sems, local_sems = sems
        x, y, c = _coords()
        me, sibling = (x, y, c), (x, y, 1 - c)
        chips = [(1 - x, y), (x, 1 - y), (1 - x, 1 - y)]

        def slot(out, dev):
            return out.at[4 * dev[0] + 2 * dev[1] + dev[2]]

        def copy(a, k, block, to, src=None):
            return pltpu.make_async_remote_copy(
                src_ref=slot(outs[a], block) if src is None else src, dst_ref=slot(outs[a], block),
                send_sem=send_sems.at[a, k], recv_sem=recv_sems.at[a, k], device_id=to, device_id_type=MESH)

        mine = [pltpu.make_async_copy(ins[a], slot(outs[a], me), local_sems.at[a]) for a in range(self.n)]
        first = []
        for a in range(self.n):
            first.append(copy(a, 0, me, sibling, src=ins[a]))
            first += [copy(a, 1 + j, me, (*chip, c), src=ins[a]) for j, chip in enumerate(chips)]
        passed = [copy(a, 4 + j, (*chip, c), sibling) for j, chip in enumerate(chips) for a in range(self.n)]
        return c, me, sibling, chips, copy, mine, first, passed

    def start(self, ins, outs, sems):
        *_, mine, first, _ = self._ctx(ins, outs, sems)
        for cp in mine + first:
            cp.start()

    def mid(self, ins, outs, sems):
        c, me, _, chips, copy, _, _, passed = self._ctx(ins, outs, sems)
        i = 0
        for j, chip in enumerate(chips):
            for a in range(self.n):
                copy(a, 1 + j, (*chip, c), me).wait_recv()
                passed[i].start()
                i += 1

    def finish(self, ins, outs, sems):
        c, me, sibling, chips, copy, mine, first, passed = self._ctx(ins, outs, sems)
        for a in range(self.n):
            copy(a, 0, sibling, me).wait_recv()
            for j, chip in enumerate(chips):
                copy(a, 4 + j, (*chip, 1 - c), me).wait_recv()
        for cp in first + passed:
            cp.wait_send()
        for cp in mine:
            cp.wait()


N_CHIPS = 4


class _SiblingSwap:
    mid = None

    def __init__(self, blocks):
        self.args = list(blocks)
        self.n = len(blocks)
        self.out_shape = [jax.ShapeDtypeStruct((N_CHIPS,) + b.shape[1:], b.dtype) for b in blocks]
        self.scratch = [pltpu.SemaphoreType.DMA((self.n, N_CHIPS)), pltpu.SemaphoreType.DMA((self.n, N_CHIPS))]

    def _copies(self, ins, outs, sems):
        send_sems, recv_sems = sems
        x, y, c = _coords()
        return [pltpu.make_async_remote_copy(
            src_ref=ins[a].at[2 * q + (1 - c)], dst_ref=outs[a].at[q],
            send_sem=send_sems.at[a, q], recv_sem=recv_sems.at[a, q],
            device_id=(x, y, 1 - c), device_id_type=MESH) for a in range(self.n) for q in range(N_CHIPS)]

    def start(self, ins, outs, sems):
        for cp in self._copies(ins, outs, sems):
            cp.start()

    def finish(self, ins, outs, sems):
        for cp in self._copies(ins, outs, sems):
            cp.wait()


class _ChipScatter:
    mid = None

    def __init__(self, blocks):
        self.args = list(blocks)
        self.n = len(blocks)
        self.out_shape = [jax.ShapeDtypeStruct(b.shape, b.dtype) for b in blocks]
        self.scratch = [pltpu.SemaphoreType.DMA((self.n, 3)), pltpu.SemaphoreType.DMA((self.n, 3)),
                        pltpu.SemaphoreType.DMA((self.n,))]

    def _ctx(self, ins, outs, sems):
        send_sems, recv_sems, local_sems = sems
        x, y, c = _coords()
        me = 2 * x + y
        mine = [pltpu.make_async_copy(ins[a].at[me], outs[a].at[me], local_sems.at[a]) for a in range(self.n)]

        def copy(a, k, landing_here):
            px, py = x ^ ((k >> 1) & 1), y ^ (k & 1)
            them = 2 * px + py
            return pltpu.make_async_remote_copy(
                src_ref=ins[a].at[them], dst_ref=outs[a].at[them if landing_here else me],
                send_sem=send_sems.at[a, k - 1], recv_sem=recv_sems.at[a, k - 1],
                device_id=(px, py, c), device_id_type=MESH)

        sent = [copy(a, k, False) for k in range(1, N_CHIPS) for a in range(self.n)]
        arrivals = [copy(a, k, True) for k in range(1, N_CHIPS) for a in range(self.n)]
        return mine, sent, arrivals

    def start(self, ins, outs, sems):
        mine, sent, _ = self._ctx(ins, outs, sems)
        for cp in mine + sent:
            cp.start()

    def finish(self, ins, outs, sems):
        mine, sent, arrivals = self._ctx(ins, outs, sems)
        for cp in arrivals:
            cp.wait_recv()
        for cp in sent:
            cp.wait_send()
        for cp in mine:
            cp.wait()


def _pair_sum(blocks, got, core, name):
    _, R, C = blocks.shape
    tr = _pick(R, (256, 128))

    def body(core_ref, mine_ref, got_ref, o_ref):
        o_ref[...] = (mine_ref[...].astype(f32) + got_ref[...].astype(f32)).astype(o_ref.dtype)

    return pl.pallas_call(
        body, name=name,
        grid_spec=pltpu.PrefetchScalarGridSpec(
            num_scalar_prefetch=1, grid=(N_CHIPS, R // tr),
            in_specs=[pl.BlockSpec((1, tr, C), lambda q, i, core_ref: (2 * q + core_ref[0], i, 0)),
                      pl.BlockSpec((1, tr, C), lambda q, i, core_ref: (q, i, 0))],
            out_specs=pl.BlockSpec((1, tr, C), lambda q, i, core_ref: (q, i, 0))),
        out_shape=jax.ShapeDtypeStruct((N_CHIPS, R, C), blocks.dtype),
        compiler_params=_params(("parallel", "parallel")),
    )(core, blocks, got)


def _run_comm(comm, name):
    n = comm.n

    def body(*refs):
        ins, outs, sems = refs[:n], refs[n:2 * n], refs[2 * n:]
        comm.start(ins, outs, sems)
        if comm.mid is not None:
            comm.mid(ins, outs, sems)
        comm.finish(ins, outs, sems)

    return pl.pallas_call(
        body, name=name, in_specs=[ANY] * n, out_specs=[ANY] * n,
        out_shape=comm.out_shape, scratch_shapes=comm.scratch,
    )(*comm.args)


def _pallas(body, args, *, name, grid, in_specs, out_specs, out_shape, scratch_shapes=(), semantics, comm=None):
    in_specs, out_specs, out_shape, scratch_shapes = list(in_specs), list(out_specs), list(out_shape), list(scratch_shapes)
    if comm is None:
        return pl.pallas_call(
            body, name=name, grid=grid, in_specs=in_specs, out_specs=out_specs, out_shape=out_shape,
            scratch_shapes=scratch_shapes, compiler_params=_params(semantics))(*args)
    a = len(in_specs)
    b = a + comm.n
    c = b + len(out_specs)
    d = c + comm.n
    e = d + len(scratch_shapes)
    total = math.prod(grid)
    mid_step = (3 * total) // 4

    def hosted(*refs):
        step = pl.program_id(0)
        for axis in range(1, len(grid)):
            step = step * grid[axis] + pl.program_id(axis)
        ins, outs, sems = refs[a:b], refs[c:d], refs[e:]

        @pl.when(step == 0)
        def _():
            comm.start(ins, outs, sems)

        body(*refs[:a], *refs[b:c], *refs[d:e])

        if comm.mid is not None:
            @pl.when(step == mid_step)
            def _():
                comm.mid(ins, outs, sems)

        @pl.when(step == total - 1)
        def _():
            comm.finish(ins, outs, sems)

    res = pl.pallas_call(
        hosted, name=name, grid=grid, in_specs=in_specs + [ANY] * comm.n, out_specs=out_specs + [ANY] * comm.n,
        out_shape=out_shape + comm.out_shape, scratch_shapes=scratch_shapes + comm.scratch,
        compiler_params=_params(("arbitrary",) * len(grid)))(*args, *comm.args)
    return list(res[:len(out_specs)]) + [list(res[len(out_specs):])]


def _all_reduce_small(vec, name):
    R, C = vec.shape

    def body(v_ref, o_ref, gath, send_sems, recv_sems):
        x, y, c = _coords()
        me = 4 * x + 2 * y + c
        gath[me] = v_ref[...]

        def copy(k):
            px, py, pc = x ^ ((k >> 2) & 1), y ^ ((k >> 1) & 1), c ^ (k & 1)
            return pltpu.make_async_remote_copy(
                src_ref=v_ref, dst_ref=gath.at[me], send_sem=send_sems.at[k - 1], recv_sem=recv_sems.at[k - 1],
                device_id=(px, py, pc), device_id_type=MESH)

        sent = [copy(k) for k in range(1, N_DEV)]
        for cp in sent:
            cp.start()
        for cp in sent:
            cp.wait()
        total = gath[0]
        for i in range(1, N_DEV):
            total = total + gath[i]
        o_ref[...] = total

    return pl.pallas_call(
        body, name=name,
        in_specs=[pl.BlockSpec(memory_space=pltpu.VMEM)], out_specs=pl.BlockSpec(memory_space=pltpu.VMEM),
        out_shape=jax.ShapeDtypeStruct((R, C), f32),
        scratch_shapes=[pltpu.VMEM((N_DEV, R, C), f32), pltpu.SemaphoreType.DMA((7,)), pltpu.SemaphoreType.DMA((7,))],
    )(vec)


def _gathered_in(g):
    return jnp.transpose(g, (1, 0, 2)).reshape(g.shape[1], -1)


def _col_blocks(dw):
    D, N = dw.shape
    return jnp.transpose(dw.reshape(D, N_DEV, N // N_DEV), (1, 0, 2)).astype(bf16)


def _alibi_slopes(n):
    return jnp.asarray(2.0 ** (-8.0 * np.arange(1, n + 1, dtype=np.float32) / n), dtype=f32)


def _heads_major(a, n):
    return jnp.transpose(a.reshape(a.shape[0], n, HEAD_DIM), (1, 0, 2))


def _heads_minor(a):
    return jnp.transpose(a, (1, 0, 2)).reshape(a.shape[1], -1)


def kernel(x, g_pre, g_post, w_in_a, w_out_a, sinks_a, w_in_b, w_out_b, w_in_c, b_f_c, w_out_c, loss_target, m_g_pre, m_g_post, m_w_in_a, m_w_out_a, m_sinks_a, m_w_in_b, m_w_out_b, m_w_in_c, m_b_f_c, m_w_out_c, v_g_pre, v_g_post, v_w_in_a, v_w_out_a, v_sinks_a, v_w_in_b, v_w_out_b, v_w_in_c, v_b_f_c, v_w_out_c):
    S, D = x.shape[1], x.shape[2]
    H = D // HEAD_DIM
    BR = H * HEAD_DIM
    n_kv = H // 8
    KV = n_kv * HEAD_DIM
    x0 = x[0]
    target = loss_target[0]
    slopes = _alibi_slopes(H)
    w_in = {0: w_in_a, 1: w_in_b, 2: w_in_c}
    w_out = {0: w_out_a, 1: w_out_b, 2: w_out_c}

    saved = []
    xi = x0
    riders = {"in_proj0": [("out", 0)], "mixer0": [("in", 1)],
              "mixer1": [("out", 1), ("in", 2), ("out", 2)], "mixer2": [("in", 3), ("out", 3)]}
    full = {}

    def rider(slot):
        keys = riders.get(slot)
        if not keys:
            return None
        shards = [(w_in if which == "in" else w_out)[i % 3][i // 3].astype(bf16) for which, i in keys]
        return _Gather(shards)

    def landed(slot, arrays):
        full.update(zip(riders[slot], arrays))

    riders["start"] = [("in", 0)]
    landed("start", _run_comm(rider("start"), name="gather_w0"))
    for i in range(DEPTH):
        kind, j = i % 3, i // 3
        nxt = rider(f"mixer{i}")
        W_in = _gathered_in(full["in", i])
        h = _rmsnorm_fwd(xi, g_pre[i:i + 1], name=f"pre_norm{i}")
        st = dict(x=xi, h=h)
        if kind == 2:
            n_main = 4 * BR
            pad = (-W_in.shape[1]) % 128
            W_in = jnp.pad(W_in, ((0, 0), (0, pad)))
            W_f = W_in[:, n_main:n_main + 128]
            fl = _matmul(h, W_f, "nn", f32, name=f"f_proj{i}")
            flT = jnp.transpose(fl[:, :H])
            bcol = b_f_c[j].reshape(H, 1)
            cumT = _fox_cum(flT, bcol, name=f"fox_cum{i}")
            cq, ck = cumT[:, None, :], cumT[:, :, None]
            st.update(flT=flT, bcol=bcol, cq=cq, ck=ck)
        st["W_in"] = W_in
        with_proj = rider(f"in_proj{i}")
        P = _matmul(h, W_in, "nn", bf16, name=f"in_proj{i}", comm=with_proj)
        if with_proj is not None:
            P, arrived = P
            landed(f"in_proj{i}", arrived)
        st["P"] = P
        if kind == 0:
            kh = _heads_major(P[:, BR:BR + KV], n_kv)
            vh = _heads_major(P[:, BR + KV:BR + 2 * KV], n_kv)
            o, u, *rest = _swa_fwd(P, kh, vh, sinks_a[j], slopes, name=f"swa_fwd{i}", comm=nxt)
            st.update(kh=kh, vh=vh)
        elif kind == 1:
            o, u, rc, *rest = _sbT_fwd(P, BR, name=f"sb_fwd{i}", comm=nxt)
            st.update(rc=rc)
        else:
            o, u, lse, *rest = _foxT_fwd(P, cq, ck, BR, name=f"fox_fwd{i}", comm=nxt)
            st.update(lse=lse)
        if nxt is not None:
            landed(f"mixer{i}", rest[0])
        W_out = full["out", i].reshape(BR, D)
        st.update(o=o, u=u, W_out=W_out)
        y = _matmul(u, W_out, "nn", f32, name=f"out_proj{i}")
        st["y"] = y
        xi = _post_fwd(xi, y, g_post[i:i + 1], name=f"post_norm{i}")
        saved.append(st)

    loss_part, dx = _loss_fwd_bwd(xi, target, name="loss")

    dg_pre, dg_post = [None] * DEPTH, [None] * DEPTH
    dsinks = [None, None]
    db_f = None
    recv = [None] * DEPTH
    core = lax.axis_index("c").astype(jnp.int32).reshape(1)
    pending = None
    for i in reversed(range(DEPTH)):
        kind, j = i % 3, i // 3
        st = saved[i]
        dy, dg_post[i] = _post_bwd(dx, st["y"], g_post[i:i + 1], name=f"post_bwd{i}")
        du = _matmul(dy, st["W_out"], "nt", bf16, name=f"du{i}")
        dW_out = _matmul(st["u"], dy, "tn", bf16, name=f"dw_out{i}")
        P = st["P"]
        if kind == 0:
            dq, dz, dkh, dvh, dsk, *rest = _swa_bwd(P, st["kh"], st["vh"], st["o"], du, sinks_a[j], slopes,
                                                    name=f"swa_bwd{i}", comm=pending)
            dsinks[j] = dsk[:, 0, :H // n_kv].reshape(H)
            dP = jnp.concatenate([dq, _heads_minor(dkh).astype(bf16), _heads_minor(dvh).astype(bf16), dz], axis=1)
        elif kind == 1:
            dq, dz, dk, dv, *rest = _sbT_bwd(P, st["o"], du, st["rc"], BR, name=f"sb_bwd{i}", comm=pending)
            dP = jnp.concatenate([dq, dk.astype(bf16), dv.astype(bf16), dz], axis=1)
        else:
            dq, dz, dk, dv, dcq, dck, *rest = _foxT_bwd(P, st["o"], du, st["lse"], st["cq"], st["ck"], BR,
                                                       name=f"fox_bwd{i}", comm=pending)
        if pending is not None:
            recv[i + 1] = rest[0]
        if kind == 2:
            dflT, db_col = _fox_cum_bwd(dcq.reshape(H, S), dck.reshape(H, S), st["flT"], st["bcol"],
                                        name=f"fox_cum_bwd{i}")
            db_f = db_col.reshape(H)
            dfl = jnp.pad(jnp.transpose(dflT), ((0, 0), (0, 128 - H))).astype(bf16)
            dP = jnp.concatenate([dq, dk.astype(bf16), dv.astype(bf16), dz, dfl], axis=1)
        dh = _matmul(dP, st["W_in"], "nt", f32, name=f"dh{i}")
        dW_in = _matmul(st["h"], dP, "tn", bf16, name=f"dw_in{i}")
        n_cols = w_in[kind].shape[2] * N_DEV
        blocks = [_col_blocks(dW_in[:, :n_cols]), dW_out.reshape(N_DEV, BR // N_DEV, D)]
        dx, dg_pre[i], got = _pre_bwd(dh, st["x"], g_pre[i:i + 1], dx, name=f"pre_bwd{i}", comm=_SiblingSwap(blocks))
        pending = _ChipScatter([_pair_sum(b, g, core, name=f"pair_sum_{t}{i}")
                                for t, b, g in zip(("in", "out"), blocks, got)])
    recv[0] = _run_comm(pending, name="scatter_dw0")

    small = jnp.concatenate(
        [jnp.concatenate(dg_pre, axis=0).reshape(-1), jnp.concatenate(dg_post, axis=0).reshape(-1),
         jnp.concatenate(dsinks), db_f, loss_part[0, :1]])
    n_small = small.shape[0]
    rows = -(-n_small // 128)
    rows = -(-rows // 8) * 8
    small = jnp.pad(small, (0, rows * 128 - n_small)).reshape(rows, 128)
    total = _all_reduce_small(small, name="reduce_small").reshape(-1)
    o0 = DEPTH * D
    grad_g_pre = total[:o0].reshape(DEPTH, D)
    grad_g_post = total[o0:2 * o0].reshape(DEPTH, D)
    grad_sinks = total[2 * o0:2 * o0 + 2 * H].reshape(2, H)
    grad_b_f = total[2 * o0 + 2 * H:2 * o0 + 3 * H].reshape(1, H)
    loss = total[2 * o0 + 3 * H]

    def small_adam(w, g, m, v, name):
        def body(w_ref, g_ref, m_ref, v_ref, d_ref, nm_ref, nv_ref):
            d_ref[...], nm_ref[...], nv_ref[...] = _adam_math(w_ref[...], g_ref[...], m_ref[...], v_ref[...])
        vm = pl.BlockSpec(memory_space=pltpu.VMEM)
        return pl.pallas_call(body, name=name, in_specs=[vm] * 4, out_specs=[vm] * 3,
                              out_shape=[jax.ShapeDtypeStruct(w.shape, f32)] * 3)(w, g, m, v)

    upd = {}
    upd["g_pre"] = (grad_g_pre,) + tuple(small_adam(g_pre, grad_g_pre, m_g_pre, v_g_pre, "adam_g_pre"))
    upd["g_post"] = (grad_g_post,) + tuple(small_adam(g_post, grad_g_post, m_g_post, v_g_post, "adam_g_post"))
    upd["sinks_a"] = (grad_sinks,) + tuple(small_adam(sinks_a, grad_sinks, m_sinks_a, v_sinks_a, "adam_sinks"))
    upd["b_f_c"] = (grad_b_f,) + tuple(small_adam(b_f_c, grad_b_f, m_b_f_c, v_b_f_c, "adam_b_f"))

    def big(i, which, w, m, v):
        return _adamw_sum(recv[i][which], w, m, v, name=f"adam_{'in' if which == 0 else 'out'}{i}")

    a_in = [big(i, 0, w_in_a[jj], m_w_in_a[jj], v_w_in_a[jj]) for jj, i in enumerate((0, 3))]
    a_out = [big(i, 1, w_out_a[jj], m_w_out_a[jj], v_w_out_a[jj]) for jj, i in enumerate((0, 3))]
    upd["w_in_a"] = tuple(jnp.stack([a_in[0][t], a_in[1][t]]) for t in range(4))
    upd["w_out_a"] = tuple(jnp.stack([a_out[0][t], a_out[1][t]]) for t in range(4))
    upd["w_in_b"] = tuple(t[None] for t in big(1, 0, w_in_b[0], m_w_in_b[0], v_w_in_b[0]))
    upd["w_out_b"] = tuple(t[None] for t in big(1, 1, w_out_b[0], m_w_out_b[0], v_w_out_b[0]))
    upd["w_in_c"] = tuple(t[None] for t in big(2, 0, w_in_c[0], m_w_in_c[0], v_w_in_c[0]))
    upd["w_out_c"] = tuple(t[None] for t in big(2, 1, w_out_c[0], m_w_out_c[0], v_w_out_c[0]))

    names = ["g_pre", "g_post", "w_in_a", "w_out_a", "sinks_a", "w_in_b", "w_out_b", "w_in_c", "b_f_c", "w_out_c"]
    return (loss, dx[None], *[upd[k][0] for k in names], *[upd[k][1] for k in names],
            *[upd[k][2] for k in names], *[upd[k][3] for k in names])
```

```python
import functools
import math

import numpy as np
import jax
import jax.numpy as jnp
from jax import lax
from jax.experimental import pallas as pl
from jax.experimental.pallas import tpu as pltpu

HEAD_DIM = 64
BLOCK = 128
NORM_EPS = 1e-6
NEG = -1e30
N_DEV = 8
DEPTH = 4
ADAM_LR, ADAM_B1, ADAM_B2, ADAM_EPS, ADAM_WD, ADAM_STEP = 0.001, 0.9, 0.999, 1e-8, 0.01, 10
VMEM_LIMIT = 56 * 1024 * 1024

bf16 = jnp.bfloat16
f32 = jnp.float32
MESH = pl.DeviceIdType.MESH
ANY = pl.BlockSpec(memory_space=pl.ANY)
SMEM = pl.BlockSpec(memory_space=pltpu.SMEM)

NN = (((1,), (0,)), ((), ()))
NT = (((1,), (1,)), ((), ()))
TN = (((0,), (0,)), ((), ()))


def _dot(a, b, dims=NN):
    return lax.dot_general(a, b, dims, preferred_element_type=f32)


def _params(sem):
    return pltpu.CompilerParams(dimension_semantics=sem, vmem_limit_bytes=VMEM_LIMIT)


def _attn_tile(S):
    return 512 if S % 512 == 0 and S >= 1024 else 128


def _pick(n, pref):
    for t in pref:
        if n % t == 0:
            return t
    return n


def _matmul(a, b, mode, out_dtype, name, comm=None):
    if mode == "nn":
        (M, K), (K2, N) = a.shape, b.shape
    elif mode == "nt":
        (M, K), (N, K2) = a.shape, b.shape
    else:
        (K, M), (K2, N) = a.shape, b.shape
    assert K == K2, (a.shape, b.shape, mode)
    tm = _pick(M, (1024, 512, 256, 128))
    tn = _pick(N, (1024, 768, 640, 512, 256, 128))
    tk = _pick(K, (2048, 1664, 1536, 1024, 512, 640, 256, 128))
    nk = K // tk
    dims = {"nn": NN, "nt": NT, "tn": TN}[mode]

    def body(a_ref, b_ref, o_ref, acc_ref):
        if nk == 1:
            o_ref[...] = _dot(a_ref[...], b_ref[...], dims).astype(o_ref.dtype)
            return
        k = pl.program_id(2)

        @pl.when(k == 0)
        def _():
            acc_ref[...] = jnp.zeros_like(acc_ref)

        acc_ref[...] += _dot(a_ref[...], b_ref[...], dims)

        @pl.when(k == nk - 1)
        def _():
            o_ref[...] = acc_ref[...].astype(o_ref.dtype)

    if mode == "tn":
        a_spec = pl.BlockSpec((tk, tm), lambda i, j, k: (k, i))
    else:
        a_spec = pl.BlockSpec((tm, tk), lambda i, j, k: (i, k))
    if mode == "nt":
        b_spec = pl.BlockSpec((tn, tk), lambda i, j, k: (j, k))
    else:
        b_spec = pl.BlockSpec((tk, tn), lambda i, j, k: (k, j))
    res = _pallas(
        body, (a, b), name=name,
        grid=(M // tm, N // tn, nk),
        in_specs=[a_spec, b_spec],
        out_specs=[pl.BlockSpec((tm, tn), lambda i, j, k: (i, j))],
        out_shape=[jax.ShapeDtypeStruct((M, N), out_dtype)],
        scratch_shapes=[pltpu.VMEM((tm, tn) if nk > 1 else (8, 128), f32)],
        semantics=("parallel", "parallel", "arbitrary"), comm=comm)
    return res[0] if comm is None else (res[0], res[1])


ROWS = 256


def _rows(S):
    return ROWS if S % ROWS == 0 else S


def _rmsnorm_fwd(x, g, name):
    S, D = x.shape
    tr = _rows(S)

    def body(x_ref, g_ref, h_ref):
        xv = x_ref[...]
        r = lax.rsqrt(jnp.mean(xv * xv, axis=-1, keepdims=True) + NORM_EPS)
        h_ref[...] = (xv * r * g_ref[...]).astype(h_ref.dtype)

    return pl.pallas_call(
        body, name=name, grid=(S // tr,),
        in_specs=[pl.BlockSpec((tr, D), lambda i: (i, 0)), pl.BlockSpec((1, D), lambda i: (0, 0))],
        out_specs=pl.BlockSpec((tr, D), lambda i: (i, 0)),
        out_shape=jax.ShapeDtypeStruct((S, D), bf16),
        compiler_params=_params(("parallel",)),
    )(x, g)


def _post_fwd(x, y, g, name):
    S, D = x.shape
    tr = _rows(S)

    def body(x_ref, y_ref, g_ref, o_ref):
        yv = y_ref[...]
        r = lax.rsqrt(jnp.mean(yv * yv, axis=-1, keepdims=True) + NORM_EPS)
        o_ref[...] = x_ref[...] + yv * r * g_ref[...]

    row = pl.BlockSpec((tr, D), lambda i: (i, 0))
    return pl.pallas_call(
        body, name=name, grid=(S // tr,),
        in_specs=[row, row, pl.BlockSpec((1, D), lambda i: (0, 0))],
        out_specs=row,
        out_shape=jax.ShapeDtypeStruct((S, D), f32),
        compiler_params=_params(("parallel",)),
    )(x, y, g)


def _loss_fwd_bwd(y, t, name):
    S, D = y.shape
    tr = _rows(S)

    def body(y_ref, t_ref, l_ref, d_ref):
        @pl.when(pl.program_id(0) == 0)
        def _():
            l_ref[...] = jnp.zeros_like(l_ref)

        e = y_ref[...] - t_ref[...]
        d_ref[...] = e * (1.0 / D)
        part = 0.5 * jnp.sum(jnp.sum(e * e, axis=-1, keepdims=True) * (1.0 / D), axis=0, keepdims=True)
        l_ref[...] += jnp.broadcast_to(part, l_ref.shape)

    row = pl.BlockSpec((tr, D), lambda i: (i, 0))
    return pl.pallas_call(
        body, name=name, grid=(S // tr,),
        in_specs=[row, row],
        out_specs=[pl.BlockSpec((8, 128), lambda i: (0, 0)), row],
        out_shape=[jax.ShapeDtypeStruct((8, 128), f32), jax.ShapeDtypeStruct((S, D), f32)],
        compiler_params=_params(("arbitrary",)),
    )(y, t)


def _post_bwd(dxn, y, g, name):
    S, D = y.shape
    tr = _rows(S)

    def body(d_ref, y_ref, g_ref, dy_ref, dg_ref):
        @pl.when(pl.program_id(0) == 0)
        def _():
            dg_ref[...] = jnp.zeros_like(dg_ref)

        yv = y_ref[...]
        d = d_ref[...]
        r = lax.rsqrt(jnp.mean(yv * yv, axis=-1, keepdims=True) + NORM_EPS)
        n = yv * r
        dn = d * g_ref[...]
        dg_ref[...] += jnp.sum(d * n, axis=0, keepdims=True)
        dy_ref[...] = (r * (dn - n * jnp.mean(dn * n, axis=-1, keepdims=True))).astype(dy_ref.dtype)

    row = pl.BlockSpec((tr, D), lambda i: (i, 0))
    vec = pl.BlockSpec((1, D), lambda i: (0, 0))
    return pl.pallas_call(
        body, name=name, grid=(S // tr,),
        in_specs=[row, row, vec],
        out_specs=[row, vec],
        out_shape=[jax.ShapeDtypeStruct((S, D), bf16), jax.ShapeDtypeStruct((1, D), f32)],
        compiler_params=_params(("arbitrary",)),
    )(dxn, y, g)


def _pre_bwd(dh, x, g, dres, name, comm=None):
    S, D = x.shape
    tr = _rows(S)

    def body(dh_ref, x_ref, g_ref, dres_ref, dx_ref, dg_ref):
        @pl.when(pl.program_id(0) == 0)
        def _():
            dg_ref[...] = jnp.zeros_like(dg_ref)

        xv = x_ref[...]
        d = dh_ref[...]
        r = lax.rsqrt(jnp.mean(xv * xv, axis=-1, keepdims=True) + NORM_EPS)
        n = xv * r
        dn = d * g_ref[...]
        dg_ref[...] += jnp.sum(d * n, axis=0, keepdims=True)
        dx_ref[...] = dres_ref[...] + r * (dn - n * jnp.mean(dn * n, axis=-1, keepdims=True))

    row = pl.BlockSpec((tr, D), lambda i: (i, 0))
    vec = pl.BlockSpec((1, D), lambda i: (0, 0))
    return _pallas(
        body, (dh, x, g, dres), name=name, grid=(S // tr,),
        in_specs=[row, row, vec, row],
        out_specs=[row, vec],
        out_shape=[jax.ShapeDtypeStruct((S, D), f32), jax.ShapeDtypeStruct((1, D), f32)],
        semantics=("arbitrary",), comm=comm)


def _sigmoid(x):
    return 1.0 / (1.0 + jnp.exp(-x))


def _gate_fwd(o, z):
    zf = z.astype(f32)
    return o * (zf * _sigmoid(zf))


def _gate_bwd(du, o, z):
    zf = z.astype(f32)
    sig = _sigmoid(zf)
    do = du * (zf * sig)
    dz = du * o * (sig * (1.0 + zf * (1.0 - sig)))
    return do, dz


def _iota2(shape, dim):
    return lax.broadcasted_iota(jnp.int32, shape, dim)


def _swa_specs(S, BR, KV, G):
    gw = G * HEAD_DIM
    qspec = pl.BlockSpec((BLOCK, gw), lambda h, n: (n, h))
    zoff = (BR + 2 * KV) // gw
    zspec = pl.BlockSpec((BLOCK, gw), lambda h, n: (n, zoff + h))
    cur = pl.BlockSpec((1, BLOCK, HEAD_DIM), lambda h, n: (h, n, 0))
    prev = pl.BlockSpec((1, BLOCK, HEAD_DIM), lambda h, n: (h, jnp.maximum(n - 1, 0), 0))
    return qspec, zspec, cur, prev


def _split3(x):
    x1 = x.astype(bf16)
    r1 = x - x1.astype(f32)
    x2 = r1.astype(bf16)
    x3 = (r1 - x2.astype(f32)).astype(bf16)
    return x1, x2, x3


def _split2(x):
    x1 = x.astype(bf16)
    return x1, (x - x1.astype(f32)).astype(bf16)


def _pair_specs(S, BR, T, kmap):
    nb = BR // 128
    q = pl.BlockSpec((T, 128), lambda p, i, j: (i, p))
    k = pl.BlockSpec((T, 128), lambda p, i, j: (kmap(i, j), nb + p))
    v = pl.BlockSpec((T, 128), lambda p, i, j: (kmap(i, j), 2 * nb + p))
    z = pl.BlockSpec((T, 128), lambda p, i, j: (i, 3 * nb + p))
    return q, k, v, z


def _fox_cum(flT, b, name):
    H, S = flT.shape
    tj = _pick(S, (256, 128))

    def body(fl_ref, b_ref, c_ref):
        j = pl.program_id(0)
        x = fl_ref[...] + b_ref[...]
        logf = jnp.minimum(x, 0.0) - jnp.log(1.0 + jnp.exp(-jnp.abs(x)))
        tri = (_iota2((S, tj), 0) <= j * tj + _iota2((S, tj), 1)).astype(bf16)
        c_ref[...] = sum(_dot(part, tri) for part in _split3(logf))

    return pl.pallas_call(
        body, name=name, grid=(S // tj,),
        in_specs=[pl.BlockSpec((H, S), lambda j: (0, 0)), pl.BlockSpec((H, 1), lambda j: (0, 0))],
        out_specs=pl.BlockSpec((H, tj), lambda j: (0, j)),
        out_shape=jax.ShapeDtypeStruct((H, S), f32),
        compiler_params=_params(("parallel",)),
    )(flT, b)


def _fox_cum_bwd(dcq, dck, flT, b, name):
    H, S = flT.shape
    tj = _pick(S, (256, 128))

    def body(dcq_ref, dck_ref, fl_ref, b_ref, o_ref, db_ref):
        j = pl.program_id(0)

        @pl.when(j == 0)
        def _():
            db_ref[...] = jnp.zeros_like(db_ref)

        tri = (_iota2((S, tj), 0) >= j * tj + _iota2((S, tj), 1)).astype(bf16)
        dlogf = sum(_dot(part, tri) for part in _split3(dcq_ref[...] - dck_ref[...]))
        x = fl_ref[...] + b_ref[...]
        dfl = dlogf * _sigmoid(-x)
        o_ref[...] = dfl
        db_ref[...] += jnp.sum(dfl, axis=-1, keepdims=True)

    blk = pl.BlockSpec((H, tj), lambda j: (0, j))
    whole = pl.BlockSpec((H, S), lambda j: (0, 0))
    col = pl.BlockSpec((H, 1), lambda j: (0, 0))
    return pl.pallas_call(
        body, name=name, grid=(S // tj,),
        in_specs=[whole, whole, blk, col],
        out_specs=[blk, col],
        out_shape=[jax.ShapeDtypeStruct((H, S), f32), jax.ShapeDtypeStruct((H, 1), f32)],
        compiler_params=_params(("arbitrary",)),
    )(dcq, dck, flT, b)


def _hsl(hh):
    return slice(hh * HEAD_DIM, (hh + 1) * HEAD_DIM)


def _swa_bias(H, n_kv):
    G = H // n_kv
    slopes = 2.0 ** (-8.0 * np.arange(1, H + 1, dtype=np.float32) / H)
    dist = (np.arange(BLOCK)[None, :] + BLOCK - np.arange(2 * BLOCK)[:, None]).astype(np.float32)
    valid = (dist >= 0) & (dist < BLOCK)
    out = np.empty((n_kv, 2 * BLOCK, G * BLOCK), np.float32)
    for h in range(H):
        out[h // G][:, (h % G) * BLOCK:(h % G + 1) * BLOCK] = np.where(valid, -(slopes[h] * dist), np.float32(NEG))
    return jnp.asarray(out)


def _swaT_parts(q_ref, kc_ref, kp_ref, vc_ref, vp_ref, bias_ref, sink_ref, kvh, n, G, scale):
    qg = jnp.concatenate([q_ref[:, _hsl(g)] for g in range(G)], axis=0)
    kband = jnp.concatenate([kp_ref[0], kc_ref[0]], axis=0)
    vband = jnp.concatenate([vp_ref[0], vc_ref[0]], axis=0)
    s = _dot(kband, qg, NT) * scale + bias_ref[0]
    s = jnp.where(_iota2(s.shape, 0) + jnp.where(n > 0, BLOCK, 0) >= BLOCK, s, NEG)
    sink = jnp.concatenate([jnp.full((1, BLOCK), sink_ref[kvh * G + g], f32) for g in range(G)], axis=1)
    m = jnp.maximum(jnp.max(s, axis=0, keepdims=True), sink)
    p = jnp.exp(s - m)
    ps = jnp.exp(sink - m)
    den = jnp.sum(p, axis=0, keepdims=True) + ps
    return qg, kband, vband, p, ps, den


def _swaT_specs(S, BR, KV, G):
    qspec, zspec, cur, prev = _swa_specs(S, BR, KV, G)
    bias = pl.BlockSpec((1, 2 * BLOCK, G * BLOCK), lambda h, n: (h, 0, 0))
    return qspec, zspec, cur, prev, bias


def _swaT_fwd(P, kh, vh, sinks, bias, name, comm=None):
    S = P.shape[0]
    n_kv = kh.shape[0]
    H = sinks.shape[0]
    G = H // n_kv
    BR, KV = H * HEAD_DIM, n_kv * HEAD_DIM
    scale = HEAD_DIM ** -0.5
    assert (BR + 2 * KV) % (G * HEAD_DIM) == 0

    def body(q_ref, z_ref, kc_ref, kp_ref, vc_ref, vp_ref, bias_ref, sink_ref, o_ref, u_ref):
        kvh, n = pl.program_id(0), pl.program_id(1)
        _, _, vband, p, _, den = _swaT_parts(q_ref, kc_ref, kp_ref, vc_ref, vp_ref, bias_ref, sink_ref, kvh, n, G, scale)
        o = jnp.transpose(_dot(vband, p.astype(bf16), TN) / den)
        for g in range(G):
            oh = o[g * BLOCK:(g + 1) * BLOCK]
            o_ref[:, _hsl(g)] = oh.astype(o_ref.dtype)
            u_ref[:, _hsl(g)] = _gate_fwd(oh, z_ref[:, _hsl(g)]).astype(u_ref.dtype)

    qspec, zspec, cur, prev, bspec = _swaT_specs(S, BR, KV, G)
    ospec = pl.BlockSpec((BLOCK, G * HEAD_DIM), lambda h, n: (n, h))
    return _pallas(
        body, (P, P, kh, kh, vh, vh, bias, sinks), name=name, grid=(n_kv, S // BLOCK),
        in_specs=[qspec, zspec, cur, prev, cur, prev, bspec, SMEM],
        out_specs=[ospec, ospec],
        out_shape=[jax.ShapeDtypeStruct((S, BR), bf16)] * 2,
        semantics=("parallel", "parallel"), comm=comm)


def _swaT_bwd(P, kh, vh, o, du, sinks, bias, name, comm=None):
    S = P.shape[0]
    n_kv = kh.shape[0]
    H = sinks.shape[0]
    G = H // n_kv
    BR, KV = H * HEAD_DIM, n_kv * HEAD_DIM
    scale = HEAD_DIM ** -0.5

    def body(q_ref, z_ref, kc_ref, kp_ref, vc_ref, vp_ref, o_ref, du_ref, bias_ref, sink_ref,
             dq_ref, dz_ref, dk_ref, dv_ref, ds_ref):
        kvh, n = pl.program_id(0), pl.program_id(1)

        @pl.when(n == 0)
        def _():
            dk_ref[...] = jnp.zeros_like(dk_ref)
            dv_ref[...] = jnp.zeros_like(dv_ref)
            ds_ref[...] = jnp.zeros_like(ds_ref)

        dos, prods = [], []
        for g in range(G):
            oh = o_ref[:, _hsl(g)].astype(f32)
            doh, dzh = _gate_bwd(du_ref[:, _hsl(g)].astype(f32), oh, z_ref[:, _hsl(g)])
            dz_ref[:, _hsl(g)] = dzh.astype(dz_ref.dtype)
            dos.append(doh.astype(bf16))
            prods.append(doh * oh)
        dog = jnp.concatenate(dos, axis=0)
        delta = jnp.sum(jnp.transpose(jnp.concatenate(prods, axis=0)), axis=0, keepdims=True)
        qg, kband, vband, p, ps, den = _swaT_parts(q_ref, kc_ref, kp_ref, vc_ref, vp_ref, bias_ref, sink_ref,
                                                   kvh, n, G, scale)
        inv = 1.0 / den
        pn = p * inv
        ds = (pn * (_dot(vband, dog, NT) - delta)).astype(bf16)
        dq = jnp.transpose(_dot(kband, ds, TN)) * scale
        for g in range(G):
            dq_ref[:, _hsl(g)] = dq[g * BLOCK:(g + 1) * BLOCK].astype(dq_ref.dtype)
        dkb = _dot(ds, qg) * scale
        dvb = _dot(pn.astype(bf16), dog)
        ds_ref[0] -= ps * inv * delta
        row_c = pl.multiple_of(n * BLOCK, BLOCK)
        dk_ref[0, pl.ds(row_c, BLOCK), :] += dkb[BLOCK:]
        dv_ref[0, pl.ds(row_c, BLOCK), :] += dvb[BLOCK:]

        @pl.when(n > 0)
        def _():
            row_p = pl.multiple_of((n - 1) * BLOCK, BLOCK)
            dk_ref[0, pl.ds(row_p, BLOCK), :] += dkb[:BLOCK]
            dv_ref[0, pl.ds(row_p, BLOCK), :] += dvb[:BLOCK]

    qspec, zspec, cur, prev, bspec = _swaT_specs(S, BR, KV, G)
    ospec = pl.BlockSpec((BLOCK, G * HEAD_DIM), lambda h, n: (n, h))
    full = pl.BlockSpec((1, S, HEAD_DIM), lambda h, n: (h, 0, 0))
    return _pallas(
        body, (P, P, kh, kh, vh, vh, o, du, bias, sinks), name=name, grid=(n_kv, S // BLOCK),
        in_specs=[qspec, zspec, cur, prev, cur, prev, ospec, ospec, bspec, SMEM],
        out_specs=[ospec, ospec, full, full, pl.BlockSpec((1, 1, G * BLOCK), lambda h, n: (h, 0, 0))],
        out_shape=[jax.ShapeDtypeStruct((S, BR), bf16)] * 2
        + [jax.ShapeDtypeStruct((n_kv, S, HEAD_DIM), f32)] * 2
        + [jax.ShapeDtypeStruct((n_kv, 1, G * BLOCK), f32)],
        semantics=("parallel", "arbitrary"), comm=comm)


def _foxT_fwd(P, cq, ck, BR, name, comm=None):
    S = P.shape[0]
    H = BR // HEAD_DIM
    T = _attn_tile(S)
    nq = S // T
    scale = HEAD_DIM ** -0.5
    kmap = lambda i, j: jnp.minimum(i, j)

    def body(q_ref, k_ref, v_ref, z_ref, cq_ref, ck_ref, o_ref, u_ref, lse_ref, m_s, l_s, acc_s):
        qb, kb = pl.program_id(1), pl.program_id(2)

        @pl.when(kb == 0)
        def _():
            m_s[...] = jnp.full_like(m_s, NEG)
            l_s[...] = jnp.zeros_like(l_s)
            acc_s[...] = jnp.zeros_like(acc_s)

        def tile(diagonal):
            for hh in range(2):
                sl = _hsl(hh)
                s = _dot(k_ref[:, sl], q_ref[:, sl], NT) * scale + cq_ref[hh] - ck_ref[hh]
                if diagonal:
                    s = jnp.where(_iota2((T, T), 0) <= _iota2((T, T), 1), s, NEG)
                m_old = m_s[hh]
                m_new = jnp.maximum(m_old, jnp.max(s, axis=0, keepdims=True))
                alpha = jnp.exp(m_old - m_new)
                p = jnp.exp(s - m_new)
                l_s[hh] = alpha * l_s[hh] + jnp.sum(p, axis=0, keepdims=True)
                acc_s[hh] = alpha * acc_s[hh] + _dot(v_ref[:, sl], p.astype(bf16), TN)
                m_s[hh] = m_new

        @pl.when(kb < qb)
        def _():
            tile(False)

        @pl.when(kb == qb)
        def _():
            tile(True)
            for hh in range(2):
                sl = _hsl(hh)
                oh = jnp.transpose(acc_s[hh] / l_s[hh])
                o_ref[:, sl] = oh.astype(o_ref.dtype)
                u_ref[:, sl] = _gate_fwd(oh, z_ref[:, sl]).astype(u_ref.dtype)
                lse_ref[hh] = m_s[hh] + jnp.log(l_s[hh])

    q, k, v, z = _pair_specs(S, BR, T, kmap)
    rowq = pl.BlockSpec((2, 1, T), lambda p, i, j: (p, 0, i))
    colk = pl.BlockSpec((2, T, 1), lambda p, i, j: (p, kmap(i, j), 0))
    ospec = pl.BlockSpec((T, 128), lambda p, i, j: (i, p))
    return _pallas(
        body, (P, P, P, P, cq, ck), name=name, grid=(H // 2, nq, nq),
        in_specs=[q, k, v, z, rowq, colk],
        out_specs=[ospec, ospec, rowq],
        out_shape=[jax.ShapeDtypeStruct((S, BR), bf16)] * 2 + [jax.ShapeDtypeStruct((H, 1, S), f32)],
        scratch_shapes=[pltpu.VMEM((2, 1, T), f32), pltpu.VMEM((2, 1, T), f32), pltpu.VMEM((2, HEAD_DIM, T), f32)],
        semantics=("parallel", "arbitrary", "arbitrary"), comm=comm)


def _foxT_bwd(P, o, du, lse, cq, ck, BR, name, comm=None):
    S = P.shape[0]
    H = BR // HEAD_DIM
    T = _attn_tile(S)
    nq = S // T
    scale = HEAD_DIM ** -0.5
    kmap = lambda i, j: jnp.minimum(i, j)

    def body(q_ref, k_ref, v_ref, z_ref, o_ref, du_ref, lse_ref, cq_ref, ck_ref,
             dq_ref, dz_ref, dk_ref, dv_ref, dcq_ref, dck_ref, do_s, delta_s, dq_s, dcq_s):
        qb, kb = pl.program_id(1), pl.program_id(2)

        @pl.when(jnp.logical_and(qb == 0, kb == 0))
        def _():
            dk_ref[...] = jnp.zeros_like(dk_ref)
            dv_ref[...] = jnp.zeros_like(dv_ref)
            dck_ref[...] = jnp.zeros_like(dck_ref)

        @pl.when(kb == 0)
        def _():
            for hh in range(2):
                sl = _hsl(hh)
                oh = o_ref[:, sl].astype(f32)
                doh, dzh = _gate_bwd(du_ref[:, sl].astype(f32), oh, z_ref[:, sl])
                dz_ref[:, sl] = dzh.astype(dz_ref.dtype)
                do_s[hh] = doh.astype(bf16)
                delta_s[hh] = jnp.sum(jnp.transpose(doh * oh), axis=0, keepdims=True)
            dq_s[...] = jnp.zeros_like(dq_s)
            dcq_s[...] = jnp.zeros_like(dcq_s)

        def tile(diagonal):
            rows = pl.ds(pl.multiple_of(kb * T, T), T)
            for hh in range(2):
                sl = _hsl(hh)
                qh, kh, vh, dob = q_ref[:, sl], k_ref[:, sl], v_ref[:, sl], do_s[hh]
                s = _dot(kh, qh, NT) * scale + cq_ref[hh] - ck_ref[hh]
                p = jnp.exp(s - lse_ref[hh])
                if diagonal:
                    p = jnp.where(_iota2((T, T), 0) <= _iota2((T, T), 1), p, 0.0)
                ds = p * (_dot(vh, dob, NT) - delta_s[hh])
                dsb = ds.astype(bf16)
                dq_s[hh] += _dot(kh, dsb, TN)
                dk_ref[rows, sl] += _dot(dsb, qh) * scale
                dv_ref[rows, sl] += _dot(p.astype(bf16), dob)
                dcq_s[hh] += jnp.sum(ds, axis=0, keepdims=True)
                dck_ref[hh, rows, :] += jnp.sum(ds, axis=1, keepdims=True)

        @pl.when(kb < qb)
        def _():
            tile(False)

        @pl.when(kb == qb)
        def _():
            tile(True)
            for hh in range(2):
                dq_ref[:, _hsl(hh)] = (jnp.transpose(dq_s[hh]) * scale).astype(dq_ref.dtype)
                dcq_ref[hh] = dcq_s[hh]

    q, k, v, z = _pair_specs(S, BR, T, kmap)
    rowq = pl.BlockSpec((2, 1, T), lambda p, i, j: (p, 0, i))
    colk = pl.BlockSpec((2, T, 1), lambda p, i, j: (p, kmap(i, j), 0))
    ospec = pl.BlockSpec((T, 128), lambda p, i, j: (i, p))
    full = pl.BlockSpec((S, 128), lambda p, i, j: (0, p))
    return _pallas(
        body, (P, P, P, P, o, du, lse, cq, ck), name=name, grid=(H // 2, nq, nq),
        in_specs=[q, k, v, z, ospec, ospec, rowq, rowq, colk],
        out_specs=[ospec, ospec, full, full, rowq, pl.BlockSpec((2, S, 1), lambda p, i, j: (p, 0, 0))],
        out_shape=[jax.ShapeDtypeStruct((S, BR), bf16)] * 2 + [jax.ShapeDtypeStruct((S, BR), f32)] * 2
        + [jax.ShapeDtypeStruct((H, 1, S), f32), jax.ShapeDtypeStruct((H, S, 1), f32)],
        scratch_shapes=[pltpu.VMEM((2, T, HEAD_DIM), bf16), pltpu.VMEM((2, 1, T), f32),
                        pltpu.VMEM((2, HEAD_DIM, T), f32), pltpu.VMEM((2, 1, T), f32)],
        semantics=("parallel", "arbitrary", "arbitrary"), comm=comm)


def _sbT_logs(kc, qh, before, scale):
    l = _dot(kc, qh, NT) * scale
    minus_abs = lax.bitcast_convert_type(lax.bitcast_convert_type(l, jnp.int32) | jnp.int32(-2 ** 31), f32)
    lsig = jnp.minimum(l, 0.0) - jnp.log(1.0 + jnp.exp(minus_abs))
    lf = lsig - l
    if before is not None:
        lf = jnp.where(before, lf, 0.0)
    return lsig, lf


def _chunk_rows(c):
    return slice(c * BLOCK, (c + 1) * BLOCK)


def _sbT_suffix_tile(lf, tri2, nc):
    hi, lo = _split2(lf)
    return jnp.concatenate(
        [_dot(tri2, jnp.concatenate([hi[_chunk_rows(c)], lo[_chunk_rows(c)]], axis=0)) for c in range(nc)], axis=0)


def _sbT_fwd(P, BR, name, comm=None):
    S = P.shape[0]
    H = BR // HEAD_DIM
    T = _attn_tile(S)
    nq = S // T
    nc = T // BLOCK
    scale = HEAD_DIM ** -0.5
    kmap = lambda i, j: jnp.maximum(i - j, 0)

    def body(q_ref, k_ref, v_ref, z_ref, o_ref, u_ref, rc_ref, r_s, acc_s):
        qb, j = pl.program_id(1), pl.program_id(2)

        @pl.when(j == 0)
        def _():
            r_s[...] = jnp.zeros_like(r_s)
            acc_s[...] = jnp.zeros_like(acc_s)

        def tile(diagonal):
            ii, jj = _iota2((BLOCK, BLOCK), 0), _iota2((BLOCK, BLOCK), 1)
            tri = (jj > ii).astype(bf16)
            tri2 = jnp.concatenate([tri, tri], axis=1)
            before = (_iota2((T, T), 0) < _iota2((T, T), 1)) if diagonal else None
            for hh in range(2):
                sl = _hsl(hh)
                lsig, lf = _sbT_logs(k_ref[:, sl], q_ref[:, sl], before, scale)
                x = lsig + _sbT_suffix_tile(lf, tri2, nc)
                r = r_s[hh]
                rc_ref[hh, 0] = r
                parts = [None] * nc
                for c in reversed(range(nc)):
                    parts[c] = jnp.exp(x[_chunk_rows(c)] + r)
                    r = r + jnp.sum(lf[_chunk_rows(c)], axis=0, keepdims=True)
                r_s[hh] = r
                a = jnp.concatenate(parts, axis=0)
                if diagonal:
                    a = jnp.where(before, a, 0.0)
                acc_s[hh] += _dot(v_ref[:, sl], a.astype(bf16), TN)

        @pl.when(j > 0)
        def _():
            @pl.when(j <= qb)
            def _():
                tile(False)

        @pl.when(j == 0)
        def _():
            tile(True)

        @pl.when(j == qb)
        def _():
            for hh in range(2):
                sl = _hsl(hh)
                oh = jnp.transpose(acc_s[hh])
                o_ref[:, sl] = oh.astype(o_ref.dtype)
                u_ref[:, sl] = _gate_fwd(oh, z_ref[:, sl]).astype(u_ref.dtype)

    q, k, v, z = _pair_specs(S, BR, T, kmap)
    ospec = pl.BlockSpec((T, 128), lambda p, i, j: (i, p))
    rc = pl.BlockSpec((2, 1, 1, T), lambda p, i, j: (p, kmap(i, j), 0, i))
    return _pallas(
        body, (P, P, P, P), name=name, grid=(H // 2, nq, nq),
        in_specs=[q, k, v, z],
        out_specs=[ospec, ospec, rc],
        out_shape=[jax.ShapeDtypeStruct((S, BR), bf16)] * 2 + [jax.ShapeDtypeStruct((H, nq, 1, S), f32)],
        scratch_shapes=[pltpu.VMEM((2, 1, T), f32), pltpu.VMEM((2, HEAD_DIM, T), f32)],
        semantics=("parallel", "arbitrary", "arbitrary"), comm=comm)


def _sbT_bwd(P, o, du, rc, BR, name, comm=None):
    S = P.shape[0]
    H = BR // HEAD_DIM
    T = _attn_tile(S)
    nq = S // T
    nc = T // BLOCK
    scale = HEAD_DIM ** -0.5
    kmap = lambda i, j: jnp.minimum(i, j)

    def body(q_ref, k_ref, v_ref, z_ref, o_ref, du_ref, rc_ref,
             dq_ref, dz_ref, dk_ref, dv_ref, do_s, dq_s, g_s):
        qb, kb = pl.program_id(1), pl.program_id(2)

        @pl.when(jnp.logical_and(qb == 0, kb == 0))
        def _():
            dk_ref[...] = jnp.zeros_like(dk_ref)
            dv_ref[...] = jnp.zeros_like(dv_ref)

        @pl.when(kb == 0)
        def _():
            for hh in range(2):
                sl = _hsl(hh)
                doh, dzh = _gate_bwd(du_ref[:, sl].astype(f32), o_ref[:, sl].astype(f32), z_ref[:, sl])
                dz_ref[:, sl] = dzh.astype(dz_ref.dtype)
                do_s[hh] = doh.astype(bf16)
            dq_s[...] = jnp.zeros_like(dq_s)
            g_s[...] = jnp.zeros_like(g_s)

        def tile(diagonal):
            ii, jj = _iota2((BLOCK, BLOCK), 0), _iota2((BLOCK, BLOCK), 1)
            tri_suffix = (jj > ii).astype(bf16)
            tri2 = jnp.concatenate([tri_suffix, tri_suffix], axis=1)
            tri_prefix = (jj < ii).astype(bf16)
            before = (_iota2((T, T), 0) < _iota2((T, T), 1)) if diagonal else None
            out_rows = pl.ds(pl.multiple_of(kb * T, T), T)
            for hh in range(2):
                sl = _hsl(hh)
                qh, kh, dob = q_ref[:, sl], k_ref[:, sl], do_s[hh]
                lsig, lf = _sbT_logs(kh, qh, before, scale)
                x = lsig + _sbT_suffix_tile(lf, tri2, nc)
                r = rc_ref[hh, 0]
                parts = [None] * nc
                for c in reversed(range(nc)):
                    parts[c] = jnp.exp(x[_chunk_rows(c)] + r)
                    r = r + jnp.sum(lf[_chunk_rows(c)], axis=0, keepdims=True)
                a = jnp.concatenate(parts, axis=0)
                if diagonal:
                    a = jnp.where(before, a, 0.0)
                g = a * _dot(v_ref[:, sl], dob, NT)
                gb = g.astype(bf16)
                gsum = g_s[hh]
                for c in range(nc):
                    parts[c] = _dot(tri_prefix, gb[_chunk_rows(c)]) + gsum
                    gsum = gsum + jnp.sum(g[_chunk_rows(c)], axis=0, keepdims=True)
                g_s[hh] = gsum
                dl = g - (g + jnp.concatenate(parts, axis=0)) * jnp.exp(lsig)
                if diagonal:
                    dl = jnp.where(before, dl, 0.0)
                dl = dl.astype(bf16)
                dq_s[hh] += _dot(kh, dl, TN)
                dk_ref[out_rows, sl] += _dot(dl, qh) * scale
                dv_ref[out_rows, sl] += _dot(a.astype(bf16), dob)

        @pl.when(kb < qb)
        def _():
            tile(False)

        @pl.when(kb == qb)
        def _():
            tile(True)
            for hh in range(2):
                dq_ref[:, _hsl(hh)] = (jnp.transpose(dq_s[hh]) * scale).astype(dq_ref.dtype)

    q, k, v, z = _pair_specs(S, BR, T, kmap)
    ospec = pl.BlockSpec((T, 128), lambda p, i, j: (i, p))
    full = pl.BlockSpec((S, 128), lambda p, i, j: (0, p))
    rcs = pl.BlockSpec((2, 1, 1, T), lambda p, i, j: (p, kmap(i, j), 0, i))
    return _pallas(
        body, (P, P, P, P, o, du, rc), name=name, grid=(H // 2, nq, nq),
        in_specs=[q, k, v, z, ospec, ospec, rcs],
        out_specs=[ospec, ospec, full, full],
        out_shape=[jax.ShapeDtypeStruct((S, BR), bf16)] * 2 + [jax.ShapeDtypeStruct((S, BR), f32)] * 2,
        scratch_shapes=[pltpu.VMEM((2, T, HEAD_DIM), bf16), pltpu.VMEM((2, HEAD_DIM, T), f32),
                        pltpu.VMEM((2, 1, T), f32)],
        semantics=("parallel", "arbitrary", "arbitrary"), comm=comm)


def _adam_math(w, g, m, v):
    m = ADAM_B1 * m + (1.0 - ADAM_B1) * g
    v = ADAM_B2 * v + (1.0 - ADAM_B2) * (g * g)
    m_hat = m / (1.0 - ADAM_B1 ** ADAM_STEP)
    v_hat = v / (1.0 - ADAM_B2 ** ADAM_STEP)
    delta = -ADAM_LR * (m_hat / (jnp.sqrt(v_hat) + ADAM_EPS) + ADAM_WD * w)
    return delta, m, v


def _adamw_sum(parts, w, m, v, name):
    R, C = w.shape
    n_parts = parts.shape[0]
    tr = _pick(R, (256, 128))

    def body(p_ref, w_ref, m_ref, v_ref, g_ref, d_ref, nm_ref, nv_ref):
        g = p_ref[0].astype(f32)
        for i in range(1, n_parts):
            g = g + p_ref[i].astype(f32)
        g_ref[...] = g
        d_ref[...], nm_ref[...], nv_ref[...] = _adam_math(w_ref[...], g, m_ref[...], v_ref[...])

    blk = pl.BlockSpec((tr, C), lambda i: (i, 0))
    return pl.pallas_call(
        body, name=name, grid=(R // tr,),
        in_specs=[pl.BlockSpec((n_parts, tr, C), lambda i: (0, i, 0)), blk, blk, blk],
        out_specs=[blk] * 4,
        out_shape=[jax.ShapeDtypeStruct((R, C), f32)] * 4,
        compiler_params=_params(("parallel",)),
    )(parts, w, m, v)


def _coords():
    return lax.axis_index("x"), lax.axis_index("y"), lax.axis_index("c")


class _Gather:
    def __init__(self, shards):
        self.args = list(shards)
        self.n = len(shards)
        self.out_shape = [jax.ShapeDtypeStruct((N_DEV,) + s.shape, s.dtype) for s in shards]
        self.scratch = [pltpu.SemaphoreType.DMA((self.n, 7)), pltpu.SemaphoreType.DMA((self.n, 7)),
                        pltpu.SemaphoreType.DMA((self.n,))]

    def _ctx(self, ins, outs, sems):
        send_sems, recv_sems, local_sems = sems
        x, y, c = _coords()
        me, sibling = (x, y, c), (x, y, 1 - c)
        chips = [(1 - x, y), (x, 1 - y), (1 - x, 1 - y)]

        def slot(out, dev):
            return out.at[4 * dev[0] + 2 * dev[1] + dev[2]]

        def copy(a, k, block, to, src=None):
            return pltpu.make_async_remote_copy(
                src_ref=slot(outs[a], block) if src is None else src, dst_ref=slot(outs[a], block),
                send_sem=send_sems.at[a, k], recv_sem=recv_sems.at[a, k], device_id=to, device_id_type=MESH)

        mine = [pltpu.make_async_copy(ins[a], slot(outs[a], me), local_sems.at[a]) for a in range(self.n)]
        first = []
        for a in range(self.n):
            first.append(copy(a, 0, me, sibling, src=ins[a]))
            first += [copy(a, 1 + j, me, (*chip, c), src=ins[a]) for j, chip in enumerate(chips)]
        passed = [copy(a, 4 + j, (*chip, c), sibling) for j, chip in enumerate(chips) for a in range(self.n)]
        return c, me, sibling, chips, copy, mine, first, passed

    def start(self, ins, outs, sems):
        *_, mine, first, _ = self._ctx(ins, outs, sems)
        for cp in mine + first:
            cp.start()

    def mid(self, ins, outs, sems):
        c, me, _, chips, copy, _, _, passed = self._ctx(ins, outs, sems)
        i = 0
        for j, chip in enumerate(chips):
            for a in range(self.n):
                copy(a, 1 + j, (*chip, c), me).wait_recv()
                passed[i].start()
                i += 1

    def finish(self, ins, outs, sems):
        c, me, sibling, chips, copy, mine, first, passed = self._ctx(ins, outs, sems)
        for a in range(self.n):
            copy(a, 0, sibling, me).wait_recv()
            for j, chip in enumerate(chips):
                copy(a, 4 + j, (*chip, 1 - c), me).wait_recv()
        for cp in first + passed:
            cp.wait_send()
        for cp in mine:
            cp.wait()


N_CHIPS = 4


class _SiblingSwap:
    mid = None

    def __init__(self, blocks):
        self.args = list(blocks)
        self.n = len(blocks)
        self.out_shape = [jax.ShapeDtypeStruct((N_CHIPS,) + b.shape[1:], b.dtype) for b in blocks]
        self.scratch = [pltpu.SemaphoreType.DMA((self.n, N_CHIPS)), pltpu.SemaphoreType.DMA((self.n, N_CHIPS))]

    def _copies(self, ins, outs, sems):
        send_sems, recv_sems = sems
        x, y, c = _coords()
        return [pltpu.make_async_remote_copy(
            src_ref=ins[a].at[2 * q + (1 - c)], dst_ref=outs[a].at[q],
            send_sem=send_sems.at[a, q], recv_sem=recv_sems.at[a, q],
            device_id=(x, y, 1 - c), device_id_type=MESH) for a in range(self.n) for q in range(N_CHIPS)]

    def start(self, ins, outs, sems):
        for cp in self._copies(ins, outs, sems):
            cp.start()

    def finish(self, ins, outs, sems):
        for cp in self._copies(ins, outs, sems):
            cp.wait()


class _ChipScatter:
    mid = None

    def __init__(self, blocks):
        self.args = list(blocks)
        self.n = len(blocks)
        self.out_shape = [jax.ShapeDtypeStruct(b.shape, b.dtype) for b in blocks]
        self.scratch = [pltpu.SemaphoreType.DMA((self.n, 3)), pltpu.SemaphoreType.DMA((self.n, 3)),
                        pltpu.SemaphoreType.DMA((self.n,))]

    def _ctx(self, ins, outs, sems):
        send_sems, recv_sems, local_sems = sems
        x, y, c = _coords()
        me = 2 * x + y
        mine = [pltpu.make_async_copy(ins[a].at[me], outs[a].at[me], local_sems.at[a]) for a in range(self.n)]

        def copy(a, k, landing_here):
            px, py = x ^ ((k >> 1) & 1), y ^ (k & 1)
            them = 2 * px + py
            return pltpu.make_async_remote_copy(
                src_ref=ins[a].at[them], dst_ref=outs[a].at[them if landing_here else me],
                send_sem=send_sems.at[a, k - 1], recv_sem=recv_sems.at[a, k - 1],
                device_id=(px, py, c), device_id_type=MESH)

        sent = [copy(a, k, False) for k in range(1, N_CHIPS) for a in range(self.n)]
        arrivals = [copy(a, k, True) for k in range(1, N_CHIPS) for a in range(self.n)]
        return mine, sent, arrivals

    def start(self, ins, outs, sems):
        mine, sent, _ = self._ctx(ins, outs, sems)
        for cp in mine + sent:
            cp.start()

    def finish(self, ins, outs, sems):
        mine, sent, arrivals = self._ctx(ins, outs, sems)
        for cp in arrivals:
            cp.wait_recv()
        for cp in sent:
            cp.wait_send()
        for cp in mine:
            cp.wait()


def _pair_sum(blocks, got, core, name):
    _, R, C = blocks.shape
    tr = _pick(R, (256, 128))

    def body(core_ref, mine_ref, got_ref, o_ref):
        o_ref[...] = (mine_ref[...].astype(f32) + got_ref[...].astype(f32)).astype(o_ref.dtype)

    return pl.pallas_call(
        body, name=name,
        grid_spec=pltpu.PrefetchScalarGridSpec(
            num_scalar_prefetch=1, grid=(N_CHIPS, R // tr),
            in_specs=[pl.BlockSpec((1, tr, C), lambda q, i, core_ref: (2 * q + core_ref[0], i, 0)),
                      pl.BlockSpec((1, tr, C), lambda q, i, core_ref: (q, i, 0))],
            out_specs=pl.BlockSpec((1, tr, C), lambda q, i, core_ref: (q, i, 0))),
        out_shape=jax.ShapeDtypeStruct((N_CHIPS, R, C), blocks.dtype),
        compiler_params=_params(("parallel", "parallel")),
    )(core, blocks, got)


def _run_comm(comm, name):
    n = comm.n

    def body(*refs):
        ins, outs, sems = refs[:n], refs[n:2 * n], refs[2 * n:]
        comm.start(ins, outs, sems)
        if comm.mid is not None:
            comm.mid(ins, outs, sems)
        comm.finish(ins, outs, sems)

    return pl.pallas_call(
        body, name=name, in_specs=[ANY] * n, out_specs=[ANY] * n,
        out_shape=comm.out_shape, scratch_shapes=comm.scratch,
    )(*comm.args)


def _pallas(body, args, *, name, grid, in_specs, out_specs, out_shape, scratch_shapes=(), semantics, comm=None):
    in_specs, out_specs, out_shape, scratch_shapes = list(in_specs), list(out_specs), list(out_shape), list(scratch_shapes)
    if comm is None:
        return pl.pallas_call(
            body, name=name, grid=grid, in_specs=in_specs, out_specs=out_specs, out_shape=out_shape,
            scratch_shapes=scratch_shapes, compiler_params=_params(semantics))(*args)
    a = len(in_specs)
    b = a + comm.n
    c = b + len(out_specs)
    d = c + comm.n
    e = d + len(scratch_shapes)
    total = math.prod(grid)
    mid_step = (3 * total) // 4

    def hosted(*refs):
        step = pl.program_id(0)
        for axis in range(1, len(grid)):
            step = step * grid[axis] + pl.program_id(axis)
        ins, outs, sems = refs[a:b], refs[c:d], refs[e:]

        @pl.when(step == 0)
        def _():
            comm.start(ins, outs, sems)

        body(*refs[:a], *refs[b:c], *refs[d:e])

        if comm.mid is not None:
            @pl.when(step == mid_step)
            def _():
                comm.mid(ins, outs, sems)

        @pl.when(step == total - 1)
        def _():
            comm.finish(ins, outs, sems)

    res = pl.pallas_call(
        hosted, name=name, grid=grid, in_specs=in_specs + [ANY] * comm.n, out_specs=out_specs + [ANY] * comm.n,
        out_shape=out_shape + comm.out_shape, scratch_shapes=scratch_shapes + comm.scratch,
        compiler_params=_params(("arbitrary",) * len(grid)))(*args, *comm.args)
    return list(res[:len(out_specs)]) + [list(res[len(out_specs):])]


def _all_reduce_small(vec, name):
    R, C = vec.shape

    def body(v_ref, o_ref, gath, send_sems, recv_sems):
        x, y, c = _coords()
        me = 4 * x + 2 * y + c
        gath[me] = v_ref[...]

        def copy(k):
            px, py, pc = x ^ ((k >> 2) & 1), y ^ ((k >> 1) & 1), c ^ (k & 1)
            return pltpu.make_async_remote_copy(
                src_ref=v_ref, dst_ref=gath.at[me], send_sem=send_sems.at[k - 1], recv_sem=recv_sems.at[k - 1],
                device_id=(px, py, pc), device_id_type=MESH)

        sent = [copy(k) for k in range(1, N_DEV)]
        for cp in sent:
            cp.start()
        for cp in sent:
            cp.wait()
        total = gath[0]
        for i in range(1, N_DEV):
            total = total + gath[i]
        o_ref[...] = total

    return pl.pallas_call(
        body, name=name,
        in_specs=[pl.BlockSpec(memory_space=pltpu.VMEM)], out_specs=pl.BlockSpec(memory_space=pltpu.VMEM),
        out_shape=jax.ShapeDtypeStruct((R, C), f32),
        scratch_shapes=[pltpu.VMEM((N_DEV, R, C), f32), pltpu.SemaphoreType.DMA((7,)), pltpu.SemaphoreType.DMA((7,))],
    )(vec)


def _gathered_in(g):
    return jnp.transpose(g, (1, 0, 2)).reshape(g.shape[1], -1)


def _col_blocks(dw):
    D, N = dw.shape
    return jnp.transpose(dw.reshape(D, N_DEV, N // N_DEV), (1, 0, 2)).astype(bf16)


def _heads_major(a, n):
    return jnp.transpose(a.reshape(a.shape[0], n, HEAD_DIM), (1, 0, 2))


def _heads_minor(a):
    return jnp.transpose(a, (1, 0, 2)).reshape(a.shape[1], -1)


def kernel(x, g_pre, g_post, w_in_a, w_out_a, sinks_a, w_in_b, w_out_b, w_in_c, b_f_c, w_out_c, loss_target, m_g_pre, m_g_post, m_w_in_a, m_w_out_a, m_sinks_a, m_w_in_b, m_w_out_b, m_w_in_c, m_b_f_c, m_w_out_c, v_g_pre, v_g_post, v_w_in_a, v_w_out_a, v_sinks_a, v_w_in_b, v_w_out_b, v_w_in_c, v_b_f_c, v_w_out_c):
    S, D = x.shape[1], x.shape[2]
    H = D // HEAD_DIM
    BR = H * HEAD_DIM
    n_kv = H // 8
    KV = n_kv * HEAD_DIM
    x0 = x[0]
    target = loss_target[0]
    swa_bias = _swa_bias(H, n_kv)
    w_in = {0: w_in_a, 1: w_in_b, 2: w_in_c}
    w_out = {0: w_out_a, 1: w_out_b, 2: w_out_c}

    saved = []
    xi = x0
    riders = {"in_proj0": [("out", 0)], "mixer0": [("in", 1)],
              "mixer1": [("out", 1), ("in", 2), ("out", 2)], "mixer2": [("in", 3), ("out", 3)]}
    full = {}

    def rider(slot):
        keys = riders.get(slot)
        if not keys:
            return None
        shards = [(w_in if which == "in" else w_out)[i % 3][i // 3].astype(bf16) for which, i in keys]
        return _Gather(shards)

    def landed(slot, arrays):
        full.update(zip(riders[slot], arrays))

    riders["start"] = [("in", 0)]
    landed("start", _run_comm(rider("start"), name="gather_w0"))
    for i in range(DEPTH):
        kind, j = i % 3, i // 3
        nxt = rider(f"mixer{i}")
        W_in = _gathered_in(full["in", i])
        h = _rmsnorm_fwd(xi, g_pre[i:i + 1], name=f"pre_norm{i}")
        st = dict(x=xi, h=h)
        if kind == 2:
            n_main = 4 * BR
            pad = (-W_in.shape[1]) % 128
            W_in = jnp.pad(W_in, ((0, 0), (0, pad)))
            W_f = W_in[:, n_main:n_main + 128]
            fl = _matmul(h, W_f, "nn", f32, name=f"f_proj{i}")
            flT = jnp.transpose(fl[:, :H])
            bcol = b_f_c[j].reshape(H, 1)
            cumT = _fox_cum(flT, bcol, name=f"fox_cum{i}")
            cq, ck = cumT[:, None, :], cumT[:, :, None]
            st.update(flT=flT, bcol=bcol, cq=cq, ck=ck)
        st["W_in"] = W_in
        with_proj = rider(f"in_proj{i}")
        P = _matmul(h, W_in, "nn", bf16, name=f"in_proj{i}", comm=with_proj)
        if with_proj is not None:
            P, arrived = P
            landed(f"in_proj{i}", arrived)
        st["P"] = P
        if kind == 0:
            kh = _heads_major(P[:, BR:BR + KV], n_kv)
            vh = _heads_major(P[:, BR + KV:BR + 2 * KV], n_kv)
            o, u, *rest = _swaT_fwd(P, kh, vh, sinks_a[j], swa_bias, name=f"swa_fwd{i}", comm=nxt)
            st.update(kh=kh, vh=vh)
        elif kind == 1:
            o, u, rc, *rest = _sbT_fwd(P, BR, name=f"sb_fwd{i}", comm=nxt)
            st.update(rc=rc)
        else:
            o, u, lse, *rest = _foxT_fwd(P, cq, ck, BR, name=f"fox_fwd{i}", comm=nxt)
            st.update(lse=lse)
        if nxt is not None:
            landed(f"mixer{i}", rest[0])
        W_out = full["out", i].reshape(BR, D)
        st.update(o=o, u=u, W_out=W_out)
        y = _matmul(u, W_out, "nn", f32, name=f"out_proj{i}")
        st["y"] = y
        xi = _post_fwd(xi, y, g_post[i:i + 1], name=f"post_norm{i}")
        saved.append(st)

    loss_part, dx = _loss_fwd_bwd(xi, target, name="loss")

    dg_pre, dg_post = [None] * DEPTH, [None] * DEPTH
    dsinks = [None, None]
    db_f = None
    recv = [None] * DEPTH
    core = lax.axis_index("c").astype(jnp.int32).reshape(1)
    pending = None
    for i in reversed(range(DEPTH)):
        kind, j = i % 3, i // 3
        st = saved[i]
        dy, dg_post[i] = _post_bwd(dx, st["y"], g_post[i:i + 1], name=f"post_bwd{i}")
        du = _matmul(dy, st["W_out"], "nt", bf16, name=f"du{i}")
        dW_out = _matmul(st["u"], dy, "tn", bf16, name=f"dw_out{i}")
        P = st["P"]
        if kind == 0:
            dq, dz, dkh, dvh, dsk, *rest = _swaT_bwd(P, st["kh"], st["vh"], st["o"], du, sinks_a[j], swa_bias,
                                                     name=f"swa_bwd{i}", comm=pending)
            dsinks[j] = jnp.sum(dsk.reshape(H, BLOCK), axis=1)
            dP = jnp.concatenate([dq, _heads_minor(dkh).astype(bf16), _heads_minor(dvh).astype(bf16), dz], axis=1)
        elif kind == 1:
            dq, dz, dk, dv, *rest = _sbT_bwd(P, st["o"], du, st["rc"], BR, name=f"sb_bwd{i}", comm=pending)
            dP = jnp.concatenate([dq, dk.astype(bf16), dv.astype(bf16), dz], axis=1)
        else:
            dq, dz, dk, dv, dcq, dck, *rest = _foxT_bwd(P, st["o"], du, st["lse"], st["cq"], st["ck"], BR,
                                                       name=f"fox_bwd{i}", comm=pending)
        if pending is not None:
            recv[i + 1] = rest[0]
        if kind == 2:
            dflT, db_col = _fox_cum_bwd(dcq.reshape(H, S), dck.reshape(H, S), st["flT"], st["bcol"],
                                        name=f"fox_cum_bwd{i}")
            db_f = db_col.reshape(H)
            dfl = jnp.pad(jnp.transpose(dflT), ((0, 0), (0, 128 - H))).astype(bf16)
            dP = jnp.concatenate([dq, dk.astype(bf16), dv.astype(bf16), dz, dfl], axis=1)
        dh = _matmul(dP, st["W_in"], "nt", f32, name=f"dh{i}")
        dW_in = _matmul(st["h"], dP, "tn", bf16, name=f"dw_in{i}")
        n_cols = w_in[kind].shape[2] * N_DEV
        blocks = [_col_blocks(dW_in[:, :n_cols]), dW_out.reshape(N_DEV, BR // N_DEV, D)]
        dx, dg_pre[i], got = _pre_bwd(dh, st["x"], g_pre[i:i + 1], dx, name=f"pre_bwd{i}", comm=_SiblingSwap(blocks))
        pending = _ChipScatter([_pair_sum(b, g, core, name=f"pair_sum_{t}{i}")
                                for t, b, g in zip(("in", "out"), blocks, got)])
    recv[0] = _run_comm(pending, name="scatter_dw0")

    small = jnp.concatenate(
        [jnp.concatenate(dg_pre, axis=0).reshape(-1), jnp.concatenate(dg_post, axis=0).reshape(-1),
         jnp.concatenate(dsinks), db_f, loss_part[0, :1]])
    n_small = small.shape[0]
    rows = -(-n_small // 128)
    rows = -(-rows // 8) * 8
    small = jnp.pad(small, (0, rows * 128 - n_small)).reshape(rows, 128)
    total = _all_reduce_small(small, name="reduce_small").reshape(-1)
    o0 = DEPTH * D
    grad_g_pre = total[:o0].reshape(DEPTH, D)
    grad_g_post = total[o0:2 * o0].reshape(DEPTH, D)
    grad_sinks = total[2 * o0:2 * o0 + 2 * H].reshape(2, H)
    grad_b_f = total[2 * o0 + 2 * H:2 * o0 + 3 * H].reshape(1, H)
    loss = total[2 * o0 + 3 * H]

    def small_adam(w, g, m, v, name):
        def body(w_ref, g_ref, m_ref, v_ref, d_ref, nm_ref, nv_ref):
            d_ref[...], nm_ref[...], nv_ref[...] = _adam_math(w_ref[...], g_ref[...], m_ref[...], v_ref[...])
        vm = pl.BlockSpec(memory_space=pltpu.VMEM)
        return pl.pallas_call(body, name=name, in_specs=[vm] * 4, out_specs=[vm] * 3,
                              out_shape=[jax.ShapeDtypeStruct(w.shape, f32)] * 3)(w, g, m, v)

    upd = {}
    upd["g_pre"] = (grad_g_pre,) + tuple(small_adam(g_pre, grad_g_pre, m_g_pre, v_g_pre, "adam_g_pre"))
    upd["g_post"] = (grad_g_post,) + tuple(small_adam(g_post, grad_g_post, m_g_post, v_g_post, "adam_g_post"))
    upd["sinks_a"] = (grad_sinks,) + tuple(small_adam(sinks_a, grad_sinks, m_sinks_a, v_sinks_a, "adam_sinks"))
    upd["b_f_c"] = (grad_b_f,) + tuple(small_adam(b_f_c, grad_b_f, m_b_f_c, v_b_f_c, "adam_b_f"))

    def big(i, which, w, m, v):
        return _adamw_sum(recv[i][which], w, m, v, name=f"adam_{'in' if which == 0 else 'out'}{i}")

    a_in = [big(i, 0, w_in_a[jj], m_w_in_a[jj], v_w_in_a[jj]) for jj, i in enumerate((0, 3))]
    a_out = [big(i, 1, w_out_a[jj], m_w_out_a[jj], v_w_out_a[jj]) for jj, i in enumerate((0, 3))]
    upd["w_in_a"] = tuple(jnp.stack([a_in[0][t], a_in[1][t]]) for t in range(4))
    upd["w_out_a"] = tuple(jnp.stack([a_out[0][t], a_out[1][t]]) for t in range(4))
    upd["w_in_b"] = tuple(t[None] for t in big(1, 0, w_in_b[0], m_w_in_b[0], v_w_in_b[0]))
    upd["w_out_b"] = tuple(t[None] for t in big(1, 1, w_out_b[0], m_w_out_b[0], v_w_out_b[0]))
    upd["w_in_c"] = tuple(t[None] for t in big(2, 0, w_in_c[0], m_w_in_c[0], v_w_in_c[0]))
    upd["w_out_c"] = tuple(t[None] for t in big(2, 1, w_out_c[0], m_w_out_c[0], v_w_out_c[0]))

    names = ["g_pre", "g_post", "w_in_a", "w_out_a", "sinks_a", "w_in_b", "w_out_b", "w_in_c", "b_f_c", "w_out_c"]
    return (loss, dx[None], *[upd[k][0] for k in names], *[upd[k][1] for k in names],
            *[upd[k][2] for k in names], *[upd[k][3] for k in names])
```

```python
import functools
import math

import numpy as np
import jax
import jax.numpy as jnp
from jax import lax
from jax.experimental import pallas as pl
from jax.experimental.pallas import tpu as pltpu

HEAD_DIM = 64
HP = 4
HW = HP * HEAD_DIM
BLOCK = 128
NORM_EPS = 1e-6
NEG = -1e30
N_DEV = 8
DEPTH = 4
ADAM_LR, ADAM_B1, ADAM_B2, ADAM_EPS, ADAM_WD, ADAM_STEP = 0.001, 0.9, 0.999, 1e-8, 0.01, 10
VMEM_LIMIT = 56 * 1024 * 1024

bf16 = jnp.bfloat16
f32 = jnp.float32
MESH = pl.DeviceIdType.MESH
ANY = pl.BlockSpec(memory_space=pl.ANY)
SMEM = pl.BlockSpec(memory_space=pltpu.SMEM)

NN = (((1,), (0,)), ((), ()))
NT = (((1,), (1,)), ((), ()))
TN = (((0,), (0,)), ((), ()))


def _dot(a, b, dims=NN):
    return lax.dot_general(a, b, dims, preferred_element_type=f32)


def _params(sem):
    return pltpu.CompilerParams(dimension_semantics=sem, vmem_limit_bytes=VMEM_LIMIT)


def _attn_tile(S):
    return 512 if S % 512 == 0 and S >= 1024 else 128


def _pick(n, pref):
    for t in pref:
        if n % t == 0:
            return t
    return n


def _matmul(a, b, mode, out_dtype, name, comm=None):
    if mode == "nn":
        (M, K), (K2, N) = a.shape, b.shape
    elif mode == "nt":
        (M, K), (N, K2) = a.shape, b.shape
    else:
        (K, M), (K2, N) = a.shape, b.shape
    assert K == K2, (a.shape, b.shape, mode)
    tm = _pick(M, (1024, 512, 256, 128))
    tn = _pick(N, (1024, 768, 640, 512, 256, 128))
    tk = _pick(K, (2048, 1664, 1536, 1024, 512, 640, 256, 128))
    nk = K // tk
    dims = {"nn": NN, "nt": NT, "tn": TN}[mode]

    def body(a_ref, b_ref, o_ref, acc_ref):
        if nk == 1:
            o_ref[...] = _dot(a_ref[...], b_ref[...], dims).astype(o_ref.dtype)
            return
        k = pl.program_id(2)

        @pl.when(k == 0)
        def _():
            acc_ref[...] = jnp.zeros_like(acc_ref)

        acc_ref[...] += _dot(a_ref[...], b_ref[...], dims)

        @pl.when(k == nk - 1)
        def _():
            o_ref[...] = acc_ref[...].astype(o_ref.dtype)

    if mode == "tn":
        a_spec = pl.BlockSpec((tk, tm), lambda i, j, k: (k, i))
    else:
        a_spec = pl.BlockSpec((tm, tk), lambda i, j, k: (i, k))
    if mode == "nt":
        b_spec = pl.BlockSpec((tn, tk), lambda i, j, k: (j, k))
    else:
        b_spec = pl.BlockSpec((tk, tn), lambda i, j, k: (k, j))
    res = _pallas(
        body, (a, b), name=name,
        grid=(M // tm, N // tn, nk),
        in_specs=[a_spec, b_spec],
        out_specs=[pl.BlockSpec((tm, tn), lambda i, j, k: (i, j))],
        out_shape=[jax.ShapeDtypeStruct((M, N), out_dtype)],
        scratch_shapes=[pltpu.VMEM((tm, tn) if nk > 1 else (8, 128), f32)],
        semantics=("parallel", "parallel", "arbitrary"), comm=comm)
    return res[0] if comm is None else (res[0], res[1])


ROWS = 256


def _rows(S):
    return ROWS if S % ROWS == 0 else S


def _rmsnorm_fwd(x, g, name, comm=None):
    S, D = x.shape
    tr = _rows(S)

    def body(x_ref, g_ref, h_ref):
        xv = x_ref[...]
        r = lax.rsqrt(jnp.mean(xv * xv, axis=-1, keepdims=True) + NORM_EPS)
        h_ref[...] = (xv * r * g_ref[...]).astype(h_ref.dtype)

    res = _pallas(
        body, (x, g), name=name, grid=(S // tr,),
        in_specs=[pl.BlockSpec((tr, D), lambda i: (i, 0)), pl.BlockSpec((1, D), lambda i: (0, 0))],
        out_specs=[pl.BlockSpec((tr, D), lambda i: (i, 0))],
        out_shape=[jax.ShapeDtypeStruct((S, D), bf16)],
        semantics=("parallel",), comm=comm)
    return res[0] if comm is None else (res[0], res[1])


def _post_fwd(x, y, g, name):
    S, D = x.shape
    tr = _rows(S)

    def body(x_ref, y_ref, g_ref, o_ref):
        yv = y_ref[...]
        r = lax.rsqrt(jnp.mean(yv * yv, axis=-1, keepdims=True) + NORM_EPS)
        o_ref[...] = x_ref[...] + yv * r * g_ref[...]

    row = pl.BlockSpec((tr, D), lambda i: (i, 0))
    return pl.pallas_call(
        body, name=name, grid=(S // tr,),
        in_specs=[row, row, pl.BlockSpec((1, D), lambda i: (0, 0))],
        out_specs=row,
        out_shape=jax.ShapeDtypeStruct((S, D), f32),
        compiler_params=_params(("parallel",)),
    )(x, y, g)


def _loss_fwd_bwd(y, t, name):
    S, D = y.shape
    tr = _rows(S)

    def body(y_ref, t_ref, l_ref, d_ref):
        @pl.when(pl.program_id(0) == 0)
        def _():
            l_ref[...] = jnp.zeros_like(l_ref)

        e = y_ref[...] - t_ref[...]
        d_ref[...] = e * (1.0 / D)
        part = 0.5 * jnp.sum(jnp.sum(e * e, axis=-1, keepdims=True) * (1.0 / D), axis=0, keepdims=True)
        l_ref[...] += jnp.broadcast_to(part, l_ref.shape)

    row = pl.BlockSpec((tr, D), lambda i: (i, 0))
    return pl.pallas_call(
        body, name=name, grid=(S // tr,),
        in_specs=[row, row],
        out_specs=[pl.BlockSpec((8, 128), lambda i: (0, 0)), row],
        out_shape=[jax.ShapeDtypeStruct((8, 128), f32), jax.ShapeDtypeStruct((S, D), f32)],
        compiler_params=_params(("arbitrary",)),
    )(y, t)


def _post_bwd(dxn, y, g, name):
    S, D = y.shape
    tr = _rows(S)

    def body(d_ref, y_ref, g_ref, dy_ref, dg_ref):
        @pl.when(pl.program_id(0) == 0)
        def _():
            dg_ref[...] = jnp.zeros_like(dg_ref)

        yv = y_ref[...]
        d = d_ref[...]
        r = lax.rsqrt(jnp.mean(yv * yv, axis=-1, keepdims=True) + NORM_EPS)
        n = yv * r
        dn = d * g_ref[...]
        dg_ref[...] += jnp.sum(d * n, axis=0, keepdims=True)
        dy_ref[...] = (r * (dn - n * jnp.mean(dn * n, axis=-1, keepdims=True))).astype(dy_ref.dtype)

    row = pl.BlockSpec((tr, D), lambda i: (i, 0))
    vec = pl.BlockSpec((1, D), lambda i: (0, 0))
    return pl.pallas_call(
        body, name=name, grid=(S // tr,),
        in_specs=[row, row, vec],
        out_specs=[row, vec],
        out_shape=[jax.ShapeDtypeStruct((S, D), bf16), jax.ShapeDtypeStruct((1, D), f32)],
        compiler_params=_params(("arbitrary",)),
    )(dxn, y, g)


def _pre_bwd(dh, x, g, dres, name, comm=None):
    S, D = x.shape
    tr = _rows(S)

    def body(dh_ref, x_ref, g_ref, dres_ref, dx_ref, dg_ref):
        @pl.when(pl.program_id(0) == 0)
        def _():
            dg_ref[...] = jnp.zeros_like(dg_ref)

        xv = x_ref[...]
        d = dh_ref[...]
        r = lax.rsqrt(jnp.mean(xv * xv, axis=-1, keepdims=True) + NORM_EPS)
        n = xv * r
        dn = d * g_ref[...]
        dg_ref[...] += jnp.sum(d * n, axis=0, keepdims=True)
        dx_ref[...] = dres_ref[...] + r * (dn - n * jnp.mean(dn * n, axis=-1, keepdims=True))

    row = pl.BlockSpec((tr, D), lambda i: (i, 0))
    vec = pl.BlockSpec((1, D), lambda i: (0, 0))
    return _pallas(
        body, (dh, x, g, dres), name=name, grid=(S // tr,),
        in_specs=[row, row, vec, row],
        out_specs=[row, vec],
        out_shape=[jax.ShapeDtypeStruct((S, D), f32), jax.ShapeDtypeStruct((1, D), f32)],
        semantics=("arbitrary",), comm=comm)


def _sigmoid(x):
    return 1.0 / (1.0 + jnp.exp(-x))


def _gate_fwd(o, z):
    zf = z.astype(f32)
    return o * (zf * _sigmoid(zf))


def _gate_bwd(du, o, z):
    zf = z.astype(f32)
    sig = _sigmoid(zf)
    do = du * (zf * sig)
    dz = du * o * (sig * (1.0 + zf * (1.0 - sig)))
    return do, dz


def _iota2(shape, dim):
    return lax.broadcasted_iota(jnp.int32, shape, dim)


def _swa_specs(S, BR, KV, G):
    gw = G * HEAD_DIM
    qspec = pl.BlockSpec((BLOCK, gw), lambda h, n: (n, h))
    zoff = (BR + 2 * KV) // gw
    zspec = pl.BlockSpec((BLOCK, gw), lambda h, n: (n, zoff + h))
    cur = pl.BlockSpec((1, BLOCK, HEAD_DIM), lambda h, n: (h, n, 0))
    prev = pl.BlockSpec((1, BLOCK, HEAD_DIM), lambda h, n: (h, jnp.maximum(n - 1, 0), 0))
    return qspec, zspec, cur, prev


def _split3(x):
    x1 = x.astype(bf16)
    r1 = x - x1.astype(f32)
    x2 = r1.astype(bf16)
    x3 = (r1 - x2.astype(f32)).astype(bf16)
    return x1, x2, x3


def _split2(x):
    x1 = x.astype(bf16)
    return x1, (x - x1.astype(f32)).astype(bf16)


def _pair_specs(S, BR, T, kmap):
    nb = BR // HW
    q = pl.BlockSpec((T, HW), lambda p, i, j: (i, p))
    k = pl.BlockSpec((T, HW), lambda p, i, j: (kmap(i, j), nb + p))
    v = pl.BlockSpec((T, HW), lambda p, i, j: (kmap(i, j), 2 * nb + p))
    z = pl.BlockSpec((T, HW), lambda p, i, j: (i, 3 * nb + p))
    return q, k, v, z


def _fox_cum(flT, b, name):
    H, S = flT.shape
    tj = _pick(S, (256, 128))

    def body(fl_ref, b_ref, c_ref):
        j = pl.program_id(0)
        x = fl_ref[...] + b_ref[...]
        logf = jnp.minimum(x, 0.0) - jnp.log(1.0 + jnp.exp(-jnp.abs(x)))
        tri = (_iota2((S, tj), 0) <= j * tj + _iota2((S, tj), 1)).astype(bf16)
        c_ref[...] = sum(_dot(part, tri) for part in _split3(logf))

    return pl.pallas_call(
        body, name=name, grid=(S // tj,),
        in_specs=[pl.BlockSpec((H, S), lambda j: (0, 0)), pl.BlockSpec((H, 1), lambda j: (0, 0))],
        out_specs=pl.BlockSpec((H, tj), lambda j: (0, j)),
        out_shape=jax.ShapeDtypeStruct((H, S), f32),
        compiler_params=_params(("parallel",)),
    )(flT, b)


def _fox_cum_bwd(dcq, dck, flT, b, name):
    H, S = flT.shape
    tj = _pick(S, (256, 128))

    def body(dcq_ref, dck_ref, fl_ref, b_ref, o_ref, db_ref):
        j = pl.program_id(0)

        @pl.when(j == 0)
        def _():
            db_ref[...] = jnp.zeros_like(db_ref)

        tri = (_iota2((S, tj), 0) >= j * tj + _iota2((S, tj), 1)).astype(bf16)
        dlogf = sum(_dot(part, tri) for part in _split3(dcq_ref[...] - dck_ref[...]))
        x = fl_ref[...] + b_ref[...]
        dfl = dlogf * _sigmoid(-x)
        o_ref[...] = dfl
        db_ref[...] += jnp.sum(dfl, axis=-1, keepdims=True)

    blk = pl.BlockSpec((H, tj), lambda j: (0, j))
    whole = pl.BlockSpec((H, S), lambda j: (0, 0))
    col = pl.BlockSpec((H, 1), lambda j: (0, 0))
    return pl.pallas_call(
        body, name=name, grid=(S // tj,),
        in_specs=[whole, whole, blk, col],
        out_specs=[blk, col],
        out_shape=[jax.ShapeDtypeStruct((H, S), f32), jax.ShapeDtypeStruct((H, 1), f32)],
        compiler_params=_params(("arbitrary",)),
    )(dcq, dck, flT, b)


def _hsl(hh):
    return slice(hh * HEAD_DIM, (hh + 1) * HEAD_DIM)


def _scaled(q, scale):
    assert math.log2(scale).is_integer()
    return (q.astype(f32) * scale).astype(bf16)


def _swa_bias(H, n_kv):
    G = H // n_kv
    slopes = 2.0 ** (-8.0 * np.arange(1, H + 1, dtype=np.float32) / H)
    dist = (np.arange(BLOCK)[None, :] + BLOCK - np.arange(2 * BLOCK)[:, None]).astype(np.float32)
    valid = (dist >= 0) & (dist < BLOCK)
    out = np.empty((n_kv, 2 * BLOCK, G * BLOCK), np.float32)
    for h in range(H):
        out[h // G][:, (h % G) * BLOCK:(h % G + 1) * BLOCK] = np.where(valid, -(slopes[h] * dist), np.float32(NEG))
    return jnp.asarray(out)


def _swaT_parts(q_ref, kc_ref, kp_ref, vc_ref, vp_ref, bias_ref, sink_ref, kvh, n, G, scale):
    qg = jnp.concatenate([q_ref[:, _hsl(g)] for g in range(G)], axis=0)
    kband = jnp.concatenate([kp_ref[0], kc_ref[0]], axis=0)
    vband = jnp.concatenate([vp_ref[0], vc_ref[0]], axis=0)
    s = _dot(kband, qg, NT) * scale + bias_ref[0]
    s = jnp.where(_iota2(s.shape, 0) + jnp.where(n > 0, BLOCK, 0) >= BLOCK, s, NEG)
    sink = jnp.concatenate([jnp.full((1, BLOCK), sink_ref[kvh * G + g], f32) for g in range(G)], axis=1)
    m = jnp.maximum(jnp.max(s, axis=0, keepdims=True), sink)
    p = jnp.exp(s - m)
    ps = jnp.exp(sink - m)
    den = jnp.sum(p, axis=0, keepdims=True) + ps
    return qg, kband, vband, p, ps, den


def _swaT_specs(S, BR, KV, G):
    qspec, zspec, cur, prev = _swa_specs(S, BR, KV, G)
    bias = pl.BlockSpec((1, 2 * BLOCK, G * BLOCK), lambda h, n: (h, 0, 0))
    return qspec, zspec, cur, prev, bias


def _swaT_fwd(P, kh, vh, sinks, bias, name, comm=None):
    S = P.shape[0]
    n_kv = kh.shape[0]
    H = sinks.shape[0]
    G = H // n_kv
    BR, KV = H * HEAD_DIM, n_kv * HEAD_DIM
    scale = HEAD_DIM ** -0.5
    assert (BR + 2 * KV) % (G * HEAD_DIM) == 0

    def body(q_ref, z_ref, kc_ref, kp_ref, vc_ref, vp_ref, bias_ref, sink_ref, o_ref, u_ref):
        kvh, n = pl.program_id(0), pl.program_id(1)
        _, _, vband, p, _, den = _swaT_parts(q_ref, kc_ref, kp_ref, vc_ref, vp_ref, bias_ref, sink_ref, kvh, n, G, scale)
        o = jnp.transpose(_dot(vband, p.astype(bf16), TN) / den)
        for g in range(G):
            oh = o[g * BLOCK:(g + 1) * BLOCK]
            o_ref[:, _hsl(g)] = oh.astype(o_ref.dtype)
            u_ref[:, _hsl(g)] = _gate_fwd(oh, z_ref[:, _hsl(g)]).astype(u_ref.dtype)

    qspec, zspec, cur, prev, bspec = _swaT_specs(S, BR, KV, G)
    ospec = pl.BlockSpec((BLOCK, G * HEAD_DIM), lambda h, n: (n, h))
    return _pallas(
        body, (P, P, kh, kh, vh, vh, bias, sinks), name=name, grid=(n_kv, S // BLOCK),
        in_specs=[qspec, zspec, cur, prev, cur, prev, bspec, SMEM],
        out_specs=[ospec, ospec],
        out_shape=[jax.ShapeDtypeStruct((S, BR), bf16)] * 2,
        semantics=("parallel", "parallel"), comm=comm)


def _swaT_bwd(P, kh, vh, o, du, sinks, bias, name, comm=None):
    S = P.shape[0]
    n_kv = kh.shape[0]
    H = sinks.shape[0]
    G = H // n_kv
    BR, KV = H * HEAD_DIM, n_kv * HEAD_DIM
    scale = HEAD_DIM ** -0.5

    def body(q_ref, z_ref, kc_ref, kp_ref, vc_ref, vp_ref, o_ref, du_ref, bias_ref, sink_ref,
             dq_ref, dz_ref, dk_ref, dv_ref, ds_ref):
        kvh, n = pl.program_id(0), pl.program_id(1)

        @pl.when(n == 0)
        def _():
            dk_ref[...] = jnp.zeros_like(dk_ref)
            dv_ref[...] = jnp.zeros_like(dv_ref)
            ds_ref[...] = jnp.zeros_like(ds_ref)

        dos, prods = [], []
        for g in range(G):
            oh = o_ref[:, _hsl(g)].astype(f32)
            doh, dzh = _gate_bwd(du_ref[:, _hsl(g)].astype(f32), oh, z_ref[:, _hsl(g)])
            dz_ref[:, _hsl(g)] = dzh.astype(dz_ref.dtype)
            dos.append(doh.astype(bf16))
            prods.append(doh * oh)
        dog = jnp.concatenate(dos, axis=0)
        delta = jnp.sum(jnp.transpose(jnp.concatenate(prods, axis=0)), axis=0, keepdims=True)
        qg, kband, vband, p, ps, den = _swaT_parts(q_ref, kc_ref, kp_ref, vc_ref, vp_ref, bias_ref, sink_ref,
                                                   kvh, n, G, scale)
        inv = 1.0 / den
        pn = p * inv
        ds = (pn * (_dot(vband, dog, NT) - delta)).astype(bf16)
        dq = jnp.transpose(_dot(kband, ds, TN)) * scale
        for g in range(G):
            dq_ref[:, _hsl(g)] = dq[g * BLOCK:(g + 1) * BLOCK].astype(dq_ref.dtype)
        dkb = _dot(ds, qg) * scale
        dvb = _dot(pn.astype(bf16), dog)
        ds_ref[0] -= ps * inv * delta
        row_c = pl.multiple_of(n * BLOCK, BLOCK)
        dk_ref[0, pl.ds(row_c, BLOCK), :] += dkb[BLOCK:]
        dv_ref[0, pl.ds(row_c, BLOCK), :] += dvb[BLOCK:]

        @pl.when(n > 0)
        def _():
            row_p = pl.multiple_of((n - 1) * BLOCK, BLOCK)
            dk_ref[0, pl.ds(row_p, BLOCK), :] += dkb[:BLOCK]
            dv_ref[0, pl.ds(row_p, BLOCK), :] += dvb[:BLOCK]

    qspec, zspec, cur, prev, bspec = _swaT_specs(S, BR, KV, G)
    ospec = pl.BlockSpec((BLOCK, G * HEAD_DIM), lambda h, n: (n, h))
    full = pl.BlockSpec((1, S, HEAD_DIM), lambda h, n: (h, 0, 0))
    return _pallas(
        body, (P, P, kh, kh, vh, vh, o, du, bias, sinks), name=name, grid=(n_kv, S // BLOCK),
        in_specs=[qspec, zspec, cur, prev, cur, prev, ospec, ospec, bspec, SMEM],
        out_specs=[ospec, ospec, full, full, pl.BlockSpec((1, 1, G * BLOCK), lambda h, n: (h, 0, 0))],
        out_shape=[jax.ShapeDtypeStruct((S, BR), bf16)] * 2
        + [jax.ShapeDtypeStruct((n_kv, S, HEAD_DIM), f32)] * 2
        + [jax.ShapeDtypeStruct((n_kv, 1, G * BLOCK), f32)],
        semantics=("parallel", "arbitrary"), comm=comm)


def _foxT_fwd(P, cq, ck, BR, name, comm=None):
    S = P.shape[0]
    H = BR // HEAD_DIM
    T = _attn_tile(S)
    nq = S // T
    scale = HEAD_DIM ** -0.5
    kmap = lambda i, j: jnp.minimum(i, j)

    def body(q_ref, k_ref, v_ref, z_ref, cq_ref, ck_ref, o_ref, u_ref, lse_ref, m_s, l_s, acc_s):
        qb, kb = pl.program_id(1), pl.program_id(2)

        @pl.when(kb == 0)
        def _():
            m_s[...] = jnp.full_like(m_s, NEG)
            l_s[...] = jnp.zeros_like(l_s)
            acc_s[...] = jnp.zeros_like(acc_s)

        def tile(diagonal):
            for hh in range(HP):
                sl = _hsl(hh)
                s = _dot(k_ref[:, sl], _scaled(q_ref[:, sl], scale), NT) + cq_ref[hh] - ck_ref[hh]
                if diagonal:
                    s = jnp.where(_iota2((T, T), 0) <= _iota2((T, T), 1), s, NEG)
                m_old = m_s[hh]
                m_new = jnp.maximum(m_old, jnp.max(s, axis=0, keepdims=True))
                alpha = jnp.exp(m_old - m_new)
                p = jnp.exp(s - m_new)
                l_s[hh] = alpha * l_s[hh] + jnp.sum(p, axis=0, keepdims=True)
                acc_s[hh] = alpha * acc_s[hh] + _dot(v_ref[:, sl], p.astype(bf16), TN)
                m_s[hh] = m_new

        @pl.when(kb < qb)
        def _():
            tile(False)

        @pl.when(kb == qb)
        def _():
            tile(True)
            for hh in range(HP):
                sl = _hsl(hh)
                oh = jnp.transpose(acc_s[hh] / l_s[hh])
                o_ref[:, sl] = oh.astype(o_ref.dtype)
                u_ref[:, sl] = _gate_fwd(oh, z_ref[:, sl]).astype(u_ref.dtype)
                lse_ref[hh] = m_s[hh] + jnp.log(l_s[hh])

    q, k, v, z = _pair_specs(S, BR, T, kmap)
    rowq = pl.BlockSpec((HP, 1,T), lambda p, i, j: (p, 0, i))
    colk = pl.BlockSpec((HP, T,1), lambda p, i, j: (p, kmap(i, j), 0))
    ospec = pl.BlockSpec((T, HW), lambda p, i, j: (i, p))
    return _pallas(
        body, (P, P, P, P, cq, ck), name=name, grid=(H // HP, nq, nq),
        in_specs=[q, k, v, z, rowq, colk],
        out_specs=[ospec, ospec, rowq],
        out_shape=[jax.ShapeDtypeStruct((S, BR), bf16)] * 2 + [jax.ShapeDtypeStruct((H, 1, S), f32)],
        scratch_shapes=[pltpu.VMEM((HP, 1,T), f32), pltpu.VMEM((HP, 1,T), f32), pltpu.VMEM((HP, HEAD_DIM,T), f32)],
        semantics=("parallel", "arbitrary", "arbitrary"), comm=comm)


def _foxT_bwd(P, o, du, lse, cq, ck, BR, name, comm=None):
    S = P.shape[0]
    H = BR // HEAD_DIM
    T = _attn_tile(S)
    nq = S // T
    scale = HEAD_DIM ** -0.5
    kmap = lambda i, j: jnp.minimum(i, j)

    def body(q_ref, k_ref, v_ref, z_ref, o_ref, du_ref, lse_ref, cq_ref, ck_ref,
             dq_ref, dz_ref, dk_ref, dv_ref, dcq_ref, dck_ref, do_s, delta_s, dq_s, dcq_s):
        qb, kb = pl.program_id(1), pl.program_id(2)

        @pl.when(jnp.logical_and(qb == 0, kb == 0))
        def _():
            dk_ref[...] = jnp.zeros_like(dk_ref)
            dv_ref[...] = jnp.zeros_like(dv_ref)
            dck_ref[...] = jnp.zeros_like(dck_ref)

        @pl.when(kb == 0)
        def _():
            for hh in range(HP):
                sl = _hsl(hh)
                oh = o_ref[:, sl].astype(f32)
                doh, dzh = _gate_bwd(du_ref[:, sl].astype(f32), oh, z_ref[:, sl])
                dz_ref[:, sl] = dzh.astype(dz_ref.dtype)
                do_s[hh] = doh.astype(bf16)
                delta_s[hh] = jnp.sum(jnp.transpose(doh * oh), axis=0, keepdims=True)
            dq_s[...] = jnp.zeros_like(dq_s)
            dcq_s[...] = jnp.zeros_like(dcq_s)

        def tile(diagonal):
            rows = pl.ds(pl.multiple_of(kb * T, T), T)
            for hh in range(HP):
                sl = _hsl(hh)
                qh, kh, vh, dob = _scaled(q_ref[:, sl], scale), k_ref[:, sl], v_ref[:, sl], do_s[hh]
                s = _dot(kh, qh, NT) + cq_ref[hh] - ck_ref[hh]
                p = jnp.exp(s - lse_ref[hh])
                if diagonal:
                    p = jnp.where(_iota2((T, T), 0) <= _iota2((T, T), 1), p, 0.0)
                ds = p * (_dot(vh, dob, NT) - delta_s[hh])
                dsb = ds.astype(bf16)
                dq_s[hh] += _dot(kh, dsb, TN)
                dk_ref[rows, sl] += _dot(dsb, qh)
                dv_ref[rows, sl] += _dot(p.astype(bf16), dob)
                dcq_s[hh] += jnp.sum(ds, axis=0, keepdims=True)
                dck_ref[hh, rows, :] += jnp.sum(ds, axis=1, keepdims=True)

        @pl.when(kb < qb)
        def _():
            tile(False)

        @pl.when(kb == qb)
        def _():
            tile(True)
            for hh in range(HP):
                dq_ref[:, _hsl(hh)] = (jnp.transpose(dq_s[hh]) * scale).astype(dq_ref.dtype)
                dcq_ref[hh] = dcq_s[hh]

    q, k, v, z = _pair_specs(S, BR, T, kmap)
    rowq = pl.BlockSpec((HP, 1,T), lambda p, i, j: (p, 0, i))
    colk = pl.BlockSpec((HP, T,1), lambda p, i, j: (p, kmap(i, j), 0))
    ospec = pl.BlockSpec((T, HW), lambda p, i, j: (i, p))
    full = pl.BlockSpec((S, HW), lambda p, i, j: (0, p))
    return _pallas(
        body, (P, P, P, P, o, du, lse, cq, ck), name=name, grid=(H // HP, nq, nq),
        in_specs=[q, k, v, z, ospec, ospec, rowq, rowq, colk],
        out_specs=[ospec, ospec, full, full, rowq, pl.BlockSpec((HP, S,1), lambda p, i, j: (p, 0, 0))],
        out_shape=[jax.ShapeDtypeStruct((S, BR), bf16)] * 2 + [jax.ShapeDtypeStruct((S, BR), f32)] * 2
        + [jax.ShapeDtypeStruct((H, 1, S), f32), jax.ShapeDtypeStruct((H, S, 1), f32)],
        scratch_shapes=[pltpu.VMEM((HP, T,HEAD_DIM), bf16), pltpu.VMEM((HP, 1,T), f32),
                        pltpu.VMEM((HP, HEAD_DIM,T), f32), pltpu.VMEM((HP, 1,T), f32)],
        semantics=("parallel", "arbitrary", "arbitrary"), comm=comm)


def _sbT_logs(kc, qh, before):
    l = _dot(kc, qh, NT)
    minus_abs = lax.bitcast_convert_type(lax.bitcast_convert_type(l, jnp.int32) | jnp.int32(-2 ** 31), f32)
    lsig = jnp.minimum(l, 0.0) - jnp.log(1.0 + jnp.exp(minus_abs))
    lf = lsig - l
    if before is not None:
        lf = jnp.where(before, lf, 0.0)
    return lsig, lf


def _chunk_rows(c):
    return slice(c * BLOCK, (c + 1) * BLOCK)


def _sbT_suffix_tile(lf, tri, nc, two_pass):
    if not two_pass:
        hi = lf.astype(bf16)
        return jnp.concatenate([_dot(tri, hi[_chunk_rows(c)]) for c in range(nc)], axis=0)
    hi, lo = _split2(lf)
    tri2 = jnp.concatenate([tri, tri], axis=1)
    return jnp.concatenate(
        [_dot(tri2, jnp.concatenate([hi[_chunk_rows(c)], lo[_chunk_rows(c)]], axis=0)) for c in range(nc)], axis=0)


def _sbT_fwd(P, BR, name, comm=None):
    S = P.shape[0]
    H = BR // HEAD_DIM
    T = _attn_tile(S)
    nq = S // T
    nc = T // BLOCK
    scale = HEAD_DIM ** -0.5
    kmap = lambda i, j: jnp.maximum(i - j, 0)

    def body(q_ref, k_ref, v_ref, z_ref, o_ref, u_ref, rc_ref, r_s, acc_s):
        qb, j = pl.program_id(1), pl.program_id(2)

        @pl.when(j == 0)
        def _():
            r_s[...] = jnp.zeros_like(r_s)
            acc_s[...] = jnp.zeros_like(acc_s)

        def tile(diagonal):
            ii, jj = _iota2((BLOCK, BLOCK), 0), _iota2((BLOCK, BLOCK), 1)
            tri = (jj > ii).astype(bf16)
            before = (_iota2((T, T), 0) < _iota2((T, T), 1)) if diagonal else None
            for hh in range(HP):
                sl = _hsl(hh)
                lsig, lf = _sbT_logs(k_ref[:, sl], _scaled(q_ref[:, sl], scale), before)
                x = lsig + _sbT_suffix_tile(lf, tri, nc, two_pass=True)
                r = r_s[hh]
                rc_ref[hh, 0] = r
                parts = [None] * nc
                for c in reversed(range(nc)):
                    parts[c] = jnp.exp(x[_chunk_rows(c)] + r)
                    r = r + jnp.sum(lf[_chunk_rows(c)], axis=0, keepdims=True)
                r_s[hh] = r
                a = jnp.concatenate(parts, axis=0)
                if diagonal:
                    a = jnp.where(before, a, 0.0)
                acc_s[hh] += _dot(v_ref[:, sl], a.astype(bf16), TN)

        @pl.when(j > 0)
        def _():
            @pl.when(j <= qb)
            def _():
                tile(False)

        @pl.when(j == 0)
        def _():
            tile(True)

        @pl.when(j == qb)
        def _():
            for hh in range(HP):
                sl = _hsl(hh)
                oh = jnp.transpose(acc_s[hh])
                o_ref[:, sl] = oh.astype(o_ref.dtype)
                u_ref[:, sl] = _gate_fwd(oh, z_ref[:, sl]).astype(u_ref.dtype)

    q, k, v, z = _pair_specs(S, BR, T, kmap)
    ospec = pl.BlockSpec((T, HW), lambda p, i, j: (i, p))
    rc = pl.BlockSpec((HP, 1,1, T), lambda p, i, j: (p, kmap(i, j), 0, i))
    return _pallas(
        body, (P, P, P, P), name=name, grid=(H // HP, nq, nq),
        in_specs=[q, k, v, z],
        out_specs=[ospec, ospec, rc],
        out_shape=[jax.ShapeDtypeStruct((S, BR), bf16)] * 2 + [jax.ShapeDtypeStruct((H, nq, 1, S), f32)],
        scratch_shapes=[pltpu.VMEM((HP, 1,T), f32), pltpu.VMEM((HP, HEAD_DIM,T), f32)],
        semantics=("parallel", "arbitrary", "arbitrary"), comm=comm)


def _sbT_bwd(P, o, du, rc, BR, name, comm=None):
    S = P.shape[0]
    H = BR // HEAD_DIM
    T = _attn_tile(S)
    nq = S // T
    nc = T // BLOCK
    scale = HEAD_DIM ** -0.5
    kmap = lambda i, j: jnp.minimum(i, j)

    def body(q_ref, k_ref, v_ref, z_ref, o_ref, du_ref, rc_ref,
             dq_ref, dz_ref, dk_ref, dv_ref, do_s, dq_s, g_s):
        qb, kb = pl.program_id(1), pl.program_id(2)

        @pl.when(jnp.logical_and(qb == 0, kb == 0))
        def _():
            dk_ref[...] = jnp.zeros_like(dk_ref)
            dv_ref[...] = jnp.zeros_like(dv_ref)

        @pl.when(kb == 0)
        def _():
            for hh in range(HP):
                sl = _hsl(hh)
                doh, dzh = _gate_bwd(du_ref[:, sl].astype(f32), o_ref[:, sl].astype(f32), z_ref[:, sl])
                dz_ref[:, sl] = dzh.astype(dz_ref.dtype)
                do_s[hh] = doh.astype(bf16)
            dq_s[...] = jnp.zeros_like(dq_s)
            g_s[...] = jnp.zeros_like(g_s)

        def tile(diagonal):
            ii, jj = _iota2((BLOCK, BLOCK), 0), _iota2((BLOCK, BLOCK), 1)
            tri_suffix = (jj > ii).astype(bf16)
            tri_prefix = (jj < ii).astype(bf16)
            before = (_iota2((T, T), 0) < _iota2((T, T), 1)) if diagonal else None
            out_rows = pl.ds(pl.multiple_of(kb * T, T), T)
            for hh in range(HP):
                sl = _hsl(hh)
                qh, kh, dob = _scaled(q_ref[:, sl], scale), k_ref[:, sl], do_s[hh]
                lsig, lf = _sbT_logs(kh, qh, before)
                x = lsig + _sbT_suffix_tile(lf, tri_suffix, nc, two_pass=False)
                r = rc_ref[hh, 0]
                parts = [None] * nc
                for c in reversed(range(nc)):
                    parts[c] = jnp.exp(x[_chunk_rows(c)] + r)
                    r = r + jnp.sum(lf[_chunk_rows(c)], axis=0, keepdims=True)
                a = jnp.concatenate(parts, axis=0)
                if diagonal:
                    a = jnp.where(before, a, 0.0)
                g = a * _dot(v_ref[:, sl], dob, NT)
                gb = g.astype(bf16)
                gsum = g_s[hh]
                for c in range(nc):
                    parts[c] = _dot(tri_prefix, gb[_chunk_rows(c)]) + gsum
                    gsum = gsum + jnp.sum(g[_chunk_rows(c)], axis=0, keepdims=True)
                g_s[hh] = gsum
                dl = g - (g + jnp.concatenate(parts, axis=0)) * jnp.exp(lsig)
                if diagonal:
                    dl = jnp.where(before, dl, 0.0)
                dl = dl.astype(bf16)
                dq_s[hh] += _dot(kh, dl, TN)
                dk_ref[out_rows, sl] += _dot(dl, qh)
                dv_ref[out_rows, sl] += _dot(a.astype(bf16), dob)

        @pl.when(kb < qb)
        def _():
            tile(False)

        @pl.when(kb == qb)
        def _():
            tile(True)
            for hh in range(HP):
                dq_ref[:, _hsl(hh)] = (jnp.transpose(dq_s[hh]) * scale).astype(dq_ref.dtype)

    q, k, v, z = _pair_specs(S, BR, T, kmap)
    ospec = pl.BlockSpec((T, HW), lambda p, i, j: (i, p))
    full = pl.BlockSpec((S, HW), lambda p, i, j: (0, p))
    rcs = pl.BlockSpec((HP, 1,1, T), lambda p, i, j: (p, kmap(i, j), 0, i))
    return _pallas(
        body, (P, P, P, P, o, du, rc), name=name, grid=(H // HP, nq, nq),
        in_specs=[q, k, v, z, ospec, ospec, rcs],
        out_specs=[ospec, ospec, full, full],
        out_shape=[jax.ShapeDtypeStruct((S, BR), bf16)] * 2 + [jax.ShapeDtypeStruct((S, BR), f32)] * 2,
        scratch_shapes=[pltpu.VMEM((HP, T,HEAD_DIM), bf16), pltpu.VMEM((HP, HEAD_DIM,T), f32),
                        pltpu.VMEM((HP, 1,T), f32)],
        semantics=("parallel", "arbitrary", "arbitrary"), comm=comm)


def _adam_math(w, g, m, v):
    m = ADAM_B1 * m + (1.0 - ADAM_B1) * g
    v = ADAM_B2 * v + (1.0 - ADAM_B2) * (g * g)
    m_hat = m / (1.0 - ADAM_B1 ** ADAM_STEP)
    v_hat = v / (1.0 - ADAM_B2 ** ADAM_STEP)
    delta = -ADAM_LR * (m_hat / (jnp.sqrt(v_hat) + ADAM_EPS) + ADAM_WD * w)
    return delta, m, v


def _adamw_sum(parts, w, m, v, name):
    R, C = w.shape
    n_parts = parts.shape[0]
    tr = _pick(R, (256, 128))

    def body(p_ref, w_ref, m_ref, v_ref, g_ref, d_ref, nm_ref, nv_ref):
        g = p_ref[0].astype(f32)
        for i in range(1, n_parts):
            g = g + p_ref[i].astype(f32)
        g_ref[...] = g
        d_ref[...], nm_ref[...], nv_ref[...] = _adam_math(w_ref[...], g, m_ref[...], v_ref[...])

    blk = pl.BlockSpec((tr, C), lambda i: (i, 0))
    return pl.pallas_call(
        body, name=name, grid=(R // tr,),
        in_specs=[pl.BlockSpec((n_parts, tr, C), lambda i: (0, i, 0)), blk, blk, blk],
        out_specs=[blk] * 4,
        out_shape=[jax.ShapeDtypeStruct((R, C), f32)] * 4,
        compiler_params=_params(("parallel",)),
    )(parts, w, m, v)


def _coords():
    return lax.axis_index("x"), lax.axis_index("y"), lax.axis_index("c")


class _Gather:
    def __init__(self, shards):
        self.args = list(shards)
        self.n = len(shards)
        self.out_shape = [jax.ShapeDtypeStruct((N_DEV,) + s.shape, s.dtype) for s in shards]
        self.scratch = [pltpu.SemaphoreType.DMA((self.n, 7)), pltpu.SemaphoreType.DMA((self.n, 7)),
                        pltpu.SemaphoreType.DMA((self.n,))]

    def _ctx(self, ins, outs, sems):
        send_sems, recv_sems, local_sems = sems
        x, y, c = _coords()
        me, sibling = (x, y, c), (x, y, 1 - c)
        chips = [(1 - x, y), (x, 1 - y), (1 - x, 1 - y)]

        def slot(out, dev):
            return out.at[4 * dev[0] + 2 * dev[1] + dev[2]]

        def copy(a, k, block, to, src=None):
            return pltpu.make_async_remote_copy(
                src_ref=slot(outs[a], block) if src is None else src, dst_ref=slot(outs[a], block),
                send_sem=send_sems.at[a, k], recv_sem=recv_sems.at[a, k], device_id=to, device_id_type=MESH)

        mine = [pltpu.make_async_copy(ins[a], slot(outs[a], me), local_sems.at[a]) for a in range(self.n)]
        first = []
        for a in range(self.n):
            first.append(copy(a, 0, me, sibling, src=ins[a]))
            first += [copy(a, 1 + j, me, (*chip, c), src=ins[a]) for j, chip in enumerate(chips)]
        passed = [copy(a, 4 + j, (*chip, c), sibling) for j, chip in enumerate(chips) for a in range(self.n)]
        return c, me, sibling, chips, copy, mine, first, passed

    def start(self, ins, outs, sems):
        *_, mine, first, _ = self._ctx(ins, outs, sems)
        for cp in mine + first:
            cp.start()

    def mid(self, ins, outs, sems):
        c, me, _, chips, copy, _, _, passed = self._ctx(ins, outs, sems)
        i = 0
        for j, chip in enumerate(chips):
            for a in range(self.n):
                copy(a, 1 + j, (*chip, c), me).wait_recv()
                passed[i].start()
                i += 1

    def finish(self, ins, outs, sems):
        c, me, sibling, chips, copy, mine, first, passed = self._ctx(ins, outs, sems)
        for a in range(self.n):
            copy(a, 0, sibling, me).wait_recv()
            for j, chip in enumerate(chips):
                copy(a, 4 + j, (*chip, 1 - c), me).wait_recv()
        for cp in first + passed:
            cp.wait_send()
        for cp in mine:
            cp.wait()


N_CHIPS = 4


class _SiblingSwap:
    mid = None

    def __init__(self, blocks):
        self.args = list(blocks)
        self.n = len(blocks)
        self.out_shape = [jax.ShapeDtypeStruct((N_CHIPS,) + b.shape[1:], b.dtype) for b in blocks]
        self.scratch = [pltpu.SemaphoreType.DMA((self.n, N_CHIPS)), pltpu.SemaphoreType.DMA((self.n, N_CHIPS))]

    def _copies(self, ins, outs, sems):
        send_sems, recv_sems = sems
        x, y, c = _coords()
        return [pltpu.make_async_remote_copy(
            src_ref=ins[a].at[2 * q + (1 - c)], dst_ref=outs[a].at[q],
            send_sem=send_sems.at[a, q], recv_sem=recv_sems.at[a, q],
            device_id=(x, y, 1 - c), device_id_type=MESH) for a in range(self.n) for q in range(N_CHIPS)]

    def start(self, ins, outs, sems):
        for cp in self._copies(ins, outs, sems):
            cp.start()

    def finish(self, ins, outs, sems):
        for cp in self._copies(ins, outs, sems):
            cp.wait()


class _ChipScatter:
    mid = None

    def __init__(self, blocks):
        self.args = list(blocks)
        self.n = len(blocks)
        self.out_shape = [jax.ShapeDtypeStruct(b.shape, b.dtype) for b in blocks]
        self.scratch = [pltpu.SemaphoreType.DMA((self.n, 3)), pltpu.SemaphoreType.DMA((self.n, 3)),
                        pltpu.SemaphoreType.DMA((self.n,))]

    def _ctx(self, ins, outs, sems):
        send_sems, recv_sems, local_sems = sems
        x, y, c = _coords()
        me = 2 * x + y
        mine = [pltpu.make_async_copy(ins[a].at[me], outs[a].at[me], local_sems.at[a]) for a in range(self.n)]

        def copy(a, k, landing_here):
            px, py = x ^ ((k >> 1) & 1), y ^ (k & 1)
            them = 2 * px + py
            return pltpu.make_async_remote_copy(
                src_ref=ins[a].at[them], dst_ref=outs[a].at[them if landing_here else me],
                send_sem=send_sems.at[a, k - 1], recv_sem=recv_sems.at[a, k - 1],
                device_id=(px, py, c), device_id_type=MESH)

        sent = [copy(a, k, False) for k in range(1, N_CHIPS) for a in range(self.n)]
        arrivals = [copy(a, k, True) for k in range(1, N_CHIPS) for a in range(self.n)]
        return mine, sent, arrivals

    def start(self, ins, outs, sems):
        mine, sent, _ = self._ctx(ins, outs, sems)
        for cp in mine + sent:
            cp.start()

    def finish(self, ins, outs, sems):
        mine, sent, arrivals = self._ctx(ins, outs, sems)
        for cp in arrivals:
            cp.wait_recv()
        for cp in sent:
            cp.wait_send()
        for cp in mine:
            cp.wait()


def _pair_sum(blocks, got, core, name):
    _, R, C = blocks.shape
    tr = _pick(R, (256, 128))

    def body(core_ref, mine_ref, got_ref, o_ref):
        o_ref[...] = (mine_ref[...].astype(f32) + got_ref[...].astype(f32)).astype(o_ref.dtype)

    return pl.pallas_call(
        body, name=name,
        grid_spec=pltpu.PrefetchScalarGridSpec(
            num_scalar_prefetch=1, grid=(N_CHIPS, R // tr),
            in_specs=[pl.BlockSpec((1, tr, C), lambda q, i, core_ref: (2 * q + core_ref[0], i, 0)),
                      pl.BlockSpec((1, tr, C), lambda q, i, core_ref: (q, i, 0))],
            out_specs=pl.BlockSpec((1, tr, C), lambda q, i, core_ref: (q, i, 0))),
        out_shape=jax.ShapeDtypeStruct((N_CHIPS, R, C), blocks.dtype),
        compiler_params=_params(("parallel", "parallel")),
    )(core, blocks, got)


def _run_comm(comm, name):
    n = comm.n

    def body(*refs):
        ins, outs, sems = refs[:n], refs[n:2 * n], refs[2 * n:]
        comm.start(ins, outs, sems)
        if comm.mid is not None:
            comm.mid(ins, outs, sems)
        comm.finish(ins, outs, sems)

    return pl.pallas_call(
        body, name=name, in_specs=[ANY] * n, out_specs=[ANY] * n,
        out_shape=comm.out_shape, scratch_shapes=comm.scratch,
    )(*comm.args)


def _pallas(body, args, *, name, grid, in_specs, out_specs, out_shape, scratch_shapes=(), semantics, comm=None):
    in_specs, out_specs, out_shape, scratch_shapes = list(in_specs), list(out_specs), list(out_shape), list(scratch_shapes)
    if comm is None:
        return pl.pallas_call(
            body, name=name, grid=grid, in_specs=in_specs, out_specs=out_specs, out_shape=out_shape,
            scratch_shapes=scratch_shapes, compiler_params=_params(semantics))(*args)
    a = len(in_specs)
    b = a + comm.n
    c = b + len(out_specs)
    d = c + comm.n
    e = d + len(scratch_shapes)
    total = math.prod(grid)
    mid_step = (3 * total) // 4

    def hosted(*refs):
        step = pl.program_id(0)
        for axis in range(1, len(grid)):
            step = step * grid[axis] + pl.program_id(axis)
        ins, outs, sems = refs[a:b], refs[c:d], refs[e:]

        @pl.when(step == 0)
        def _():
            comm.start(ins, outs, sems)

        body(*refs[:a], *refs[b:c], *refs[d:e])

        if comm.mid is not None:
            @pl.when(step == mid_step)
            def _():
                comm.mid(ins, outs, sems)

        @pl.when(step == total - 1)
        def _():
            comm.finish(ins, outs, sems)

    res = pl.pallas_call(
        hosted, name=name, grid=grid, in_specs=in_specs + [ANY] * comm.n, out_specs=out_specs + [ANY] * comm.n,
        out_shape=out_shape + comm.out_shape, scratch_shapes=scratch_shapes + comm.scratch,
        compiler_params=_params(("arbitrary",) * len(grid)))(*args, *comm.args)
    return list(res[:len(out_specs)]) + [list(res[len(out_specs):])]


def _all_reduce_small(vec, name):
    R, C = vec.shape

    def body(v_ref, o_ref, gath, send_sems, recv_sems):
        x, y, c = _coords()
        me = 4 * x + 2 * y + c
        gath[me] = v_ref[...]

        def copy(k):
            px, py, pc = x ^ ((k >> 2) & 1), y ^ ((k >> 1) & 1), c ^ (k & 1)
            return pltpu.make_async_remote_copy(
                src_ref=v_ref, dst_ref=gath.at[me], send_sem=send_sems.at[k - 1], recv_sem=recv_sems.at[k - 1],
                device_id=(px, py, pc), device_id_type=MESH)

        sent = [copy(k) for k in range(1, N_DEV)]
        for cp in sent:
            cp.start()
        for cp in sent:
            cp.wait()
        total = gath[0]
        for i in range(1, N_DEV):
            total = total + gath[i]
        o_ref[...] = total

    return pl.pallas_call(
        body, name=name,
        in_specs=[pl.BlockSpec(memory_space=pltpu.VMEM)], out_specs=pl.BlockSpec(memory_space=pltpu.VMEM),
        out_shape=jax.ShapeDtypeStruct((R, C), f32),
        scratch_shapes=[pltpu.VMEM((N_DEV, R, C), f32), pltpu.SemaphoreType.DMA((7,)), pltpu.SemaphoreType.DMA((7,))],
    )(vec)


def _gathered_in(g):
    return jnp.transpose(g, (1, 0, 2)).reshape(g.shape[1], -1)


def _col_blocks(dw):
    D, N = dw.shape
    return jnp.transpose(dw.reshape(D, N_DEV, N // N_DEV), (1, 0, 2)).astype(bf16)


def _heads_major(a, n):
    return jnp.transpose(a.reshape(a.shape[0], n, HEAD_DIM), (1, 0, 2))


def _heads_minor(a):
    return jnp.transpose(a, (1, 0, 2)).reshape(a.shape[1], -1)


def kernel(x, g_pre, g_post, w_in_a, w_out_a, sinks_a, w_in_b, w_out_b, w_in_c, b_f_c, w_out_c, loss_target, m_g_pre, m_g_post, m_w_in_a, m_w_out_a, m_sinks_a, m_w_in_b, m_w_out_b, m_w_in_c, m_b_f_c, m_w_out_c, v_g_pre, v_g_post, v_w_in_a, v_w_out_a, v_sinks_a, v_w_in_b, v_w_out_b, v_w_in_c, v_b_f_c, v_w_out_c):
    S, D = x.shape[1], x.shape[2]
    H = D // HEAD_DIM
    BR = H * HEAD_DIM
    n_kv = H // 8
    KV = n_kv * HEAD_DIM
    x0 = x[0]
    target = loss_target[0]
    swa_bias = _swa_bias(H, n_kv)
    w_in = {0: w_in_a, 1: w_in_b, 2: w_in_c}
    w_out = {0: w_out_a, 1: w_out_b, 2: w_out_c}

    saved = []
    xi = x0
    riders = {"pre_norm0": [("in", 0)], "in_proj0": [("out", 0)], "mixer0": [("in", 1)],
              "mixer1": [("out", 1), ("in", 2), ("out", 2)], "mixer2": [("in", 3), ("out", 3)]}
    full = {}

    def rider(slot):
        keys = riders.get(slot)
        if not keys:
            return None
        shards = [(w_in if which == "in" else w_out)[i % 3][i // 3].astype(bf16) for which, i in keys]
        return _Gather(shards)

    def landed(slot, arrays):
        full.update(zip(riders[slot], arrays))

    for i in range(DEPTH):
        kind, j = i % 3, i // 3
        nxt = rider(f"mixer{i}")
        with_norm = rider(f"pre_norm{i}")
        h = _rmsnorm_fwd(xi, g_pre[i:i + 1], name=f"pre_norm{i}", comm=with_norm)
        if with_norm is not None:
            h, arrived = h
            landed(f"pre_norm{i}", arrived)
        W_in = _gathered_in(full["in", i])
        st = dict(x=xi, h=h)
        if kind == 2:
            n_main = 4 * BR
            pad = (-W_in.shape[1]) % 128
            W_in = jnp.pad(W_in, ((0, 0), (0, pad)))
            W_f = W_in[:, n_main:n_main + 128]
            fl = _matmul(h, W_f, "nn", f32, name=f"f_proj{i}")
            flT = jnp.transpose(fl[:, :H])
            bcol = b_f_c[j].reshape(H, 1)
            cumT = _fox_cum(flT, bcol, name=f"fox_cum{i}")
            cq, ck = cumT[:, None, :], cumT[:, :, None]
            st.update(flT=flT, bcol=bcol, cq=cq, ck=ck)
        st["W_in"] = W_in
        with_proj = rider(f"in_proj{i}")
        P = _matmul(h, W_in, "nn", bf16, name=f"in_proj{i}", comm=with_proj)
        if with_proj is not None:
            P, arrived = P
            landed(f"in_proj{i}", arrived)
        st["P"] = P
        if kind == 0:
            kh = _heads_major(P[:, BR:BR + KV], n_kv)
            vh = _heads_major(P[:, BR + KV:BR + 2 * KV], n_kv)
            o, u, *rest = _swaT_fwd(P, kh, vh, sinks_a[j], swa_bias, name=f"swa_fwd{i}", comm=nxt)
            st.update(kh=kh, vh=vh)
        elif kind == 1:
            o, u, rc, *rest = _sbT_fwd(P, BR, name=f"sb_fwd{i}", comm=nxt)
            st.update(rc=rc)
        else:
            o, u, lse, *rest = _foxT_fwd(P, cq, ck, BR, name=f"fox_fwd{i}", comm=nxt)
            st.update(lse=lse)
        if nxt is not None:
            landed(f"mixer{i}", rest[0])
        W_out = full["out", i].reshape(BR, D)
        st.update(o=o, u=u, W_out=W_out)
        y = _matmul(u, W_out, "nn", f32, name=f"out_proj{i}")
        st["y"] = y
        xi = _post_fwd(xi, y, g_post[i:i + 1], name=f"post_norm{i}")
        saved.append(st)

    loss_part, dx = _loss_fwd_bwd(xi, target, name="loss")

    dg_pre, dg_post = [None] * DEPTH, [None] * DEPTH
    dsinks = [None, None]
    db_f = None
    recv = [None] * DEPTH
    core = lax.axis_index("c").astype(jnp.int32).reshape(1)
    pending = None
    for i in reversed(range(DEPTH)):
        kind, j = i % 3, i // 3
        st = saved[i]
        dy, dg_post[i] = _post_bwd(dx, st["y"], g_post[i:i + 1], name=f"post_bwd{i}")
        du = _matmul(dy, st["W_out"], "nt", bf16, name=f"du{i}")
        dW_out = _matmul(st["u"], dy, "tn", bf16, name=f"dw_out{i}")
        P = st["P"]
        if kind == 0:
            dq, dz, dkh, dvh, dsk, *rest = _swaT_bwd(P, st["kh"], st["vh"], st["o"], du, sinks_a[j], swa_bias,
                                                     name=f"swa_bwd{i}", comm=pending)
            dsinks[j] = jnp.sum(dsk.reshape(H, BLOCK), axis=1)
            dP = jnp.concatenate([dq, _heads_minor(dkh).astype(bf16), _heads_minor(dvh).astype(bf16), dz], axis=1)
        elif kind == 1:
            dq, dz, dk, dv, *rest = _sbT_bwd(P, st["o"], du, st["rc"], BR, name=f"sb_bwd{i}", comm=pending)
            dP = jnp.concatenate([dq, dk.astype(bf16), dv.astype(bf16), dz], axis=1)
        else:
            dq, dz, dk, dv, dcq, dck, *rest = _foxT_bwd(P, st["o"], du, st["lse"], st["cq"], st["ck"], BR,
                                                       name=f"fox_bwd{i}", comm=pending)
        if pending is not None:
            recv[i + 1] = rest[0]
        if kind == 2:
            dflT, db_col = _fox_cum_bwd(dcq.reshape(H, S), dck.reshape(H, S), st["flT"], st["bcol"],
                                        name=f"fox_cum_bwd{i}")
            db_f = db_col.reshape(H)
            dfl = jnp.pad(jnp.transpose(dflT), ((0, 0), (0, 128 - H))).astype(bf16)
            dP = jnp.concatenate([dq, dk.astype(bf16), dv.astype(bf16), dz, dfl], axis=1)
        dh = _matmul(dP, st["W_in"], "nt", f32, name=f"dh{i}")
        dW_in = _matmul(st["h"], dP, "tn", bf16, name=f"dw_in{i}")
        n_cols = w_in[kind].shape[2] * N_DEV
        blocks = [_col_blocks(dW_in[:, :n_cols]), dW_out.reshape(N_DEV, BR // N_DEV, D)]
        dx, dg_pre[i], got = _pre_bwd(dh, st["x"], g_pre[i:i + 1], dx, name=f"pre_bwd{i}", comm=_SiblingSwap(blocks))
        pending = _ChipScatter([_pair_sum(b, g, core, name=f"pair_sum_{t}{i}")
                                for t, b, g in zip(("in", "out"), blocks, got)])
    recv[0] = _run_comm(pending, name="scatter_dw0")

    small = jnp.concatenate(
        [jnp.concatenate(dg_pre, axis=0).reshape(-1), jnp.concatenate(dg_post, axis=0).reshape(-1),
         jnp.concatenate(dsinks), db_f, loss_part[0, :1]])
    n_small = small.shape[0]
    rows = -(-n_small // 128)
    rows = -(-rows // 8) * 8
    small = jnp.pad(small, (0, rows * 128 - n_small)).reshape(rows, 128)
    total = _all_reduce_small(small, name="reduce_small").reshape(-1)
    o0 = DEPTH * D
    grad_g_pre = total[:o0].reshape(DEPTH, D)
    grad_g_post = total[o0:2 * o0].reshape(DEPTH, D)
    grad_sinks = total[2 * o0:2 * o0 + 2 * H].reshape(2, H)
    grad_b_f = total[2 * o0 + 2 * H:2 * o0 + 3 * H].reshape(1, H)
    loss = total[2 * o0 + 3 * H]

    def small_adam(w, g, m, v, name):
        def body(w_ref, g_ref, m_ref, v_ref, d_ref, nm_ref, nv_ref):
            d_ref[...], nm_ref[...], nv_ref[...] = _adam_math(w_ref[...], g_ref[...], m_ref[...], v_ref[...])
        vm = pl.BlockSpec(memory_space=pltpu.VMEM)
        return pl.pallas_call(body, name=name, in_specs=[vm] * 4, out_specs=[vm] * 3,
                              out_shape=[jax.ShapeDtypeStruct(w.shape, f32)] * 3)(w, g, m, v)

    upd = {}
    upd["g_pre"] = (grad_g_pre,) + tuple(small_adam(g_pre, grad_g_pre, m_g_pre, v_g_pre, "adam_g_pre"))
    upd["g_post"] = (grad_g_post,) + tuple(small_adam(g_post, grad_g_post, m_g_post, v_g_post, "adam_g_post"))
    upd["sinks_a"] = (grad_sinks,) + tuple(small_adam(sinks_a, grad_sinks, m_sinks_a, v_sinks_a, "adam_sinks"))
    upd["b_f_c"] = (grad_b_f,) + tuple(small_adam(b_f_c, grad_b_f, m_b_f_c, v_b_f_c, "adam_b_f"))

    def big(i, which, w, m, v):
        return _adamw_sum(recv[i][which], w, m, v, name=f"adam_{'in' if which == 0 else 'out'}{i}")

    a_in = [big(i, 0, w_in_a[jj], m_w_in_a[jj], v_w_in_a[jj]) for jj, i in enumerate((0, 3))]
    a_out = [big(i, 1, w_out_a[jj], m_w_out_a[jj], v_w_out_a[jj]) for jj, i in enumerate((0, 3))]
    upd["w_in_a"] = tuple(jnp.stack([a_in[0][t], a_in[1][t]]) for t in range(4))
    upd["w_out_a"] = tuple(jnp.stack([a_out[0][t], a_out[1][t]]) for t in range(4))
    upd["w_in_b"] = tuple(t[None] for t in big(1, 0, w_in_b[0], m_w_in_b[0], v_w_in_b[0]))
    upd["w_out_b"] = tuple(t[None] for t in big(1, 1, w_out_b[0], m_w_out_b[0], v_w_out_b[0]))
    upd["w_in_c"] = tuple(t[None] for t in big(2, 0, w_in_c[0], m_w_in_c[0], v_w_in_c[0]))
    upd["w_out_c"] = tuple(t[None] for t in big(2, 1, w_out_c[0], m_w_out_c[0], v_w_out_c[0]))

    names = ["g_pre", "g_post", "w_in_a", "w_out_a", "sinks_a", "w_in_b", "w_out_b", "w_in_c", "b_f_c", "w_out_c"]
    return (loss, dx[None], *[upd[k][0] for k in names], *[upd[k][1] for k in names],
            *[upd[k][2] for k in names], *[upd[k][3] for k in names])
```

```python
import functools
import math

import numpy as np
import jax
import jax.numpy as jnp
from jax import lax
from jax.experimental import pallas as pl
from jax.experimental.pallas import tpu as pltpu

HEAD_DIM = 64
HP = 8
HW = HP * HEAD_DIM
BLOCK = 128
NORM_EPS = 1e-6
NEG = -1e30
N_DEV = 8
DEPTH = 4
ADAM_LR, ADAM_B1, ADAM_B2, ADAM_EPS, ADAM_WD, ADAM_STEP = 0.001, 0.9, 0.999, 1e-8, 0.01, 10
VMEM_LIMIT = 56 * 1024 * 1024

bf16 = jnp.bfloat16
f32 = jnp.float32
MESH = pl.DeviceIdType.MESH
ANY = pl.BlockSpec(memory_space=pl.ANY)
SMEM = pl.BlockSpec(memory_space=pltpu.SMEM)

NN = (((1,), (0,)), ((), ()))
NT = (((1,), (1,)), ((), ()))
TN = (((0,), (0,)), ((), ()))


def _dot(a, b, dims=NN):
    return lax.dot_general(a, b, dims, preferred_element_type=f32)


def _params(sem):
    return pltpu.CompilerParams(dimension_semantics=sem, vmem_limit_bytes=VMEM_LIMIT)


def _attn_tile(S):
    return 512 if S % 512 == 0 and S >= 1024 else 128


def _pick(n, pref):
    for t in pref:
        if n % t == 0:
            return t
    return n


def _matmul(a, b, mode, out_dtype, name, comm=None):
    if mode == "nn":
        (M, K), (K2, N) = a.shape, b.shape
    elif mode == "nt":
        (M, K), (N, K2) = a.shape, b.shape
    else:
        (K, M), (K2, N) = a.shape, b.shape
    assert K == K2, (a.shape, b.shape, mode)
    tm = _pick(M, (1024, 512, 256, 128))
    tn = _pick(N, (1024, 768, 640, 512, 256, 128))
    tk = _pick(K, (2048, 1664, 1536, 1024, 512, 640, 256, 128))
    nk = K // tk
    dims = {"nn": NN, "nt": NT, "tn": TN}[mode]

    def body(a_ref, b_ref, o_ref, acc_ref):
        if nk == 1:
            o_ref[...] = _dot(a_ref[...], b_ref[...], dims).astype(o_ref.dtype)
            return
        k = pl.program_id(2)

        @pl.when(k == 0)
        def _():
            acc_ref[...] = jnp.zeros_like(acc_ref)

        acc_ref[...] += _dot(a_ref[...], b_ref[...], dims)

        @pl.when(k == nk - 1)
        def _():
            o_ref[...] = acc_ref[...].astype(o_ref.dtype)

    if mode == "tn":
        a_spec = pl.BlockSpec((tk, tm), lambda i, j, k: (k, i))
    else:
        a_spec = pl.BlockSpec((tm, tk), lambda i, j, k: (i, k))
    if mode == "nt":
        b_spec = pl.BlockSpec((tn, tk), lambda i, j, k: (j, k))
    else:
        b_spec = pl.BlockSpec((tk, tn), lambda i, j, k: (k, j))
    res = _pallas(
        body, (a, b), name=name,
        grid=(M // tm, N // tn, nk),
        in_specs=[a_spec, b_spec],
        out_specs=[pl.BlockSpec((tm, tn), lambda i, j, k: (i, j))],
        out_shape=[jax.ShapeDtypeStruct((M, N), out_dtype)],
        scratch_shapes=[pltpu.VMEM((tm, tn) if nk > 1 else (8, 128), f32)],
        semantics=("parallel", "parallel", "arbitrary"), comm=comm)
    return res[0] if comm is None else (res[0], res[1])


ROWS = 256


def _rows(S):
    return ROWS if S % ROWS == 0 else S


def _rmsnorm_fwd(x, g, name, comm=None):
    S, D = x.shape
    tr = _rows(S)

    def body(x_ref, g_ref, h_ref):
        xv = x_ref[...]
        r = lax.rsqrt(jnp.mean(xv * xv, axis=-1, keepdims=True) + NORM_EPS)
        h_ref[...] = (xv * r * g_ref[...]).astype(h_ref.dtype)

    res = _pallas(
        body, (x, g), name=name, grid=(S // tr,),
        in_specs=[pl.BlockSpec((tr, D), lambda i: (i, 0)), pl.BlockSpec((1, D), lambda i: (0, 0))],
        out_specs=[pl.BlockSpec((tr, D), lambda i: (i, 0))],
        out_shape=[jax.ShapeDtypeStruct((S, D), bf16)],
        semantics=("parallel",), comm=comm)
    return res[0] if comm is None else (res[0], res[1])


def _post_fwd(x, y, g, name):
    S, D = x.shape
    tr = _rows(S)

    def body(x_ref, y_ref, g_ref, o_ref):
        yv = y_ref[...]
        r = lax.rsqrt(jnp.mean(yv * yv, axis=-1, keepdims=True) + NORM_EPS)
        o_ref[...] = x_ref[...] + yv * r * g_ref[...]

    row = pl.BlockSpec((tr, D), lambda i: (i, 0))
    return pl.pallas_call(
        body, name=name, grid=(S // tr,),
        in_specs=[row, row, pl.BlockSpec((1, D), lambda i: (0, 0))],
        out_specs=row,
        out_shape=jax.ShapeDtypeStruct((S, D), f32),
        compiler_params=_params(("parallel",)),
    )(x, y, g)


def _loss_fwd_bwd(y, t, name):
    S, D = y.shape
    tr = _rows(S)

    def body(y_ref, t_ref, l_ref, d_ref):
        @pl.when(pl.program_id(0) == 0)
        def _():
            l_ref[...] = jnp.zeros_like(l_ref)

        e = y_ref[...] - t_ref[...]
        d_ref[...] = e * (1.0 / D)
        part = 0.5 * jnp.sum(jnp.sum(e * e, axis=-1, keepdims=True) * (1.0 / D), axis=0, keepdims=True)
        l_ref[...] += jnp.broadcast_to(part, l_ref.shape)

    row = pl.BlockSpec((tr, D), lambda i: (i, 0))
    return pl.pallas_call(
        body, name=name, grid=(S // tr,),
        in_specs=[row, row],
        out_specs=[pl.BlockSpec((8, 128), lambda i: (0, 0)), row],
        out_shape=[jax.ShapeDtypeStruct((8, 128), f32), jax.ShapeDtypeStruct((S, D), f32)],
        compiler_params=_params(("arbitrary",)),
    )(y, t)


def _post_bwd(dxn, y, g, name):
    S, D = y.shape
    tr = _rows(S)

    def body(d_ref, y_ref, g_ref, dy_ref, dg_ref):
        @pl.when(pl.program_id(0) == 0)
        def _():
            dg_ref[...] = jnp.zeros_like(dg_ref)

        yv = y_ref[...]
        d = d_ref[...]
        r = lax.rsqrt(jnp.mean(yv * yv, axis=-1, keepdims=True) + NORM_EPS)
        n = yv * r
        dn = d * g_ref[...]
        dg_ref[...] += jnp.sum(d * n, axis=0, keepdims=True)
        dy_ref[...] = (r * (dn - n * jnp.mean(dn * n, axis=-1, keepdims=True))).astype(dy_ref.dtype)

    row = pl.BlockSpec((tr, D), lambda i: (i, 0))
    vec = pl.BlockSpec((1, D), lambda i: (0, 0))
    return pl.pallas_call(
        body, name=name, grid=(S // tr,),
        in_specs=[row, row, vec],
        out_specs=[row, vec],
        out_shape=[jax.ShapeDtypeStruct((S, D), bf16), jax.ShapeDtypeStruct((1, D), f32)],
        compiler_params=_params(("arbitrary",)),
    )(dxn, y, g)


def _pre_bwd(dh, x, g, dres, name, comm=None):
    S, D = x.shape
    tr = _rows(S)

    def body(dh_ref, x_ref, g_ref, dres_ref, dx_ref, dg_ref):
        @pl.when(pl.program_id(0) == 0)
        def _():
            dg_ref[...] = jnp.zeros_like(dg_ref)

        xv = x_ref[...]
        d = dh_ref[...]
        r = lax.rsqrt(jnp.mean(xv * xv, axis=-1, keepdims=True) + NORM_EPS)
        n = xv * r
        dn = d * g_ref[...]
        dg_ref[...] += jnp.sum(d * n, axis=0, keepdims=True)
        dx_ref[...] = dres_ref[...] + r * (dn - n * jnp.mean(dn * n, axis=-1, keepdims=True))

    row = pl.BlockSpec((tr, D), lambda i: (i, 0))
    vec = pl.BlockSpec((1, D), lambda i: (0, 0))
    return _pallas(
        body, (dh, x, g, dres), name=name, grid=(S // tr,),
        in_specs=[row, row, vec, row],
        out_specs=[row, vec],
        out_shape=[jax.ShapeDtypeStruct((S, D), f32), jax.ShapeDtypeStruct((1, D), f32)],
        semantics=("arbitrary",), comm=comm)


def _sigmoid(x):
    return 1.0 / (1.0 + jnp.exp(-x))


def _gate_fwd(o, z):
    zf = z.astype(f32)
    return o * (zf * _sigmoid(zf))


def _gate_bwd(du, o, z):
    zf = z.astype(f32)
    sig = _sigmoid(zf)
    do = du * (zf * sig)
    dz = du * o * (sig * (1.0 + zf * (1.0 - sig)))
    return do, dz


def _iota2(shape, dim):
    return lax.broadcasted_iota(jnp.int32, shape, dim)


def _swa_specs(S, BR, KV, G):
    gw = G * HEAD_DIM
    qspec = pl.BlockSpec((BLOCK, gw), lambda h, n: (n, h))
    zoff = (BR + 2 * KV) // gw
    zspec = pl.BlockSpec((BLOCK, gw), lambda h, n: (n, zoff + h))
    cur = pl.BlockSpec((1, BLOCK, HEAD_DIM), lambda h, n: (h, n, 0))
    prev = pl.BlockSpec((1, BLOCK, HEAD_DIM), lambda h, n: (h, jnp.maximum(n - 1, 0), 0))
    return qspec, zspec, cur, prev


def _split3(x):
    x1 = x.astype(bf16)
    r1 = x - x1.astype(f32)
    x2 = r1.astype(bf16)
    x3 = (r1 - x2.astype(f32)).astype(bf16)
    return x1, x2, x3


def _split2(x):
    x1 = x.astype(bf16)
    return x1, (x - x1.astype(f32)).astype(bf16)


def _pair_specs(S, BR, T, kmap):
    nb = BR // HW
    q = pl.BlockSpec((T, HW), lambda p, i, j: (i, p))
    k = pl.BlockSpec((T, HW), lambda p, i, j: (kmap(i, j), nb + p))
    v = pl.BlockSpec((T, HW), lambda p, i, j: (kmap(i, j), 2 * nb + p))
    z = pl.BlockSpec((T, HW), lambda p, i, j: (i, 3 * nb + p))
    return q, k, v, z


def _fox_cum(flT, b, name):
    H, S = flT.shape
    tj = _pick(S, (256, 128))

    def body(fl_ref, b_ref, c_ref):
        j = pl.program_id(0)
        x = fl_ref[...] + b_ref[...]
        logf = jnp.minimum(x, 0.0) - jnp.log(1.0 + jnp.exp(-jnp.abs(x)))
        tri = (_iota2((S, tj), 0) <= j * tj + _iota2((S, tj), 1)).astype(bf16)
        c_ref[...] = sum(_dot(part, tri) for part in _split3(logf))

    return pl.pallas_call(
        body, name=name, grid=(S // tj,),
        in_specs=[pl.BlockSpec((H, S), lambda j: (0, 0)), pl.BlockSpec((H, 1), lambda j: (0, 0))],
        out_specs=pl.BlockSpec((H, tj), lambda j: (0, j)),
        out_shape=jax.ShapeDtypeStruct((H, S), f32),
        compiler_params=_params(("parallel",)),
    )(flT, b)


def _fox_cum_bwd(dcq, dck, flT, b, name):
    H, S = flT.shape
    tj = _pick(S, (256, 128))

    def body(dcq_ref, dck_ref, fl_ref, b_ref, o_ref, db_ref):
        j = pl.program_id(0)

        @pl.when(j == 0)
        def _():
            db_ref[...] = jnp.zeros_like(db_ref)

        tri = (_iota2((S, tj), 0) >= j * tj + _iota2((S, tj), 1)).astype(bf16)
        dlogf = sum(_dot(part, tri) for part in _split3(dcq_ref[...] - dck_ref[...]))
        x = fl_ref[...] + b_ref[...]
        dfl = dlogf * _sigmoid(-x)
        o_ref[...] = dfl
        db_ref[...] += jnp.sum(dfl, axis=-1, keepdims=True)

    blk = pl.BlockSpec((H, tj), lambda j: (0, j))
    whole = pl.BlockSpec((H, S), lambda j: (0, 0))
    col = pl.BlockSpec((H, 1), lambda j: (0, 0))
    return pl.pallas_call(
        body, name=name, grid=(S // tj,),
        in_specs=[whole, whole, blk, col],
        out_specs=[blk, col],
        out_shape=[jax.ShapeDtypeStruct((H, S), f32), jax.ShapeDtypeStruct((H, 1), f32)],
        compiler_params=_params(("arbitrary",)),
    )(dcq, dck, flT, b)


def _hsl(hh):
    return slice(hh * HEAD_DIM, (hh + 1) * HEAD_DIM)


def _scaled(q, scale):
    assert math.log2(scale).is_integer()
    return (q.astype(f32) * scale).astype(bf16)


def _swa_bias(H, n_kv):
    G = H // n_kv
    slopes = 2.0 ** (-8.0 * np.arange(1, H + 1, dtype=np.float32) / H)
    dist = (np.arange(BLOCK)[None, :] + BLOCK - np.arange(2 * BLOCK)[:, None]).astype(np.float32)
    valid = (dist >= 0) & (dist < BLOCK)
    out = np.empty((n_kv, 2 * BLOCK, G * BLOCK), np.float32)
    for h in range(H):
        out[h // G][:, (h % G) * BLOCK:(h % G + 1) * BLOCK] = np.where(valid, -(slopes[h] * dist), np.float32(NEG))
    return jnp.asarray(out)


def _swaT_parts(q_ref, kc_ref, kp_ref, vc_ref, vp_ref, bias_ref, sink_ref, kvh, n, G, scale):
    qg = jnp.concatenate([q_ref[:, _hsl(g)] for g in range(G)], axis=0)
    kband = jnp.concatenate([kp_ref[0], kc_ref[0]], axis=0)
    vband = jnp.concatenate([vp_ref[0], vc_ref[0]], axis=0)
    s = _dot(kband, qg, NT) * scale + bias_ref[0]
    s = jnp.where(_iota2(s.shape, 0) + jnp.where(n > 0, BLOCK, 0) >= BLOCK, s, NEG)
    sink = jnp.concatenate([jnp.full((1, BLOCK), sink_ref[kvh * G + g], f32) for g in range(G)], axis=1)
    m = jnp.maximum(jnp.max(s, axis=0, keepdims=True), sink)
    p = jnp.exp(s - m)
    ps = jnp.exp(sink - m)
    den = jnp.sum(p, axis=0, keepdims=True) + ps
    return qg, kband, vband, p, ps, den


def _swaT_specs(S, BR, KV, G):
    qspec, zspec, cur, prev = _swa_specs(S, BR, KV, G)
    bias = pl.BlockSpec((1, 2 * BLOCK, G * BLOCK), lambda h, n: (h, 0, 0))
    return qspec, zspec, cur, prev, bias


def _swaT_fwd(P, kh, vh, sinks, bias, name, comm=None):
    S = P.shape[0]
    n_kv = kh.shape[0]
    H = sinks.shape[0]
    G = H // n_kv
    BR, KV = H * HEAD_DIM, n_kv * HEAD_DIM
    scale = HEAD_DIM ** -0.5
    assert (BR + 2 * KV) % (G * HEAD_DIM) == 0

    def body(q_ref, z_ref, kc_ref, kp_ref, vc_ref, vp_ref, bias_ref, sink_ref, o_ref, u_ref):
        kvh, n = pl.program_id(0), pl.program_id(1)
        _, _, vband, p, _, den = _swaT_parts(q_ref, kc_ref, kp_ref, vc_ref, vp_ref, bias_ref, sink_ref, kvh, n, G, scale)
        o = jnp.transpose(_dot(vband, p.astype(bf16), TN) / den)
        for g in range(G):
            oh = o[g * BLOCK:(g + 1) * BLOCK]
            o_ref[:, _hsl(g)] = oh.astype(o_ref.dtype)
            u_ref[:, _hsl(g)] = _gate_fwd(oh, z_ref[:, _hsl(g)]).astype(u_ref.dtype)

    qspec, zspec, cur, prev, bspec = _swaT_specs(S, BR, KV, G)
    ospec = pl.BlockSpec((BLOCK, G * HEAD_DIM), lambda h, n: (n, h))
    return _pallas(
        body, (P, P, kh, kh, vh, vh, bias, sinks), name=name, grid=(n_kv, S // BLOCK),
        in_specs=[qspec, zspec, cur, prev, cur, prev, bspec, SMEM],
        out_specs=[ospec, ospec],
        out_shape=[jax.ShapeDtypeStruct((S, BR), bf16)] * 2,
        semantics=("parallel", "parallel"), comm=comm)


def _swaT_bwd(P, kh, vh, o, du, sinks, bias, name, comm=None):
    S = P.shape[0]
    n_kv = kh.shape[0]
    H = sinks.shape[0]
    G = H // n_kv
    BR, KV = H * HEAD_DIM, n_kv * HEAD_DIM
    scale = HEAD_DIM ** -0.5

    def body(q_ref, z_ref, kc_ref, kp_ref, vc_ref, vp_ref, o_ref, du_ref, bias_ref, sink_ref,
             dq_ref, dz_ref, dk_ref, dv_ref, ds_ref):
        kvh, n = pl.program_id(0), pl.program_id(1)

        @pl.when(n == 0)
        def _():
            dk_ref[...] = jnp.zeros_like(dk_ref)
            dv_ref[...] = jnp.zeros_like(dv_ref)
            ds_ref[...] = jnp.zeros_like(ds_ref)

        dos, prods = [], []
        for g in range(G):
            oh = o_ref[:, _hsl(g)].astype(f32)
            doh, dzh = _gate_bwd(du_ref[:, _hsl(g)].astype(f32), oh, z_ref[:, _hsl(g)])
            dz_ref[:, _hsl(g)] = dzh.astype(dz_ref.dtype)
            dos.append(doh.astype(bf16))
            prods.append(doh * oh)
        dog = jnp.concatenate(dos, axis=0)
        delta = jnp.sum(jnp.transpose(jnp.concatenate(prods, axis=0)), axis=0, keepdims=True)
        qg, kband, vband, p, ps, den = _swaT_parts(q_ref, kc_ref, kp_ref, vc_ref, vp_ref, bias_ref, sink_ref,
                                                   kvh, n, G, scale)
        inv = 1.0 / den
        pn = p * inv
        ds = (pn * (_dot(vband, dog, NT) - delta)).astype(bf16)
        dq = jnp.transpose(_dot(kband, ds, TN)) * scale
        for g in range(G):
            dq_ref[:, _hsl(g)] = dq[g * BLOCK:(g + 1) * BLOCK].astype(dq_ref.dtype)
        dkb = _dot(ds, qg) * scale
        dvb = _dot(pn.astype(bf16), dog)
        ds_ref[0] -= ps * inv * delta
        row_c = pl.multiple_of(n * BLOCK, BLOCK)
        dk_ref[0, pl.ds(row_c, BLOCK), :] += dkb[BLOCK:]
        dv_ref[0, pl.ds(row_c, BLOCK), :] += dvb[BLOCK:]

        @pl.when(n > 0)
        def _():
            row_p = pl.multiple_of((n - 1) * BLOCK, BLOCK)
            dk_ref[0, pl.ds(row_p, BLOCK), :] += dkb[:BLOCK]
            dv_ref[0, pl.ds(row_p, BLOCK), :] += dvb[:BLOCK]

    qspec, zspec, cur, prev, bspec = _swaT_specs(S, BR, KV, G)
    ospec = pl.BlockSpec((BLOCK, G * HEAD_DIM), lambda h, n: (n, h))
    full = pl.BlockSpec((1, S, HEAD_DIM), lambda h, n: (h, 0, 0))
    return _pallas(
        body, (P, P, kh, kh, vh, vh, o, du, bias, sinks), name=name, grid=(n_kv, S // BLOCK),
        in_specs=[qspec, zspec, cur, prev, cur, prev, ospec, ospec, bspec, SMEM],
        out_specs=[ospec, ospec, full, full, pl.BlockSpec((1, 1, G * BLOCK), lambda h, n: (h, 0, 0))],
        out_shape=[jax.ShapeDtypeStruct((S, BR), bf16)] * 2
        + [jax.ShapeDtypeStruct((n_kv, S, HEAD_DIM), f32)] * 2
        + [jax.ShapeDtypeStruct((n_kv, 1, G * BLOCK), f32)],
        semantics=("parallel", "arbitrary"), comm=comm)


def _foxT_fwd(P, cq, ck, BR, name, comm=None):
    S = P.shape[0]
    H = BR // HEAD_DIM
    T = _attn_tile(S)
    nq = S // T
    scale = HEAD_DIM ** -0.5
    kmap = lambda i, j: jnp.minimum(i, j)

    def body(q_ref, k_ref, v_ref, z_ref, cq_ref, ck_ref, o_ref, u_ref, lse_ref, m_s, l_s, acc_s):
        qb, kb = pl.program_id(1), pl.program_id(2)

        @pl.when(kb == 0)
        def _():
            m_s[...] = jnp.full_like(m_s, NEG)
            l_s[...] = jnp.zeros_like(l_s)
            acc_s[...] = jnp.zeros_like(acc_s)

        def tile(diagonal):
            for hh in range(HP):
                sl = _hsl(hh)
                s = _dot(k_ref[:, sl], _scaled(q_ref[:, sl], scale), NT) + cq_ref[hh] - ck_ref[hh]
                if diagonal:
                    s = jnp.where(_iota2((T, T), 0) <= _iota2((T, T), 1), s, NEG)
                m_old = m_s[hh]
                m_new = jnp.maximum(m_old, jnp.max(s, axis=0, keepdims=True))
                alpha = jnp.exp(m_old - m_new)
                p = jnp.exp(s - m_new)
                l_s[hh] = alpha * l_s[hh] + jnp.sum(p, axis=0, keepdims=True)
                acc_s[hh] = alpha * acc_s[hh] + _dot(v_ref[:, sl], p.astype(bf16), TN)
                m_s[hh] = m_new

        @pl.when(kb < qb)
        def _():
            tile(False)

        @pl.when(kb == qb)
        def _():
            tile(True)
            for hh in range(HP):
                sl = _hsl(hh)
                oh = jnp.transpose(acc_s[hh] / l_s[hh])
                o_ref[:, sl] = oh.astype(o_ref.dtype)
                u_ref[:, sl] = _gate_fwd(oh, z_ref[:, sl]).astype(u_ref.dtype)
                lse_ref[hh] = m_s[hh] + jnp.log(l_s[hh])

    q, k, v, z = _pair_specs(S, BR, T, kmap)
    rowq = pl.BlockSpec((HP, 1,T), lambda p, i, j: (p, 0, i))
    colk = pl.BlockSpec((HP, T,1), lambda p, i, j: (p, kmap(i, j), 0))
    ospec = pl.BlockSpec((T, HW), lambda p, i, j: (i, p))
    return _pallas(
        body, (P, P, P, P, cq, ck), name=name, grid=(H // HP, nq, nq),
        in_specs=[q, k, v, z, rowq, colk],
        out_specs=[ospec, ospec, rowq],
        out_shape=[jax.ShapeDtypeStruct((S, BR), bf16)] * 2 + [jax.ShapeDtypeStruct((H, 1, S), f32)],
        scratch_shapes=[pltpu.VMEM((HP, 1,T), f32), pltpu.VMEM((HP, 1,T), f32), pltpu.VMEM((HP, HEAD_DIM,T), f32)],
        semantics=("parallel", "arbitrary", "arbitrary"), comm=comm)


def _foxT_bwd(P, o, du, lse, cq, ck, BR, name, comm=None):
    S = P.shape[0]
    H = BR // HEAD_DIM
    T = _attn_tile(S)
    nq = S // T
    scale = HEAD_DIM ** -0.5
    kmap = lambda i, j: jnp.minimum(i, j)

    def body(q_ref, k_ref, v_ref, z_ref, o_ref, du_ref, lse_ref, cq_ref, ck_ref,
             dq_ref, dz_ref, dk_ref, dv_ref, dcq_ref, dck_ref, do_s, delta_s, dq_s, dcq_s):
        qb, kb = pl.program_id(1), pl.program_id(2)

        @pl.when(jnp.logical_and(qb == 0, kb == 0))
        def _():
            dk_ref[...] = jnp.zeros_like(dk_ref)
            dv_ref[...] = jnp.zeros_like(dv_ref)
            dck_ref[...] = jnp.zeros_like(dck_ref)

        @pl.when(kb == 0)
        def _():
            for hh in range(HP):
                sl = _hsl(hh)
                oh = o_ref[:, sl].astype(f32)
                doh, dzh = _gate_bwd(du_ref[:, sl].astype(f32), oh, z_ref[:, sl])
                dz_ref[:, sl] = dzh.astype(dz_ref.dtype)
                do_s[hh] = doh.astype(bf16)
                delta_s[hh] = jnp.sum(jnp.transpose(doh * oh), axis=0, keepdims=True)
            dq_s[...] = jnp.zeros_like(dq_s)
            dcq_s[...] = jnp.zeros_like(dcq_s)

        def tile(diagonal):
            rows = pl.ds(pl.multiple_of(kb * T, T), T)
            for hh in range(HP):
                sl = _hsl(hh)
                qh, kh, vh, dob = _scaled(q_ref[:, sl], scale), k_ref[:, sl], v_ref[:, sl], do_s[hh]
                s = _dot(kh, qh, NT) + cq_ref[hh] - ck_ref[hh]
                p = jnp.exp(s - lse_ref[hh])
                if diagonal:
                    p = jnp.where(_iota2((T, T), 0) <= _iota2((T, T), 1), p, 0.0)
                ds = p * (_dot(vh, dob, NT) - delta_s[hh])
                dsb = ds.astype(bf16)
                dq_s[hh] += _dot(kh, dsb, TN)
                dk_ref[rows, sl] += _dot(dsb, qh)
                dv_ref[rows, sl] += _dot(p.astype(bf16), dob)
                dcq_s[hh] += jnp.sum(ds, axis=0, keepdims=True)
                dck_ref[hh, rows, :] += jnp.sum(ds, axis=1, keepdims=True)

        @pl.when(kb < qb)
        def _():
            tile(False)

        @pl.when(kb == qb)
        def _():
            tile(True)
            for hh in range(HP):
                dq_ref[:, _hsl(hh)] = (jnp.transpose(dq_s[hh]) * scale).astype(dq_ref.dtype)
                dcq_ref[hh] = dcq_s[hh]

    q, k, v, z = _pair_specs(S, BR, T, kmap)
    rowq = pl.BlockSpec((HP, 1,T), lambda p, i, j: (p, 0, i))
    colk = pl.BlockSpec((HP, T,1), lambda p, i, j: (p, kmap(i, j), 0))
    ospec = pl.BlockSpec((T, HW), lambda p, i, j: (i, p))
    full = pl.BlockSpec((S, HW), lambda p, i, j: (0, p))
    return _pallas(
        body, (P, P, P, P, o, du, lse, cq, ck), name=name, grid=(H // HP, nq, nq),
        in_specs=[q, k, v, z, ospec, ospec, rowq, rowq, colk],
        out_specs=[ospec, ospec, full, full, rowq, pl.BlockSpec((HP, S,1), lambda p, i, j: (p, 0, 0))],
        out_shape=[jax.ShapeDtypeStruct((S, BR), bf16)] * 2 + [jax.ShapeDtypeStruct((S, BR), f32)] * 2
        + [jax.ShapeDtypeStruct((H, 1, S), f32), jax.ShapeDtypeStruct((H, S, 1), f32)],
        scratch_shapes=[pltpu.VMEM((HP, T,HEAD_DIM), bf16), pltpu.VMEM((HP, 1,T), f32),
                        pltpu.VMEM((HP, HEAD_DIM,T), f32), pltpu.VMEM((HP, 1,T), f32)],
        semantics=("parallel", "arbitrary", "arbitrary"), comm=comm)


def _sbT_logs(kc, qh, before):
    l = _dot(kc, qh, NT)
    minus_abs = lax.bitcast_convert_type(lax.bitcast_convert_type(l, jnp.int32) | jnp.int32(-2 ** 31), f32)
    lsig = jnp.minimum(l, 0.0) - jnp.log(1.0 + jnp.exp(minus_abs))
    lf = lsig - l
    if before is not None:
        lf = jnp.where(before, lf, 0.0)
    return lsig, lf


def _chunk_rows(c):
    return slice(c * BLOCK, (c + 1) * BLOCK)


def _sbT_suffix_tile(lf, tri, nc, two_pass):
    if not two_pass:
        hi = lf.astype(bf16)
        return jnp.concatenate([_dot(tri, hi[_chunk_rows(c)]) for c in range(nc)], axis=0)
    hi, lo = _split2(lf)
    tri2 = jnp.concatenate([tri, tri], axis=1)
    return jnp.concatenate(
        [_dot(tri2, jnp.concatenate([hi[_chunk_rows(c)], lo[_chunk_rows(c)]], axis=0)) for c in range(nc)], axis=0)


def _sbT_fwd(P, BR, name, comm=None):
    S = P.shape[0]
    H = BR // HEAD_DIM
    T = _attn_tile(S)
    nq = S // T
    nc = T // BLOCK
    scale = HEAD_DIM ** -0.5
    kmap = lambda i, j: jnp.maximum(i - j, 0)

    def body(q_ref, k_ref, v_ref, z_ref, o_ref, u_ref, rc_ref, r_s, acc_s):
        qb, j = pl.program_id(1), pl.program_id(2)

        @pl.when(j == 0)
        def _():
            r_s[...] = jnp.zeros_like(r_s)
            acc_s[...] = jnp.zeros_like(acc_s)

        def tile(diagonal):
            ii, jj = _iota2((BLOCK, BLOCK), 0), _iota2((BLOCK, BLOCK), 1)
            tri = (jj > ii).astype(bf16)
            before = (_iota2((T, T), 0) < _iota2((T, T), 1)) if diagonal else None
            for hh in range(HP):
                sl = _hsl(hh)
                lsig, lf = _sbT_logs(k_ref[:, sl], _scaled(q_ref[:, sl], scale), before)
                x = lsig + _sbT_suffix_tile(lf, tri, nc, two_pass=True)
                r = r_s[hh]
                rc_ref[hh, 0] = r
                parts = [None] * nc
                for c in reversed(range(nc)):
                    parts[c] = jnp.exp(x[_chunk_rows(c)] + r)
                    r = r + jnp.sum(lf[_chunk_rows(c)], axis=0, keepdims=True)
                r_s[hh] = r
                a = jnp.concatenate(parts, axis=0)
                if diagonal:
                    a = jnp.where(before, a, 0.0)
                acc_s[hh] += _dot(v_ref[:, sl], a.astype(bf16), TN)

        @pl.when(j > 0)
        def _():
            @pl.when(j <= qb)
            def _():
                tile(False)

        @pl.when(j == 0)
        def _():
            tile(True)

        @pl.when(j == qb)
        def _():
            for hh in range(HP):
                sl = _hsl(hh)
                oh = jnp.transpose(acc_s[hh])
                o_ref[:, sl] = oh.astype(o_ref.dtype)
                u_ref[:, sl] = _gate_fwd(oh, z_ref[:, sl]).astype(u_ref.dtype)

    q, k, v, z = _pair_specs(S, BR, T, kmap)
    ospec = pl.BlockSpec((T, HW), lambda p, i, j: (i, p))
    rc = pl.BlockSpec((HP, 1,1, T), lambda p, i, j: (p, kmap(i, j), 0, i))
    return _pallas(
        body, (P, P, P, P), name=name, grid=(H // HP, nq, nq),
        in_specs=[q, k, v, z],
        out_specs=[ospec, ospec, rc],
        out_shape=[jax.ShapeDtypeStruct((S, BR), bf16)] * 2 + [jax.ShapeDtypeStruct((H, nq, 1, S), f32)],
        scratch_shapes=[pltpu.VMEM((HP, 1,T), f32), pltpu.VMEM((HP, HEAD_DIM,T), f32)],
        semantics=("parallel", "arbitrary", "arbitrary"), comm=comm)


def _sbT_bwd(P, o, du, rc, BR, name, comm=None):
    S = P.shape[0]
    H = BR // HEAD_DIM
    T = _attn_tile(S)
    nq = S // T
    nc = T // BLOCK
    scale = HEAD_DIM ** -0.5
    kmap = lambda i, j: jnp.minimum(i, j)

    def body(q_ref, k_ref, v_ref, z_ref, o_ref, du_ref, rc_ref,
             dq_ref, dz_ref, dk_ref, dv_ref, do_s, dq_s, g_s):
        qb, kb = pl.program_id(1), pl.program_id(2)

        @pl.when(jnp.logical_and(qb == 0, kb == 0))
        def _():
            dk_ref[...] = jnp.zeros_like(dk_ref)
            dv_ref[...] = jnp.zeros_like(dv_ref)

        @pl.when(kb == 0)
        def _():
            for hh in range(HP):
                sl = _hsl(hh)
                doh, dzh = _gate_bwd(du_ref[:, sl].astype(f32), o_ref[:, sl].astype(f32), z_ref[:, sl])
                dz_ref[:, sl] = dzh.astype(dz_ref.dtype)
                do_s[hh] = doh.astype(bf16)
            dq_s[...] = jnp.zeros_like(dq_s)
            g_s[...] = jnp.zeros_like(g_s)

        def tile(diagonal):
            ii, jj = _iota2((BLOCK, BLOCK), 0), _iota2((BLOCK, BLOCK), 1)
            tri_suffix = (jj > ii).astype(bf16)
            tri_prefix = (jj < ii).astype(bf16)
            before = (_iota2((T, T), 0) < _iota2((T, T), 1)) if diagonal else None
            out_rows = pl.ds(pl.multiple_of(kb * T, T), T)
            for hh in range(HP):
                sl = _hsl(hh)
                qh, kh, dob = _scaled(q_ref[:, sl], scale), k_ref[:, sl], do_s[hh]
                lsig, lf = _sbT_logs(kh, qh, before)
                x = lsig + _sbT_suffix_tile(lf, tri_suffix, nc, two_pass=False)
                r = rc_ref[hh, 0]
                parts = [None] * nc
                for c in reversed(range(nc)):
                    parts[c] = jnp.exp(x[_chunk_rows(c)] + r)
                    r = r + jnp.sum(lf[_chunk_rows(c)], axis=0, keepdims=True)
                a = jnp.concatenate(parts, axis=0)
                if diagonal:
                    a = jnp.where(before, a, 0.0)
                g = a * _dot(v_ref[:, sl], dob, NT)
                gb = g.astype(bf16)
                gsum = g_s[hh]
                for c in range(nc):
                    parts[c] = _dot(tri_prefix, gb[_chunk_rows(c)]) + gsum
                    gsum = gsum + jnp.sum(g[_chunk_rows(c)], axis=0, keepdims=True)
                g_s[hh] = gsum
                dl = g - (g + jnp.concatenate(parts, axis=0)) * jnp.exp(lsig)
                if diagonal:
                    dl = jnp.where(before, dl, 0.0)
                dl = dl.astype(bf16)
                dq_s[hh] += _dot(kh, dl, TN)
                dk_ref[out_rows, sl] += _dot(dl, qh)
                dv_ref[out_rows, sl] += _dot(a.astype(bf16), dob)

        @pl.when(kb < qb)
        def _():
            tile(False)

        @pl.when(kb == qb)
        def _():
            tile(True)
            for hh in range(HP):
                dq_ref[:, _hsl(hh)] = (jnp.transpose(dq_s[hh]) * scale).astype(dq_ref.dtype)

    q, k, v, z = _pair_specs(S, BR, T, kmap)
    ospec = pl.BlockSpec((T, HW), lambda p, i, j: (i, p))
    full = pl.BlockSpec((S, HW), lambda p, i, j: (0, p))
    rcs = pl.BlockSpec((HP, 1,1, T), lambda p, i, j: (p, kmap(i, j), 0, i))
    return _pallas(
        body, (P, P, P, P, o, du, rc), name=name, grid=(H // HP, nq, nq),
        in_specs=[q, k, v, z, ospec, ospec, rcs],
        out_specs=[ospec, ospec, full, full],
        out_shape=[jax.ShapeDtypeStruct((S, BR), bf16)] * 2 + [jax.ShapeDtypeStruct((S, BR), f32)] * 2,
        scratch_shapes=[pltpu.VMEM((HP, T,HEAD_DIM), bf16), pltpu.VMEM((HP, HEAD_DIM,T), f32),
                        pltpu.VMEM((HP, 1,T), f32)],
        semantics=("parallel", "arbitrary", "arbitrary"), comm=comm)


def _adam_math(w, g, m, v):
    m = ADAM_B1 * m + (1.0 - ADAM_B1) * g
    v = ADAM_B2 * v + (1.0 - ADAM_B2) * (g * g)
    m_hat = m / (1.0 - ADAM_B1 ** ADAM_STEP)
    v_hat = v / (1.0 - ADAM_B2 ** ADAM_STEP)
    delta = -ADAM_LR * (m_hat / (jnp.sqrt(v_hat) + ADAM_EPS) + ADAM_WD * w)
    return delta, m, v


def _adamw_sum(parts, w, m, v, name, comm=None):
    L, R, C = w.shape
    assert len(parts) == L
    n_parts = parts[0].shape[0]
    tr = _pick(R, (256, 128))

    def body(*refs):
        p_refs = refs[:L]
        w_ref, m_ref, v_ref, g_ref, d_ref, nm_ref, nv_ref = refs[L:]
        layer = pl.program_id(0)
        g = None
        for l in range(L):
            gl = p_refs[l][0].astype(f32)
            for i in range(1, n_parts):
                gl = gl + p_refs[l][i].astype(f32)
            g = gl if g is None else jnp.where(layer == l, gl, g)
        g_ref[0] = g
        d_ref[0], nm_ref[0], nv_ref[0] = _adam_math(w_ref[0], g, m_ref[0], v_ref[0])

    blk = pl.BlockSpec((1, tr, C), lambda l, i: (l, i, 0))
    return _pallas(
        body, (*parts, w, m, v), name=name, grid=(L, R // tr),
        in_specs=[pl.BlockSpec((n_parts, tr, C), lambda l, i: (0, i, 0))] * L + [blk, blk, blk],
        out_specs=[blk] * 4,
        out_shape=[jax.ShapeDtypeStruct((L, R, C), f32)] * 4,
        semantics=("parallel", "parallel"), comm=comm)


def _coords():
    return lax.axis_index("x"), lax.axis_index("y"), lax.axis_index("c")


class _Gather:
    def __init__(self, shards):
        self.args = list(shards)
        self.n = len(shards)
        self.out_shape = [jax.ShapeDtypeStruct((N_DEV,) + s.shape, s.dtype) for s in shards]
        self.scratch = [pltpu.SemaphoreType.DMA((self.n, 7)), pltpu.SemaphoreType.DMA((self.n, 7)),
                        pltpu.SemaphoreType.DMA((self.n,))]

    def _ctx(self, ins, outs, sems):
        send_sems, recv_sems, local_sems = sems
        x, y, c = _coords()
        me, sibling = (x, y, c), (x, y, 1 - c)
        chips = [(1 - x, y), (x, 1 - y), (1 - x, 1 - y)]

        def slot(out, dev):
            return out.at[4 * dev[0] + 2 * dev[1] + dev[2]]

        def copy(a, k, block, to, src=None):
            return pltpu.make_async_remote_copy(
                src_ref=slot(outs[a], block) if src is None else src, dst_ref=slot(outs[a], block),
                send_sem=send_sems.at[a, k], recv_sem=recv_sems.at[a, k], device_id=to, device_id_type=MESH)

        mine = [pltpu.make_async_copy(ins[a], slot(outs[a], me), local_sems.at[a]) for a in range(self.n)]
        first = []
        for a in range(self.n):
            first.append(copy(a, 0, me, sibling, src=ins[a]))
            first += [copy(a, 1 + j, me, (*chip, c), src=ins[a]) for j, chip in enumerate(chips)]
        passed = [copy(a, 4 + j, (*chip, c), sibling) for j, chip in enumerate(chips) for a in range(self.n)]
        return c, me, sibling, chips, copy, mine, first, passed

    def start(self, ins, outs, sems):
        *_, mine, first, _ = self._ctx(ins, outs, sems)
        for cp in mine + first:
            cp.start()

    def mid(self, ins, outs, sems):
        c, me, _, chips, copy, _, _, passed = self._ctx(ins, outs, sems)
        i = 0
        for j, chip in enumerate(chips):
            for a in range(self.n):
                copy(a, 1 + j, (*chip, c), me).wait_recv()
                passed[i].start()
                i += 1

    def finish(self, ins, outs, sems):
        c, me, sibling, chips, copy, mine, first, passed = self._ctx(ins, outs, sems)
        for a in range(self.n):
            copy(a, 0, sibling, me).wait_recv()
            for j, chip in enumerate(chips):
                copy(a, 4 + j, (*chip, 1 - c), me).wait_recv()
        for cp in first + passed:
            cp.wait_send()
        for cp in mine:
            cp.wait()


N_CHIPS = 4


class _SiblingSwap:
    mid = None

    def __init__(self, blocks):
        self.args = list(blocks)
        self.n = len(blocks)
        self.out_shape = [jax.ShapeDtypeStruct((N_CHIPS,) + b.shape[1:], b.dtype) for b in blocks]
        self.scratch = [pltpu.SemaphoreType.DMA((self.n, N_CHIPS)), pltpu.SemaphoreType.DMA((self.n, N_CHIPS))]

    def _copies(self, ins, outs, sems):
        send_sems, recv_sems = sems
        x, y, c = _coords()
        return [pltpu.make_async_remote_copy(
            src_ref=ins[a].at[2 * q + (1 - c)], dst_ref=outs[a].at[q],
            send_sem=send_sems.at[a, q], recv_sem=recv_sems.at[a, q],
            device_id=(x, y, 1 - c), device_id_type=MESH) for a in range(self.n) for q in range(N_CHIPS)]

    def start(self, ins, outs, sems):
        for cp in self._copies(ins, outs, sems):
            cp.start()

    def finish(self, ins, outs, sems):
        for cp in self._copies(ins, outs, sems):
            cp.wait()


class _ChipScatter:
    mid = None

    def __init__(self, blocks):
        self.args = list(blocks)
        self.n = len(blocks)
        self.out_shape = [jax.ShapeDtypeStruct(b.shape, b.dtype) for b in blocks]
        self.scratch = [pltpu.SemaphoreType.DMA((self.n, 3)), pltpu.SemaphoreType.DMA((self.n, 3)),
                        pltpu.SemaphoreType.DMA((self.n,))]

    def _ctx(self, ins, outs, sems):
        send_sems, recv_sems, local_sems = sems
        x, y, c = _coords()
        me = 2 * x + y
        mine = [pltpu.make_async_copy(ins[a].at[me], outs[a].at[me], local_sems.at[a]) for a in range(self.n)]

        def copy(a, k, landing_here):
            px, py = x ^ ((k >> 1) & 1), y ^ (k & 1)
            them = 2 * px + py
            return pltpu.make_async_remote_copy(
                src_ref=ins[a].at[them], dst_ref=outs[a].at[them if landing_here else me],
                send_sem=send_sems.at[a, k - 1], recv_sem=recv_sems.at[a, k - 1],
                device_id=(px, py, c), device_id_type=MESH)

        sent = [copy(a, k, False) for k in range(1, N_CHIPS) for a in range(self.n)]
        arrivals = [copy(a, k, True) for k in range(1, N_CHIPS) for a in range(self.n)]
        return mine, sent, arrivals

    def start(self, ins, outs, sems):
        mine, sent, _ = self._ctx(ins, outs, sems)
        for cp in mine + sent:
            cp.start()

    def finish(self, ins, outs, sems):
        mine, sent, arrivals = self._ctx(ins, outs, sems)
        for cp in arrivals:
            cp.wait_recv()
        for cp in sent:
            cp.wait_send()
        for cp in mine:
            cp.wait()


def _pair_sum(blocks, got, core, name):
    _, R, C = blocks.shape
    tr = _pick(R, (256, 128))

    def body(core_ref, mine_ref, got_ref, o_ref):
        o_ref[...] = (mine_ref[...].astype(f32) + got_ref[...].astype(f32)).astype(o_ref.dtype)

    return pl.pallas_call(
        body, name=name,
        grid_spec=pltpu.PrefetchScalarGridSpec(
            num_scalar_prefetch=1, grid=(N_CHIPS, R // tr),
            in_specs=[pl.BlockSpec((1, tr, C), lambda q, i, core_ref: (2 * q + core_ref[0], i, 0)),
                      pl.BlockSpec((1, tr, C), lambda q, i, core_ref: (q, i, 0))],
            out_specs=pl.BlockSpec((1, tr, C), lambda q, i, core_ref: (q, i, 0))),
        out_shape=jax.ShapeDtypeStruct((N_CHIPS, R, C), blocks.dtype),
        compiler_params=_params(("parallel", "parallel")),
    )(core, blocks, got)


def _run_comm(comm, name):
    n = comm.n

    def body(*refs):
        ins, outs, sems = refs[:n], refs[n:2 * n], refs[2 * n:]
        comm.start(ins, outs, sems)
        if comm.mid is not None:
            comm.mid(ins, outs, sems)
        comm.finish(ins, outs, sems)

    return pl.pallas_call(
        body, name=name, in_specs=[ANY] * n, out_specs=[ANY] * n,
        out_shape=comm.out_shape, scratch_shapes=comm.scratch,
    )(*comm.args)


def _pallas(body, args, *, name, grid, in_specs, out_specs, out_shape, scratch_shapes=(), semantics, comm=None):
    in_specs, out_specs, out_shape, scratch_shapes = list(in_specs), list(out_specs), list(out_shape), list(scratch_shapes)
    if comm is None:
        return pl.pallas_call(
            body, name=name, grid=grid, in_specs=in_specs, out_specs=out_specs, out_shape=out_shape,
            scratch_shapes=scratch_shapes, compiler_params=_params(semantics))(*args)
    a = len(in_specs)
    b = a + comm.n
    c = b + len(out_specs)
    d = c + comm.n
    e = d + len(scratch_shapes)
    total = math.prod(grid)
    mid_step = (3 * total) // 4

    def hosted(*refs):
        step = pl.program_id(0)
        for axis in range(1, len(grid)):
            step = step * grid[axis] + pl.program_id(axis)
        ins, outs, sems = refs[a:b], refs[c:d], refs[e:]

        @pl.when(step == 0)
        def _():
            comm.start(ins, outs, sems)

        body(*refs[:a], *refs[b:c], *refs[d:e])

        if comm.mid is not None:
            @pl.when(step == mid_step)
            def _():
                comm.mid(ins, outs, sems)

        @pl.when(step == total - 1)
        def _():
            comm.finish(ins, outs, sems)

    res = pl.pallas_call(
        hosted, name=name, grid=grid, in_specs=in_specs + [ANY] * comm.n, out_specs=out_specs + [ANY] * comm.n,
        out_shape=out_shape + comm.out_shape, scratch_shapes=scratch_shapes + comm.scratch,
        compiler_params=_params(("arbitrary",) * len(grid)))(*args, *comm.args)
    return list(res[:len(out_specs)]) + [list(res[len(out_specs):])]


def _all_reduce_small(vec, name):
    R, C = vec.shape

    def body(v_ref, o_ref, gath, send_sems, recv_sems):
        x, y, c = _coords()
        me = 4 * x + 2 * y + c
        gath[me] = v_ref[...]

        def copy(k):
            px, py, pc = x ^ ((k >> 2) & 1), y ^ ((k >> 1) & 1), c ^ (k & 1)
            return pltpu.make_async_remote_copy(
                src_ref=v_ref, dst_ref=gath.at[me], send_sem=send_sems.at[k - 1], recv_sem=recv_sems.at[k - 1],
                device_id=(px, py, pc), device_id_type=MESH)

        sent = [copy(k) for k in range(1, N_DEV)]
        for cp in sent:
            cp.start()
        for cp in sent:
            cp.wait()
        total = gath[0]
        for i in range(1, N_DEV):
            total = total + gath[i]
        o_ref[...] = total

    return pl.pallas_call(
        body, name=name,
        in_specs=[pl.BlockSpec(memory_space=pltpu.VMEM)], out_specs=pl.BlockSpec(memory_space=pltpu.VMEM),
        out_shape=jax.ShapeDtypeStruct((R, C), f32),
        scratch_shapes=[pltpu.VMEM((N_DEV, R, C), f32), pltpu.SemaphoreType.DMA((7,)), pltpu.SemaphoreType.DMA((7,))],
    )(vec)


def _gathered_in(g):
    return jnp.transpose(g, (1, 0, 2)).reshape(g.shape[1], -1)


def _col_blocks(dw):
    D, N = dw.shape
    return jnp.transpose(dw.reshape(D, N_DEV, N // N_DEV), (1, 0, 2)).astype(bf16)


def _heads_major(a, n):
    return jnp.transpose(a.reshape(a.shape[0], n, HEAD_DIM), (1, 0, 2))


def _heads_minor(a):
    return jnp.transpose(a, (1, 0, 2)).reshape(a.shape[1], -1)


def kernel(x, g_pre, g_post, w_in_a, w_out_a, sinks_a, w_in_b, w_out_b, w_in_c, b_f_c, w_out_c, loss_target, m_g_pre, m_g_post, m_w_in_a, m_w_out_a, m_sinks_a, m_w_in_b, m_w_out_b, m_w_in_c, m_b_f_c, m_w_out_c, v_g_pre, v_g_post, v_w_in_a, v_w_out_a, v_sinks_a, v_w_in_b, v_w_out_b, v_w_in_c, v_b_f_c, v_w_out_c):
    S, D = x.shape[1], x.shape[2]
    H = D // HEAD_DIM
    BR = H * HEAD_DIM
    n_kv = H // 8
    KV = n_kv * HEAD_DIM
    x0 = x[0]
    target = loss_target[0]
    swa_bias = _swa_bias(H, n_kv)
    w_in = {0: w_in_a, 1: w_in_b, 2: w_in_c}
    w_out = {0: w_out_a, 1: w_out_b, 2: w_out_c}

    saved = []
    xi = x0
    riders = {"pre_norm0": [("in", 0)], "in_proj0": [("out", 0)], "mixer0": [("in", 1)],
              "mixer1": [("out", 1), ("in", 2), ("out", 2)], "mixer2": [("in", 3), ("out", 3)]}
    full = {}

    def rider(slot):
        keys = riders.get(slot)
        if not keys:
            return None
        shards = [(w_in if which == "in" else w_out)[i % 3][i // 3].astype(bf16) for which, i in keys]
        return _Gather(shards)

    def landed(slot, arrays):
        full.update(zip(riders[slot], arrays))

    for i in range(DEPTH):
        kind, j = i % 3, i // 3
        nxt = rider(f"mixer{i}")
        with_norm = rider(f"pre_norm{i}")
        h = _rmsnorm_fwd(xi, g_pre[i:i + 1], name=f"pre_norm{i}", comm=with_norm)
        if with_norm is not None:
            h, arrived = h
            landed(f"pre_norm{i}", arrived)
        W_in = _gathered_in(full["in", i])
        st = dict(x=xi, h=h)
        if kind == 2:
            n_main = 4 * BR
            pad = (-W_in.shape[1]) % 128
            W_in = jnp.pad(W_in, ((0, 0), (0, pad)))
            W_f = W_in[:, n_main:n_main + 128]
            fl = _matmul(h, W_f, "nn", f32, name=f"f_proj{i}")
            flT = jnp.transpose(fl[:, :H])
            bcol = b_f_c[j].reshape(H, 1)
            cumT = _fox_cum(flT, bcol, name=f"fox_cum{i}")
            cq, ck = cumT[:, None, :], cumT[:, :, None]
            st.update(flT=flT, bcol=bcol, cq=cq, ck=ck)
        st["W_in"] = W_in
        with_proj = rider(f"in_proj{i}")
        P = _matmul(h, W_in, "nn", bf16, name=f"in_proj{i}", comm=with_proj)
        if with_proj is not None:
            P, arrived = P
            landed(f"in_proj{i}", arrived)
        st["P"] = P
        if kind == 0:
            kh = _heads_major(P[:, BR:BR + KV], n_kv)
            vh = _heads_major(P[:, BR + KV:BR + 2 * KV], n_kv)
            o, u, *rest = _swaT_fwd(P, kh, vh, sinks_a[j], swa_bias, name=f"swa_fwd{i}", comm=nxt)
            st.update(kh=kh, vh=vh)
        elif kind == 1:
            o, u, rc, *rest = _sbT_fwd(P, BR, name=f"sb_fwd{i}", comm=nxt)
            st.update(rc=rc)
        else:
            o, u, lse, *rest = _foxT_fwd(P, cq, ck, BR, name=f"fox_fwd{i}", comm=nxt)
            st.update(lse=lse)
        if nxt is not None:
            landed(f"mixer{i}", rest[0])
        W_out = full["out", i].reshape(BR, D)
        st.update(o=o, u=u, W_out=W_out)
        y = _matmul(u, W_out, "nn", f32, name=f"out_proj{i}")
        st["y"] = y
        xi = _post_fwd(xi, y, g_post[i:i + 1], name=f"post_norm{i}")
        saved.append(st)

    loss_part, dx = _loss_fwd_bwd(xi, target, name="loss")

    dg_pre, dg_post = [None] * DEPTH, [None] * DEPTH
    dsinks = [None, None]
    db_f = None
    recv = [None] * DEPTH
    core = lax.axis_index("c").astype(jnp.int32).reshape(1)
    pending = None
    for i in reversed(range(DEPTH)):
        kind, j = i % 3, i // 3
        st = saved[i]
        dy, dg_post[i] = _post_bwd(dx, st["y"], g_post[i:i + 1], name=f"post_bwd{i}")
        du = _matmul(dy, st["W_out"], "nt", bf16, name=f"du{i}")
        dW_out = _matmul(st["u"], dy, "tn", bf16, name=f"dw_out{i}")
        P = st["P"]
        if kind == 0:
            dq, dz, dkh, dvh, dsk, *rest = _swaT_bwd(P, st["kh"], st["vh"], st["o"], du, sinks_a[j], swa_bias,
                                                     name=f"swa_bwd{i}", comm=pending)
            dsinks[j] = jnp.sum(dsk.reshape(H, BLOCK), axis=1)
            dP = jnp.concatenate([dq, _heads_minor(dkh).astype(bf16), _heads_minor(dvh).astype(bf16), dz], axis=1)
        elif kind == 1:
            dq, dz, dk, dv, *rest = _sbT_bwd(P, st["o"], du, st["rc"], BR, name=f"sb_bwd{i}", comm=pending)
            dP = jnp.concatenate([dq, dk.astype(bf16), dv.astype(bf16), dz], axis=1)
        else:
            dq, dz, dk, dv, dcq, dck, *rest = _foxT_bwd(P, st["o"], du, st["lse"], st["cq"], st["ck"], BR,
                                                       name=f"fox_bwd{i}", comm=pending)
        if pending is not None:
            recv[i + 1] = rest[0]
        if kind == 2:
            dflT, db_col = _fox_cum_bwd(dcq.reshape(H, S), dck.reshape(H, S), st["flT"], st["bcol"],
                                        name=f"fox_cum_bwd{i}")
            db_f = db_col.reshape(H)
            dfl = jnp.pad(jnp.transpose(dflT), ((0, 0), (0, 128 - H))).astype(bf16)
            dP = jnp.concatenate([dq, dk.astype(bf16), dv.astype(bf16), dz, dfl], axis=1)
        dh = _matmul(dP, st["W_in"], "nt", f32, name=f"dh{i}")
        dW_in = _matmul(st["h"], dP, "tn", bf16, name=f"dw_in{i}")
        n_cols = w_in[kind].shape[2] * N_DEV
        blocks = [_col_blocks(dW_in[:, :n_cols]), dW_out.reshape(N_DEV, BR // N_DEV, D)]
        dx, dg_pre[i], got = _pre_bwd(dh, st["x"], g_pre[i:i + 1], dx, name=f"pre_bwd{i}", comm=_SiblingSwap(blocks))
        sums = [_pair_sum(b, g, core, name=f"pair_sum_{t}{i}") for t, b, g in zip(("in", "out"), blocks, got)]
        pending = _ChipScatter(sums) if i > 0 else None
    tail_in, tail_out = _ChipScatter(sums[:1]), _ChipScatter(sums[1:])

    small = jnp.concatenate(
        [jnp.concatenate(dg_pre, axis=0).reshape(-1), jnp.concatenate(dg_post, axis=0).reshape(-1),
         jnp.concatenate(dsinks), db_f, loss_part[0, :1]])
    n_small = small.shape[0]
    rows = -(-n_small // 128)
    rows = -(-rows // 8) * 8
    small = jnp.pad(small, (0, rows * 128 - n_small)).reshape(rows, 128)
    total = _all_reduce_small(small, name="reduce_small").reshape(-1)
    o0 = DEPTH * D
    grad_g_pre = total[:o0].reshape(DEPTH, D)
    grad_g_post = total[o0:2 * o0].reshape(DEPTH, D)
    grad_sinks = total[2 * o0:2 * o0 + 2 * H].reshape(2, H)
    grad_b_f = total[2 * o0 + 2 * H:2 * o0 + 3 * H].reshape(1, H)
    loss = total[2 * o0 + 3 * H]

    def small_adam(w, g, m, v, name):
        def body(w_ref, g_ref, m_ref, v_ref, d_ref, nm_ref, nv_ref):
            d_ref[...], nm_ref[...], nv_ref[...] = _adam_math(w_ref[...], g_ref[...], m_ref[...], v_ref[...])
        vm = pl.BlockSpec(memory_space=pltpu.VMEM)
        return pl.pallas_call(body, name=name, in_specs=[vm] * 4, out_specs=[vm] * 3,
                              out_shape=[jax.ShapeDtypeStruct(w.shape, f32)] * 3)(w, g, m, v)

    upd = {}
    upd["g_pre"] = (grad_g_pre,) + tuple(small_adam(g_pre, grad_g_pre, m_g_pre, v_g_pre, "adam_g_pre"))
    upd["g_post"] = (grad_g_post,) + tuple(small_adam(g_post, grad_g_post, m_g_post, v_g_post, "adam_g_post"))
    upd["sinks_a"] = (grad_sinks,) + tuple(small_adam(sinks_a, grad_sinks, m_sinks_a, v_sinks_a, "adam_sinks"))
    upd["b_f_c"] = (grad_b_f,) + tuple(small_adam(b_f_c, grad_b_f, m_b_f_c, v_b_f_c, "adam_b_f"))

    def big(layers, which, w, m, v, tag, comm=None):
        return _adamw_sum([recv[i][which] for i in layers], w, m, v, name=f"adam_{tag}", comm=comm)

    *upd["w_in_c"], got_in = big((2,), 0, w_in_c, m_w_in_c, v_w_in_c, "in_c", comm=tail_in)
    *upd["w_in_b"], got_out = big((1,), 0, w_in_b, m_w_in_b, v_w_in_b, "in_b", comm=tail_out)
    recv[0] = [got_in[0], got_out[0]]
    upd["w_out_b"] = big((1,), 1, w_out_b, m_w_out_b, v_w_out_b, "out_b")
    upd["w_out_c"] = big((2,), 1, w_out_c, m_w_out_c, v_w_out_c, "out_c")
    upd["w_in_a"] = big((0, 3), 0, w_in_a, m_w_in_a, v_w_in_a, "in_a")
    upd["w_out_a"] = big((0, 3), 1, w_out_a, m_w_out_a, v_w_out_a, "out_a")

    names = ["g_pre", "g_post", "w_in_a", "w_out_a", "sinks_a", "w_in_b", "w_out_b", "w_in_c", "b_f_c", "w_out_c"]
    return (loss, dx[None], *[upd[k][0] for k in names], *[upd[k][1] for k in names],
            *[upd[k][2] for k in names], *[upd[k][3] for k in names])
```

```python
import functools
import math

import numpy as np
import jax
import jax.numpy as jnp
from jax import lax
from jax.experimental import pallas as pl
from jax.experimental.pallas import tpu as pltpu

HEAD_DIM = 64
HP = 8
HW = HP * HEAD_DIM
BLOCK = 128
NORM_EPS = 1e-6
NEG = -1e30
N_DEV = 8
DEPTH = 4
ADAM_LR, ADAM_B1, ADAM_B2, ADAM_EPS, ADAM_WD, ADAM_STEP = 0.001, 0.9, 0.999, 1e-8, 0.01, 10
VMEM_LIMIT = 56 * 1024 * 1024

bf16 = jnp.bfloat16
f32 = jnp.float32
MESH = pl.DeviceIdType.MESH
ANY = pl.BlockSpec(memory_space=pl.ANY)
SMEM = pl.BlockSpec(memory_space=pltpu.SMEM)

NN = (((1,), (0,)), ((), ()))
NT = (((1,), (1,)), ((), ()))
TN = (((0,), (0,)), ((), ()))


def _dot(a, b, dims=NN):
    return lax.dot_general(a, b, dims, preferred_element_type=f32)


def _params(sem):
    return pltpu.CompilerParams(dimension_semantics=sem, vmem_limit_bytes=VMEM_LIMIT)


def _attn_tile(S):
    return 512 if S % 512 == 0 and S >= 1024 else 128


def _pick(n, pref):
    for t in pref:
        if n % t == 0:
            return t
    return n


def _matmul(a, b, mode, out_dtype, name, comm=None):
    if mode == "nn":
        (M, K), (K2, N) = a.shape, b.shape
    elif mode == "nt":
        (M, K), (N, K2) = a.shape, b.shape
    else:
        (K, M), (K2, N) = a.shape, b.shape
    assert K == K2, (a.shape, b.shape, mode)
    tm = _pick(M, (1024, 512, 256, 128))
    tn = _pick(N, (1024, 768, 640, 512, 256, 128))
    tk = _pick(K, (2048, 1664, 1536, 1024, 512, 640, 256, 128))
    nk = K // tk
    dims = {"nn": NN, "nt": NT, "tn": TN}[mode]

    def body(a_ref, b_ref, o_ref, acc_ref):
        if nk == 1:
            o_ref[...] = _dot(a_ref[...], b_ref[...], dims).astype(o_ref.dtype)
            return
        k = pl.program_id(2)

        @pl.when(k == 0)
        def _():
            acc_ref[...] = jnp.zeros_like(acc_ref)

        acc_ref[...] += _dot(a_ref[...], b_ref[...], dims)

        @pl.when(k == nk - 1)
        def _():
            o_ref[...] = acc_ref[...].astype(o_ref.dtype)

    if mode == "tn":
        a_spec = pl.BlockSpec((tk, tm), lambda i, j, k: (k, i))
    else:
        a_spec = pl.BlockSpec((tm, tk), lambda i, j, k: (i, k))
    if mode == "nt":
        b_spec = pl.BlockSpec((tn, tk), lambda i, j, k: (j, k))
    else:
        b_spec = pl.BlockSpec((tk, tn), lambda i, j, k: (k, j))
    res = _pallas(
        body, (a, b), name=name,
        grid=(M // tm, N // tn, nk),
        in_specs=[a_spec, b_spec],
        out_specs=[pl.BlockSpec((tm, tn), lambda i, j, k: (i, j))],
        out_shape=[jax.ShapeDtypeStruct((M, N), out_dtype)],
        scratch_shapes=[pltpu.VMEM((tm, tn) if nk > 1 else (8, 128), f32)],
        semantics=("parallel", "parallel", "arbitrary"), comm=comm)
    return res[0] if comm is None else (res[0], res[1])


ROWS = 256


def _rows(S):
    return ROWS if S % ROWS == 0 else S


def _rmsnorm_fwd(x, g, name, comm=None):
    S, D = x.shape
    tr = _rows(S)

    def body(x_ref, g_ref, h_ref):
        xv = x_ref[...]
        r = lax.rsqrt(jnp.mean(xv * xv, axis=-1, keepdims=True) + NORM_EPS)
        h_ref[...] = (xv * r * g_ref[...]).astype(h_ref.dtype)

    res = _pallas(
        body, (x, g), name=name, grid=(S // tr,),
        in_specs=[pl.BlockSpec((tr, D), lambda i: (i, 0)), pl.BlockSpec((1, D), lambda i: (0, 0))],
        out_specs=[pl.BlockSpec((tr, D), lambda i: (i, 0))],
        out_shape=[jax.ShapeDtypeStruct((S, D), bf16)],
        semantics=("parallel",), comm=comm)
    return res[0] if comm is None else (res[0], res[1])


def _post_fwd(x, y, g, name):
    S, D = x.shape
    tr = _rows(S)

    def body(x_ref, y_ref, g_ref, o_ref):
        yv = y_ref[...]
        r = lax.rsqrt(jnp.mean(yv * yv, axis=-1, keepdims=True) + NORM_EPS)
        o_ref[...] = x_ref[...] + yv * r * g_ref[...]

    row = pl.BlockSpec((tr, D), lambda i: (i, 0))
    return pl.pallas_call(
        body, name=name, grid=(S // tr,),
        in_specs=[row, row, pl.BlockSpec((1, D), lambda i: (0, 0))],
        out_specs=row,
        out_shape=jax.ShapeDtypeStruct((S, D), f32),
        compiler_params=_params(("parallel",)),
    )(x, y, g)


def _loss_fwd_bwd(y, t, name):
    S, D = y.shape
    tr = _rows(S)

    def body(y_ref, t_ref, l_ref, d_ref):
        @pl.when(pl.program_id(0) == 0)
        def _():
            l_ref[...] = jnp.zeros_like(l_ref)

        e = y_ref[...] - t_ref[...]
        d_ref[...] = e * (1.0 / D)
        part = 0.5 * jnp.sum(jnp.sum(e * e, axis=-1, keepdims=True) * (1.0 / D), axis=0, keepdims=True)
        l_ref[...] += jnp.broadcast_to(part, l_ref.shape)

    row = pl.BlockSpec((tr, D), lambda i: (i, 0))
    return pl.pallas_call(
        body, name=name, grid=(S // tr,),
        in_specs=[row, row],
        out_specs=[pl.BlockSpec((8, 128), lambda i: (0, 0)), row],
        out_shape=[jax.ShapeDtypeStruct((8, 128), f32), jax.ShapeDtypeStruct((S, D), f32)],
        compiler_params=_params(("arbitrary",)),
    )(y, t)


def _post_bwd(dxn, y, g, name):
    S, D = y.shape
    tr = _rows(S)

    def body(d_ref, y_ref, g_ref, dy_ref, dg_ref):
        @pl.when(pl.program_id(0) == 0)
        def _():
            dg_ref[...] = jnp.zeros_like(dg_ref)

        yv = y_ref[...]
        d = d_ref[...]
        r = lax.rsqrt(jnp.mean(yv * yv, axis=-1, keepdims=True) + NORM_EPS)
        n = yv * r
        dn = d * g_ref[...]
        dg_ref[...] += jnp.sum(d * n, axis=0, keepdims=True)
        dy_ref[...] = (r * (dn - n * jnp.mean(dn * n, axis=-1, keepdims=True))).astype(dy_ref.dtype)

    row = pl.BlockSpec((tr, D), lambda i: (i, 0))
    vec = pl.BlockSpec((1, D), lambda i: (0, 0))
    return pl.pallas_call(
        body, name=name, grid=(S // tr,),
        in_specs=[row, row, vec],
        out_specs=[row, vec],
        out_shape=[jax.ShapeDtypeStruct((S, D), bf16), jax.ShapeDtypeStruct((1, D), f32)],
        compiler_params=_params(("arbitrary",)),
    )(dxn, y, g)


def _pre_bwd(dh, x, g, dres, name, comm=None):
    S, D = x.shape
    tr = _rows(S)

    def body(dh_ref, x_ref, g_ref, dres_ref, dx_ref, dg_ref):
        @pl.when(pl.program_id(0) == 0)
        def _():
            dg_ref[...] = jnp.zeros_like(dg_ref)

        xv = x_ref[...]
        d = dh_ref[...]
        r = lax.rsqrt(jnp.mean(xv * xv, axis=-1, keepdims=True) + NORM_EPS)
        n = xv * r
        dn = d * g_ref[...]
        dg_ref[...] += jnp.sum(d * n, axis=0, keepdims=True)
        dx_ref[...] = dres_ref[...] + r * (dn - n * jnp.mean(dn * n, axis=-1, keepdims=True))

    row = pl.BlockSpec((tr, D), lambda i: (i, 0))
    vec = pl.BlockSpec((1, D), lambda i: (0, 0))
    return _pallas(
        body, (dh, x, g, dres), name=name, grid=(S // tr,),
        in_specs=[row, row, vec, row],
        out_specs=[row, vec],
        out_shape=[jax.ShapeDtypeStruct((S, D), f32), jax.ShapeDtypeStruct((1, D), f32)],
        semantics=("arbitrary",), comm=comm)


def _sigmoid(x):
    return 1.0 / (1.0 + jnp.exp(-x))


def _gate_fwd(o, z):
    zf = z.astype(f32)
    return o * (zf * _sigmoid(zf))


def _gate_bwd(du, o, z):
    zf = z.astype(f32)
    sig = _sigmoid(zf)
    do = du * (zf * sig)
    dz = du * o * (sig * (1.0 + zf * (1.0 - sig)))
    return do, dz


def _iota2(shape, dim):
    return lax.broadcasted_iota(jnp.int32, shape, dim)


def _swa_specs(S, BR, KV, G):
    gw = G * HEAD_DIM
    qspec = pl.BlockSpec((BLOCK, gw), lambda h, n: (n, h))
    zoff = (BR + 2 * KV) // gw
    zspec = pl.BlockSpec((BLOCK, gw), lambda h, n: (n, zoff + h))
    cur = pl.BlockSpec((1, BLOCK, HEAD_DIM), lambda h, n: (h, n, 0))
    prev = pl.BlockSpec((1, BLOCK, HEAD_DIM), lambda h, n: (h, jnp.maximum(n - 1, 0), 0))
    return qspec, zspec, cur, prev


def _split3(x):
    x1 = x.astype(bf16)
    r1 = x - x1.astype(f32)
    x2 = r1.astype(bf16)
    x3 = (r1 - x2.astype(f32)).astype(bf16)
    return x1, x2, x3


def _split2(x):
    x1 = x.astype(bf16)
    return x1, (x - x1.astype(f32)).astype(bf16)


def _pair_specs(S, BR, T, kmap):
    nb = BR // HW
    q = pl.BlockSpec((T, HW), lambda p, i, j: (i, p))
    k = pl.BlockSpec((T, HW), lambda p, i, j: (kmap(i, j), nb + p))
    v = pl.BlockSpec((T, HW), lambda p, i, j: (kmap(i, j), 2 * nb + p))
    z = pl.BlockSpec((T, HW), lambda p, i, j: (i, 3 * nb + p))
    return q, k, v, z


def _fox_cum(flT, b, name):
    H, S = flT.shape
    tj = _pick(S, (256, 128))

    def body(fl_ref, b_ref, c_ref):
        j = pl.program_id(0)
        x = fl_ref[...] + b_ref[...]
        logf = jnp.minimum(x, 0.0) - jnp.log(1.0 + jnp.exp(-jnp.abs(x)))
        tri = (_iota2((S, tj), 0) <= j * tj + _iota2((S, tj), 1)).astype(bf16)
        c_ref[...] = sum(_dot(part, tri) for part in _split3(logf))

    return pl.pallas_call(
        body, name=name, grid=(S // tj,),
        in_specs=[pl.BlockSpec((H, S), lambda j: (0, 0)), pl.BlockSpec((H, 1), lambda j: (0, 0))],
        out_specs=pl.BlockSpec((H, tj), lambda j: (0, j)),
        out_shape=jax.ShapeDtypeStruct((H, S), f32),
        compiler_params=_params(("parallel",)),
    )(flT, b)


def _fox_cum_bwd(dcq, dck, flT, b, name):
    H, S = flT.shape
    tj = _pick(S, (256, 128))

    def body(dcq_ref, dck_ref, fl_ref, b_ref, o_ref, db_ref):
        j = pl.program_id(0)

        @pl.when(j == 0)
        def _():
            db_ref[...] = jnp.zeros_like(db_ref)

        tri = (_iota2((S, tj), 0) >= j * tj + _iota2((S, tj), 1)).astype(bf16)
        dlogf = sum(_dot(part, tri) for part in _split3(dcq_ref[...] - dck_ref[...]))
        x = fl_ref[...] + b_ref[...]
        dfl = dlogf * _sigmoid(-x)
        o_ref[...] = dfl
        db_ref[...] += jnp.sum(dfl, axis=-1, keepdims=True)

    blk = pl.BlockSpec((H, tj), lambda j: (0, j))
    whole = pl.BlockSpec((H, S), lambda j: (0, 0))
    col = pl.BlockSpec((H, 1), lambda j: (0, 0))
    return pl.pallas_call(
        body, name=name, grid=(S // tj,),
        in_specs=[whole, whole, blk, col],
        out_specs=[blk, col],
        out_shape=[jax.ShapeDtypeStruct((H, S), f32), jax.ShapeDtypeStruct((H, 1), f32)],
        compiler_params=_params(("arbitrary",)),
    )(dcq, dck, flT, b)


def _hsl(hh):
    return slice(hh * HEAD_DIM, (hh + 1) * HEAD_DIM)


def _scaled(q, scale):
    assert math.log2(scale).is_integer()
    return (q.astype(f32) * scale).astype(bf16)


def _swa_bias(H, n_kv):
    G = H // n_kv
    slopes = 2.0 ** (-8.0 * np.arange(1, H + 1, dtype=np.float32) / H)
    dist = (np.arange(BLOCK)[None, :] + BLOCK - np.arange(2 * BLOCK)[:, None]).astype(np.float32)
    valid = (dist >= 0) & (dist < BLOCK)
    out = np.empty((n_kv, 2 * BLOCK, G * BLOCK), np.float32)
    for h in range(H):
        out[h // G][:, (h % G) * BLOCK:(h % G + 1) * BLOCK] = np.where(valid, -(slopes[h] * dist), np.float32(NEG))
    return jnp.asarray(out)


def _swaT_parts(q_ref, kc_ref, kp_ref, vc_ref, vp_ref, bias_ref, sink_ref, kvh, n, G, scale):
    qg = jnp.concatenate([q_ref[:, _hsl(g)] for g in range(G)], axis=0)
    kband = jnp.concatenate([kp_ref[0], kc_ref[0]], axis=0)
    vband = jnp.concatenate([vp_ref[0], vc_ref[0]], axis=0)
    s = _dot(kband, qg, NT) * scale + bias_ref[0]
    s = jnp.where(_iota2(s.shape, 0) + jnp.where(n > 0, BLOCK, 0) >= BLOCK, s, NEG)
    sink = jnp.concatenate([jnp.full((1, BLOCK), sink_ref[kvh * G + g], f32) for g in range(G)], axis=1)
    m = jnp.maximum(jnp.max(s, axis=0, keepdims=True), sink)
    p = jnp.exp(s - m)
    ps = jnp.exp(sink - m)
    den = jnp.sum(p, axis=0, keepdims=True) + ps
    return qg, kband, vband, p, ps, den


def _swaT_specs(S, BR, KV, G):
    qspec, zspec, cur, prev = _swa_specs(S, BR, KV, G)
    bias = pl.BlockSpec((1, 2 * BLOCK, G * BLOCK), lambda h, n: (h, 0, 0))
    return qspec, zspec, cur, prev, bias


def _swaT_fwd(P, kh, vh, sinks, bias, name, comm=None):
    S = P.shape[0]
    n_kv = kh.shape[0]
    H = sinks.shape[0]
    G = H // n_kv
    BR, KV = H * HEAD_DIM, n_kv * HEAD_DIM
    scale = HEAD_DIM ** -0.5
    assert (BR + 2 * KV) % (G * HEAD_DIM) == 0

    def body(q_ref, z_ref, kc_ref, kp_ref, vc_ref, vp_ref, bias_ref, sink_ref, o_ref, u_ref):
        kvh, n = pl.program_id(0), pl.program_id(1)
        _, _, vband, p, _, den = _swaT_parts(q_ref, kc_ref, kp_ref, vc_ref, vp_ref, bias_ref, sink_ref, kvh, n, G, scale)
        o = jnp.transpose(_dot(vband, p.astype(bf16), TN) / den)
        for g in range(G):
            oh = o[g * BLOCK:(g + 1) * BLOCK]
            o_ref[:, _hsl(g)] = oh.astype(o_ref.dtype)
            u_ref[:, _hsl(g)] = _gate_fwd(oh, z_ref[:, _hsl(g)]).astype(u_ref.dtype)

    qspec, zspec, cur, prev, bspec = _swaT_specs(S, BR, KV, G)
    ospec = pl.BlockSpec((BLOCK, G * HEAD_DIM), lambda h, n: (n, h))
    return _pallas(
        body, (P, P, kh, kh, vh, vh, bias, sinks), name=name, grid=(n_kv, S // BLOCK),
        in_specs=[qspec, zspec, cur, prev, cur, prev, bspec, SMEM],
        out_specs=[ospec, ospec],
        out_shape=[jax.ShapeDtypeStruct((S, BR), bf16)] * 2,
        semantics=("parallel", "parallel"), comm=comm)


def _swaT_bwd(P, kh, vh, o, du, sinks, bias, name, comm=None):
    S = P.shape[0]
    n_kv = kh.shape[0]
    H = sinks.shape[0]
    G = H // n_kv
    BR, KV = H * HEAD_DIM, n_kv * HEAD_DIM
    scale = HEAD_DIM ** -0.5

    def body(q_ref, z_ref, kc_ref, kp_ref, vc_ref, vp_ref, o_ref, du_ref, bias_ref, sink_ref,
             dq_ref, dz_ref, dk_ref, dv_ref, ds_ref):
        kvh, n = pl.program_id(0), pl.program_id(1)

        @pl.when(n == 0)
        def _():
            dk_ref[...] = jnp.zeros_like(dk_ref)
            dv_ref[...] = jnp.zeros_like(dv_ref)
            ds_ref[...] = jnp.zeros_like(ds_ref)

        dos, prods = [], []
        for g in range(G):
            oh = o_ref[:, _hsl(g)].astype(f32)
            doh, dzh = _gate_bwd(du_ref[:, _hsl(g)].astype(f32), oh, z_ref[:, _hsl(g)])
            dz_ref[:, _hsl(g)] = dzh.astype(dz_ref.dtype)
            dos.append(doh.astype(bf16))
            prods.append(doh * oh)
        dog = jnp.concatenate(dos, axis=0)
        delta = jnp.sum(jnp.transpose(jnp.concatenate(prods, axis=0)), axis=0, keepdims=True)
        qg, kband, vband, p, ps, den = _swaT_parts(q_ref, kc_ref, kp_ref, vc_ref, vp_ref, bias_ref, sink_ref,
                                                   kvh, n, G, scale)
        inv = 1.0 / den
        pn = p * inv
        ds = (pn * (_dot(vband, dog, NT) - delta)).astype(bf16)
        dq = jnp.transpose(_dot(kband, ds, TN)) * scale
        for g in range(G):
            dq_ref[:, _hsl(g)] = dq[g * BLOCK:(g + 1) * BLOCK].astype(dq_ref.dtype)
        dkb = _dot(ds, qg) * scale
        dvb = _dot(pn.astype(bf16), dog)
        ds_ref[0] -= ps * inv * delta
        row_c = pl.multiple_of(n * BLOCK, BLOCK)
        dk_ref[0, pl.ds(row_c, BLOCK), :] += dkb[BLOCK:]
        dv_ref[0, pl.ds(row_c, BLOCK), :] += dvb[BLOCK:]

        @pl.when(n > 0)
        def _():
            row_p = pl.multiple_of((n - 1) * BLOCK, BLOCK)
            dk_ref[0, pl.ds(row_p, BLOCK), :] += dkb[:BLOCK]
            dv_ref[0, pl.ds(row_p, BLOCK), :] += dvb[:BLOCK]

    qspec, zspec, cur, prev, bspec = _swaT_specs(S, BR, KV, G)
    ospec = pl.BlockSpec((BLOCK, G * HEAD_DIM), lambda h, n: (n, h))
    full = pl.BlockSpec((1, S, HEAD_DIM), lambda h, n: (h, 0, 0))
    return _pallas(
        body, (P, P, kh, kh, vh, vh, o, du, bias, sinks), name=name, grid=(n_kv, S // BLOCK),
        in_specs=[qspec, zspec, cur, prev, cur, prev, ospec, ospec, bspec, SMEM],
        out_specs=[ospec, ospec, full, full, pl.BlockSpec((1, 1, G * BLOCK), lambda h, n: (h, 0, 0))],
        out_shape=[jax.ShapeDtypeStruct((S, BR), bf16)] * 2
        + [jax.ShapeDtypeStruct((n_kv, S, HEAD_DIM), f32)] * 2
        + [jax.ShapeDtypeStruct((n_kv, 1, G * BLOCK), f32)],
        semantics=("parallel", "arbitrary"), comm=comm)


def _foxT_fwd(P, cq, ck, BR, name, comm=None):
    S = P.shape[0]
    H = BR // HEAD_DIM
    T = _attn_tile(S)
    nq = S // T
    scale = HEAD_DIM ** -0.5
    kmap = lambda i, j: jnp.minimum(i, j)

    def body(q_ref, k_ref, v_ref, z_ref, cq_ref, ck_ref, o_ref, u_ref, lse_ref, m_s, l_s, acc_s):
        qb, kb = pl.program_id(1), pl.program_id(2)

        @pl.when(kb == 0)
        def _():
            m_s[...] = jnp.full_like(m_s, NEG)
            l_s[...] = jnp.zeros_like(l_s)
            acc_s[...] = jnp.zeros_like(acc_s)

        def tile(diagonal):
            for hh in range(HP):
                sl = _hsl(hh)
                s = _dot(k_ref[:, sl], _scaled(q_ref[:, sl], scale), NT) + cq_ref[hh] - ck_ref[hh]
                if diagonal:
                    s = jnp.where(_iota2((T, T), 0) <= _iota2((T, T), 1), s, NEG)
                m_old = m_s[hh]
                m_new = jnp.maximum(m_old, jnp.max(s, axis=0, keepdims=True))
                alpha = jnp.exp(m_old - m_new)
                p = jnp.exp(s - m_new)
                l_s[hh] = alpha * l_s[hh] + jnp.sum(p, axis=0, keepdims=True)
                acc_s[hh] = alpha * acc_s[hh] + _dot(v_ref[:, sl], p.astype(bf16), TN)
                m_s[hh] = m_new

        @pl.when(kb < qb)
        def _():
            tile(False)

        @pl.when(kb == qb)
        def _():
            tile(True)
            for hh in range(HP):
                sl = _hsl(hh)
                oh = jnp.transpose(acc_s[hh] / l_s[hh])
                o_ref[:, sl] = oh.astype(o_ref.dtype)
                u_ref[:, sl] = _gate_fwd(oh, z_ref[:, sl]).astype(u_ref.dtype)
                lse_ref[hh] = m_s[hh] + jnp.log(l_s[hh])

    q, k, v, z = _pair_specs(S, BR, T, kmap)
    rowq = pl.BlockSpec((HP, 1,T), lambda p, i, j: (p, 0, i))
    colk = pl.BlockSpec((HP, T,1), lambda p, i, j: (p, kmap(i, j), 0))
    ospec = pl.BlockSpec((T, HW), lambda p, i, j: (i, p))
    return _pallas(
        body, (P, P, P, P, cq, ck), name=name, grid=(H // HP, nq, nq),
        in_specs=[q, k, v, z, rowq, colk],
        out_specs=[ospec, ospec, rowq],
        out_shape=[jax.ShapeDtypeStruct((S, BR), bf16)] * 2 + [jax.ShapeDtypeStruct((H, 1, S), f32)],
        scratch_shapes=[pltpu.VMEM((HP, 1,T), f32), pltpu.VMEM((HP, 1,T), f32), pltpu.VMEM((HP, HEAD_DIM,T), f32)],
        semantics=("parallel", "arbitrary", "arbitrary"), comm=comm)


def _foxT_bwd(P, o, du, lse, cq, ck, BR, name, comm=None):
    S = P.shape[0]
    H = BR // HEAD_DIM
    T = _attn_tile(S)
    nq = S // T
    scale = HEAD_DIM ** -0.5
    kmap = lambda i, j: jnp.minimum(i, j)

    def body(q_ref, k_ref, v_ref, z_ref, o_ref, du_ref, lse_ref, cq_ref, ck_ref,
             dq_ref, dz_ref, dk_ref, dv_ref, dcq_ref, dck_ref, do_s, delta_s, dq_s, dcq_s):
        qb, kb = pl.program_id(1), pl.program_id(2)

        @pl.when(jnp.logical_and(qb == 0, kb == 0))
        def _():
            dk_ref[...] = jnp.zeros_like(dk_ref)
            dv_ref[...] = jnp.zeros_like(dv_ref)
            dck_ref[...] = jnp.zeros_like(dck_ref)

        @pl.when(kb == 0)
        def _():
            for hh in range(HP):
                sl = _hsl(hh)
                oh = o_ref[:, sl].astype(f32)
                doh, dzh = _gate_bwd(du_ref[:, sl].astype(f32), oh, z_ref[:, sl])
                dz_ref[:, sl] = dzh.astype(dz_ref.dtype)
                do_s[hh] = doh.astype(bf16)
                delta_s[hh] = jnp.sum(jnp.transpose(doh * oh), axis=0, keepdims=True)
            dq_s[...] = jnp.zeros_like(dq_s)
            dcq_s[...] = jnp.zeros_like(dcq_s)

        def tile(diagonal):
            rows = pl.ds(pl.multiple_of(kb * T, T), T)
            for hh in range(HP):
                sl = _hsl(hh)
                qh, kh, vh, dob = _scaled(q_ref[:, sl], scale), k_ref[:, sl], v_ref[:, sl], do_s[hh]
                s = _dot(kh, qh, NT) + cq_ref[hh] - ck_ref[hh]
                p = jnp.exp(s - lse_ref[hh])
                if diagonal:
                    p = jnp.where(_iota2((T, T), 0) <= _iota2((T, T), 1), p, 0.0)
                ds = p * (_dot(vh, dob, NT) - delta_s[hh])
                dsb = ds.astype(bf16)
                dq_s[hh] += _dot(kh, dsb, TN)
                dk_ref[rows, sl] += _dot(dsb, qh)
                dv_ref[rows, sl] += _dot(p.astype(bf16), dob)
                dcq_s[hh] += jnp.sum(ds, axis=0, keepdims=True)
                dck_ref[hh, rows, :] += jnp.sum(ds, axis=1, keepdims=True)

        @pl.when(kb < qb)
        def _():
            tile(False)

        @pl.when(kb == qb)
        def _():
            tile(True)
            for hh in range(HP):
                dq_ref[:, _hsl(hh)] = (jnp.transpose(dq_s[hh]) * scale).astype(dq_ref.dtype)
                dcq_ref[hh] = dcq_s[hh]

    q, k, v, z = _pair_specs(S, BR, T, kmap)
    rowq = pl.BlockSpec((HP, 1,T), lambda p, i, j: (p, 0, i))
    colk = pl.BlockSpec((HP, T,1), lambda p, i, j: (p, kmap(i, j), 0))
    ospec = pl.BlockSpec((T, HW), lambda p, i, j: (i, p))
    full = pl.BlockSpec((S, HW), lambda p, i, j: (0, p))
    return _pallas(
        body, (P, P, P, P, o, du, lse, cq, ck), name=name, grid=(H // HP, nq, nq),
        in_specs=[q, k, v, z, ospec, ospec, rowq, rowq, colk],
        out_specs=[ospec, ospec, full, full, rowq, pl.BlockSpec((HP, S,1), lambda p, i, j: (p, 0, 0))],
        out_shape=[jax.ShapeDtypeStruct((S, BR), bf16)] * 2 + [jax.ShapeDtypeStruct((S, BR), f32)] * 2
        + [jax.ShapeDtypeStruct((H, 1, S), f32), jax.ShapeDtypeStruct((H, S, 1), f32)],
        scratch_shapes=[pltpu.VMEM((HP, T,HEAD_DIM), bf16), pltpu.VMEM((HP, 1,T), f32),
                        pltpu.VMEM((HP, HEAD_DIM,T), f32), pltpu.VMEM((HP, 1,T), f32)],
        semantics=("parallel", "arbitrary", "arbitrary"), comm=comm)


def _sbT_logs(kc, qh, before):
    l = _dot(kc, qh, NT)
    minus_abs = lax.bitcast_convert_type(lax.bitcast_convert_type(l, jnp.int32) | jnp.int32(-2 ** 31), f32)
    lsig = jnp.minimum(l, 0.0) - jnp.log(1.0 + jnp.exp(minus_abs))
    lf = lsig - l
    if before is not None:
        lf = jnp.where(before, lf, 0.0)
    return lsig, lf


def _chunk_rows(c):
    return slice(c * BLOCK, (c + 1) * BLOCK)


def _sbT_suffix_tile(lf, tri, nc, two_pass):
    if not two_pass:
        hi = lf.astype(bf16)
        return jnp.concatenate([_dot(tri, hi[_chunk_rows(c)]) for c in range(nc)], axis=0)
    hi, lo = _split2(lf)
    tri2 = jnp.concatenate([tri, tri], axis=1)
    return jnp.concatenate(
        [_dot(tri2, jnp.concatenate([hi[_chunk_rows(c)], lo[_chunk_rows(c)]], axis=0)) for c in range(nc)], axis=0)


def _sbT_fwd(P, BR, name, comm=None):
    S = P.shape[0]
    H = BR // HEAD_DIM
    T = _attn_tile(S)
    nq = S // T
    nc = T // BLOCK
    scale = HEAD_DIM ** -0.5
    kmap = lambda i, j: jnp.maximum(i - j, 0)

    def body(q_ref, k_ref, v_ref, z_ref, o_ref, u_ref, rc_ref, r_s, acc_s):
        qb, j = pl.program_id(1), pl.program_id(2)

        @pl.when(j == 0)
        def _():
            r_s[...] = jnp.zeros_like(r_s)
            acc_s[...] = jnp.zeros_like(acc_s)

        def tile(diagonal):
            ii, jj = _iota2((BLOCK, BLOCK), 0), _iota2((BLOCK, BLOCK), 1)
            tri = (jj > ii).astype(bf16)
            before = (_iota2((T, T), 0) < _iota2((T, T), 1)) if diagonal else None
            for hh in range(HP):
                sl = _hsl(hh)
                lsig, lf = _sbT_logs(k_ref[:, sl], _scaled(q_ref[:, sl], scale), before)
                x = lsig + _sbT_suffix_tile(lf, tri, nc, two_pass=True)
                r = r_s[hh]
                rc_ref[hh, 0] = r
                parts = [None] * nc
                for c in reversed(range(nc)):
                    parts[c] = jnp.exp(x[_chunk_rows(c)] + r)
                    r = r + jnp.sum(lf[_chunk_rows(c)], axis=0, keepdims=True)
                r_s[hh] = r
                a = jnp.concatenate(parts, axis=0)
                if diagonal:
                    a = jnp.where(before, a, 0.0)
                acc_s[hh] += _dot(v_ref[:, sl], a.astype(bf16), TN)

        @pl.when(j > 0)
        def _():
            @pl.when(j <= qb)
            def _():
                tile(False)

        @pl.when(j == 0)
        def _():
            tile(True)

        @pl.when(j == qb)
        def _():
            for hh in range(HP):
                sl = _hsl(hh)
                oh = jnp.transpose(acc_s[hh])
                o_ref[:, sl] = oh.astype(o_ref.dtype)
                u_ref[:, sl] = _gate_fwd(oh, z_ref[:, sl]).astype(u_ref.dtype)

    q, k, v, z = _pair_specs(S, BR, T, kmap)
    ospec = pl.BlockSpec((T, HW), lambda p, i, j: (i, p))
    rc = pl.BlockSpec((HP, 1,1, T), lambda p, i, j: (p, kmap(i, j), 0, i))
    return _pallas(
        body, (P, P, P, P), name=name, grid=(H // HP, nq, nq),
        in_specs=[q, k, v, z],
        out_specs=[ospec, ospec, rc],
        out_shape=[jax.ShapeDtypeStruct((S, BR), bf16)] * 2 + [jax.ShapeDtypeStruct((H, nq, 1, S), f32)],
        scratch_shapes=[pltpu.VMEM((HP, 1,T), f32), pltpu.VMEM((HP, HEAD_DIM,T), f32)],
        semantics=("parallel", "arbitrary", "arbitrary"), comm=comm)


def _sbT_bwd(P, o, du, rc, BR, name, comm=None):
    S = P.shape[0]
    H = BR // HEAD_DIM
    T = _attn_tile(S)
    nq = S // T
    nc = T // BLOCK
    scale = HEAD_DIM ** -0.5
    kmap = lambda i, j: jnp.minimum(i, j)

    def body(q_ref, k_ref, v_ref, z_ref, o_ref, du_ref, rc_ref,
             dq_ref, dz_ref, dk_ref, dv_ref, do_s, dq_s, g_s):
        qb, kb = pl.program_id(1), pl.program_id(2)

        @pl.when(jnp.logical_and(qb == 0, kb == 0))
        def _():
            dk_ref[...] = jnp.zeros_like(dk_ref)
            dv_ref[...] = jnp.zeros_like(dv_ref)

        @pl.when(kb == 0)
        def _():
            for hh in range(HP):
                sl = _hsl(hh)
                doh, dzh = _gate_bwd(du_ref[:, sl].astype(f32), o_ref[:, sl].astype(f32), z_ref[:, sl])
                dz_ref[:, sl] = dzh.astype(dz_ref.dtype)
                do_s[hh] = doh.astype(bf16)
            dq_s[...] = jnp.zeros_like(dq_s)
            g_s[...] = jnp.zeros_like(g_s)

        def tile(diagonal):
            ii, jj = _iota2((BLOCK, BLOCK), 0), _iota2((BLOCK, BLOCK), 1)
            tri_suffix = (jj > ii).astype(bf16)
            tri_prefix = (jj < ii).astype(bf16)
            before = (_iota2((T, T), 0) < _iota2((T, T), 1)) if diagonal else None
            out_rows = pl.ds(pl.multiple_of(kb * T, T), T)
            for hh in range(HP):
                sl = _hsl(hh)
                qh, kh, dob = _scaled(q_ref[:, sl], scale), k_ref[:, sl], do_s[hh]
                lsig, lf = _sbT_logs(kh, qh, before)
                x = lsig + _sbT_suffix_tile(lf, tri_suffix, nc, two_pass=False)
                r = rc_ref[hh, 0]
                parts = [None] * nc
                for c in reversed(range(nc)):
                    parts[c] = jnp.exp(x[_chunk_rows(c)] + r)
                    r = r + jnp.sum(lf[_chunk_rows(c)], axis=0, keepdims=True)
                a = jnp.concatenate(parts, axis=0)
                if diagonal:
                    a = jnp.where(before, a, 0.0)
                g = a * _dot(v_ref[:, sl], dob, NT)
                gb = g.astype(bf16)
                gsum = g_s[hh]
                for c in range(nc):
                    parts[c] = _dot(tri_prefix, gb[_chunk_rows(c)]) + gsum
                    gsum = gsum + jnp.sum(g[_chunk_rows(c)], axis=0, keepdims=True)
                g_s[hh] = gsum
                dl = g - (g + jnp.concatenate(parts, axis=0)) * jnp.exp(lsig)
                if diagonal:
                    dl = jnp.where(before, dl, 0.0)
                dl = dl.astype(bf16)
                dq_s[hh] += _dot(kh, dl, TN)
                dk_ref[out_rows, sl] += _dot(dl, qh)
                dv_ref[out_rows, sl] += _dot(a.astype(bf16), dob)

        @pl.when(kb < qb)
        def _():
            tile(False)

        @pl.when(kb == qb)
        def _():
            tile(True)
            for hh in range(HP):
                dq_ref[:, _hsl(hh)] = (jnp.transpose(dq_s[hh]) * scale).astype(dq_ref.dtype)

    q, k, v, z = _pair_specs(S, BR, T, kmap)
    ospec = pl.BlockSpec((T, HW), lambda p, i, j: (i, p))
    full = pl.BlockSpec((S, HW), lambda p, i, j: (0, p))
    rcs = pl.BlockSpec((HP, 1,1, T), lambda p, i, j: (p, kmap(i, j), 0, i))
    return _pallas(
        body, (P, P, P, P, o, du, rc), name=name, grid=(H // HP, nq, nq),
        in_specs=[q, k, v, z, ospec, ospec, rcs],
        out_specs=[ospec, ospec, full, full],
        out_shape=[jax.ShapeDtypeStruct((S, BR), bf16)] * 2 + [jax.ShapeDtypeStruct((S, BR), f32)] * 2,
        scratch_shapes=[pltpu.VMEM((HP, T,HEAD_DIM), bf16), pltpu.VMEM((HP, HEAD_DIM,T), f32),
                        pltpu.VMEM((HP, 1,T), f32)],
        semantics=("parallel", "arbitrary", "arbitrary"), comm=comm)


def _adam_math(w, g, m, v):
    m = ADAM_B1 * m + (1.0 - ADAM_B1) * g
    v = ADAM_B2 * v + (1.0 - ADAM_B2) * (g * g)
    m_hat = m / (1.0 - ADAM_B1 ** ADAM_STEP)
    v_hat = v / (1.0 - ADAM_B2 ** ADAM_STEP)
    delta = -ADAM_LR * (m_hat / (jnp.sqrt(v_hat) + ADAM_EPS) + ADAM_WD * w)
    return delta, m, v


def _adamw_sum(parts, w, m, v, name):
    L, R, C = w.shape
    assert len(parts) == L
    n_parts = parts[0].shape[0]
    tr = _pick(R, (256, 128))

    def body(*refs):
        p_refs = refs[:L]
        w_ref, m_ref, v_ref, g_ref, d_ref, nm_ref, nv_ref = refs[L:]
        layer = pl.program_id(0)
        g = None
        for l in range(L):
            gl = p_refs[l][0].astype(f32)
            for i in range(1, n_parts):
                gl = gl + p_refs[l][i].astype(f32)
            g = gl if g is None else jnp.where(layer == l, gl, g)
        g_ref[0] = g
        d_ref[0], nm_ref[0], nv_ref[0] = _adam_math(w_ref[0], g, m_ref[0], v_ref[0])

    blk = pl.BlockSpec((1, tr, C), lambda l, i: (l, i, 0))
    return _pallas(
        body, (*parts, w, m, v), name=name, grid=(L, R // tr),
        in_specs=[pl.BlockSpec((n_parts, tr, C), lambda l, i: (0, i, 0))] * L + [blk, blk, blk],
        out_specs=[blk] * 4,
        out_shape=[jax.ShapeDtypeStruct((L, R, C), f32)] * 4,
        semantics=("parallel", "parallel"))


def _coords():
    return lax.axis_index("x"), lax.axis_index("y"), lax.axis_index("c")


class _Gather:
    def __init__(self, shards):
        self.args = list(shards)
        self.n = len(shards)
        self.out_shape = [jax.ShapeDtypeStruct((N_DEV,) + s.shape, s.dtype) for s in shards]
        self.scratch = [pltpu.SemaphoreType.DMA((self.n, 7)), pltpu.SemaphoreType.DMA((self.n, 7)),
                        pltpu.SemaphoreType.DMA((self.n,))]

    def _ctx(self, ins, outs, sems):
        send_sems, recv_sems, local_sems = sems
        x, y, c = _coords()
        me, sibling = (x, y, c), (x, y, 1 - c)
        chips = [(1 - x, y), (x, 1 - y), (1 - x, 1 - y)]

        def slot(out, dev):
            return out.at[4 * dev[0] + 2 * dev[1] + dev[2]]

        def copy(a, k, block, to, src=None):
            return pltpu.make_async_remote_copy(
                src_ref=slot(outs[a], block) if src is None else src, dst_ref=slot(outs[a], block),
                send_sem=send_sems.at[a, k], recv_sem=recv_sems.at[a, k], device_id=to, device_id_type=MESH)

        mine = [pltpu.make_async_copy(ins[a], slot(outs[a], me), local_sems.at[a]) for a in range(self.n)]
        first = []
        for a in range(self.n):
            first.append(copy(a, 0, me, sibling, src=ins[a]))
            first += [copy(a, 1 + j, me, (*chip, c), src=ins[a]) for j, chip in enumerate(chips)]
        passed = [copy(a, 4 + j, (*chip, c), sibling) for j, chip in enumerate(chips) for a in range(self.n)]
        return c, me, sibling, chips, copy, mine, first, passed

    def start(self, ins, outs, sems):
        *_, mine, first, _ = self._ctx(ins, outs, sems)
        for cp in mine + first:
            cp.start()

    def mid(self, ins, outs, sems):
        c, me, _, chips, copy, _, _, passed = self._ctx(ins, outs, sems)
        i = 0
        for j, chip in enumerate(chips):
            for a in range(self.n):
                copy(a, 1 + j, (*chip, c), me).wait_recv()
                passed[i].start()
                i += 1

    def finish(self, ins, outs, sems):
        c, me, sibling, chips, copy, mine, first, passed = self._ctx(ins, outs, sems)
        for a in range(self.n):
            copy(a, 0, sibling, me).wait_recv()
            for j, chip in enumerate(chips):
                copy(a, 4 + j, (*chip, 1 - c), me).wait_recv()
        for cp in first + passed:
            cp.wait_send()
        for cp in mine:
            cp.wait()


N_CHIPS = 4


class _SiblingSwap:
    mid = None

    def __init__(self, blocks):
        self.args = list(blocks)
        self.n = len(blocks)
        self.out_shape = [jax.ShapeDtypeStruct((N_CHIPS,) + b.shape[1:], b.dtype) for b in blocks]
        self.scratch = [pltpu.SemaphoreType.DMA((self.n, N_CHIPS)), pltpu.SemaphoreType.DMA((self.n, N_CHIPS))]

    def _copies(self, ins, outs, sems):
        send_sems, recv_sems = sems
        x, y, c = _coords()
        return [pltpu.make_async_remote_copy(
            src_ref=ins[a].at[2 * q + (1 - c)], dst_ref=outs[a].at[q],
            send_sem=send_sems.at[a, q], recv_sem=recv_sems.at[a, q],
            device_id=(x, y, 1 - c), device_id_type=MESH) for a in range(self.n) for q in range(N_CHIPS)]

    def start(self, ins, outs, sems):
        for cp in self._copies(ins, outs, sems):
            cp.start()

    def finish(self, ins, outs, sems):
        for cp in self._copies(ins, outs, sems):
            cp.wait()


class _ChipScatter:
    mid = None

    def __init__(self, blocks):
        self.args = list(blocks)
        self.n = len(blocks)
        self.out_shape = [jax.ShapeDtypeStruct(b.shape, b.dtype) for b in blocks]
        self.scratch = [pltpu.SemaphoreType.DMA((self.n, 3)), pltpu.SemaphoreType.DMA((self.n, 3)),
                        pltpu.SemaphoreType.DMA((self.n,))]

    def _ctx(self, ins, outs, sems):
        send_sems, recv_sems, local_sems = sems
        x, y, c = _coords()
        me = 2 * x + y
        mine = [pltpu.make_async_copy(ins[a].at[me], outs[a].at[me], local_sems.at[a]) for a in range(self.n)]

        def copy(a, k, landing_here):
            px, py = x ^ ((k >> 1) & 1), y ^ (k & 1)
            them = 2 * px + py
            return pltpu.make_async_remote_copy(
                src_ref=ins[a].at[them], dst_ref=outs[a].at[them if landing_here else me],
                send_sem=send_sems.at[a, k - 1], recv_sem=recv_sems.at[a, k - 1],
                device_id=(px, py, c), device_id_type=MESH)

        sent = [copy(a, k, False) for k in range(1, N_CHIPS) for a in range(self.n)]
        arrivals = [copy(a, k, True) for k in range(1, N_CHIPS) for a in range(self.n)]
        return mine, sent, arrivals

    def start(self, ins, outs, sems):
        mine, sent, _ = self._ctx(ins, outs, sems)
        for cp in mine + sent:
            cp.start()

    def finish(self, ins, outs, sems):
        mine, sent, arrivals = self._ctx(ins, outs, sems)
        for cp in arrivals:
            cp.wait_recv()
        for cp in sent:
            cp.wait_send()
        for cp in mine:
            cp.wait()


def _pair_sum(blocks, got, core, name):
    _, R, C = blocks.shape
    tr = _pick(R, (256, 128))

    def body(core_ref, mine_ref, got_ref, o_ref):
        o_ref[...] = (mine_ref[...].astype(f32) + got_ref[...].astype(f32)).astype(o_ref.dtype)

    return pl.pallas_call(
        body, name=name,
        grid_spec=pltpu.PrefetchScalarGridSpec(
            num_scalar_prefetch=1, grid=(N_CHIPS, R // tr),
            in_specs=[pl.BlockSpec((1, tr, C), lambda q, i, core_ref: (2 * q + core_ref[0], i, 0)),
                      pl.BlockSpec((1, tr, C), lambda q, i, core_ref: (q, i, 0))],
            out_specs=pl.BlockSpec((1, tr, C), lambda q, i, core_ref: (q, i, 0))),
        out_shape=jax.ShapeDtypeStruct((N_CHIPS, R, C), blocks.dtype),
        compiler_params=_params(("parallel", "parallel")),
    )(core, blocks, got)


def _run_comm(comm, name):
    n = comm.n

    def body(*refs):
        ins, outs, sems = refs[:n], refs[n:2 * n], refs[2 * n:]
        comm.start(ins, outs, sems)
        if comm.mid is not None:
            comm.mid(ins, outs, sems)
        comm.finish(ins, outs, sems)

    return pl.pallas_call(
        body, name=name, in_specs=[ANY] * n, out_specs=[ANY] * n,
        out_shape=comm.out_shape, scratch_shapes=comm.scratch,
    )(*comm.args)


def _pallas(body, args, *, name, grid, in_specs, out_specs, out_shape, scratch_shapes=(), semantics, comm=None):
    in_specs, out_specs, out_shape, scratch_shapes = list(in_specs), list(out_specs), list(out_shape), list(scratch_shapes)
    if comm is None:
        return pl.pallas_call(
            body, name=name, grid=grid, in_specs=in_specs, out_specs=out_specs, out_shape=out_shape,
            scratch_shapes=scratch_shapes, compiler_params=_params(semantics))(*args)
    a = len(in_specs)
    b = a + comm.n
    c = b + len(out_specs)
    d = c + comm.n
    e = d + len(scratch_shapes)
    total = math.prod(grid)
    mid_step = (7 * total) // 8

    def hosted(*refs):
        step = pl.program_id(0)
        for axis in range(1, len(grid)):
            step = step * grid[axis] + pl.program_id(axis)
        ins, outs, sems = refs[a:b], refs[c:d], refs[e:]

        @pl.when(step == 0)
        def _():
            comm.start(ins, outs, sems)

        body(*refs[:a], *refs[b:c], *refs[d:e])

        if comm.mid is not None:
            @pl.when(step == mid_step)
            def _():
                comm.mid(ins, outs, sems)

        @pl.when(step == total - 1)
        def _():
            comm.finish(ins, outs, sems)

    res = pl.pallas_call(
        hosted, name=name, grid=grid, in_specs=in_specs + [ANY] * comm.n, out_specs=out_specs + [ANY] * comm.n,
        out_shape=out_shape + comm.out_shape, scratch_shapes=scratch_shapes + comm.scratch,
        compiler_params=_params(("arbitrary",) * len(grid)))(*args, *comm.args)
    return list(res[:len(out_specs)]) + [list(res[len(out_specs):])]


def _all_reduce_small(vec, name):
    R, C = vec.shape

    def body(v_ref, o_ref, gath, send_sems, recv_sems):
        x, y, c = _coords()
        me = 4 * x + 2 * y + c
        gath[me] = v_ref[...]

        def copy(k):
            px, py, pc = x ^ ((k >> 2) & 1), y ^ ((k >> 1) & 1), c ^ (k & 1)
            return pltpu.make_async_remote_copy(
                src_ref=v_ref, dst_ref=gath.at[me], send_sem=send_sems.at[k - 1], recv_sem=recv_sems.at[k - 1],
                device_id=(px, py, pc), device_id_type=MESH)

        sent = [copy(k) for k in range(1, N_DEV)]
        for cp in sent:
            cp.start()
        for cp in sent:
            cp.wait()
        total = gath[0]
        for i in range(1, N_DEV):
            total = total + gath[i]
        o_ref[...] = total

    return pl.pallas_call(
        body, name=name,
        in_specs=[pl.BlockSpec(memory_space=pltpu.VMEM)], out_specs=pl.BlockSpec(memory_space=pltpu.VMEM),
        out_shape=jax.ShapeDtypeStruct((R, C), f32),
        scratch_shapes=[pltpu.VMEM((N_DEV, R, C), f32), pltpu.SemaphoreType.DMA((7,)), pltpu.SemaphoreType.DMA((7,))],
    )(vec)


def _gathered_in(g):
    return jnp.transpose(g, (1, 0, 2)).reshape(g.shape[1], -1)


def _col_blocks(dw):
    D, N = dw.shape
    return jnp.transpose(dw.reshape(D, N_DEV, N // N_DEV), (1, 0, 2)).astype(bf16)


def _heads_major(a, n):
    return jnp.transpose(a.reshape(a.shape[0], n, HEAD_DIM), (1, 0, 2))


def _heads_minor(a):
    return jnp.transpose(a, (1, 0, 2)).reshape(a.shape[1], -1)


def kernel(x, g_pre, g_post, w_in_a, w_out_a, sinks_a, w_in_b, w_out_b, w_in_c, b_f_c, w_out_c, loss_target, m_g_pre, m_g_post, m_w_in_a, m_w_out_a, m_sinks_a, m_w_in_b, m_w_out_b, m_w_in_c, m_b_f_c, m_w_out_c, v_g_pre, v_g_post, v_w_in_a, v_w_out_a, v_sinks_a, v_w_in_b, v_w_out_b, v_w_in_c, v_b_f_c, v_w_out_c):
    S, D = x.shape[1], x.shape[2]
    H = D // HEAD_DIM
    BR = H * HEAD_DIM
    n_kv = H // 8
    KV = n_kv * HEAD_DIM
    x0 = x[0]
    target = loss_target[0]
    swa_bias = _swa_bias(H, n_kv)
    w_in = {0: w_in_a, 1: w_in_b, 2: w_in_c}
    w_out = {0: w_out_a, 1: w_out_b, 2: w_out_c}

    saved = []
    xi = x0
    riders = {"pre_norm0": [("in", 0)], "in_proj0": [("out", 0)], "mixer0": [("in", 1)],
              "mixer1": [("out", 1), ("in", 2)], "mixer2": [("out", 2), ("in", 3), ("out", 3)]}
    full = {}

    def rider(slot):
        keys = riders.get(slot)
        if not keys:
            return None
        shards = [(w_in if which == "in" else w_out)[i % 3][i // 3].astype(bf16) for which, i in keys]
        return _Gather(shards)

    def landed(slot, arrays):
        full.update(zip(riders[slot], arrays))

    for i in range(DEPTH):
        kind, j = i % 3, i // 3
        nxt = rider(f"mixer{i}")
        with_norm = rider(f"pre_norm{i}")
        h = _rmsnorm_fwd(xi, g_pre[i:i + 1], name=f"pre_norm{i}", comm=with_norm)
        if with_norm is not None:
            h, arrived = h
            landed(f"pre_norm{i}", arrived)
        W_in = _gathered_in(full["in", i])
        st = dict(x=xi, h=h)
        if kind == 2:
            n_main = 4 * BR
            pad = (-W_in.shape[1]) % 128
            W_in = jnp.pad(W_in, ((0, 0), (0, pad)))
            W_f = W_in[:, n_main:n_main + 128]
            fl = _matmul(h, W_f, "nn", f32, name=f"f_proj{i}")
            flT = jnp.transpose(fl[:, :H])
            bcol = b_f_c[j].reshape(H, 1)
            cumT = _fox_cum(flT, bcol, name=f"fox_cum{i}")
            cq, ck = cumT[:, None, :], cumT[:, :, None]
            st.update(flT=flT, bcol=bcol, cq=cq, ck=ck)
        st["W_in"] = W_in
        with_proj = rider(f"in_proj{i}")
        P = _matmul(h, W_in, "nn", bf16, name=f"in_proj{i}", comm=with_proj)
        if with_proj is not None:
            P, arrived = P
            landed(f"in_proj{i}", arrived)
        st["P"] = P
        if kind == 0:
            kh = _heads_major(P[:, BR:BR + KV], n_kv)
            vh = _heads_major(P[:, BR + KV:BR + 2 * KV], n_kv)
            o, u, *rest = _swaT_fwd(P, kh, vh, sinks_a[j], swa_bias, name=f"swa_fwd{i}", comm=nxt)
            st.update(kh=kh, vh=vh)
        elif kind == 1:
            o, u, rc, *rest = _sbT_fwd(P, BR, name=f"sb_fwd{i}", comm=nxt)
            st.update(rc=rc)
        else:
            o, u, lse, *rest = _foxT_fwd(P, cq, ck, BR, name=f"fox_fwd{i}", comm=nxt)
            st.update(lse=lse)
        if nxt is not None:
            landed(f"mixer{i}", rest[0])
        W_out = full["out", i].reshape(BR, D)
        st.update(o=o, u=u, W_out=W_out)
        y = _matmul(u, W_out, "nn", f32, name=f"out_proj{i}")
        st["y"] = y
        xi = _post_fwd(xi, y, g_post[i:i + 1], name=f"post_norm{i}")
        saved.append(st)

    loss_part, dx = _loss_fwd_bwd(xi, target, name="loss")

    dg_pre, dg_post = [None] * DEPTH, [None] * DEPTH
    dsinks = [None, None]
    db_f = None
    recv = [None] * DEPTH
    core = lax.axis_index("c").astype(jnp.int32).reshape(1)
    pending = None
    for i in reversed(range(DEPTH)):
        kind, j = i % 3, i // 3
        st = saved[i]
        dy, dg_post[i] = _post_bwd(dx, st["y"], g_post[i:i + 1], name=f"post_bwd{i}")
        du = _matmul(dy, st["W_out"], "nt", bf16, name=f"du{i}")
        dW_out = _matmul(st["u"], dy, "tn", bf16, name=f"dw_out{i}")
        P = st["P"]
        if kind == 0:
            dq, dz, dkh, dvh, dsk, *rest = _swaT_bwd(P, st["kh"], st["vh"], st["o"], du, sinks_a[j], swa_bias,
                                                     name=f"swa_bwd{i}", comm=pending)
            dsinks[j] = jnp.sum(dsk.reshape(H, BLOCK), axis=1)
            dP = jnp.concatenate([dq, _heads_minor(dkh).astype(bf16), _heads_minor(dvh).astype(bf16), dz], axis=1)
        elif kind == 1:
            dq, dz, dk, dv, *rest = _sbT_bwd(P, st["o"], du, st["rc"], BR, name=f"sb_bwd{i}", comm=pending)
            dP = jnp.concatenate([dq, dk.astype(bf16), dv.astype(bf16), dz], axis=1)
        else:
            dq, dz, dk, dv, dcq, dck, *rest = _foxT_bwd(P, st["o"], du, st["lse"], st["cq"], st["ck"], BR,
                                                       name=f"fox_bwd{i}", comm=pending)
        if pending is not None:
            recv[i + 1] = rest[0]
        if kind == 2:
            dflT, db_col = _fox_cum_bwd(dcq.reshape(H, S), dck.reshape(H, S), st["flT"], st["bcol"],
                                        name=f"fox_cum_bwd{i}")
            db_f = db_col.reshape(H)
            dfl = jnp.pad(jnp.transpose(dflT), ((0, 0), (0, 128 - H))).astype(bf16)
            dP = jnp.concatenate([dq, dk.astype(bf16), dv.astype(bf16), dz, dfl], axis=1)
        dW_in = _matmul(st["h"], dP, "tn", bf16, name=f"dw_in{i}")
        n_cols = w_in[kind].shape[2] * N_DEV
        blocks = [_col_blocks(dW_in[:, :n_cols]), dW_out.reshape(N_DEV, BR // N_DEV, D)]
        swap = _SiblingSwap(blocks)
        if i > 0:
            dh = _matmul(dP, st["W_in"], "nt", f32, name=f"dh{i}")
            dx, dg_pre[i], got = _pre_bwd(dh, st["x"], g_pre[i:i + 1], dx, name=f"pre_bwd{i}", comm=swap)
        else:
            dh, got = _matmul(dP, st["W_in"], "nt", f32, name=f"dh{i}", comm=swap)
            dx, dg_pre[i] = _pre_bwd(dh, st["x"], g_pre[i:i + 1], dx, name=f"pre_bwd{i}")
        sums = [_pair_sum(b, g, core, name=f"pair_sum_{t}{i}") for t, b, g in zip(("in", "out"), blocks, got)]
        pending = _ChipScatter(sums)
    recv[0] = _run_comm(pending, name="scatter_dw0")

    small = jnp.concatenate(
        [jnp.concatenate(dg_pre, axis=0).reshape(-1), jnp.concatenate(dg_post, axis=0).reshape(-1),
         jnp.concatenate(dsinks), db_f, loss_part[0, :1]])
    n_small = small.shape[0]
    rows = -(-n_small // 128)
    rows = -(-rows // 8) * 8
    small = jnp.pad(small, (0, rows * 128 - n_small)).reshape(rows, 128)
    total = _all_reduce_small(small, name="reduce_small").reshape(-1)
    o0 = DEPTH * D
    grad_g_pre = total[:o0].reshape(DEPTH, D)
    grad_g_post = total[o0:2 * o0].reshape(DEPTH, D)
    grad_sinks = total[2 * o0:2 * o0 + 2 * H].reshape(2, H)
    grad_b_f = total[2 * o0 + 2 * H:2 * o0 + 3 * H].reshape(1, H)
    loss = total[2 * o0 + 3 * H]

    def small_adam(w, g, m, v, name):
        def body(w_ref, g_ref, m_ref, v_ref, d_ref, nm_ref, nv_ref):
            d_ref[...], nm_ref[...], nv_ref[...] = _adam_math(w_ref[...], g_ref[...], m_ref[...], v_ref[...])
        vm = pl.BlockSpec(memory_space=pltpu.VMEM)
        return pl.pallas_call(body, name=name, in_specs=[vm] * 4, out_specs=[vm] * 3,
                              out_shape=[jax.ShapeDtypeStruct(w.shape, f32)] * 3)(w, g, m, v)

    upd = {}
    upd["g_pre"] = (grad_g_pre,) + tuple(small_adam(g_pre, grad_g_pre, m_g_pre, v_g_pre, "adam_g_pre"))
    upd["g_post"] = (grad_g_post,) + tuple(small_adam(g_post, grad_g_post, m_g_post, v_g_post, "adam_g_post"))
    upd["sinks_a"] = (grad_sinks,) + tuple(small_adam(sinks_a, grad_sinks, m_sinks_a, v_sinks_a, "adam_sinks"))
    upd["b_f_c"] = (grad_b_f,) + tuple(small_adam(b_f_c, grad_b_f, m_b_f_c, v_b_f_c, "adam_b_f"))

    def big(layers, which, w, m, v, tag):
        return _adamw_sum([recv[i][which] for i in layers], w, m, v, name=f"adam_{tag}")

    upd["w_in_c"] = big((2,), 0, w_in_c, m_w_in_c, v_w_in_c, "in_c")
    upd["w_in_b"] = big((1,), 0, w_in_b, m_w_in_b, v_w_in_b, "in_b")
    upd["w_out_b"] = big((1,), 1, w_out_b, m_w_out_b, v_w_out_b, "out_b")
    upd["w_out_c"] = big((2,), 1, w_out_c, m_w_out_c, v_w_out_c, "out_c")
    upd["w_in_a"] = big((0, 3), 0, w_in_a, m_w_in_a, v_w_in_a, "in_a")
    upd["w_out_a"] = big((0, 3), 1, w_out_a, m_w_out_a, v_w_out_a, "out_a")

    names = ["g_pre", "g_post", "w_in_a", "w_out_a", "sinks_a", "w_in_b", "w_out_b", "w_in_c", "b_f_c", "w_out_c"]
    return (loss, dx[None], *[upd[k][0] for k in names], *[upd[k][1] for k in names],
            *[upd[k][2] for k in names], *[upd[k][3] for k in names])
```

```python
import functools
import math

import numpy as np
import jax
import jax.numpy as jnp
from jax import lax
from jax.experimental import pallas as pl
from jax.experimental.pallas import tpu as pltpu

HEAD_DIM = 64
HP = 8
HW = HP * HEAD_DIM
BLOCK = 128
NORM_EPS = 1e-6
NEG = -1e30
N_DEV = 8
DEPTH = 4
ADAM_LR, ADAM_B1, ADAM_B2, ADAM_EPS, ADAM_WD, ADAM_STEP = 0.001, 0.9, 0.999, 1e-8, 0.01, 10
VMEM_LIMIT = 56 * 1024 * 1024

bf16 = jnp.bfloat16
f32 = jnp.float32
MESH = pl.DeviceIdType.MESH
ANY = pl.BlockSpec(memory_space=pl.ANY)
SMEM = pl.BlockSpec(memory_space=pltpu.SMEM)

NN = (((1,), (0,)), ((), ()))
NT = (((1,), (1,)), ((), ()))
TN = (((0,), (0,)), ((), ()))


def _dot(a, b, dims=NN):
    return lax.dot_general(a, b, dims, preferred_element_type=f32)


def _params(sem):
    return pltpu.CompilerParams(dimension_semantics=sem, vmem_limit_bytes=VMEM_LIMIT)


def _attn_tile(S):
    return 512 if S % 512 == 0 and S >= 1024 else 128


def _pick(n, pref):
    for t in pref:
        if n % t == 0:
            return t
    return n


def _matmul(a, b, mode, out_dtype, name, comm=None):
    if mode == "nn":
        (M, K), (K2, N) = a.shape, b.shape
    elif mode == "nt":
        (M, K), (N, K2) = a.shape, b.shape
    else:
        (K, M), (K2, N) = a.shape, b.shape
    assert K == K2, (a.shape, b.shape, mode)
    tm = _pick(M, (1024, 512, 256, 128))
    tn = _pick(N, (1024, 768, 640, 512, 256, 128))
    tk = _pick(K, (2048, 1664, 1536, 1024, 512, 640, 256, 128))
    nk = K // tk
    dims = {"nn": NN, "nt": NT, "tn": TN}[mode]

    def body(a_ref, b_ref, o_ref, acc_ref):
        if nk == 1:
            o_ref[...] = _dot(a_ref[...], b_ref[...], dims).astype(o_ref.dtype)
            return
        k = pl.program_id(2)

        @pl.when(k == 0)
        def _():
            acc_ref[...] = jnp.zeros_like(acc_ref)

        acc_ref[...] += _dot(a_ref[...], b_ref[...], dims)

        @pl.when(k == nk - 1)
        def _():
            o_ref[...] = acc_ref[...].astype(o_ref.dtype)

    if mode == "tn":
        a_spec = pl.BlockSpec((tk, tm), lambda i, j, k: (k, i))
    else:
        a_spec = pl.BlockSpec((tm, tk), lambda i, j, k: (i, k))
    if mode == "nt":
        b_spec = pl.BlockSpec((tn, tk), lambda i, j, k: (j, k))
    else:
        b_spec = pl.BlockSpec((tk, tn), lambda i, j, k: (k, j))
    res = _pallas(
        body, (a, b), name=name,
        grid=(M // tm, N // tn, nk),
        in_specs=[a_spec, b_spec],
        out_specs=[pl.BlockSpec((tm, tn), lambda i, j, k: (i, j))],
        out_shape=[jax.ShapeDtypeStruct((M, N), out_dtype)],
        scratch_shapes=[pltpu.VMEM((tm, tn) if nk > 1 else (8, 128), f32)],
        semantics=("parallel", "parallel", "arbitrary"), comm=comm)
    return res[0] if comm is None else (res[0], res[1])


ROWS = 256


def _rows(S):
    return ROWS if S % ROWS == 0 else S


def _rmsnorm_fwd(x, g, name, comm=None):
    S, D = x.shape
    tr = _rows(S)

    def body(x_ref, g_ref, h_ref):
        xv = x_ref[...]
        r = lax.rsqrt(jnp.mean(xv * xv, axis=-1, keepdims=True) + NORM_EPS)
        h_ref[...] = (xv * r * g_ref[...]).astype(h_ref.dtype)

    res = _pallas(
        body, (x, g), name=name, grid=(S // tr,),
        in_specs=[pl.BlockSpec((tr, D), lambda i: (i, 0)), pl.BlockSpec((1, D), lambda i: (0, 0))],
        out_specs=[pl.BlockSpec((tr, D), lambda i: (i, 0))],
        out_shape=[jax.ShapeDtypeStruct((S, D), bf16)],
        semantics=("parallel",), comm=comm)
    return res[0] if comm is None else (res[0], res[1])


def _post_fwd(x, y, g, name):
    S, D = x.shape
    tr = _rows(S)

    def body(x_ref, y_ref, g_ref, o_ref):
        yv = y_ref[...]
        r = lax.rsqrt(jnp.mean(yv * yv, axis=-1, keepdims=True) + NORM_EPS)
        o_ref[...] = x_ref[...] + yv * r * g_ref[...]

    row = pl.BlockSpec((tr, D), lambda i: (i, 0))
    return pl.pallas_call(
        body, name=name, grid=(S // tr,),
        in_specs=[row, row, pl.BlockSpec((1, D), lambda i: (0, 0))],
        out_specs=row,
        out_shape=jax.ShapeDtypeStruct((S, D), f32),
        compiler_params=_params(("parallel",)),
    )(x, y, g)


def _loss_fwd_bwd(y, t, name):
    S, D = y.shape
    tr = _rows(S)

    def body(y_ref, t_ref, l_ref, d_ref):
        @pl.when(pl.program_id(0) == 0)
        def _():
            l_ref[...] = jnp.zeros_like(l_ref)

        e = y_ref[...] - t_ref[...]
        d_ref[...] = e * (1.0 / D)
        part = 0.5 * jnp.sum(jnp.sum(e * e, axis=-1, keepdims=True) * (1.0 / D), axis=0, keepdims=True)
        l_ref[...] += jnp.broadcast_to(part, l_ref.shape)

    row = pl.BlockSpec((tr, D), lambda i: (i, 0))
    return pl.pallas_call(
        body, name=name, grid=(S // tr,),
        in_specs=[row, row],
        out_specs=[pl.BlockSpec((8, 128), lambda i: (0, 0)), row],
        out_shape=[jax.ShapeDtypeStruct((8, 128), f32), jax.ShapeDtypeStruct((S, D), f32)],
        compiler_params=_params(("arbitrary",)),
    )(y, t)


def _post_bwd(dxn, y, g, name):
    S, D = y.shape
    tr = _rows(S)

    def body(d_ref, y_ref, g_ref, dy_ref, dg_ref):
        @pl.when(pl.program_id(0) == 0)
        def _():
            dg_ref[...] = jnp.zeros_like(dg_ref)

        yv = y_ref[...]
        d = d_ref[...]
        r = lax.rsqrt(jnp.mean(yv * yv, axis=-1, keepdims=True) + NORM_EPS)
        n = yv * r
        dn = d * g_ref[...]
        dg_ref[...] += jnp.sum(d * n, axis=0, keepdims=True)
        dy_ref[...] = (r * (dn - n * jnp.mean(dn * n, axis=-1, keepdims=True))).astype(dy_ref.dtype)

    row = pl.BlockSpec((tr, D), lambda i: (i, 0))
    vec = pl.BlockSpec((1, D), lambda i: (0, 0))
    return pl.pallas_call(
        body, name=name, grid=(S // tr,),
        in_specs=[row, row, vec],
        out_specs=[row, vec],
        out_shape=[jax.ShapeDtypeStruct((S, D), bf16), jax.ShapeDtypeStruct((1, D), f32)],
        compiler_params=_params(("arbitrary",)),
    )(dxn, y, g)


def _pre_bwd(dh, x, g, dres, name, comm=None):
    S, D = x.shape
    tr = _rows(S)

    def body(dh_ref, x_ref, g_ref, dres_ref, dx_ref, dg_ref):
        @pl.when(pl.program_id(0) == 0)
        def _():
            dg_ref[...] = jnp.zeros_like(dg_ref)

        xv = x_ref[...]
        d = dh_ref[...]
        r = lax.rsqrt(jnp.mean(xv * xv, axis=-1, keepdims=True) + NORM_EPS)
        n = xv * r
        dn = d * g_ref[...]
        dg_ref[...] += jnp.sum(d * n, axis=0, keepdims=True)
        dx_ref[...] = dres_ref[...] + r * (dn - n * jnp.mean(dn * n, axis=-1, keepdims=True))

    row = pl.BlockSpec((tr, D), lambda i: (i, 0))
    vec = pl.BlockSpec((1, D), lambda i: (0, 0))
    return _pallas(
        body, (dh, x, g, dres), name=name, grid=(S // tr,),
        in_specs=[row, row, vec, row],
        out_specs=[row, vec],
        out_shape=[jax.ShapeDtypeStruct((S, D), f32), jax.ShapeDtypeStruct((1, D), f32)],
        semantics=("arbitrary",), comm=comm)


def _sigmoid(x):
    return 1.0 / (1.0 + jnp.exp(-x))


def _gate_fwd(o, z):
    zf = z.astype(f32)
    return o * (zf * _sigmoid(zf))


def _gate_bwd(du, o, z):
    zf = z.astype(f32)
    sig = _sigmoid(zf)
    do = du * (zf * sig)
    dz = du * o * (sig * (1.0 + zf * (1.0 - sig)))
    return do, dz


def _iota2(shape, dim):
    return lax.broadcasted_iota(jnp.int32, shape, dim)


def _swa_specs(S, BR, KV, G):
    gw = G * HEAD_DIM
    qspec = pl.BlockSpec((BLOCK, gw), lambda h, n: (n, h))
    zoff = (BR + 2 * KV) // gw
    zspec = pl.BlockSpec((BLOCK, gw), lambda h, n: (n, zoff + h))
    cur = pl.BlockSpec((1, BLOCK, HEAD_DIM), lambda h, n: (h, n, 0))
    prev = pl.BlockSpec((1, BLOCK, HEAD_DIM), lambda h, n: (h, jnp.maximum(n - 1, 0), 0))
    return qspec, zspec, cur, prev


def _split3(x):
    x1 = x.astype(bf16)
    r1 = x - x1.astype(f32)
    x2 = r1.astype(bf16)
    x3 = (r1 - x2.astype(f32)).astype(bf16)
    return x1, x2, x3


def _split2(x):
    x1 = x.astype(bf16)
    return x1, (x - x1.astype(f32)).astype(bf16)


def _pair_specs(S, BR, T, kmap):
    nb = BR // HW
    q = pl.BlockSpec((T, HW), lambda p, i, j: (i, p))
    k = pl.BlockSpec((T, HW), lambda p, i, j: (kmap(i, j), nb + p))
    v = pl.BlockSpec((T, HW), lambda p, i, j: (kmap(i, j), 2 * nb + p))
    z = pl.BlockSpec((T, HW), lambda p, i, j: (i, 3 * nb + p))
    return q, k, v, z


def _fox_cum(flT, b, name):
    H, S = flT.shape
    tj = _pick(S, (256, 128))

    def body(fl_ref, b_ref, c_ref):
        j = pl.program_id(0)
        x = fl_ref[...] + b_ref[...]
        logf = jnp.minimum(x, 0.0) - jnp.log(1.0 + jnp.exp(-jnp.abs(x)))
        tri = (_iota2((S, tj), 0) <= j * tj + _iota2((S, tj), 1)).astype(bf16)
        c_ref[...] = sum(_dot(part, tri) for part in _split3(logf))

    return pl.pallas_call(
        body, name=name, grid=(S // tj,),
        in_specs=[pl.BlockSpec((H, S), lambda j: (0, 0)), pl.BlockSpec((H, 1), lambda j: (0, 0))],
        out_specs=pl.BlockSpec((H, tj), lambda j: (0, j)),
        out_shape=jax.ShapeDtypeStruct((H, S), f32),
        compiler_params=_params(("parallel",)),
    )(flT, b)


def _fox_cum_bwd(dcq, dck, flT, b, name):
    H, S = flT.shape
    tj = _pick(S, (256, 128))

    def body(dcq_ref, dck_ref, fl_ref, b_ref, o_ref, db_ref):
        j = pl.program_id(0)

        @pl.when(j == 0)
        def _():
            db_ref[...] = jnp.zeros_like(db_ref)

        tri = (_iota2((S, tj), 0) >= j * tj + _iota2((S, tj), 1)).astype(bf16)
        dlogf = sum(_dot(part, tri) for part in _split3(dcq_ref[...] - dck_ref[...]))
        x = fl_ref[...] + b_ref[...]
        dfl = dlogf * _sigmoid(-x)
        o_ref[...] = dfl
        db_ref[...] += jnp.sum(dfl, axis=-1, keepdims=True)

    blk = pl.BlockSpec((H, tj), lambda j: (0, j))
    whole = pl.BlockSpec((H, S), lambda j: (0, 0))
    col = pl.BlockSpec((H, 1), lambda j: (0, 0))
    return pl.pallas_call(
        body, name=name, grid=(S // tj,),
        in_specs=[whole, whole, blk, col],
        out_specs=[blk, col],
        out_shape=[jax.ShapeDtypeStruct((H, S), f32), jax.ShapeDtypeStruct((H, 1), f32)],
        compiler_params=_params(("arbitrary",)),
    )(dcq, dck, flT, b)


def _hsl(hh):
    return slice(hh * HEAD_DIM, (hh + 1) * HEAD_DIM)


def _scaled(q, scale):
    assert math.log2(scale).is_integer()
    return (q.astype(f32) * scale).astype(bf16)


def _swa_bias(H, n_kv):
    G = H // n_kv
    slopes = 2.0 ** (-8.0 * np.arange(1, H + 1, dtype=np.float32) / H)
    dist = (np.arange(BLOCK)[None, :] + BLOCK - np.arange(2 * BLOCK)[:, None]).astype(np.float32)
    valid = (dist >= 0) & (dist < BLOCK)
    out = np.empty((n_kv, 2 * BLOCK, G * BLOCK), np.float32)
    for h in range(H):
        out[h // G][:, (h % G) * BLOCK:(h % G + 1) * BLOCK] = np.where(valid, -(slopes[h] * dist), np.float32(NEG))
    return jnp.asarray(out)


def _swaT_parts(q_ref, kc_ref, kp_ref, vc_ref, vp_ref, bias_ref, sink_ref, kvh, n, G, scale):
    qg = jnp.concatenate([q_ref[:, _hsl(g)] for g in range(G)], axis=0)
    kband = jnp.concatenate([kp_ref[0], kc_ref[0]], axis=0)
    vband = jnp.concatenate([vp_ref[0], vc_ref[0]], axis=0)
    s = _dot(kband, qg, NT) * scale + bias_ref[0]
    s = jnp.where(_iota2(s.shape, 0) + jnp.where(n > 0, BLOCK, 0) >= BLOCK, s, NEG)
    sink = jnp.concatenate([jnp.full((1, BLOCK), sink_ref[kvh * G + g], f32) for g in range(G)], axis=1)
    m = jnp.maximum(jnp.max(s, axis=0, keepdims=True), sink)
    p = jnp.exp(s - m)
    ps = jnp.exp(sink - m)
    den = jnp.sum(p, axis=0, keepdims=True) + ps
    return qg, kband, vband, p, ps, den


def _swaT_specs(S, BR, KV, G):
    qspec, zspec, cur, prev = _swa_specs(S, BR, KV, G)
    bias = pl.BlockSpec((1, 2 * BLOCK, G * BLOCK), lambda h, n: (h, 0, 0))
    return qspec, zspec, cur, prev, bias


def _swaT_fwd(P, kh, vh, sinks, bias, name, comm=None):
    S = P.shape[0]
    n_kv = kh.shape[0]
    H = sinks.shape[0]
    G = H // n_kv
    BR, KV = H * HEAD_DIM, n_kv * HEAD_DIM
    scale = HEAD_DIM ** -0.5
    assert (BR + 2 * KV) % (G * HEAD_DIM) == 0

    def body(q_ref, z_ref, kc_ref, kp_ref, vc_ref, vp_ref, bias_ref, sink_ref, o_ref, u_ref):
        kvh, n = pl.program_id(0), pl.program_id(1)
        _, _, vband, p, _, den = _swaT_parts(q_ref, kc_ref, kp_ref, vc_ref, vp_ref, bias_ref, sink_ref, kvh, n, G, scale)
        o = jnp.transpose(_dot(vband, p.astype(bf16), TN) / den)
        for g in range(G):
            oh = o[g * BLOCK:(g + 1) * BLOCK]
            o_ref[:, _hsl(g)] = oh.astype(o_ref.dtype)
            u_ref[:, _hsl(g)] = _gate_fwd(oh, z_ref[:, _hsl(g)]).astype(u_ref.dtype)

    qspec, zspec, cur, prev, bspec = _swaT_specs(S, BR, KV, G)
    ospec = pl.BlockSpec((BLOCK, G * HEAD_DIM), lambda h, n: (n, h))
    return _pallas(
        body, (P, P, kh, kh, vh, vh, bias, sinks), name=name, grid=(n_kv, S // BLOCK),
        in_specs=[qspec, zspec, cur, prev, cur, prev, bspec, SMEM],
        out_specs=[ospec, ospec],
        out_shape=[jax.ShapeDtypeStruct((S, BR), bf16)] * 2,
        semantics=("parallel", "parallel"), comm=comm)


def _swaT_bwd(P, kh, vh, o, du, sinks, bias, name, comm=None):
    S = P.shape[0]
    n_kv = kh.shape[0]
    H = sinks.shape[0]
    G = H // n_kv
    BR, KV = H * HEAD_DIM, n_kv * HEAD_DIM
    scale = HEAD_DIM ** -0.5

    def body(q_ref, z_ref, kc_ref, kp_ref, vc_ref, vp_ref, o_ref, du_ref, bias_ref, sink_ref,
             dq_ref, dz_ref, dk_ref, dv_ref, ds_ref):
        kvh, n = pl.program_id(0), pl.program_id(1)

        @pl.when(n == 0)
        def _():
            dk_ref[...] = jnp.zeros_like(dk_ref)
            dv_ref[...] = jnp.zeros_like(dv_ref)
            ds_ref[...] = jnp.zeros_like(ds_ref)

        dos, prods = [], []
        for g in range(G):
            oh = o_ref[:, _hsl(g)].astype(f32)
            doh, dzh = _gate_bwd(du_ref[:, _hsl(g)].astype(f32), oh, z_ref[:, _hsl(g)])
            dz_ref[:, _hsl(g)] = dzh.astype(dz_ref.dtype)
            dos.append(doh.astype(bf16))
            prods.append(doh * oh)
        dog = jnp.concatenate(dos, axis=0)
        delta = jnp.sum(jnp.transpose(jnp.concatenate(prods, axis=0)), axis=0, keepdims=True)
        qg, kband, vband, p, ps, den = _swaT_parts(q_ref, kc_ref, kp_ref, vc_ref, vp_ref, bias_ref, sink_ref,
                                                   kvh, n, G, scale)
        inv = 1.0 / den
        pn = p * inv
        ds = (pn * (_dot(vband, dog, NT) - delta)).astype(bf16)
        dq = jnp.transpose(_dot(kband, ds, TN)) * scale
        for g in range(G):
            dq_ref[:, _hsl(g)] = dq[g * BLOCK:(g + 1) * BLOCK].astype(dq_ref.dtype)
        dkb = _dot(ds, qg) * scale
        dvb = _dot(pn.astype(bf16), dog)
        ds_ref[0] -= ps * inv * delta
        row_c = pl.multiple_of(n * BLOCK, BLOCK)
        dk_ref[0, pl.ds(row_c, BLOCK), :] += dkb[BLOCK:]
        dv_ref[0, pl.ds(row_c, BLOCK), :] += dvb[BLOCK:]

        @pl.when(n > 0)
        def _():
            row_p = pl.multiple_of((n - 1) * BLOCK, BLOCK)
            dk_ref[0, pl.ds(row_p, BLOCK), :] += dkb[:BLOCK]
            dv_ref[0, pl.ds(row_p, BLOCK), :] += dvb[:BLOCK]

    qspec, zspec, cur, prev, bspec = _swaT_specs(S, BR, KV, G)
    ospec = pl.BlockSpec((BLOCK, G * HEAD_DIM), lambda h, n: (n, h))
    full = pl.BlockSpec((1, S, HEAD_DIM), lambda h, n: (h, 0, 0))
    return _pallas(
        body, (P, P, kh, kh, vh, vh, o, du, bias, sinks), name=name, grid=(n_kv, S // BLOCK),
        in_specs=[qspec, zspec, cur, prev, cur, prev, ospec, ospec, bspec, SMEM],
        out_specs=[ospec, ospec, full, full, pl.BlockSpec((1, 1, G * BLOCK), lambda h, n: (h, 0, 0))],
        out_shape=[jax.ShapeDtypeStruct((S, BR), bf16)] * 2
        + [jax.ShapeDtypeStruct((n_kv, S, HEAD_DIM), f32)] * 2
        + [jax.ShapeDtypeStruct((n_kv, 1, G * BLOCK), f32)],
        semantics=("parallel", "arbitrary"), comm=comm)


def _foxT_fwd(P, cq, ck, BR, name, comm=None):
    S = P.shape[0]
    H = BR // HEAD_DIM
    T = _attn_tile(S)
    nq = S // T
    scale = HEAD_DIM ** -0.5
    kmap = lambda i, j: jnp.minimum(i, j)

    def body(q_ref, k_ref, v_ref, z_ref, cq_ref, ck_ref, o_ref, u_ref, lse_ref, m_s, l_s, acc_s):
        qb, kb = pl.program_id(1), pl.program_id(2)

        @pl.when(kb == 0)
        def _():
            m_s[...] = jnp.full_like(m_s, NEG)
            l_s[...] = jnp.zeros_like(l_s)
            acc_s[...] = jnp.zeros_like(acc_s)

        def tile(diagonal):
            for hh in range(HP):
                sl = _hsl(hh)
                s = _dot(k_ref[:, sl], _scaled(q_ref[:, sl], scale), NT) + cq_ref[hh] - ck_ref[hh]
                if diagonal:
                    s = jnp.where(_iota2((T, T), 0) <= _iota2((T, T), 1), s, NEG)
                m_old = m_s[hh]
                m_new = jnp.maximum(m_old, jnp.max(s, axis=0, keepdims=True))
                alpha = jnp.exp(m_old - m_new)
                p = jnp.exp(s - m_new)
                l_s[hh] = alpha * l_s[hh] + jnp.sum(p, axis=0, keepdims=True)
                acc_s[hh] = alpha * acc_s[hh] + _dot(v_ref[:, sl], p.astype(bf16), TN)
                m_s[hh] = m_new

        @pl.when(kb < qb)
        def _():
            tile(False)

        @pl.when(kb == qb)
        def _():
            tile(True)
            for hh in range(HP):
                sl = _hsl(hh)
                oh = jnp.transpose(acc_s[hh] / l_s[hh])
                o_ref[:, sl] = oh.astype(o_ref.dtype)
                u_ref[:, sl] = _gate_fwd(oh, z_ref[:, sl]).astype(u_ref.dtype)
                lse_ref[hh] = m_s[hh] + jnp.log(l_s[hh])

    q, k, v, z = _pair_specs(S, BR, T, kmap)
    rowq = pl.BlockSpec((HP, 1,T), lambda p, i, j: (p, 0, i))
    colk = pl.BlockSpec((HP, T,1), lambda p, i, j: (p, kmap(i, j), 0))
    ospec = pl.BlockSpec((T, HW), lambda p, i, j: (i, p))
    return _pallas(
        body, (P, P, P, P, cq, ck), name=name, grid=(H // HP, nq, nq),
        in_specs=[q, k, v, z, rowq, colk],
        out_specs=[ospec, ospec, rowq],
        out_shape=[jax.ShapeDtypeStruct((S, BR), bf16)] * 2 + [jax.ShapeDtypeStruct((H, 1, S), f32)],
        scratch_shapes=[pltpu.VMEM((HP, 1,T), f32), pltpu.VMEM((HP, 1,T), f32), pltpu.VMEM((HP, HEAD_DIM,T), f32)],
        semantics=("parallel", "arbitrary", "arbitrary"), comm=comm)


def _foxT_bwd(P, o, du, lse, cq, ck, BR, name, comm=None):
    S = P.shape[0]
    H = BR // HEAD_DIM
    T = _attn_tile(S)
    nq = S // T
    scale = HEAD_DIM ** -0.5
    kmap = lambda i, j: jnp.minimum(i, j)

    def body(q_ref, k_ref, v_ref, z_ref, o_ref, du_ref, lse_ref, cq_ref, ck_ref,
             dq_ref, dz_ref, dk_ref, dv_ref, dcq_ref, dck_ref, do_s, delta_s, dq_s, dcq_s):
        qb, kb = pl.program_id(1), pl.program_id(2)

        @pl.when(jnp.logical_and(qb == 0, kb == 0))
        def _():
            dk_ref[...] = jnp.zeros_like(dk_ref)
            dv_ref[...] = jnp.zeros_like(dv_ref)
            dck_ref[...] = jnp.zeros_like(dck_ref)

        @pl.when(kb == 0)
        def _():
            for hh in range(HP):
                sl = _hsl(hh)
                oh = o_ref[:, sl].astype(f32)
                doh, dzh = _gate_bwd(du_ref[:, sl].astype(f32), oh, z_ref[:, sl])
                dz_ref[:, sl] = dzh.astype(dz_ref.dtype)
                do_s[hh] = doh.astype(bf16)
                delta_s[hh] = jnp.sum(jnp.transpose(doh * oh), axis=0, keepdims=True)
            dq_s[...] = jnp.zeros_like(dq_s)
            dcq_s[...] = jnp.zeros_like(dcq_s)

        def tile(diagonal):
            rows = pl.ds(pl.multiple_of(kb * T, T), T)
            for hh in range(HP):
                sl = _hsl(hh)
                qh, kh, vh, dob = _scaled(q_ref[:, sl], scale), k_ref[:, sl], v_ref[:, sl], do_s[hh]
                s = _dot(kh, qh, NT) + cq_ref[hh] - ck_ref[hh]
                p = jnp.exp(s - lse_ref[hh])
                if diagonal:
                    p = jnp.where(_iota2((T, T), 0) <= _iota2((T, T), 1), p, 0.0)
                ds = p * (_dot(vh, dob, NT) - delta_s[hh])
                dsb = ds.astype(bf16)
                dq_s[hh] += _dot(kh, dsb, TN)
                dk_ref[rows, sl] += _dot(dsb, qh)
                dv_ref[rows, sl] += _dot(p.astype(bf16), dob)
                dcq_s[hh] += jnp.sum(ds, axis=0, keepdims=True)
                dck_ref[hh, rows, :] += jnp.sum(ds, axis=1, keepdims=True)

        @pl.when(kb < qb)
        def _():
            tile(False)

        @pl.when(kb == qb)
        def _():
            tile(True)
            for hh in range(HP):
                dq_ref[:, _hsl(hh)] = (jnp.transpose(dq_s[hh]) * scale).astype(dq_ref.dtype)
                dcq_ref[hh] = dcq_s[hh]

    q, k, v, z = _pair_specs(S, BR, T, kmap)
    rowq = pl.BlockSpec((HP, 1,T), lambda p, i, j: (p, 0, i))
    colk = pl.BlockSpec((HP, T,1), lambda p, i, j: (p, kmap(i, j), 0))
    ospec = pl.BlockSpec((T, HW), lambda p, i, j: (i, p))
    full = pl.BlockSpec((S, HW), lambda p, i, j: (0, p))
    return _pallas(
        body, (P, P, P, P, o, du, lse, cq, ck), name=name, grid=(H // HP, nq, nq),
        in_specs=[q, k, v, z, ospec, ospec, rowq, rowq, colk],
        out_specs=[ospec, ospec, full, full, rowq, pl.BlockSpec((HP, S,1), lambda p, i, j: (p, 0, 0))],
        out_shape=[jax.ShapeDtypeStruct((S, BR), bf16)] * 2 + [jax.ShapeDtypeStruct((S, BR), f32)] * 2
        + [jax.ShapeDtypeStruct((H, 1, S), f32), jax.ShapeDtypeStruct((H, S, 1), f32)],
        scratch_shapes=[pltpu.VMEM((HP, T,HEAD_DIM), bf16), pltpu.VMEM((HP, 1,T), f32),
                        pltpu.VMEM((HP, HEAD_DIM,T), f32), pltpu.VMEM((HP, 1,T), f32)],
        semantics=("parallel", "arbitrary", "arbitrary"), comm=comm)


def _sbT_logs(kc, qh, before):
    l = _dot(kc, qh, NT)
    minus_abs = lax.bitcast_convert_type(lax.bitcast_convert_type(l, jnp.int32) | jnp.int32(-2 ** 31), f32)
    lsig = jnp.minimum(l, 0.0) - jnp.log(1.0 + jnp.exp(minus_abs))
    lf = lsig - l
    if before is not None:
        lf = jnp.where(before, lf, 0.0)
    return lsig, lf


def _chunk_rows(c):
    return slice(c * BLOCK, (c + 1) * BLOCK)


def _sbT_suffix_tile(lf, tri, nc, two_pass):
    if not two_pass:
        hi = lf.astype(bf16)
        return jnp.concatenate([_dot(tri, hi[_chunk_rows(c)]) for c in range(nc)], axis=0)
    hi, lo = _split2(lf)
    tri2 = jnp.concatenate([tri, tri], axis=1)
    return jnp.concatenate(
        [_dot(tri2, jnp.concatenate([hi[_chunk_rows(c)], lo[_chunk_rows(c)]], axis=0)) for c in range(nc)], axis=0)


def _sbT_fwd(P, BR, name, comm=None):
    S = P.shape[0]
    H = BR // HEAD_DIM
    T = _attn_tile(S)
    nq = S // T
    nc = T // BLOCK
    scale = HEAD_DIM ** -0.5
    kmap = lambda i, j: jnp.maximum(i - j, 0)

    def body(q_ref, k_ref, v_ref, z_ref, o_ref, u_ref, rc_ref, r_s, acc_s):
        qb, j = pl.program_id(1), pl.program_id(2)

        @pl.when(j == 0)
        def _():
            r_s[...] = jnp.zeros_like(r_s)
            acc_s[...] = jnp.zeros_like(acc_s)

        def tile(diagonal):
            ii, jj = _iota2((BLOCK, BLOCK), 0), _iota2((BLOCK, BLOCK), 1)
            tri = (jj > ii).astype(bf16)
            before = (_iota2((T, T), 0) < _iota2((T, T), 1)) if diagonal else None
            for hh in range(HP):
                sl = _hsl(hh)
                lsig, lf = _sbT_logs(k_ref[:, sl], _scaled(q_ref[:, sl], scale), before)
                x = lsig + _sbT_suffix_tile(lf, tri, nc, two_pass=True)
                r = r_s[hh]
                rc_ref[hh, 0] = r
                parts = [None] * nc
                for c in reversed(range(nc)):
                    parts[c] = jnp.exp(x[_chunk_rows(c)] + r)
                    r = r + jnp.sum(lf[_chunk_rows(c)], axis=0, keepdims=True)
                r_s[hh] = r
                a = jnp.concatenate(parts, axis=0)
                if diagonal:
                    a = jnp.where(before, a, 0.0)
                acc_s[hh] += _dot(v_ref[:, sl], a.astype(bf16), TN)

        @pl.when(j > 0)
        def _():
            @pl.when(j <= qb)
            def _():
                tile(False)

        @pl.when(j == 0)
        def _():
            tile(True)

        @pl.when(j == qb)
        def _():
            for hh in range(HP):
                sl = _hsl(hh)
                oh = jnp.transpose(acc_s[hh])
                o_ref[:, sl] = oh.astype(o_ref.dtype)
                u_ref[:, sl] = _gate_fwd(oh, z_ref[:, sl]).astype(u_ref.dtype)

    q, k, v, z = _pair_specs(S, BR, T, kmap)
    ospec = pl.BlockSpec((T, HW), lambda p, i, j: (i, p))
    rc = pl.BlockSpec((HP, 1,1, T), lambda p, i, j: (p, kmap(i, j), 0, i))
    return _pallas(
        body, (P, P, P, P), name=name, grid=(H // HP, nq, nq),
        in_specs=[q, k, v, z],
        out_specs=[ospec, ospec, rc],
        out_shape=[jax.ShapeDtypeStruct((S, BR), bf16)] * 2 + [jax.ShapeDtypeStruct((H, nq, 1, S), f32)],
        scratch_shapes=[pltpu.VMEM((HP, 1,T), f32), pltpu.VMEM((HP, HEAD_DIM,T), f32)],
        semantics=("parallel", "arbitrary", "arbitrary"), comm=comm)


def _sbT_bwd(P, o, du, rc, BR, name, comm=None):
    S = P.shape[0]
    H = BR // HEAD_DIM
    T = _attn_tile(S)
    nq = S // T
    nc = T // BLOCK
    scale = HEAD_DIM ** -0.5
    kmap = lambda i, j: jnp.minimum(i, j)

    def body(q_ref, k_ref, v_ref, z_ref, o_ref, du_ref, rc_ref,
             dq_ref, dz_ref, dk_ref, dv_ref, do_s, dq_s, g_s):
        qb, kb = pl.program_id(1), pl.program_id(2)

        @pl.when(jnp.logical_and(qb == 0, kb == 0))
        def _():
            dk_ref[...] = jnp.zeros_like(dk_ref)
            dv_ref[...] = jnp.zeros_like(dv_ref)

        @pl.when(kb == 0)
        def _():
            for hh in range(HP):
                sl = _hsl(hh)
                doh, dzh = _gate_bwd(du_ref[:, sl].astype(f32), o_ref[:, sl].astype(f32), z_ref[:, sl])
                dz_ref[:, sl] = dzh.astype(dz_ref.dtype)
                do_s[hh] = doh.astype(bf16)
            dq_s[...] = jnp.zeros_like(dq_s)
            g_s[...] = jnp.zeros_like(g_s)

        def tile(diagonal):
            ii, jj = _iota2((BLOCK, BLOCK), 0), _iota2((BLOCK, BLOCK), 1)
            tri_suffix = (jj > ii).astype(bf16)
            tri_prefix = (jj < ii).astype(bf16)
            before = (_iota2((T, T), 0) < _iota2((T, T), 1)) if diagonal else None
            out_rows = pl.ds(pl.multiple_of(kb * T, T), T)
            for hh in range(HP):
                sl = _hsl(hh)
                qh, kh, dob = _scaled(q_ref[:, sl], scale), k_ref[:, sl], do_s[hh]
                lsig, lf = _sbT_logs(kh, qh, before)
                x = lsig + _sbT_suffix_tile(lf, tri_suffix, nc, two_pass=False)
                r = rc_ref[hh, 0]
                parts = [None] * nc
                for c in reversed(range(nc)):
                    parts[c] = jnp.exp(x[_chunk_rows(c)] + r)
                    r = r + jnp.sum(lf[_chunk_rows(c)], axis=0, keepdims=True)
                a = jnp.concatenate(parts, axis=0)
                if diagonal:
                    a = jnp.where(before, a, 0.0)
                g = a * _dot(v_ref[:, sl], dob, NT)
                gb = g.astype(bf16)
                gsum = g_s[hh]
                for c in range(nc):
                    parts[c] = _dot(tri_prefix, gb[_chunk_rows(c)]) + gsum
                    gsum = gsum + jnp.sum(g[_chunk_rows(c)], axis=0, keepdims=True)
                g_s[hh] = gsum
                dl = g - (g + jnp.concatenate(parts, axis=0)) * jnp.exp(lsig)
                if diagonal:
                    dl = jnp.where(before, dl, 0.0)
                dl = dl.astype(bf16)
                dq_s[hh] += _dot(kh, dl, TN)
                dk_ref[out_rows, sl] += _dot(dl, qh)
                dv_ref[out_rows, sl] += _dot(a.astype(bf16), dob)

        @pl.when(kb < qb)
        def _():
            tile(False)

        @pl.when(kb == qb)
        def _():
            tile(True)
            for hh in range(HP):
                dq_ref[:, _hsl(hh)] = (jnp.transpose(dq_s[hh]) * scale).astype(dq_ref.dtype)

    q, k, v, z = _pair_specs(S, BR, T, kmap)
    ospec = pl.BlockSpec((T, HW), lambda p, i, j: (i, p))
    full = pl.BlockSpec((S, HW), lambda p, i, j: (0, p))
    rcs = pl.BlockSpec((HP, 1,1, T), lambda p, i, j: (p, kmap(i, j), 0, i))
    return _pallas(
        body, (P, P, P, P, o, du, rc), name=name, grid=(H // HP, nq, nq),
        in_specs=[q, k, v, z, ospec, ospec, rcs],
        out_specs=[ospec, ospec, full, full],
        out_shape=[jax.ShapeDtypeStruct((S, BR), bf16)] * 2 + [jax.ShapeDtypeStruct((S, BR), f32)] * 2,
        scratch_shapes=[pltpu.VMEM((HP, T,HEAD_DIM), bf16), pltpu.VMEM((HP, HEAD_DIM,T), f32),
                        pltpu.VMEM((HP, 1,T), f32)],
        semantics=("parallel", "arbitrary", "arbitrary"), comm=comm)


def _adam_math(w, g, m, v):
    m = ADAM_B1 * m + (1.0 - ADAM_B1) * g
    v = ADAM_B2 * v + (1.0 - ADAM_B2) * (g * g)
    m_hat = m / (1.0 - ADAM_B1 ** ADAM_STEP)
    v_hat = v / (1.0 - ADAM_B2 ** ADAM_STEP)
    delta = -ADAM_LR * (m_hat / (jnp.sqrt(v_hat) + ADAM_EPS) + ADAM_WD * w)
    return delta, m, v


def _adamw_sum(parts, w, m, v, name):
    L, R, C = w.shape
    assert len(parts) == L
    n_parts = parts[0].shape[0]
    tr = _pick(R, (256, 192, 128))

    def body(*refs):
        p_refs = refs[:L]
        w_ref, m_ref, v_ref, g_ref, d_ref, nm_ref, nv_ref = refs[L:]
        layer = pl.program_id(0)
        g = None
        for l in range(L):
            gl = p_refs[l][0].astype(f32)
            for i in range(1, n_parts):
                gl = gl + p_refs[l][i].astype(f32)
            g = gl if g is None else jnp.where(layer == l, gl, g)
        g_ref[0] = g
        d_ref[0], nm_ref[0], nv_ref[0] = _adam_math(w_ref[0], g, m_ref[0], v_ref[0])

    blk = pl.BlockSpec((1, tr, C), lambda l, i: (l, i, 0))
    return _pallas(
        body, (*parts, w, m, v), name=name, grid=(L, R // tr),
        in_specs=[pl.BlockSpec((n_parts, tr, C), lambda l, i: (0, i, 0))] * L + [blk, blk, blk],
        out_specs=[blk] * 4,
        out_shape=[jax.ShapeDtypeStruct((L, R, C), f32)] * 4,
        semantics=("parallel", "parallel"))


def _coords():
    return lax.axis_index("x"), lax.axis_index("y"), lax.axis_index("c")


class _Gather:
    def __init__(self, shards):
        self.args = list(shards)
        self.n = len(shards)
        self.out_shape = [jax.ShapeDtypeStruct((N_DEV,) + s.shape, s.dtype) for s in shards]
        self.scratch = [pltpu.SemaphoreType.DMA((self.n, 7)), pltpu.SemaphoreType.DMA((self.n, 7)),
                        pltpu.SemaphoreType.DMA((self.n,))]

    def _ctx(self, ins, outs, sems):
        send_sems, recv_sems, local_sems = sems
        x, y, c = _coords()
        me, sibling = (x, y, c), (x, y, 1 - c)
        chips = [(1 - x, y), (x, 1 - y), (1 - x, 1 - y)]

        def slot(out, dev):
            return out.at[4 * dev[0] + 2 * dev[1] + dev[2]]

        def copy(a, k, block, to, src=None):
            return pltpu.make_async_remote_copy(
                src_ref=slot(outs[a], block) if src is None else src, dst_ref=slot(outs[a], block),
                send_sem=send_sems.at[a, k], recv_sem=recv_sems.at[a, k], device_id=to, device_id_type=MESH)

        mine = [pltpu.make_async_copy(ins[a], slot(outs[a], me), local_sems.at[a]) for a in range(self.n)]
        first = []
        for a in range(self.n):
            first.append(copy(a, 0, me, sibling, src=ins[a]))
            first += [copy(a, 1 + j, me, (*chip, c), src=ins[a]) for j, chip in enumerate(chips)]
        passed = [copy(a, 4 + j, (*chip, c), sibling) for j, chip in enumerate(chips) for a in range(self.n)]
        return c, me, sibling, chips, copy, mine, first, passed

    def start(self, ins, outs, sems):
        *_, mine, first, _ = self._ctx(ins, outs, sems)
        for cp in mine + first:
            cp.start()

    def mid(self, ins, outs, sems):
        c, me, _, chips, copy, _, _, passed = self._ctx(ins, outs, sems)
        i = 0
        for j, chip in enumerate(chips):
            for a in range(self.n):
                copy(a, 1 + j, (*chip, c), me).wait_recv()
                passed[i].start()
                i += 1

    def finish(self, ins, outs, sems):
        c, me, sibling, chips, copy, mine, first, passed = self._ctx(ins, outs, sems)
        for a in range(self.n):
            copy(a, 0, sibling, me).wait_recv()
            for j, chip in enumerate(chips):
                copy(a, 4 + j, (*chip, 1 - c), me).wait_recv()
        for cp in first + passed:
            cp.wait_send()
        for cp in mine:
            cp.wait()


N_CHIPS = 4


class _SiblingSwap:
    mid = None

    def __init__(self, blocks):
        self.args = list(blocks)
        self.n = len(blocks)
        self.out_shape = [jax.ShapeDtypeStruct((N_CHIPS,) + b.shape[1:], b.dtype) for b in blocks]
        self.scratch = [pltpu.SemaphoreType.DMA((self.n, N_CHIPS)), pltpu.SemaphoreType.DMA((self.n, N_CHIPS))]

    def _copies(self, ins, outs, sems):
        send_sems, recv_sems = sems
        x, y, c = _coords()
        return [pltpu.make_async_remote_copy(
            src_ref=ins[a].at[2 * q + (1 - c)], dst_ref=outs[a].at[q],
            send_sem=send_sems.at[a, q], recv_sem=recv_sems.at[a, q],
            device_id=(x, y, 1 - c), device_id_type=MESH) for a in range(self.n) for q in range(N_CHIPS)]

    def start(self, ins, outs, sems):
        for cp in self._copies(ins, outs, sems):
            cp.start()

    def finish(self, ins, outs, sems):
        for cp in self._copies(ins, outs, sems):
            cp.wait()


class _ChipScatter:
    mid = None

    def __init__(self, blocks):
        self.args = list(blocks)
        self.n = len(blocks)
        self.out_shape = [jax.ShapeDtypeStruct(b.shape, b.dtype) for b in blocks]
        self.scratch = [pltpu.SemaphoreType.DMA((self.n, 3)), pltpu.SemaphoreType.DMA((self.n, 3)),
                        pltpu.SemaphoreType.DMA((self.n,))]

    def _ctx(self, ins, outs, sems):
        send_sems, recv_sems, local_sems = sems
        x, y, c = _coords()
        me = 2 * x + y
        mine = [pltpu.make_async_copy(ins[a].at[me], outs[a].at[me], local_sems.at[a]) for a in range(self.n)]

        def copy(a, k, landing_here):
            px, py = x ^ ((k >> 1) & 1), y ^ (k & 1)
            them = 2 * px + py
            return pltpu.make_async_remote_copy(
                src_ref=ins[a].at[them], dst_ref=outs[a].at[them if landing_here else me],
                send_sem=send_sems.at[a, k - 1], recv_sem=recv_sems.at[a, k - 1],
                device_id=(px, py, c), device_id_type=MESH)

        sent = [copy(a, k, False) for k in range(1, N_CHIPS) for a in range(self.n)]
        arrivals = [copy(a, k, True) for k in range(1, N_CHIPS) for a in range(self.n)]
        return mine, sent, arrivals

    def start(self, ins, outs, sems):
        mine, sent, _ = self._ctx(ins, outs, sems)
        for cp in mine + sent:
            cp.start()

    def finish(self, ins, outs, sems):
        mine, sent, arrivals = self._ctx(ins, outs, sems)
        for cp in arrivals:
            cp.wait_recv()
        for cp in sent:
            cp.wait_send()
        for cp in mine:
            cp.wait()


def _pair_sum(blocks, got, core, name):
    _, R, C = blocks.shape
    tr = _pick(R, (256, 192, 128))

    def body(core_ref, mine_ref, got_ref, o_ref):
        o_ref[...] = (mine_ref[...].astype(f32) + got_ref[...].astype(f32)).astype(o_ref.dtype)

    return pl.pallas_call(
        body, name=name,
        grid_spec=pltpu.PrefetchScalarGridSpec(
            num_scalar_prefetch=1, grid=(N_CHIPS, R // tr),
            in_specs=[pl.BlockSpec((1, tr, C), lambda q, i, core_ref: (2 * q + core_ref[0], i, 0)),
                      pl.BlockSpec((1, tr, C), lambda q, i, core_ref: (q, i, 0))],
            out_specs=pl.BlockSpec((1, tr, C), lambda q, i, core_ref: (q, i, 0))),
        out_shape=jax.ShapeDtypeStruct((N_CHIPS, R, C), blocks.dtype),
        compiler_params=_params(("parallel", "parallel")),
    )(core, blocks, got)


def _run_comm(comm, name):
    n = comm.n

    def body(*refs):
        ins, outs, sems = refs[:n], refs[n:2 * n], refs[2 * n:]
        comm.start(ins, outs, sems)
        if comm.mid is not None:
            comm.mid(ins, outs, sems)
        comm.finish(ins, outs, sems)

    return pl.pallas_call(
        body, name=name, in_specs=[ANY] * n, out_specs=[ANY] * n,
        out_shape=comm.out_shape, scratch_shapes=comm.scratch,
    )(*comm.args)


def _pallas(body, args, *, name, grid, in_specs, out_specs, out_shape, scratch_shapes=(), semantics, comm=None):
    in_specs, out_specs, out_shape, scratch_shapes = list(in_specs), list(out_specs), list(out_shape), list(scratch_shapes)
    if comm is None:
        return pl.pallas_call(
            body, name=name, grid=grid, in_specs=in_specs, out_specs=out_specs, out_shape=out_shape,
            scratch_shapes=scratch_shapes, compiler_params=_params(semantics))(*args)
    a = len(in_specs)
    b = a + comm.n
    c = b + len(out_specs)
    d = c + comm.n
    e = d + len(scratch_shapes)
    total = math.prod(grid)
    mid_step = (7 * total) // 8

    def hosted(*refs):
        step = pl.program_id(0)
        for axis in range(1, len(grid)):
            step = step * grid[axis] + pl.program_id(axis)
        ins, outs, sems = refs[a:b], refs[c:d], refs[e:]

        @pl.when(step == 0)
        def _():
            comm.start(ins, outs, sems)

        body(*refs[:a], *refs[b:c], *refs[d:e])

        if comm.mid is not None:
            @pl.when(step == mid_step)
            def _():
                comm.mid(ins, outs, sems)

        @pl.when(step == total - 1)
        def _():
            comm.finish(ins, outs, sems)

    res = pl.pallas_call(
        hosted, name=name, grid=grid, in_specs=in_specs + [ANY] * comm.n, out_specs=out_specs + [ANY] * comm.n,
        out_shape=out_shape + comm.out_shape, scratch_shapes=scratch_shapes + comm.scratch,
        compiler_params=_params(("arbitrary",) * len(grid)))(*args, *comm.args)
    return list(res[:len(out_specs)]) + [list(res[len(out_specs):])]


def _all_reduce_small(vec, name):
    R, C = vec.shape

    def body(v_ref, o_ref, gath, send_sems, recv_sems):
        x, y, c = _coords()
        me = 4 * x + 2 * y + c
        gath[me] = v_ref[...]

        def copy(k):
            px, py, pc = x ^ ((k >> 2) & 1), y ^ ((k >> 1) & 1), c ^ (k & 1)
            return pltpu.make_async_remote_copy(
                src_ref=v_ref, dst_ref=gath.at[me], send_sem=send_sems.at[k - 1], recv_sem=recv_sems.at[k - 1],
                device_id=(px, py, pc), device_id_type=MESH)

        sent = [copy(k) for k in range(1, N_DEV)]
        for cp in sent:
            cp.start()
        for cp in sent:
            cp.wait()
        total = gath[0]
        for i in range(1, N_DEV):
            total = total + gath[i]
        o_ref[...] = total

    return pl.pallas_call(
        body, name=name,
        in_specs=[pl.BlockSpec(memory_space=pltpu.VMEM)], out_specs=pl.BlockSpec(memory_space=pltpu.VMEM),
        out_shape=jax.ShapeDtypeStruct((R, C), f32),
        scratch_shapes=[pltpu.VMEM((N_DEV, R, C), f32), pltpu.SemaphoreType.DMA((7,)), pltpu.SemaphoreType.DMA((7,))],
    )(vec)


def _gathered_in(g):
    return jnp.transpose(g, (1, 0, 2)).reshape(g.shape[1], -1)


def _col_blocks(dw):
    D, N = dw.shape
    return jnp.transpose(dw.reshape(D, N_DEV, N // N_DEV), (1, 0, 2)).astype(bf16)


def _heads_major(a, n):
    return jnp.transpose(a.reshape(a.shape[0], n, HEAD_DIM), (1, 0, 2))


def _heads_minor(a):
    return jnp.transpose(a, (1, 0, 2)).reshape(a.shape[1], -1)


def kernel(x, g_pre, g_post, w_in_a, w_out_a, sinks_a, w_in_b, w_out_b, w_in_c, b_f_c, w_out_c, loss_target, m_g_pre, m_g_post, m_w_in_a, m_w_out_a, m_sinks_a, m_w_in_b, m_w_out_b, m_w_in_c, m_b_f_c, m_w_out_c, v_g_pre, v_g_post, v_w_in_a, v_w_out_a, v_sinks_a, v_w_in_b, v_w_out_b, v_w_in_c, v_b_f_c, v_w_out_c):
    S, D = x.shape[1], x.shape[2]
    H = D // HEAD_DIM
    BR = H * HEAD_DIM
    n_kv = H // 8
    KV = n_kv * HEAD_DIM
    x0 = x[0]
    target = loss_target[0]
    swa_bias = _swa_bias(H, n_kv)
    w_in = {0: w_in_a, 1: w_in_b, 2: w_in_c}
    w_out = {0: w_out_a, 1: w_out_b, 2: w_out_c}

    saved = []
    xi = x0
    riders = {"pre_norm0": [("in", 0)], "in_proj0": [("out", 0)], "mixer0": [("in", 1)],
              "mixer1": [("out", 1), ("in", 2)], "mixer2": [("out", 2), ("in", 3), ("out", 3)]}
    full = {}

    def held_transposed(kind):
        c = w_in[kind].shape[2]
        return c % BLOCK != 0 and c % 16 == 0

    def shard_of(which, i):
        kind, j = i % 3, i // 3
        if which == "out":
            return w_out[kind][j].astype(bf16)
        if held_transposed(kind):
            return jnp.swapaxes(w_in[kind], 1, 2)[j].astype(bf16)
        return w_in[kind][j].astype(bf16)

    def rider(slot):
        keys = riders.get(slot)
        if not keys:
            return None
        return _Gather([shard_of(which, i) for which, i in keys])

    def landed(slot, arrays):
        full.update(zip(riders[slot], arrays))

    for i in range(DEPTH):
        kind, j = i % 3, i // 3
        nxt = rider(f"mixer{i}")
        with_norm = rider(f"pre_norm{i}")
        h = _rmsnorm_fwd(xi, g_pre[i:i + 1], name=f"pre_norm{i}", comm=with_norm)
        if with_norm is not None:
            h, arrived = h
            landed(f"pre_norm{i}", arrived)
        w_t = held_transposed(kind)
        W_in = full["in", i].reshape(-1, D) if w_t else _gathered_in(full["in", i])
        st = dict(x=xi, h=h)
        if kind == 2:
            n_main = 4 * BR
            pad = (-W_in.shape[1]) % 128
            W_in = jnp.pad(W_in, ((0, 0), (0, pad)))
            W_f = W_in[:, n_main:n_main + 128]
            fl = _matmul(h, W_f, "nn", f32, name=f"f_proj{i}")
            flT = jnp.transpose(fl[:, :H])
            bcol = b_f_c[j].reshape(H, 1)
            cumT = _fox_cum(flT, bcol, name=f"fox_cum{i}")
            cq, ck = cumT[:, None, :], cumT[:, :, None]
            st.update(flT=flT, bcol=bcol, cq=cq, ck=ck)
        st["W_in"] = W_in
        with_proj = rider(f"in_proj{i}")
        P = _matmul(h, W_in, "nt" if w_t else "nn", bf16, name=f"in_proj{i}", comm=with_proj)
        if with_proj is not None:
            P, arrived = P
            landed(f"in_proj{i}", arrived)
        st["P"] = P
        if kind == 0:
            kh = _heads_major(P[:, BR:BR + KV], n_kv)
            vh = _heads_major(P[:, BR + KV:BR + 2 * KV], n_kv)
            o, u, *rest = _swaT_fwd(P, kh, vh, sinks_a[j], swa_bias, name=f"swa_fwd{i}", comm=nxt)
            st.update(kh=kh, vh=vh)
        elif kind == 1:
            o, u, rc, *rest = _sbT_fwd(P, BR, name=f"sb_fwd{i}", comm=nxt)
            st.update(rc=rc)
        else:
            o, u, lse, *rest = _foxT_fwd(P, cq, ck, BR, name=f"fox_fwd{i}", comm=nxt)
            st.update(lse=lse)
        if nxt is not None:
            landed(f"mixer{i}", rest[0])
        W_out = full["out", i].reshape(BR, D)
        st.update(o=o, u=u, W_out=W_out)
        y = _matmul(u, W_out, "nn", f32, name=f"out_proj{i}")
        st["y"] = y
        xi = _post_fwd(xi, y, g_post[i:i + 1], name=f"post_norm{i}")
        saved.append(st)

    loss_part, dx = _loss_fwd_bwd(xi, target, name="loss")

    dg_pre, dg_post = [None] * DEPTH, [None] * DEPTH
    dsinks = [None, None]
    db_f = None
    recv = [None] * DEPTH
    core = lax.axis_index("c").astype(jnp.int32).reshape(1)
    pending = None
    for i in reversed(range(DEPTH)):
        kind, j = i % 3, i // 3
        st = saved[i]
        dy, dg_post[i] = _post_bwd(dx, st["y"], g_post[i:i + 1], name=f"post_bwd{i}")
        du = _matmul(dy, st["W_out"], "nt", bf16, name=f"du{i}")
        dW_out = _matmul(st["u"], dy, "tn", bf16, name=f"dw_out{i}")
        P = st["P"]
        if kind == 0:
            dq, dz, dkh, dvh, dsk, *rest = _swaT_bwd(P, st["kh"], st["vh"], st["o"], du, sinks_a[j], swa_bias,
                                                     name=f"swa_bwd{i}", comm=pending)
            dsinks[j] = jnp.sum(dsk.reshape(H, BLOCK), axis=1)
            dP = jnp.concatenate([dq, _heads_minor(dkh).astype(bf16), _heads_minor(dvh).astype(bf16), dz], axis=1)
        elif kind == 1:
            dq, dz, dk, dv, *rest = _sbT_bwd(P, st["o"], du, st["rc"], BR, name=f"sb_bwd{i}", comm=pending)
            dP = jnp.concatenate([dq, dk.astype(bf16), dv.astype(bf16), dz], axis=1)
        else:
            dq, dz, dk, dv, dcq, dck, *rest = _foxT_bwd(P, st["o"], du, st["lse"], st["cq"], st["ck"], BR,
                                                       name=f"fox_bwd{i}", comm=pending)
        if pending is not None:
            recv[i + 1] = rest[0]
        if kind == 2:
            dflT, db_col = _fox_cum_bwd(dcq.reshape(H, S), dck.reshape(H, S), st["flT"], st["bcol"],
                                        name=f"fox_cum_bwd{i}")
            db_f = db_col.reshape(H)
            dfl = jnp.pad(jnp.transpose(dflT), ((0, 0), (0, 128 - H))).astype(bf16)
            dP = jnp.concatenate([dq, dk.astype(bf16), dv.astype(bf16), dz, dfl], axis=1)
        w_t = held_transposed(kind)
        c_shard = w_in[kind].shape[2]
        if w_t:
            dW_in = _matmul(dP, st["h"], "tn", bf16, name=f"dw_in{i}").reshape(N_DEV, c_shard, D)
        else:
            dW_in = _col_blocks(_matmul(st["h"], dP, "tn", bf16, name=f"dw_in{i}")[:, :c_shard * N_DEV])
        blocks = [dW_in, dW_out.reshape(N_DEV, BR // N_DEV, D)]
        swap = _SiblingSwap(blocks)
        dh_mode = "nn" if w_t else "nt"
        if i > 0:
            dh = _matmul(dP, st["W_in"], dh_mode, f32, name=f"dh{i}")
            dx, dg_pre[i], got = _pre_bwd(dh, st["x"], g_pre[i:i + 1], dx, name=f"pre_bwd{i}", comm=swap)
        else:
            dh, got = _matmul(dP, st["W_in"], dh_mode, f32, name=f"dh{i}", comm=swap)
            dx, dg_pre[i] = _pre_bwd(dh, st["x"], g_pre[i:i + 1], dx, name=f"pre_bwd{i}")
        sums = [_pair_sum(b, g, core, name=f"pair_sum_{t}{i}") for t, b, g in zip(("in", "out"), blocks, got)]
        pending = _ChipScatter(sums)
    recv[0] = _run_comm(pending, name="scatter_dw0")

    small = jnp.concatenate(
        [jnp.concatenate(dg_pre, axis=0).reshape(-1), jnp.concatenate(dg_post, axis=0).reshape(-1),
         jnp.concatenate(dsinks), db_f, loss_part[0, :1]])
    n_small = small.shape[0]
    rows = -(-n_small // 128)
    rows = -(-rows // 8) * 8
    small = jnp.pad(small, (0, rows * 128 - n_small)).reshape(rows, 128)
    total = _all_reduce_small(small, name="reduce_small").reshape(-1)
    o0 = DEPTH * D
    grad_g_pre = total[:o0].reshape(DEPTH, D)
    grad_g_post = total[o0:2 * o0].reshape(DEPTH, D)
    grad_sinks = total[2 * o0:2 * o0 + 2 * H].reshape(2, H)
    grad_b_f = total[2 * o0 + 2 * H:2 * o0 + 3 * H].reshape(1, H)
    loss = total[2 * o0 + 3 * H]

    def small_adam(w, g, m, v, name):
        def body(w_ref, g_ref, m_ref, v_ref, d_ref, nm_ref, nv_ref):
            d_ref[...], nm_ref[...], nv_ref[...] = _adam_math(w_ref[...], g_ref[...], m_ref[...], v_ref[...])
        vm = pl.BlockSpec(memory_space=pltpu.VMEM)
        return pl.pallas_call(body, name=name, in_specs=[vm] * 4, out_specs=[vm] * 3,
                              out_shape=[jax.ShapeDtypeStruct(w.shape, f32)] * 3)(w, g, m, v)

    upd = {}
    upd["g_pre"] = (grad_g_pre,) + tuple(small_adam(g_pre, grad_g_pre, m_g_pre, v_g_pre, "adam_g_pre"))
    upd["g_post"] = (grad_g_post,) + tuple(small_adam(g_post, grad_g_post, m_g_post, v_g_post, "adam_g_post"))
    upd["sinks_a"] = (grad_sinks,) + tuple(small_adam(sinks_a, grad_sinks, m_sinks_a, v_sinks_a, "adam_sinks"))
    upd["b_f_c"] = (grad_b_f,) + tuple(small_adam(b_f_c, grad_b_f, m_b_f_c, v_b_f_c, "adam_b_f"))

    def big(layers, which, w, m, v, tag):
        return _adamw_sum([recv[i][which] for i in layers], w, m, v, name=f"adam_{tag}")

    upd["w_in_c"] = big((2,), 0, w_in_c, m_w_in_c, v_w_in_c, "in_c")
    upd["w_in_b"] = big((1,), 0, w_in_b, m_w_in_b, v_w_in_b, "in_b")
    upd["w_out_b"] = big((1,), 1, w_out_b, m_w_out_b, v_w_out_b, "out_b")
    upd["w_out_c"] = big((2,), 1, w_out_c, m_w_out_c, v_w_out_c, "out_c")
    if held_transposed(0):
        flip = lambda a: jnp.swapaxes(a, 1, 2)
        upd["w_in_a"] = [flip(t) for t in big((0, 3), 0, flip(w_in_a), flip(m_w_in_a), flip(v_w_in_a), "in_a")]
    else:
        upd["w_in_a"] = big((0, 3), 0, w_in_a, m_w_in_a, v_w_in_a, "in_a")
    upd["w_out_a"] = big((0, 3), 1, w_out_a, m_w_out_a, v_w_out_a, "out_a")

    names = ["g_pre", "g_post", "w_in_a", "w_out_a", "sinks_a", "w_in_b", "w_out_b", "w_in_c", "b_f_c", "w_out_c"]
    return (loss, dx[None], *[upd[k][0] for k in names], *[upd[k][1] for k in names],
            *[upd[k][2] for k in names], *[upd[k][3] for k in names])
```

```python
import functools
import math

import numpy as np
import jax
import jax.numpy as jnp
from jax import lax
from jax.experimental import pallas as pl
from jax.experimental.pallas import tpu as pltpu

HEAD_DIM = 64
HP = 8
HW = HP * HEAD_DIM
BLOCK = 128
NORM_EPS = 1e-6
NEG = -1e30
N_DEV = 8
DEPTH = 4
ADAM_LR, ADAM_B1, ADAM_B2, ADAM_EPS, ADAM_WD, ADAM_STEP = 0.001, 0.9, 0.999, 1e-8, 0.01, 10
VMEM_LIMIT = 56 * 1024 * 1024

bf16 = jnp.bfloat16
f32 = jnp.float32
MESH = pl.DeviceIdType.MESH
ANY = pl.BlockSpec(memory_space=pl.ANY)
SMEM = pl.BlockSpec(memory_space=pltpu.SMEM)

NN = (((1,), (0,)), ((), ()))
NT = (((1,), (1,)), ((), ()))
TN = (((0,), (0,)), ((), ()))


def _dot(a, b, dims=NN):
    return lax.dot_general(a, b, dims, preferred_element_type=f32)


def _params(sem):
    return pltpu.CompilerParams(dimension_semantics=sem, vmem_limit_bytes=VMEM_LIMIT)


def _attn_tile(S):
    return 512 if S % 512 == 0 and S >= 1024 else 128


def _pick(n, pref):
    for t in pref:
        if n % t == 0:
            return t
    return n


def _matmul(a, b, mode, out_dtype, name, comm=None):
    if mode == "nn":
        (M, K), (K2, N) = a.shape, b.shape
    elif mode == "nt":
        (M, K), (N, K2) = a.shape, b.shape
    else:
        (K, M), (K2, N) = a.shape, b.shape
    assert K == K2, (a.shape, b.shape, mode)
    tm = _pick(M, (1024, 512, 256, 128))
    tn = _pick(N, (1024, 768, 640, 512, 256, 128))
    tk = _pick(K, (2048, 1664, 1536, 1024, 512, 640, 256, 128))
    nk = K // tk
    dims = {"nn": NN, "nt": NT, "tn": TN}[mode]

    def body(a_ref, b_ref, o_ref, acc_ref):
        if nk == 1:
            o_ref[...] = _dot(a_ref[...], b_ref[...], dims).astype(o_ref.dtype)
            return
        k = pl.program_id(2)

        @pl.when(k == 0)
        def _():
            acc_ref[...] = jnp.zeros_like(acc_ref)

        acc_ref[...] += _dot(a_ref[...], b_ref[...], dims)

        @pl.when(k == nk - 1)
        def _():
            o_ref[...] = acc_ref[...].astype(o_ref.dtype)

    if mode == "tn":
        a_spec = pl.BlockSpec((tk, tm), lambda i, j, k: (k, i))
    else:
        a_spec = pl.BlockSpec((tm, tk), lambda i, j, k: (i, k))
    if mode == "nt":
        b_spec = pl.BlockSpec((tn, tk), lambda i, j, k: (j, k))
    else:
        b_spec = pl.BlockSpec((tk, tn), lambda i, j, k: (k, j))
    res = _pallas(
        body, (a, b), name=name,
        grid=(M // tm, N // tn, nk),
        in_specs=[a_spec, b_spec],
        out_specs=[pl.BlockSpec((tm, tn), lambda i, j, k: (i, j))],
        out_shape=[jax.ShapeDtypeStruct((M, N), out_dtype)],
        scratch_shapes=[pltpu.VMEM((tm, tn) if nk > 1 else (8, 128), f32)],
        semantics=("parallel", "parallel", "arbitrary"), comm=comm)
    return res[0] if comm is None else (res[0], res[1])


ROWS = 256


def _rows(S):
    return ROWS if S % ROWS == 0 else S


def _rmsnorm_fwd(x, g, name, comm=None):
    S, D = x.shape
    tr = _rows(S)

    def body(x_ref, g_ref, h_ref):
        xv = x_ref[...]
        r = lax.rsqrt(jnp.mean(xv * xv, axis=-1, keepdims=True) + NORM_EPS)
        h_ref[...] = (xv * r * g_ref[...]).astype(h_ref.dtype)

    res = _pallas(
        body, (x, g), name=name, grid=(S // tr,),
        in_specs=[pl.BlockSpec((tr, D), lambda i: (i, 0)), pl.BlockSpec((1, D), lambda i: (0, 0))],
        out_specs=[pl.BlockSpec((tr, D), lambda i: (i, 0))],
        out_shape=[jax.ShapeDtypeStruct((S, D), bf16)],
        semantics=("parallel",), comm=comm)
    return res[0] if comm is None else (res[0], res[1])


def _post_fwd(x, y, g, name):
    S, D = x.shape
    tr = _rows(S)

    def body(x_ref, y_ref, g_ref, o_ref):
        yv = y_ref[...]
        r = lax.rsqrt(jnp.mean(yv * yv, axis=-1, keepdims=True) + NORM_EPS)
        o_ref[...] = x_ref[...] + yv * r * g_ref[...]

    row = pl.BlockSpec((tr, D), lambda i: (i, 0))
    return pl.pallas_call(
        body, name=name, grid=(S // tr,),
        in_specs=[row, row, pl.BlockSpec((1, D), lambda i: (0, 0))],
        out_specs=row,
        out_shape=jax.ShapeDtypeStruct((S, D), f32),
        compiler_params=_params(("parallel",)),
    )(x, y, g)


def _loss_fwd_bwd(y, t, name):
    S, D = y.shape
    tr = _rows(S)

    def body(y_ref, t_ref, l_ref, d_ref):
        @pl.when(pl.program_id(0) == 0)
        def _():
            l_ref[...] = jnp.zeros_like(l_ref)

        e = y_ref[...] - t_ref[...]
        d_ref[...] = e * (1.0 / D)
        part = 0.5 * jnp.sum(jnp.sum(e * e, axis=-1, keepdims=True) * (1.0 / D), axis=0, keepdims=True)
        l_ref[...] += jnp.broadcast_to(part, l_ref.shape)

    row = pl.BlockSpec((tr, D), lambda i: (i, 0))
    return pl.pallas_call(
        body, name=name, grid=(S // tr,),
        in_specs=[row, row],
        out_specs=[pl.BlockSpec((8, 128), lambda i: (0, 0)), row],
        out_shape=[jax.ShapeDtypeStruct((8, 128), f32), jax.ShapeDtypeStruct((S, D), f32)],
        compiler_params=_params(("arbitrary",)),
    )(y, t)


def _post_bwd(dxn, y, g, name):
    S, D = y.shape
    tr = _rows(S)

    def body(d_ref, y_ref, g_ref, dy_ref, dg_ref):
        @pl.when(pl.program_id(0) == 0)
        def _():
            dg_ref[...] = jnp.zeros_like(dg_ref)

        yv = y_ref[...]
        d = d_ref[...]
        r = lax.rsqrt(jnp.mean(yv * yv, axis=-1, keepdims=True) + NORM_EPS)
        n = yv * r
        dn = d * g_ref[...]
        dg_ref[...] += jnp.sum(d * n, axis=0, keepdims=True)
        dy_ref[...] = (r * (dn - n * jnp.mean(dn * n, axis=-1, keepdims=True))).astype(dy_ref.dtype)

    row = pl.BlockSpec((tr, D), lambda i: (i, 0))
    vec = pl.BlockSpec((1, D), lambda i: (0, 0))
    return pl.pallas_call(
        body, name=name, grid=(S // tr,),
        in_specs=[row, row, vec],
        out_specs=[row, vec],
        out_shape=[jax.ShapeDtypeStruct((S, D), bf16), jax.ShapeDtypeStruct((1, D), f32)],
        compiler_params=_params(("arbitrary",)),
    )(dxn, y, g)


def _pre_bwd(dh, x, g, dres, name, comm=None):
    S, D = x.shape
    tr = _rows(S)

    def body(dh_ref, x_ref, g_ref, dres_ref, dx_ref, dg_ref):
        @pl.when(pl.program_id(0) == 0)
        def _():
            dg_ref[...] = jnp.zeros_like(dg_ref)

        xv = x_ref[...]
        d = dh_ref[...]
        r = lax.rsqrt(jnp.mean(xv * xv, axis=-1, keepdims=True) + NORM_EPS)
        n = xv * r
        dn = d * g_ref[...]
        dg_ref[...] += jnp.sum(d * n, axis=0, keepdims=True)
        dx_ref[...] = dres_ref[...] + r * (dn - n * jnp.mean(dn * n, axis=-1, keepdims=True))

    row = pl.BlockSpec((tr, D), lambda i: (i, 0))
    vec = pl.BlockSpec((1, D), lambda i: (0, 0))
    return _pallas(
        body, (dh, x, g, dres), name=name, grid=(S // tr,),
        in_specs=[row, row, vec, row],
        out_specs=[row, vec],
        out_shape=[jax.ShapeDtypeStruct((S, D), f32), jax.ShapeDtypeStruct((1, D), f32)],
        semantics=("arbitrary",), comm=comm)


def _sigmoid(x):
    return 1.0 / (1.0 + jnp.exp(-x))


def _gate_fwd(o, z):
    zf = z.astype(f32)
    return o * (zf * _sigmoid(zf))


def _gate_bwd(du, o, z):
    zf = z.astype(f32)
    sig = _sigmoid(zf)
    do = du * (zf * sig)
    dz = du * o * (sig * (1.0 + zf * (1.0 - sig)))
    return do, dz


def _iota2(shape, dim):
    return lax.broadcasted_iota(jnp.int32, shape, dim)


def _swa_specs(S, BR, KV, G):
    gw = G * HEAD_DIM
    qspec = pl.BlockSpec((BLOCK, gw), lambda h, n: (n, h))
    zoff = (BR + 2 * KV) // gw
    zspec = pl.BlockSpec((BLOCK, gw), lambda h, n: (n, zoff + h))
    cur = pl.BlockSpec((1, BLOCK, HEAD_DIM), lambda h, n: (h, n, 0))
    prev = pl.BlockSpec((1, BLOCK, HEAD_DIM), lambda h, n: (h, jnp.maximum(n - 1, 0), 0))
    return qspec, zspec, cur, prev


def _split3(x):
    x1 = x.astype(bf16)
    r1 = x - x1.astype(f32)
    x2 = r1.astype(bf16)
    x3 = (r1 - x2.astype(f32)).astype(bf16)
    return x1, x2, x3


def _split2(x):
    x1 = x.astype(bf16)
    return x1, (x - x1.astype(f32)).astype(bf16)


def _pair_specs(S, BR, T, kmap):
    nb = BR // HW
    q = pl.BlockSpec((T, HW), lambda p, i, j: (i, p))
    k = pl.BlockSpec((T, HW), lambda p, i, j: (kmap(i, j), nb + p))
    v = pl.BlockSpec((T, HW), lambda p, i, j: (kmap(i, j), 2 * nb + p))
    z = pl.BlockSpec((T, HW), lambda p, i, j: (i, 3 * nb + p))
    return q, k, v, z


def _fox_cum(flT, b, name):
    H, S = flT.shape
    tj = _pick(S, (256, 128))

    def body(fl_ref, b_ref, c_ref):
        j = pl.program_id(0)
        x = fl_ref[...] + b_ref[...]
        logf = jnp.minimum(x, 0.0) - jnp.log(1.0 + jnp.exp(-jnp.abs(x)))
        tri = (_iota2((S, tj), 0) <= j * tj + _iota2((S, tj), 1)).astype(bf16)
        c_ref[...] = sum(_dot(part, tri) for part in _split3(logf))

    return pl.pallas_call(
        body, name=name, grid=(S // tj,),
        in_specs=[pl.BlockSpec((H, S), lambda j: (0, 0)), pl.BlockSpec((H, 1), lambda j: (0, 0))],
        out_specs=pl.BlockSpec((H, tj), lambda j: (0, j)),
        out_shape=jax.ShapeDtypeStruct((H, S), f32),
        compiler_params=_params(("parallel",)),
    )(flT, b)


def _fox_cum_bwd(dcq, dck, flT, b, name):
    H, S = flT.shape
    tj = _pick(S, (256, 128))

    def body(dcq_ref, dck_ref, fl_ref, b_ref, o_ref, db_ref):
        j = pl.program_id(0)

        @pl.when(j == 0)
        def _():
            db_ref[...] = jnp.zeros_like(db_ref)

        tri = (_iota2((S, tj), 0) >= j * tj + _iota2((S, tj), 1)).astype(bf16)
        dlogf = sum(_dot(part, tri) for part in _split3(dcq_ref[...] - dck_ref[...]))
        x = fl_ref[...] + b_ref[...]
        dfl = dlogf * _sigmoid(-x)
        o_ref[...] = dfl
        db_ref[...] += jnp.sum(dfl, axis=-1, keepdims=True)

    blk = pl.BlockSpec((H, tj), lambda j: (0, j))
    whole = pl.BlockSpec((H, S), lambda j: (0, 0))
    col = pl.BlockSpec((H, 1), lambda j: (0, 0))
    return pl.pallas_call(
        body, name=name, grid=(S // tj,),
        in_specs=[whole, whole, blk, col],
        out_specs=[blk, col],
        out_shape=[jax.ShapeDtypeStruct((H, S), f32), jax.ShapeDtypeStruct((H, 1), f32)],
        compiler_params=_params(("arbitrary",)),
    )(dcq, dck, flT, b)


def _hsl(hh):
    return slice(hh * HEAD_DIM, (hh + 1) * HEAD_DIM)


def _scaled(q, scale):
    assert math.log2(scale).is_integer()
    return (q.astype(f32) * scale).astype(bf16)


def _swa_bias(H, n_kv):
    G = H // n_kv
    slopes = 2.0 ** (-8.0 * np.arange(1, H + 1, dtype=np.float32) / H)
    dist = (np.arange(BLOCK)[None, :] + BLOCK - np.arange(2 * BLOCK)[:, None]).astype(np.float32)
    valid = (dist >= 0) & (dist < BLOCK)
    out = np.empty((n_kv, 2 * BLOCK, G * BLOCK), np.float32)
    for h in range(H):
        out[h // G][:, (h % G) * BLOCK:(h % G + 1) * BLOCK] = np.where(valid, -(slopes[h] * dist), np.float32(NEG))
    return jnp.asarray(out)


def _swaT_parts(q_ref, kc_ref, kp_ref, vc_ref, vp_ref, bias_ref, sink_ref, kvh, n, G, scale):
    qg = jnp.concatenate([q_ref[:, _hsl(g)] for g in range(G)], axis=0)
    kband = jnp.concatenate([kp_ref[0], kc_ref[0]], axis=0)
    vband = jnp.concatenate([vp_ref[0], vc_ref[0]], axis=0)
    s = _dot(kband, qg, NT) * scale + bias_ref[0]
    s = jnp.where(_iota2(s.shape, 0) + jnp.where(n > 0, BLOCK, 0) >= BLOCK, s, NEG)
    sink = jnp.concatenate([jnp.full((1, BLOCK), sink_ref[kvh * G + g], f32) for g in range(G)], axis=1)
    m = jnp.maximum(jnp.max(s, axis=0, keepdims=True), sink)
    p = jnp.exp(s - m)
    ps = jnp.exp(sink - m)
    den = jnp.sum(p, axis=0, keepdims=True) + ps
    return qg, kband, vband, p, ps, den


def _swaT_specs(S, BR, KV, G):
    qspec, zspec, cur, prev = _swa_specs(S, BR, KV, G)
    bias = pl.BlockSpec((1, 2 * BLOCK, G * BLOCK), lambda h, n: (h, 0, 0))
    return qspec, zspec, cur, prev, bias


def _swaT_fwd(P, kh, vh, sinks, bias, name, comm=None):
    S = P.shape[0]
    n_kv = kh.shape[0]
    H = sinks.shape[0]
    G = H // n_kv
    BR, KV = H * HEAD_DIM, n_kv * HEAD_DIM
    scale = HEAD_DIM ** -0.5
    assert (BR + 2 * KV) % (G * HEAD_DIM) == 0

    def body(q_ref, z_ref, kc_ref, kp_ref, vc_ref, vp_ref, bias_ref, sink_ref, o_ref, u_ref):
        kvh, n = pl.program_id(0), pl.program_id(1)
        _, _, vband, p, _, den = _swaT_parts(q_ref, kc_ref, kp_ref, vc_ref, vp_ref, bias_ref, sink_ref, kvh, n, G, scale)
        o = jnp.transpose(_dot(vband, p.astype(bf16), TN) / den)
        for g in range(G):
            oh = o[g * BLOCK:(g + 1) * BLOCK]
            o_ref[:, _hsl(g)] = oh.astype(o_ref.dtype)
            u_ref[:, _hsl(g)] = _gate_fwd(oh, z_ref[:, _hsl(g)]).astype(u_ref.dtype)

    qspec, zspec, cur, prev, bspec = _swaT_specs(S, BR, KV, G)
    ospec = pl.BlockSpec((BLOCK, G * HEAD_DIM), lambda h, n: (n, h))
    return _pallas(
        body, (P, P, kh, kh, vh, vh, bias, sinks), name=name, grid=(n_kv, S // BLOCK),
        in_specs=[qspec, zspec, cur, prev, cur, prev, bspec, SMEM],
        out_specs=[ospec, ospec],
        out_shape=[jax.ShapeDtypeStruct((S, BR), bf16)] * 2,
        semantics=("parallel", "parallel"), comm=comm)


def _swaT_bwd(P, kh, vh, o, du, sinks, bias, name, comm=None):
    S = P.shape[0]
    n_kv = kh.shape[0]
    H = sinks.shape[0]
    G = H // n_kv
    BR, KV = H * HEAD_DIM, n_kv * HEAD_DIM
    scale = HEAD_DIM ** -0.5

    def body(q_ref, z_ref, kc_ref, kp_ref, vc_ref, vp_ref, o_ref, du_ref, bias_ref, sink_ref,
             dq_ref, dz_ref, dk_ref, dv_ref, ds_ref):
        kvh, n = pl.program_id(0), pl.program_id(1)

        @pl.when(n == 0)
        def _():
            dk_ref[...] = jnp.zeros_like(dk_ref)
            dv_ref[...] = jnp.zeros_like(dv_ref)
            ds_ref[...] = jnp.zeros_like(ds_ref)

        dos, prods = [], []
        for g in range(G):
            oh = o_ref[:, _hsl(g)].astype(f32)
            doh, dzh = _gate_bwd(du_ref[:, _hsl(g)].astype(f32), oh, z_ref[:, _hsl(g)])
            dz_ref[:, _hsl(g)] = dzh.astype(dz_ref.dtype)
            dos.append(doh.astype(bf16))
            prods.append(doh * oh)
        dog = jnp.concatenate(dos, axis=0)
        delta = jnp.sum(jnp.transpose(jnp.concatenate(prods, axis=0)), axis=0, keepdims=True)
        qg, kband, vband, p, ps, den = _swaT_parts(q_ref, kc_ref, kp_ref, vc_ref, vp_ref, bias_ref, sink_ref,
                                                   kvh, n, G, scale)
        inv = 1.0 / den
        pn = p * inv
        ds = (pn * (_dot(vband, dog, NT) - delta)).astype(bf16)
        dq = jnp.transpose(_dot(kband, ds, TN)) * scale
        for g in range(G):
            dq_ref[:, _hsl(g)] = dq[g * BLOCK:(g + 1) * BLOCK].astype(dq_ref.dtype)
        dkb = _dot(ds, qg) * scale
        dvb = _dot(pn.astype(bf16), dog)
        ds_ref[0] -= ps * inv * delta
        row_c = pl.multiple_of(n * BLOCK, BLOCK)
        dk_ref[0, pl.ds(row_c, BLOCK), :] += dkb[BLOCK:]
        dv_ref[0, pl.ds(row_c, BLOCK), :] += dvb[BLOCK:]

        @pl.when(n > 0)
        def _():
            row_p = pl.multiple_of((n - 1) * BLOCK, BLOCK)
            dk_ref[0, pl.ds(row_p, BLOCK), :] += dkb[:BLOCK]
            dv_ref[0, pl.ds(row_p, BLOCK), :] += dvb[:BLOCK]

    qspec, zspec, cur, prev, bspec = _swaT_specs(S, BR, KV, G)
    ospec = pl.BlockSpec((BLOCK, G * HEAD_DIM), lambda h, n: (n, h))
    full = pl.BlockSpec((1, S, HEAD_DIM), lambda h, n: (h, 0, 0))
    return _pallas(
        body, (P, P, kh, kh, vh, vh, o, du, bias, sinks), name=name, grid=(n_kv, S // BLOCK),
        in_specs=[qspec, zspec, cur, prev, cur, prev, ospec, ospec, bspec, SMEM],
        out_specs=[ospec, ospec, full, full, pl.BlockSpec((1, 1, G * BLOCK), lambda h, n: (h, 0, 0))],
        out_shape=[jax.ShapeDtypeStruct((S, BR), bf16)] * 2
        + [jax.ShapeDtypeStruct((n_kv, S, HEAD_DIM), f32)] * 2
        + [jax.ShapeDtypeStruct((n_kv, 1, G * BLOCK), f32)],
        semantics=("parallel", "arbitrary"), comm=comm)


def _foxT_fwd(P, cq, ck, BR, name, comm=None):
    S = P.shape[0]
    H = BR // HEAD_DIM
    T = _attn_tile(S)
    nq = S // T
    scale = HEAD_DIM ** -0.5
    kmap = lambda i, j: jnp.minimum(i, j)

    def body(q_ref, k_ref, v_ref, z_ref, cq_ref, ck_ref, o_ref, u_ref, lse_ref, m_s, l_s, acc_s):
        qb, kb = pl.program_id(1), pl.program_id(2)

        @pl.when(kb == 0)
        def _():
            m_s[...] = jnp.full_like(m_s, NEG)
            l_s[...] = jnp.zeros_like(l_s)
            acc_s[...] = jnp.zeros_like(acc_s)

        def tile(diagonal):
            for hh in range(HP):
                sl = _hsl(hh)
                s = _dot(k_ref[:, sl], _scaled(q_ref[:, sl], scale), NT) + cq_ref[hh] - ck_ref[hh]
                if diagonal:
                    s = jnp.where(_iota2((T, T), 0) <= _iota2((T, T), 1), s, NEG)
                m_old = m_s[hh]
                m_new = jnp.maximum(m_old, jnp.max(s, axis=0, keepdims=True))
                alpha = jnp.exp(m_old - m_new)
                p = jnp.exp(s - m_new)
                l_s[hh] = alpha * l_s[hh] + jnp.sum(p, axis=0, keepdims=True)
                acc_s[hh] = alpha * acc_s[hh] + _dot(v_ref[:, sl], p.astype(bf16), TN)
                m_s[hh] = m_new

        @pl.when(kb < qb)
        def _():
            tile(False)

        @pl.when(kb == qb)
        def _():
            tile(True)
            for hh in range(HP):
                sl = _hsl(hh)
                oh = jnp.transpose(acc_s[hh] / l_s[hh])
                o_ref[:, sl] = oh.astype(o_ref.dtype)
                u_ref[:, sl] = _gate_fwd(oh, z_ref[:, sl]).astype(u_ref.dtype)
                lse_ref[hh] = m_s[hh] + jnp.log(l_s[hh])

    q, k, v, z = _pair_specs(S, BR, T, kmap)
    rowq = pl.BlockSpec((HP, 1,T), lambda p, i, j: (p, 0, i))
    colk = pl.BlockSpec((HP, T,1), lambda p, i, j: (p, kmap(i, j), 0))
    ospec = pl.BlockSpec((T, HW), lambda p, i, j: (i, p))
    return _pallas(
        body, (P, P, P, P, cq, ck), name=name, grid=(H // HP, nq, nq),
        in_specs=[q, k, v, z, rowq, colk],
        out_specs=[ospec, ospec, rowq],
        out_shape=[jax.ShapeDtypeStruct((S, BR), bf16)] * 2 + [jax.ShapeDtypeStruct((H, 1, S), f32)],
        scratch_shapes=[pltpu.VMEM((HP, 1,T), f32), pltpu.VMEM((HP, 1,T), f32), pltpu.VMEM((HP, HEAD_DIM,T), f32)],
        semantics=("parallel", "arbitrary", "arbitrary"), comm=comm)


def _foxT_bwd(P, o, du, lse, cq, ck, BR, name, comm=None):
    S = P.shape[0]
    H = BR // HEAD_DIM
    T = _attn_tile(S)
    nq = S // T
    scale = HEAD_DIM ** -0.5
    kmap = lambda i, j: jnp.minimum(i, j)

    def body(q_ref, k_ref, v_ref, z_ref, o_ref, du_ref, lse_ref, cq_ref, ck_ref,
             dq_ref, dz_ref, dk_ref, dv_ref, dcq_ref, dck_ref, do_s, delta_s, dq_s, dcq_s):
        qb, kb = pl.program_id(1), pl.program_id(2)

        @pl.when(jnp.logical_and(qb == 0, kb == 0))
        def _():
            dk_ref[...] = jnp.zeros_like(dk_ref)
            dv_ref[...] = jnp.zeros_like(dv_ref)
            dck_ref[...] = jnp.zeros_like(dck_ref)

        @pl.when(kb == 0)
        def _():
            for hh in range(HP):
                sl = _hsl(hh)
                oh = o_ref[:, sl].astype(f32)
                doh, dzh = _gate_bwd(du_ref[:, sl].astype(f32), oh, z_ref[:, sl])
                dz_ref[:, sl] = dzh.astype(dz_ref.dtype)
                do_s[hh] = doh.astype(bf16)
                delta_s[hh] = jnp.sum(jnp.transpose(doh * oh), axis=0, keepdims=True)
            dq_s[...] = jnp.zeros_like(dq_s)
            dcq_s[...] = jnp.zeros_like(dcq_s)

        def tile(diagonal):
            rows = pl.ds(pl.multiple_of(kb * T, T), T)
            for hh in range(HP):
                sl = _hsl(hh)
                qh, kh, vh, dob = _scaled(q_ref[:, sl], scale), k_ref[:, sl], v_ref[:, sl], do_s[hh]
                s = _dot(kh, qh, NT) + cq_ref[hh] - ck_ref[hh]
                p = jnp.exp(s - lse_ref[hh])
                if diagonal:
                    p = jnp.where(_iota2((T, T), 0) <= _iota2((T, T), 1), p, 0.0)
                ds = p * (_dot(vh, dob, NT) - delta_s[hh])
                dsb = ds.astype(bf16)
                dq_s[hh] += _dot(kh, dsb, TN)
                dk_ref[rows, sl] += _dot(dsb, qh)
                dv_ref[rows, sl] += _dot(p.astype(bf16), dob)
                dcq_s[hh] += jnp.sum(ds, axis=0, keepdims=True)
                dck_ref[hh, rows, :] += jnp.sum(ds, axis=1, keepdims=True)

        @pl.when(kb < qb)
        def _():
            tile(False)

        @pl.when(kb == qb)
        def _():
            tile(True)
            for hh in range(HP):
                dq_ref[:, _hsl(hh)] = (jnp.transpose(dq_s[hh]) * scale).astype(dq_ref.dtype)
                dcq_ref[hh] = dcq_s[hh]

    q, k, v, z = _pair_specs(S, BR, T, kmap)
    rowq = pl.BlockSpec((HP, 1,T), lambda p, i, j: (p, 0, i))
    colk = pl.BlockSpec((HP, T,1), lambda p, i, j: (p, kmap(i, j), 0))
    ospec = pl.BlockSpec((T, HW), lambda p, i, j: (i, p))
    full = pl.BlockSpec((S, HW), lambda p, i, j: (0, p))
    return _pallas(
        body, (P, P, P, P, o, du, lse, cq, ck), name=name, grid=(H // HP, nq, nq),
        in_specs=[q, k, v, z, ospec, ospec, rowq, rowq, colk],
        out_specs=[ospec, ospec, full, full, rowq, pl.BlockSpec((HP, S,1), lambda p, i, j: (p, 0, 0))],
        out_shape=[jax.ShapeDtypeStruct((S, BR), bf16)] * 2 + [jax.ShapeDtypeStruct((S, BR), f32)] * 2
        + [jax.ShapeDtypeStruct((H, 1, S), f32), jax.ShapeDtypeStruct((H, S, 1), f32)],
        scratch_shapes=[pltpu.VMEM((HP, T,HEAD_DIM), bf16), pltpu.VMEM((HP, 1,T), f32),
                        pltpu.VMEM((HP, HEAD_DIM,T), f32), pltpu.VMEM((HP, 1,T), f32)],
        semantics=("parallel", "arbitrary", "arbitrary"), comm=comm)


def _sbT_logs(kc, qh, before):
    l = _dot(kc, qh, NT)
    minus_abs = lax.bitcast_convert_type(lax.bitcast_convert_type(l, jnp.int32) | jnp.int32(-2 ** 31), f32)
    lsig = jnp.minimum(l, 0.0) - jnp.log(1.0 + jnp.exp(minus_abs))
    lf = lsig - l
    if before is not None:
        lf = jnp.where(before, lf, 0.0)
    return lsig, lf


def _chunk_rows(c):
    return slice(c * BLOCK, (c + 1) * BLOCK)


def _sbT_suffix_tile(lf, tri, nc, two_pass):
    if not two_pass:
        hi = lf.astype(bf16)
        return jnp.concatenate([_dot(tri, hi[_chunk_rows(c)]) for c in range(nc)], axis=0)
    hi, lo = _split2(lf)
    tri2 = jnp.concatenate([tri, tri], axis=1)
    return jnp.concatenate(
        [_dot(tri2, jnp.concatenate([hi[_chunk_rows(c)], lo[_chunk_rows(c)]], axis=0)) for c in range(nc)], axis=0)


def _sbT_fwd(P, BR, name, comm=None):
    S = P.shape[0]
    H = BR // HEAD_DIM
    T = _attn_tile(S)
    nq = S // T
    nc = T // BLOCK
    scale = HEAD_DIM ** -0.5
    kmap = lambda i, j: jnp.maximum(i - j, 0)

    def body(q_ref, k_ref, v_ref, z_ref, o_ref, u_ref, rc_ref, r_s, acc_s):
        qb, j = pl.program_id(1), pl.program_id(2)

        @pl.when(j == 0)
        def _():
            r_s[...] = jnp.zeros_like(r_s)
            acc_s[...] = jnp.zeros_like(acc_s)

        def tile(diagonal):
            ii, jj = _iota2((BLOCK, BLOCK), 0), _iota2((BLOCK, BLOCK), 1)
            tri = (jj > ii).astype(bf16)
            before = (_iota2((T, T), 0) < _iota2((T, T), 1)) if diagonal else None
            for hh in range(HP):
                sl = _hsl(hh)
                lsig, lf = _sbT_logs(k_ref[:, sl], _scaled(q_ref[:, sl], scale), before)
                x = lsig + _sbT_suffix_tile(lf, tri, nc, two_pass=True)
                r = r_s[hh]
                rc_ref[hh, 0] = r
                parts = [None] * nc
                for c in reversed(range(nc)):
                    parts[c] = jnp.exp(x[_chunk_rows(c)] + r)
                    r = r + jnp.sum(lf[_chunk_rows(c)], axis=0, keepdims=True)
                r_s[hh] = r
                a = jnp.concatenate(parts, axis=0)
                if diagonal:
                    a = jnp.where(before, a, 0.0)
                acc_s[hh] += _dot(v_ref[:, sl], a.astype(bf16), TN)

        @pl.when(j > 0)
        def _():
            @pl.when(j <= qb)
            def _():
                tile(False)

        @pl.when(j == 0)
        def _():
            tile(True)

        @pl.when(j == qb)
        def _():
            for hh in range(HP):
                sl = _hsl(hh)
                oh = jnp.transpose(acc_s[hh])
                o_ref[:, sl] = oh.astype(o_ref.dtype)
                u_ref[:, sl] = _gate_fwd(oh, z_ref[:, sl]).astype(u_ref.dtype)

    q, k, v, z = _pair_specs(S, BR, T, kmap)
    ospec = pl.BlockSpec((T, HW), lambda p, i, j: (i, p))
    rc = pl.BlockSpec((HP, 1,1, T), lambda p, i, j: (p, kmap(i, j), 0, i))
    return _pallas(
        body, (P, P, P, P), name=name, grid=(H // HP, nq, nq),
        in_specs=[q, k, v, z],
        out_specs=[ospec, ospec, rc],
        out_shape=[jax.ShapeDtypeStruct((S, BR), bf16)] * 2 + [jax.ShapeDtypeStruct((H, nq, 1, S), f32)],
        scratch_shapes=[pltpu.VMEM((HP, 1,T), f32), pltpu.VMEM((HP, HEAD_DIM,T), f32)],
        semantics=("parallel", "arbitrary", "arbitrary"), comm=comm)


def _sbT_bwd(P, o, du, rc, BR, name, comm=None):
    S = P.shape[0]
    H = BR // HEAD_DIM
    T = _attn_tile(S)
    nq = S // T
    nc = T // BLOCK
    scale = HEAD_DIM ** -0.5
    kmap = lambda i, j: jnp.minimum(i, j)

    def body(q_ref, k_ref, v_ref, z_ref, o_ref, du_ref, rc_ref,
             dq_ref, dz_ref, dk_ref, dv_ref, do_s, dq_s, g_s):
        qb, kb = pl.program_id(1), pl.program_id(2)

        @pl.when(jnp.logical_and(qb == 0, kb == 0))
        def _():
            dk_ref[...] = jnp.zeros_like(dk_ref)
            dv_ref[...] = jnp.zeros_like(dv_ref)

        @pl.when(kb == 0)
        def _():
            for hh in range(HP):
                sl = _hsl(hh)
                doh, dzh = _gate_bwd(du_ref[:, sl].astype(f32), o_ref[:, sl].astype(f32), z_ref[:, sl])
                dz_ref[:, sl] = dzh.astype(dz_ref.dtype)
                do_s[hh] = doh.astype(bf16)
            dq_s[...] = jnp.zeros_like(dq_s)
            g_s[...] = jnp.zeros_like(g_s)

        def tile(diagonal):
            ii, jj = _iota2((BLOCK, BLOCK), 0), _iota2((BLOCK, BLOCK), 1)
            tri_suffix = (jj > ii).astype(bf16)
            tri_prefix = (jj < ii).astype(bf16)
            before = (_iota2((T, T), 0) < _iota2((T, T), 1)) if diagonal else None
            out_rows = pl.ds(pl.multiple_of(kb * T, T), T)
            for hh in range(HP):
                sl = _hsl(hh)
                qh, kh, dob = _scaled(q_ref[:, sl], scale), k_ref[:, sl], do_s[hh]
                lsig, lf = _sbT_logs(kh, qh, before)
                x = lsig + _sbT_suffix_tile(lf, tri_suffix, nc, two_pass=False)
                r = rc_ref[hh, 0]
                parts = [None] * nc
                for c in reversed(range(nc)):
                    parts[c] = jnp.exp(x[_chunk_rows(c)] + r)
                    r = r + jnp.sum(lf[_chunk_rows(c)], axis=0, keepdims=True)
                a = jnp.concatenate(parts, axis=0)
                if diagonal:
                    a = jnp.where(before, a, 0.0)
                g = a * _dot(v_ref[:, sl], dob, NT)
                gb = g.astype(bf16)
                gsum = g_s[hh]
                for c in range(nc):
                    parts[c] = _dot(tri_prefix, gb[_chunk_rows(c)]) + gsum
                    gsum = gsum + jnp.sum(g[_chunk_rows(c)], axis=0, keepdims=True)
                g_s[hh] = gsum
                dl = g - (g + jnp.concatenate(parts, axis=0)) * jnp.exp(lsig)
                if diagonal:
                    dl = jnp.where(before, dl, 0.0)
                dl = dl.astype(bf16)
                dq_s[hh] += _dot(kh, dl, TN)
                dk_ref[out_rows, sl] += _dot(dl, qh)
                dv_ref[out_rows, sl] += _dot(a.astype(bf16), dob)

        @pl.when(kb < qb)
        def _():
            tile(False)

        @pl.when(kb == qb)
        def _():
            tile(True)
            for hh in range(HP):
                dq_ref[:, _hsl(hh)] = (jnp.transpose(dq_s[hh]) * scale).astype(dq_ref.dtype)

    q, k, v, z = _pair_specs(S, BR, T, kmap)
    ospec = pl.BlockSpec((T, HW), lambda p, i, j: (i, p))
    full = pl.BlockSpec((S, HW), lambda p, i, j: (0, p))
    rcs = pl.BlockSpec((HP, 1,1, T), lambda p, i, j: (p, kmap(i, j), 0, i))
    return _pallas(
        body, (P, P, P, P, o, du, rc), name=name, grid=(H // HP, nq, nq),
        in_specs=[q, k, v, z, ospec, ospec, rcs],
        out_specs=[ospec, ospec, full, full],
        out_shape=[jax.ShapeDtypeStruct((S, BR), bf16)] * 2 + [jax.ShapeDtypeStruct((S, BR), f32)] * 2,
        scratch_shapes=[pltpu.VMEM((HP, T,HEAD_DIM), bf16), pltpu.VMEM((HP, HEAD_DIM,T), f32),
                        pltpu.VMEM((HP, 1,T), f32)],
        semantics=("parallel", "arbitrary", "arbitrary"), comm=comm)


def _slab(R, C):
    tr = _pick(R, (256, 192, 128))
    if tr != R or R % 8 == 0:
        return tr, C
    return R, _pick(C, (256, 128))


def _adam_math(w, g, m, v):
    m = ADAM_B1 * m + (1.0 - ADAM_B1) * g
    v = ADAM_B2 * v + (1.0 - ADAM_B2) * (g * g)
    m_hat = m / (1.0 - ADAM_B1 ** ADAM_STEP)
    v_hat = v / (1.0 - ADAM_B2 ** ADAM_STEP)
    delta = -ADAM_LR * (m_hat / (jnp.sqrt(v_hat) + ADAM_EPS) + ADAM_WD * w)
    return delta, m, v


def _adamw_sum(parts, w, m, v, name):
    L, R, C = w.shape
    assert len(parts) == L
    n_parts = parts[0].shape[0]
    tr, tc = _slab(R, C)

    def body(*refs):
        p_refs = refs[:L]
        w_ref, m_ref, v_ref, g_ref, d_ref, nm_ref, nv_ref = refs[L:]
        layer = pl.program_id(0)
        g = None
        for l in range(L):
            gl = p_refs[l][0].astype(f32)
            for i in range(1, n_parts):
                gl = gl + p_refs[l][i].astype(f32)
            g = gl if g is None else jnp.where(layer == l, gl, g)
        g_ref[0] = g
        d_ref[0], nm_ref[0], nv_ref[0] = _adam_math(w_ref[0], g, m_ref[0], v_ref[0])

    blk = pl.BlockSpec((1, tr, tc), lambda l, i, j: (l, i, j))
    return _pallas(
        body, (*parts, w, m, v), name=name, grid=(L, R // tr, C // tc),
        in_specs=[pl.BlockSpec((n_parts, tr, tc), lambda l, i, j: (0, i, j))] * L + [blk, blk, blk],
        out_specs=[blk] * 4,
        out_shape=[jax.ShapeDtypeStruct((L, R, C), f32)] * 4,
        semantics=("parallel", "parallel", "parallel"))


def _coords():
    return lax.axis_index("x"), lax.axis_index("y"), lax.axis_index("c")


class _Gather:
    def __init__(self, shards):
        self.args = list(shards)
        self.n = len(shards)
        self.out_shape = [jax.ShapeDtypeStruct((N_DEV,) + s.shape, s.dtype) for s in shards]
        self.scratch = [pltpu.SemaphoreType.DMA((self.n, 7)), pltpu.SemaphoreType.DMA((self.n, 7)),
                        pltpu.SemaphoreType.DMA((self.n,))]

    def _ctx(self, ins, outs, sems):
        send_sems, recv_sems, local_sems = sems
        x, y, c = _coords()
        me, sibling = (x, y, c), (x, y, 1 - c)
        chips = [(1 - x, y), (x, 1 - y), (1 - x, 1 - y)]

        def slot(out, dev):
            return out.at[4 * dev[0] + 2 * dev[1] + dev[2]]

        def copy(a, k, block, to, src=None):
            return pltpu.make_async_remote_copy(
                src_ref=slot(outs[a], block) if src is None else src, dst_ref=slot(outs[a], block),
                send_sem=send_sems.at[a, k], recv_sem=recv_sems.at[a, k], device_id=to, device_id_type=MESH)

        mine = [pltpu.make_async_copy(ins[a], slot(outs[a], me), local_sems.at[a]) for a in range(self.n)]
        first = []
        for a in range(self.n):
            first.append(copy(a, 0, me, sibling, src=ins[a]))
            first += [copy(a, 1 + j, me, (*chip, c), src=ins[a]) for j, chip in enumerate(chips)]
        passed = [copy(a, 4 + j, (*chip, c), sibling) for j, chip in enumerate(chips) for a in range(self.n)]
        return c, me, sibling, chips, copy, mine, first, passed

    def start(self, ins, outs, sems):
        *_, mine, first, _ = self._ctx(ins, outs, sems)
        for cp in mine + first:
            cp.start()

    def mid(self, ins, outs, sems):
        c, me, _, chips, copy, _, _, passed = self._ctx(ins, outs, sems)
        i = 0
        for j, chip in enumerate(chips):
            for a in range(self.n):
                copy(a, 1 + j, (*chip, c), me).wait_recv()
                passed[i].start()
                i += 1

    def finish(self, ins, outs, sems):
        c, me, sibling, chips, copy, mine, first, passed = self._ctx(ins, outs, sems)
        for a in range(self.n):
            copy(a, 0, sibling, me).wait_recv()
            for j, chip in enumerate(chips):
                copy(a, 4 + j, (*chip, 1 - c), me).wait_recv()
        for cp in first + passed:
            cp.wait_send()
        for cp in mine:
            cp.wait()


N_CHIPS = 4


class _SiblingSwap:
    mid = None

    def __init__(self, blocks):
        self.args = list(blocks)
        self.n = len(blocks)
        self.out_shape = [jax.ShapeDtypeStruct((N_CHIPS,) + b.shape[1:], b.dtype) for b in blocks]
        self.scratch = [pltpu.SemaphoreType.DMA((self.n, N_CHIPS)), pltpu.SemaphoreType.DMA((self.n, N_CHIPS))]

    def _copies(self, ins, outs, sems):
        send_sems, recv_sems = sems
        x, y, c = _coords()
        return [pltpu.make_async_remote_copy(
            src_ref=ins[a].at[2 * q + (1 - c)], dst_ref=outs[a].at[q],
            send_sem=send_sems.at[a, q], recv_sem=recv_sems.at[a, q],
            device_id=(x, y, 1 - c), device_id_type=MESH) for a in range(self.n) for q in range(N_CHIPS)]

    def start(self, ins, outs, sems):
        for cp in self._copies(ins, outs, sems):
            cp.start()

    def finish(self, ins, outs, sems):
        for cp in self._copies(ins, outs, sems):
            cp.wait()


class _ChipScatter:
    mid = None

    def __init__(self, blocks):
        self.args = list(blocks)
        self.n = len(blocks)
        self.out_shape = [jax.ShapeDtypeStruct(b.shape, b.dtype) for b in blocks]
        self.scratch = [pltpu.SemaphoreType.DMA((self.n, 3)), pltpu.SemaphoreType.DMA((self.n, 3)),
                        pltpu.SemaphoreType.DMA((self.n,))]

    def _ctx(self, ins, outs, sems):
        send_sems, recv_sems, local_sems = sems
        x, y, c = _coords()
        me = 2 * x + y
        mine = [pltpu.make_async_copy(ins[a].at[me], outs[a].at[me], local_sems.at[a]) for a in range(self.n)]

        def copy(a, k, landing_here):
            px, py = x ^ ((k >> 1) & 1), y ^ (k & 1)
            them = 2 * px + py
            return pltpu.make_async_remote_copy(
                src_ref=ins[a].at[them], dst_ref=outs[a].at[them if landing_here else me],
                send_sem=send_sems.at[a, k - 1], recv_sem=recv_sems.at[a, k - 1],
                device_id=(px, py, c), device_id_type=MESH)

        sent = [copy(a, k, False) for k in range(1, N_CHIPS) for a in range(self.n)]
        arrivals = [copy(a, k, True) for k in range(1, N_CHIPS) for a in range(self.n)]
        return mine, sent, arrivals

    def start(self, ins, outs, sems):
        mine, sent, _ = self._ctx(ins, outs, sems)
        for cp in mine + sent:
            cp.start()

    def finish(self, ins, outs, sems):
        mine, sent, arrivals = self._ctx(ins, outs, sems)
        for cp in arrivals:
            cp.wait_recv()
        for cp in sent:
            cp.wait_send()
        for cp in mine:
            cp.wait()


def _pair_sum(blocks, got, core, name):
    _, R, C = blocks.shape
    tr, tc = _slab(R, C)

    def body(core_ref, mine_ref, got_ref, o_ref):
        o_ref[...] = (mine_ref[...].astype(f32) + got_ref[...].astype(f32)).astype(o_ref.dtype)

    return pl.pallas_call(
        body, name=name,
        grid_spec=pltpu.PrefetchScalarGridSpec(
            num_scalar_prefetch=1, grid=(N_CHIPS, R // tr, C // tc),
            in_specs=[pl.BlockSpec((1, tr, tc), lambda q, i, j, core_ref: (2 * q + core_ref[0], i, j)),
                      pl.BlockSpec((1, tr, tc), lambda q, i, j, core_ref: (q, i, j))],
            out_specs=pl.BlockSpec((1, tr, tc), lambda q, i, j, core_ref: (q, i, j))),
        out_shape=jax.ShapeDtypeStruct((N_CHIPS, R, C), blocks.dtype),
        compiler_params=_params(("parallel", "parallel", "parallel")),
    )(core, blocks, got)


def _run_comm(comm, name):
    n = comm.n

    def body(*refs):
        ins, outs, sems = refs[:n], refs[n:2 * n], refs[2 * n:]
        comm.start(ins, outs, sems)
        if comm.mid is not None:
            comm.mid(ins, outs, sems)
        comm.finish(ins, outs, sems)

    return pl.pallas_call(
        body, name=name, in_specs=[ANY] * n, out_specs=[ANY] * n,
        out_shape=comm.out_shape, scratch_shapes=comm.scratch,
    )(*comm.args)


def _pallas(body, args, *, name, grid, in_specs, out_specs, out_shape, scratch_shapes=(), semantics, comm=None):
    in_specs, out_specs, out_shape, scratch_shapes = list(in_specs), list(out_specs), list(out_shape), list(scratch_shapes)
    if comm is None:
        return pl.pallas_call(
            body, name=name, grid=grid, in_specs=in_specs, out_specs=out_specs, out_shape=out_shape,
            scratch_shapes=scratch_shapes, compiler_params=_params(semantics))(*args)
    a = len(in_specs)
    b = a + comm.n
    c = b + len(out_specs)
    d = c + comm.n
    e = d + len(scratch_shapes)
    total = math.prod(grid)
    mid_step = (7 * total) // 8

    def hosted(*refs):
        step = pl.program_id(0)
        for axis in range(1, len(grid)):
            step = step * grid[axis] + pl.program_id(axis)
        ins, outs, sems = refs[a:b], refs[c:d], refs[e:]

        @pl.when(step == 0)
        def _():
            comm.start(ins, outs, sems)

        body(*refs[:a], *refs[b:c], *refs[d:e])

        if comm.mid is not None:
            @pl.when(step == mid_step)
            def _():
                comm.mid(ins, outs, sems)

        @pl.when(step == total - 1)
        def _():
            comm.finish(ins, outs, sems)

    res = pl.pallas_call(
        hosted, name=name, grid=grid, in_specs=in_specs + [ANY] * comm.n, out_specs=out_specs + [ANY] * comm.n,
        out_shape=out_shape + comm.out_shape, scratch_shapes=scratch_shapes + comm.scratch,
        compiler_params=_params(("arbitrary",) * len(grid)))(*args, *comm.args)
    return list(res[:len(out_specs)]) + [list(res[len(out_specs):])]


def _all_reduce_small(vec, name):
    R, C = vec.shape

    def body(v_ref, o_ref, gath, send_sems, recv_sems):
        x, y, c = _coords()
        me = 4 * x + 2 * y + c
        gath[me] = v_ref[...]

        def copy(k):
            px, py, pc = x ^ ((k >> 2) & 1), y ^ ((k >> 1) & 1), c ^ (k & 1)
            return pltpu.make_async_remote_copy(
                src_ref=v_ref, dst_ref=gath.at[me], send_sem=send_sems.at[k - 1], recv_sem=recv_sems.at[k - 1],
                device_id=(px, py, pc), device_id_type=MESH)

        sent = [copy(k) for k in range(1, N_DEV)]
        for cp in sent:
            cp.start()
        for cp in sent:
            cp.wait()
        total = gath[0]
        for i in range(1, N_DEV):
            total = total + gath[i]
        o_ref[...] = total

    return pl.pallas_call(
        body, name=name,
        in_specs=[pl.BlockSpec(memory_space=pltpu.VMEM)], out_specs=pl.BlockSpec(memory_space=pltpu.VMEM),
        out_shape=jax.ShapeDtypeStruct((R, C), f32),
        scratch_shapes=[pltpu.VMEM((N_DEV, R, C), f32), pltpu.SemaphoreType.DMA((7,)), pltpu.SemaphoreType.DMA((7,))],
    )(vec)


def _gathered_in(g):
    return jnp.transpose(g, (1, 0, 2)).reshape(g.shape[1], -1)


def _col_blocks(dw):
    D, N = dw.shape
    return jnp.transpose(dw.reshape(D, N_DEV, N // N_DEV), (1, 0, 2)).astype(bf16)


def _heads_major(a, n):
    return jnp.transpose(a.reshape(a.shape[0], n, HEAD_DIM), (1, 0, 2))


def _heads_minor(a):
    return jnp.transpose(a, (1, 0, 2)).reshape(a.shape[1], -1)


def kernel(x, g_pre, g_post, w_in_a, w_out_a, sinks_a, w_in_b, w_out_b, w_in_c, b_f_c, w_out_c, loss_target, m_g_pre, m_g_post, m_w_in_a, m_w_out_a, m_sinks_a, m_w_in_b, m_w_out_b, m_w_in_c, m_b_f_c, m_w_out_c, v_g_pre, v_g_post, v_w_in_a, v_w_out_a, v_sinks_a, v_w_in_b, v_w_out_b, v_w_in_c, v_b_f_c, v_w_out_c):
    S, D = x.shape[1], x.shape[2]
    H = D // HEAD_DIM
    BR = H * HEAD_DIM
    n_kv = H // 8
    KV = n_kv * HEAD_DIM
    x0 = x[0]
    target = loss_target[0]
    swa_bias = _swa_bias(H, n_kv)
    w_in = {0: w_in_a, 1: w_in_b, 2: w_in_c}
    w_out = {0: w_out_a, 1: w_out_b, 2: w_out_c}

    saved = []
    xi = x0
    riders = {"pre_norm0": [("in", 0)], "in_proj0": [("out", 0)], "mixer0": [("in", 1)],
              "mixer1": [("out", 1), ("in", 2)], "mixer2": [("out", 2), ("in", 3), ("out", 3)]}
    full = {}

    def held_transposed(kind):
        return w_in[kind].shape[2] % BLOCK != 0

    def shard_of(which, i):
        kind, j = i % 3, i // 3
        if which == "out":
            return w_out[kind][j].astype(bf16)
        if held_transposed(kind):
            return jnp.swapaxes(w_in[kind], 1, 2)[j].astype(bf16)
        return w_in[kind][j].astype(bf16)

    def rider(slot):
        keys = riders.get(slot)
        if not keys:
            return None
        return _Gather([shard_of(which, i) for which, i in keys])

    def landed(slot, arrays):
        full.update(zip(riders[slot], arrays))

    for i in range(DEPTH):
        kind, j = i % 3, i // 3
        nxt = rider(f"mixer{i}")
        with_norm = rider(f"pre_norm{i}")
        h = _rmsnorm_fwd(xi, g_pre[i:i + 1], name=f"pre_norm{i}", comm=with_norm)
        if with_norm is not None:
            h, arrived = h
            landed(f"pre_norm{i}", arrived)
        w_t = held_transposed(kind)
        W_in = full["in", i].reshape(-1, D) if w_t else _gathered_in(full["in", i])
        st = dict(x=xi, h=h)
        if kind == 2:
            n_main = 4 * BR
            if w_t:
                W_in = jnp.pad(W_in, ((0, (-W_in.shape[0]) % BLOCK), (0, 0)))
                fl = _matmul(h, W_in[n_main:n_main + BLOCK], "nt", f32, name=f"f_proj{i}")
            else:
                W_in = jnp.pad(W_in, ((0, 0), (0, (-W_in.shape[1]) % BLOCK)))
                fl = _matmul(h, W_in[:, n_main:n_main + BLOCK], "nn", f32, name=f"f_proj{i}")
            flT = jnp.transpose(fl[:, :H])
            bcol = b_f_c[j].reshape(H, 1)
            cumT = _fox_cum(flT, bcol, name=f"fox_cum{i}")
            cq, ck = cumT[:, None, :], cumT[:, :, None]
            st.update(flT=flT, bcol=bcol, cq=cq, ck=ck)
        st["W_in"] = W_in
        with_proj = rider(f"in_proj{i}")
        P = _matmul(h, W_in, "nt" if w_t else "nn", bf16, name=f"in_proj{i}", comm=with_proj)
        if with_proj is not None:
            P, arrived = P
            landed(f"in_proj{i}", arrived)
        st["P"] = P
        if kind == 0:
            kh = _heads_major(P[:, BR:BR + KV], n_kv)
            vh = _heads_major(P[:, BR + KV:BR + 2 * KV], n_kv)
            o, u, *rest = _swaT_fwd(P, kh, vh, sinks_a[j], swa_bias, name=f"swa_fwd{i}", comm=nxt)
            st.update(kh=kh, vh=vh)
        elif kind == 1:
            o, u, rc, *rest = _sbT_fwd(P, BR, name=f"sb_fwd{i}", comm=nxt)
            st.update(rc=rc)
        else:
            o, u, lse, *rest = _foxT_fwd(P, cq, ck, BR, name=f"fox_fwd{i}", comm=nxt)
            st.update(lse=lse)
        if nxt is not None:
            landed(f"mixer{i}", rest[0])
        W_out = full["out", i].reshape(BR, D)
        st.update(o=o, u=u, W_out=W_out)
        y = _matmul(u, W_out, "nn", f32, name=f"out_proj{i}")
        st["y"] = y
        xi = _post_fwd(xi, y, g_post[i:i + 1], name=f"post_norm{i}")
        saved.append(st)

    loss_part, dx = _loss_fwd_bwd(xi, target, name="loss")

    dg_pre, dg_post = [None] * DEPTH, [None] * DEPTH
    dsinks = [None, None]
    db_f = None
    recv = [None] * DEPTH
    core = lax.axis_index("c").astype(jnp.int32).reshape(1)
    pending = None
    for i in reversed(range(DEPTH)):
        kind, j = i % 3, i // 3
        st = saved[i]
        dy, dg_post[i] = _post_bwd(dx, st["y"], g_post[i:i + 1], name=f"post_bwd{i}")
        du = _matmul(dy, st["W_out"], "nt", bf16, name=f"du{i}")
        dW_out = _matmul(st["u"], dy, "tn", bf16, name=f"dw_out{i}")
        P = st["P"]
        if kind == 0:
            dq, dz, dkh, dvh, dsk, *rest = _swaT_bwd(P, st["kh"], st["vh"], st["o"], du, sinks_a[j], swa_bias,
                                                     name=f"swa_bwd{i}", comm=pending)
            dsinks[j] = jnp.sum(dsk.reshape(H, BLOCK), axis=1)
            dP = jnp.concatenate([dq, _heads_minor(dkh).astype(bf16), _heads_minor(dvh).astype(bf16), dz], axis=1)
        elif kind == 1:
            dq, dz, dk, dv, *rest = _sbT_bwd(P, st["o"], du, st["rc"], BR, name=f"sb_bwd{i}", comm=pending)
            dP = jnp.concatenate([dq, dk.astype(bf16), dv.astype(bf16), dz], axis=1)
        else:
            dq, dz, dk, dv, dcq, dck, *rest = _foxT_bwd(P, st["o"], du, st["lse"], st["cq"], st["ck"], BR,
                                                       name=f"fox_bwd{i}", comm=pending)
        if pending is not None:
            recv[i + 1] = rest[0]
        if kind == 2:
            dflT, db_col = _fox_cum_bwd(dcq.reshape(H, S), dck.reshape(H, S), st["flT"], st["bcol"],
                                        name=f"fox_cum_bwd{i}")
            db_f = db_col.reshape(H)
            dfl = jnp.pad(jnp.transpose(dflT), ((0, 0), (0, 128 - H))).astype(bf16)
            dP = jnp.concatenate([dq, dk.astype(bf16), dv.astype(bf16), dz, dfl], axis=1)
        w_t = held_transposed(kind)
        c_shard = w_in[kind].shape[2]
        if w_t:
            dW_in = _matmul(dP, st["h"], "tn", bf16, name=f"dw_in{i}")[:c_shard * N_DEV].reshape(N_DEV, c_shard, D)
        else:
            dW_in = _col_blocks(_matmul(st["h"], dP, "tn", bf16, name=f"dw_in{i}")[:, :c_shard * N_DEV])
        blocks = [dW_in, dW_out.reshape(N_DEV, BR // N_DEV, D)]
        swap = _SiblingSwap(blocks)
        dh_mode = "nn" if w_t else "nt"
        if i > 0:
            dh = _matmul(dP, st["W_in"], dh_mode, f32, name=f"dh{i}")
            dx, dg_pre[i], got = _pre_bwd(dh, st["x"], g_pre[i:i + 1], dx, name=f"pre_bwd{i}", comm=swap)
        else:
            dh, got = _matmul(dP, st["W_in"], dh_mode, f32, name=f"dh{i}", comm=swap)
            dx, dg_pre[i] = _pre_bwd(dh, st["x"], g_pre[i:i + 1], dx, name=f"pre_bwd{i}")
        sums = [_pair_sum(b, g, core, name=f"pair_sum_{t}{i}") for t, b, g in zip(("in", "out"), blocks, got)]
        pending = _ChipScatter(sums)
    recv[0] = _run_comm(pending, name="scatter_dw0")

    small = jnp.concatenate(
        [jnp.concatenate(dg_pre, axis=0).reshape(-1), jnp.concatenate(dg_post, axis=0).reshape(-1),
         jnp.concatenate(dsinks), db_f, loss_part[0, :1]])
    n_small = small.shape[0]
    rows = -(-n_small // 128)
    rows = -(-rows // 8) * 8
    small = jnp.pad(small, (0, rows * 128 - n_small)).reshape(rows, 128)
    total = _all_reduce_small(small, name="reduce_small").reshape(-1)
    o0 = DEPTH * D
    grad_g_pre = total[:o0].reshape(DEPTH, D)
    grad_g_post = total[o0:2 * o0].reshape(DEPTH, D)
    grad_sinks = total[2 * o0:2 * o0 + 2 * H].reshape(2, H)
    grad_b_f = total[2 * o0 + 2 * H:2 * o0 + 3 * H].reshape(1, H)
    loss = total[2 * o0 + 3 * H]

    def small_adam(w, g, m, v, name):
        def body(w_ref, g_ref, m_ref, v_ref, d_ref, nm_ref, nv_ref):
            d_ref[...], nm_ref[...], nv_ref[...] = _adam_math(w_ref[...], g_ref[...], m_ref[...], v_ref[...])
        vm = pl.BlockSpec(memory_space=pltpu.VMEM)
        return pl.pallas_call(body, name=name, in_specs=[vm] * 4, out_specs=[vm] * 3,
                              out_shape=[jax.ShapeDtypeStruct(w.shape, f32)] * 3)(w, g, m, v)

    upd = {}
    upd["g_pre"] = (grad_g_pre,) + tuple(small_adam(g_pre, grad_g_pre, m_g_pre, v_g_pre, "adam_g_pre"))
    upd["g_post"] = (grad_g_post,) + tuple(small_adam(g_post, grad_g_post, m_g_post, v_g_post, "adam_g_post"))
    upd["sinks_a"] = (grad_sinks,) + tuple(small_adam(sinks_a, grad_sinks, m_sinks_a, v_sinks_a, "adam_sinks"))
    upd["b_f_c"] = (grad_b_f,) + tuple(small_adam(b_f_c, grad_b_f, m_b_f_c, v_b_f_c, "adam_b_f"))

    def big(layers, which, w, m, v, tag):
        return _adamw_sum([recv[i][which] for i in layers], w, m, v, name=f"adam_{tag}")

    def big_in(kind, layers, w, m, v, tag):
        if not held_transposed(kind):
            return big(layers, 0, w, m, v, tag)
        flip = lambda a: jnp.swapaxes(a, 1, 2)
        return [flip(t) for t in big(layers, 0, flip(w), flip(m), flip(v), tag)]

    upd["w_in_c"] = big_in(2, (2,), w_in_c, m_w_in_c, v_w_in_c, "in_c")
    upd["w_in_b"] = big_in(1, (1,), w_in_b, m_w_in_b, v_w_in_b, "in_b")
    upd["w_out_b"] = big((1,), 1, w_out_b, m_w_out_b, v_w_out_b, "out_b")
    upd["w_out_c"] = big((2,), 1, w_out_c, m_w_out_c, v_w_out_c, "out_c")
    upd["w_in_a"] = big_in(0, (0, 3), w_in_a, m_w_in_a, v_w_in_a, "in_a")
    upd["w_out_a"] = big((0, 3), 1, w_out_a, m_w_out_a, v_w_out_a, "out_a")

    names = ["g_pre", "g_post", "w_in_a", "w_out_a", "sinks_a", "w_in_b", "w_out_b", "w_in_c", "b_f_c", "w_out_c"]
    return (loss, dx[None], *[upd[k][0] for k in names], *[upd[k][1] for k in names],
            *[upd[k][2] for k in names], *[upd[k][3] for k in names])
```

```python
import functools
import math

import numpy as np
import jax
import jax.numpy as jnp
from jax import lax
from jax.experimental import pallas as pl
from jax.experimental.pallas import tpu as pltpu

HEAD_DIM = 64
HP = 8
HW = HP * HEAD_DIM
BLOCK = 128
NORM_EPS = 1e-6
NEG = -1e30
N_DEV = 8
DEPTH = 4
ADAM_LR, ADAM_B1, ADAM_B2, ADAM_EPS, ADAM_WD, ADAM_STEP = 0.001, 0.9, 0.999, 1e-8, 0.01, 10
VMEM_LIMIT = 56 * 1024 * 1024

bf16 = jnp.bfloat16
f32 = jnp.float32
MESH = pl.DeviceIdType.MESH
ANY = pl.BlockSpec(memory_space=pl.ANY)
SMEM = pl.BlockSpec(memory_space=pltpu.SMEM)

NN = (((1,), (0,)), ((), ()))
NT = (((1,), (1,)), ((), ()))
TN = (((0,), (0,)), ((), ()))


def _dot(a, b, dims=NN):
    return lax.dot_general(a, b, dims, preferred_element_type=f32)


def _params(sem):
    return pltpu.CompilerParams(dimension_semantics=sem, vmem_limit_bytes=VMEM_LIMIT)


def _attn_tile(S):
    return 512 if S % 512 == 0 and S >= 1024 else 128


def _pick(n, pref):
    for t in pref:
        if n % t == 0:
            return t
    return n


def _matmul(a, b, mode, out_dtype, name, comm=None):
    if mode == "nn":
        (M, K), (K2, N) = a.shape, b.shape
    elif mode == "nt":
        (M, K), (N, K2) = a.shape, b.shape
    else:
        (K, M), (K2, N) = a.shape, b.shape
    assert K == K2, (a.shape, b.shape, mode)
    tm = _pick(M, (1024, 640, 512, 256, 128))
    tn = _pick(N, (1024, 768, 640, 512, 256, 128))
    tk = _pick(K, (2048, 1664, 1536, 1024, 512, 640, 256, 128))
    nk = K // tk
    dims = {"nn": NN, "nt": NT, "tn": TN}[mode]

    def body(a_ref, b_ref, o_ref, acc_ref):
        if nk == 1:
            o_ref[...] = _dot(a_ref[...], b_ref[...], dims).astype(o_ref.dtype)
            return
        k = pl.program_id(2)

        @pl.when(k == 0)
        def _():
            acc_ref[...] = jnp.zeros_like(acc_ref)

        acc_ref[...] += _dot(a_ref[...], b_ref[...], dims)

        @pl.when(k == nk - 1)
        def _():
            o_ref[...] = acc_ref[...].astype(o_ref.dtype)

    if mode == "tn":
        a_spec = pl.BlockSpec((tk, tm), lambda i, j, k: (k, i))
    else:
        a_spec = pl.BlockSpec((tm, tk), lambda i, j, k: (i, k))
    if mode == "nt":
        b_spec = pl.BlockSpec((tn, tk), lambda i, j, k: (j, k))
    else:
        b_spec = pl.BlockSpec((tk, tn), lambda i, j, k: (k, j))
    res = _pallas(
        body, (a, b), name=name,
        grid=(M // tm, N // tn, nk),
        in_specs=[a_spec, b_spec],
        out_specs=[pl.BlockSpec((tm, tn), lambda i, j, k: (i, j))],
        out_shape=[jax.ShapeDtypeStruct((M, N), out_dtype)],
        scratch_shapes=[pltpu.VMEM((tm, tn) if nk > 1 else (8, 128), f32)],
        semantics=("parallel", "parallel", "arbitrary"), comm=comm)
    return res[0] if comm is None else (res[0], res[1])


ROWS = 256


def _rows(S):
    return ROWS if S % ROWS == 0 else S


def _rmsnorm_fwd(x, g, name, comm=None):
    S, D = x.shape
    tr = _rows(S)

    def body(x_ref, g_ref, h_ref):
        xv = x_ref[...]
        r = lax.rsqrt(jnp.mean(xv * xv, axis=-1, keepdims=True) + NORM_EPS)
        h_ref[...] = (xv * r * g_ref[...]).astype(h_ref.dtype)

    res = _pallas(
        body, (x, g), name=name, grid=(S // tr,),
        in_specs=[pl.BlockSpec((tr, D), lambda i: (i, 0)), pl.BlockSpec((1, D), lambda i: (0, 0))],
        out_specs=[pl.BlockSpec((tr, D), lambda i: (i, 0))],
        out_shape=[jax.ShapeDtypeStruct((S, D), bf16)],
        semantics=("parallel",), comm=comm)
    return res[0] if comm is None else (res[0], res[1])


def _post_fwd(x, y, g, name):
    S, D = x.shape
    tr = _rows(S)

    def body(x_ref, y_ref, g_ref, o_ref):
        yv = y_ref[...]
        r = lax.rsqrt(jnp.mean(yv * yv, axis=-1, keepdims=True) + NORM_EPS)
        o_ref[...] = x_ref[...] + yv * r * g_ref[...]

    row = pl.BlockSpec((tr, D), lambda i: (i, 0))
    return pl.pallas_call(
        body, name=name, grid=(S // tr,),
        in_specs=[row, row, pl.BlockSpec((1, D), lambda i: (0, 0))],
        out_specs=row,
        out_shape=jax.ShapeDtypeStruct((S, D), f32),
        compiler_params=_params(("parallel",)),
    )(x, y, g)


def _loss_fwd_bwd(y, t, name):
    S, D = y.shape
    tr = _rows(S)

    def body(y_ref, t_ref, l_ref, d_ref):
        @pl.when(pl.program_id(0) == 0)
        def _():
            l_ref[...] = jnp.zeros_like(l_ref)

        e = y_ref[...] - t_ref[...]
        d_ref[...] = e * (1.0 / D)
        part = 0.5 * jnp.sum(jnp.sum(e * e, axis=-1, keepdims=True) * (1.0 / D), axis=0, keepdims=True)
        l_ref[...] += jnp.broadcast_to(part, l_ref.shape)

    row = pl.BlockSpec((tr, D), lambda i: (i, 0))
    return pl.pallas_call(
        body, name=name, grid=(S // tr,),
        in_specs=[row, row],
        out_specs=[pl.BlockSpec((8, 128), lambda i: (0, 0)), row],
        out_shape=[jax.ShapeDtypeStruct((8, 128), f32), jax.ShapeDtypeStruct((S, D), f32)],
        compiler_params=_params(("arbitrary",)),
    )(y, t)


def _post_bwd(dxn, y, g, name):
    S, D = y.shape
    tr = _rows(S)

    def body(d_ref, y_ref, g_ref, dy_ref, dg_ref):
        @pl.when(pl.program_id(0) == 0)
        def _():
            dg_ref[...] = jnp.zeros_like(dg_ref)

        yv = y_ref[...]
        d = d_ref[...]
        r = lax.rsqrt(jnp.mean(yv * yv, axis=-1, keepdims=True) + NORM_EPS)
        n = yv * r
        dn = d * g_ref[...]
        dg_ref[...] += jnp.sum(d * n, axis=0, keepdims=True)
        dy_ref[...] = (r * (dn - n * jnp.mean(dn * n, axis=-1, keepdims=True))).astype(dy_ref.dtype)

    row = pl.BlockSpec((tr, D), lambda i: (i, 0))
    vec = pl.BlockSpec((1, D), lambda i: (0, 0))
    return pl.pallas_call(
        body, name=name, grid=(S // tr,),
        in_specs=[row, row, vec],
        out_specs=[row, vec],
        out_shape=[jax.ShapeDtypeStruct((S, D), bf16), jax.ShapeDtypeStruct((1, D), f32)],
        compiler_params=_params(("arbitrary",)),
    )(dxn, y, g)


def _pre_bwd(dh, x, g, dres, name, comm=None):
    S, D = x.shape
    tr = _rows(S)

    def body(dh_ref, x_ref, g_ref, dres_ref, dx_ref, dg_ref):
        @pl.when(pl.program_id(0) == 0)
        def _():
            dg_ref[...] = jnp.zeros_like(dg_ref)

        xv = x_ref[...]
        d = dh_ref[...]
        r = lax.rsqrt(jnp.mean(xv * xv, axis=-1, keepdims=True) + NORM_EPS)
        n = xv * r
        dn = d * g_ref[...]
        dg_ref[...] += jnp.sum(d * n, axis=0, keepdims=True)
        dx_ref[...] = dres_ref[...] + r * (dn - n * jnp.mean(dn * n, axis=-1, keepdims=True))

    row = pl.BlockSpec((tr, D), lambda i: (i, 0))
    vec = pl.BlockSpec((1, D), lambda i: (0, 0))
    return _pallas(
        body, (dh, x, g, dres), name=name, grid=(S // tr,),
        in_specs=[row, row, vec, row],
        out_specs=[row, vec],
        out_shape=[jax.ShapeDtypeStruct((S, D), f32), jax.ShapeDtypeStruct((1, D), f32)],
        semantics=("arbitrary",), comm=comm)


def _sigmoid(x):
    return 1.0 / (1.0 + jnp.exp(-x))


def _gate_fwd(o, z):
    zf = z.astype(f32)
    return o * (zf * _sigmoid(zf))


def _gate_bwd(du, o, z):
    zf = z.astype(f32)
    sig = _sigmoid(zf)
    do = du * (zf * sig)
    dz = du * o * (sig * (1.0 + zf * (1.0 - sig)))
    return do, dz


def _iota2(shape, dim):
    return lax.broadcasted_iota(jnp.int32, shape, dim)


def _swa_specs(S, BR, KV, G):
    gw = G * HEAD_DIM
    qspec = pl.BlockSpec((BLOCK, gw), lambda h, n: (n, h))
    zoff = (BR + 2 * KV) // gw
    zspec = pl.BlockSpec((BLOCK, gw), lambda h, n: (n, zoff + h))
    cur = pl.BlockSpec((1, BLOCK, HEAD_DIM), lambda h, n: (h, n, 0))
    prev = pl.BlockSpec((1, BLOCK, HEAD_DIM), lambda h, n: (h, jnp.maximum(n - 1, 0), 0))
    return qspec, zspec, cur, prev


def _split3(x):
    x1 = x.astype(bf16)
    r1 = x - x1.astype(f32)
    x2 = r1.astype(bf16)
    x3 = (r1 - x2.astype(f32)).astype(bf16)
    return x1, x2, x3


def _split2(x):
    x1 = x.astype(bf16)
    return x1, (x - x1.astype(f32)).astype(bf16)


def _pair_specs(S, BR, T, kmap):
    nb = BR // HW
    q = pl.BlockSpec((T, HW), lambda p, i, j: (i, p))
    k = pl.BlockSpec((T, HW), lambda p, i, j: (kmap(i, j), nb + p))
    v = pl.BlockSpec((T, HW), lambda p, i, j: (kmap(i, j), 2 * nb + p))
    z = pl.BlockSpec((T, HW), lambda p, i, j: (i, 3 * nb + p))
    return q, k, v, z


def _fox_cum(flT, b, name):
    H, S = flT.shape
    tj = _pick(S, (256, 128))

    def body(fl_ref, b_ref, c_ref):
        j = pl.program_id(0)
        x = fl_ref[...] + b_ref[...]
        logf = jnp.minimum(x, 0.0) - jnp.log(1.0 + jnp.exp(-jnp.abs(x)))
        tri = (_iota2((S, tj), 0) <= j * tj + _iota2((S, tj), 1)).astype(bf16)
        c_ref[...] = sum(_dot(part, tri) for part in _split3(logf))

    return pl.pallas_call(
        body, name=name, grid=(S // tj,),
        in_specs=[pl.BlockSpec((H, S), lambda j: (0, 0)), pl.BlockSpec((H, 1), lambda j: (0, 0))],
        out_specs=pl.BlockSpec((H, tj), lambda j: (0, j)),
        out_shape=jax.ShapeDtypeStruct((H, S), f32),
        compiler_params=_params(("parallel",)),
    )(flT, b)


def _fox_cum_bwd(dcq, dck, flT, b, name):
    H, S = flT.shape
    tj = _pick(S, (256, 128))

    def body(dcq_ref, dck_ref, fl_ref, b_ref, o_ref, db_ref):
        j = pl.program_id(0)

        @pl.when(j == 0)
        def _():
            db_ref[...] = jnp.zeros_like(db_ref)

        tri = (_iota2((S, tj), 0) >= j * tj + _iota2((S, tj), 1)).astype(bf16)
        dlogf = sum(_dot(part, tri) for part in _split3(dcq_ref[...] - dck_ref[...]))
        x = fl_ref[...] + b_ref[...]
        dfl = dlogf * _sigmoid(-x)
        o_ref[...] = dfl
        db_ref[...] += jnp.sum(dfl, axis=-1, keepdims=True)

    blk = pl.BlockSpec((H, tj), lambda j: (0, j))
    whole = pl.BlockSpec((H, S), lambda j: (0, 0))
    col = pl.BlockSpec((H, 1), lambda j: (0, 0))
    return pl.pallas_call(
        body, name=name, grid=(S // tj,),
        in_specs=[whole, whole, blk, col],
        out_specs=[blk, col],
        out_shape=[jax.ShapeDtypeStruct((H, S), f32), jax.ShapeDtypeStruct((H, 1), f32)],
        compiler_params=_params(("arbitrary",)),
    )(dcq, dck, flT, b)


def _hsl(hh):
    return slice(hh * HEAD_DIM, (hh + 1) * HEAD_DIM)


def _scaled(q, scale):
    assert math.log2(scale).is_integer()
    return (q.astype(f32) * scale).astype(bf16)


def _swa_bias(H, n_kv):
    G = H // n_kv
    slopes = 2.0 ** (-8.0 * np.arange(1, H + 1, dtype=np.float32) / H)
    dist = (np.arange(BLOCK)[None, :] + BLOCK - np.arange(2 * BLOCK)[:, None]).astype(np.float32)
    valid = (dist >= 0) & (dist < BLOCK)
    out = np.empty((n_kv, 2 * BLOCK, G * BLOCK), np.float32)
    for h in range(H):
        out[h // G][:, (h % G) * BLOCK:(h % G + 1) * BLOCK] = np.where(valid, -(slopes[h] * dist), np.float32(NEG))
    return jnp.asarray(out)


def _swaT_parts(q_ref, kc_ref, kp_ref, vc_ref, vp_ref, bias_ref, sink_ref, kvh, n, G, scale):
    qg = jnp.concatenate([q_ref[:, _hsl(g)] for g in range(G)], axis=0)
    kband = jnp.concatenate([kp_ref[0], kc_ref[0]], axis=0)
    vband = jnp.concatenate([vp_ref[0], vc_ref[0]], axis=0)
    s = _dot(kband, qg, NT) * scale + bias_ref[0]
    s = jnp.where(_iota2(s.shape, 0) + jnp.where(n > 0, BLOCK, 0) >= BLOCK, s, NEG)
    sink = jnp.concatenate([jnp.full((1, BLOCK), sink_ref[kvh * G + g], f32) for g in range(G)], axis=1)
    m = jnp.maximum(jnp.max(s, axis=0, keepdims=True), sink)
    p = jnp.exp(s - m)
    ps = jnp.exp(sink - m)
    den = jnp.sum(p, axis=0, keepdims=True) + ps
    return qg, kband, vband, p, ps, den


def _swaT_specs(S, BR, KV, G):
    qspec, zspec, cur, prev = _swa_specs(S, BR, KV, G)
    bias = pl.BlockSpec((1, 2 * BLOCK, G * BLOCK), lambda h, n: (h, 0, 0))
    return qspec, zspec, cur, prev, bias


def _swaT_fwd(P, kh, vh, sinks, bias, name, comm=None):
    S = P.shape[0]
    n_kv = kh.shape[0]
    H = sinks.shape[0]
    G = H // n_kv
    BR, KV = H * HEAD_DIM, n_kv * HEAD_DIM
    scale = HEAD_DIM ** -0.5
    assert (BR + 2 * KV) % (G * HEAD_DIM) == 0

    def body(q_ref, z_ref, kc_ref, kp_ref, vc_ref, vp_ref, bias_ref, sink_ref, o_ref, u_ref):
        kvh, n = pl.program_id(0), pl.program_id(1)
        _, _, vband, p, _, den = _swaT_parts(q_ref, kc_ref, kp_ref, vc_ref, vp_ref, bias_ref, sink_ref, kvh, n, G, scale)
        o = jnp.transpose(_dot(vband, p.astype(bf16), TN) / den)
        for g in range(G):
            oh = o[g * BLOCK:(g + 1) * BLOCK]
            o_ref[:, _hsl(g)] = oh.astype(o_ref.dtype)
            u_ref[:, _hsl(g)] = _gate_fwd(oh, z_ref[:, _hsl(g)]).astype(u_ref.dtype)

    qspec, zspec, cur, prev, bspec = _swaT_specs(S, BR, KV, G)
    ospec = pl.BlockSpec((BLOCK, G * HEAD_DIM), lambda h, n: (n, h))
    return _pallas(
        body, (P, P, kh, kh, vh, vh, bias, sinks), name=name, grid=(n_kv, S // BLOCK),
        in_specs=[qspec, zspec, cur, prev, cur, prev, bspec, SMEM],
        out_specs=[ospec, ospec],
        out_shape=[jax.ShapeDtypeStruct((S, BR), bf16)] * 2,
        semantics=("parallel", "parallel"), comm=comm)


def _swaT_bwd(P, kh, vh, o, du, sinks, bias, name, comm=None):
    S = P.shape[0]
    n_kv = kh.shape[0]
    H = sinks.shape[0]
    G = H // n_kv
    BR, KV = H * HEAD_DIM, n_kv * HEAD_DIM
    scale = HEAD_DIM ** -0.5

    def body(q_ref, z_ref, kc_ref, kp_ref, vc_ref, vp_ref, o_ref, du_ref, bias_ref, sink_ref,
             dq_ref, dz_ref, dk_ref, dv_ref, ds_ref):
        kvh, n = pl.program_id(0), pl.program_id(1)

        @pl.when(n == 0)
        def _():
            dk_ref[...] = jnp.zeros_like(dk_ref)
            dv_ref[...] = jnp.zeros_like(dv_ref)
            ds_ref[...] = jnp.zeros_like(ds_ref)

        dos, prods = [], []
        for g in range(G):
            oh = o_ref[:, _hsl(g)].astype(f32)
            doh, dzh = _gate_bwd(du_ref[:, _hsl(g)].astype(f32), oh, z_ref[:, _hsl(g)])
            dz_ref[:, _hsl(g)] = dzh.astype(dz_ref.dtype)
            dos.append(doh.astype(bf16))
            prods.append(doh * oh)
        dog = jnp.concatenate(dos, axis=0)
        delta = jnp.sum(jnp.transpose(jnp.concatenate(prods, axis=0)), axis=0, keepdims=True)
        qg, kband, vband, p, ps, den = _swaT_parts(q_ref, kc_ref, kp_ref, vc_ref, vp_ref, bias_ref, sink_ref,
                                                   kvh, n, G, scale)
        inv = 1.0 / den
        pn = p * inv
        ds = (pn * (_dot(vband, dog, NT) - delta)).astype(bf16)
        dq = jnp.transpose(_dot(kband, ds, TN)) * scale
        for g in range(G):
            dq_ref[:, _hsl(g)] = dq[g * BLOCK:(g + 1) * BLOCK].astype(dq_ref.dtype)
        dkb = _dot(ds, qg) * scale
        dvb = _dot(pn.astype(bf16), dog)
        ds_ref[0] -= ps * inv * delta
        row_c = pl.multiple_of(n * BLOCK, BLOCK)
        dk_ref[0, pl.ds(row_c, BLOCK), :] += dkb[BLOCK:]
        dv_ref[0, pl.ds(row_c, BLOCK), :] += dvb[BLOCK:]

        @pl.when(n > 0)
        def _():
            row_p = pl.multiple_of((n - 1) * BLOCK, BLOCK)
            dk_ref[0, pl.ds(row_p, BLOCK), :] += dkb[:BLOCK]
            dv_ref[0, pl.ds(row_p, BLOCK), :] += dvb[:BLOCK]

    qspec, zspec, cur, prev, bspec = _swaT_specs(S, BR, KV, G)
    ospec = pl.BlockSpec((BLOCK, G * HEAD_DIM), lambda h, n: (n, h))
    full = pl.BlockSpec((1, S, HEAD_DIM), lambda h, n: (h, 0, 0))
    return _pallas(
        body, (P, P, kh, kh, vh, vh, o, du, bias, sinks), name=name, grid=(n_kv, S // BLOCK),
        in_specs=[qspec, zspec, cur, prev, cur, prev, ospec, ospec, bspec, SMEM],
        out_specs=[ospec, ospec, full, full, pl.BlockSpec((1, 1, G * BLOCK), lambda h, n: (h, 0, 0))],
        out_shape=[jax.ShapeDtypeStruct((S, BR), bf16)] * 2
        + [jax.ShapeDtypeStruct((n_kv, S, HEAD_DIM), f32)] * 2
        + [jax.ShapeDtypeStruct((n_kv, 1, G * BLOCK), f32)],
        semantics=("parallel", "arbitrary"), comm=comm)


def _foxT_fwd(P, cq, ck, BR, name, comm=None):
    S = P.shape[0]
    H = BR // HEAD_DIM
    T = _attn_tile(S)
    nq = S // T
    scale = HEAD_DIM ** -0.5
    kmap = lambda i, j: jnp.minimum(i, j)

    def body(q_ref, k_ref, v_ref, z_ref, cq_ref, ck_ref, o_ref, u_ref, lse_ref, m_s, l_s, acc_s):
        qb, kb = pl.program_id(1), pl.program_id(2)

        @pl.when(kb == 0)
        def _():
            m_s[...] = jnp.full_like(m_s, NEG)
            l_s[...] = jnp.zeros_like(l_s)
            acc_s[...] = jnp.zeros_like(acc_s)

        def tile(diagonal):
            for hh in range(HP):
                sl = _hsl(hh)
                s = _dot(k_ref[:, sl], _scaled(q_ref[:, sl], scale), NT) + cq_ref[hh] - ck_ref[hh]
                if diagonal:
                    s = jnp.where(_iota2((T, T), 0) <= _iota2((T, T), 1), s, NEG)
                m_old = m_s[hh]
                m_new = jnp.maximum(m_old, jnp.max(s, axis=0, keepdims=True))
                alpha = jnp.exp(m_old - m_new)
                p = jnp.exp(s - m_new)
                l_s[hh] = alpha * l_s[hh] + jnp.sum(p, axis=0, keepdims=True)
                acc_s[hh] = alpha * acc_s[hh] + _dot(v_ref[:, sl], p.astype(bf16), TN)
                m_s[hh] = m_new

        @pl.when(kb < qb)
        def _():
            tile(False)

        @pl.when(kb == qb)
        def _():
            tile(True)
            for hh in range(HP):
                sl = _hsl(hh)
                oh = jnp.transpose(acc_s[hh] / l_s[hh])
                o_ref[:, sl] = oh.astype(o_ref.dtype)
                u_ref[:, sl] = _gate_fwd(oh, z_ref[:, sl]).astype(u_ref.dtype)
                lse_ref[hh] = m_s[hh] + jnp.log(l_s[hh])

    q, k, v, z = _pair_specs(S, BR, T, kmap)
    rowq = pl.BlockSpec((HP, 1,T), lambda p, i, j: (p, 0, i))
    colk = pl.BlockSpec((HP, T,1), lambda p, i, j: (p, kmap(i, j), 0))
    ospec = pl.BlockSpec((T, HW), lambda p, i, j: (i, p))
    return _pallas(
        body, (P, P, P, P, cq, ck), name=name, grid=(H // HP, nq, nq),
        in_specs=[q, k, v, z, rowq, colk],
        out_specs=[ospec, ospec, rowq],
        out_shape=[jax.ShapeDtypeStruct((S, BR), bf16)] * 2 + [jax.ShapeDtypeStruct((H, 1, S), f32)],
        scratch_shapes=[pltpu.VMEM((HP, 1,T), f32), pltpu.VMEM((HP, 1,T), f32), pltpu.VMEM((HP, HEAD_DIM,T), f32)],
        semantics=("parallel", "arbitrary", "arbitrary"), comm=comm)


def _foxT_bwd(P, o, du, lse, cq, ck, BR, name, comm=None):
    S = P.shape[0]
    H = BR // HEAD_DIM
    T = _attn_tile(S)
    nq = S // T
    scale = HEAD_DIM ** -0.5
    kmap = lambda i, j: jnp.minimum(i, j)

    def body(q_ref, k_ref, v_ref, z_ref, o_ref, du_ref, lse_ref, cq_ref, ck_ref,
             dq_ref, dz_ref, dk_ref, dv_ref, dcq_ref, dck_ref, do_s, delta_s, dq_s, dcq_s):
        qb, kb = pl.program_id(1), pl.program_id(2)

        @pl.when(jnp.logical_and(qb == 0, kb == 0))
        def _():
            dk_ref[...] = jnp.zeros_like(dk_ref)
            dv_ref[...] = jnp.zeros_like(dv_ref)
            dck_ref[...] = jnp.zeros_like(dck_ref)

        @pl.when(kb == 0)
        def _():
            for hh in range(HP):
                sl = _hsl(hh)
                oh = o_ref[:, sl].astype(f32)
                doh, dzh = _gate_bwd(du_ref[:, sl].astype(f32), oh, z_ref[:, sl])
                dz_ref[:, sl] = dzh.astype(dz_ref.dtype)
                do_s[hh] = doh.astype(bf16)
                delta_s[hh] = jnp.sum(jnp.transpose(doh * oh), axis=0, keepdims=True)
            dq_s[...] = jnp.zeros_like(dq_s)
            dcq_s[...] = jnp.zeros_like(dcq_s)

        def tile(diagonal):
            rows = pl.ds(pl.multiple_of(kb * T, T), T)
            for hh in range(HP):
                sl = _hsl(hh)
                qh, kh, vh, dob = _scaled(q_ref[:, sl], scale), k_ref[:, sl], v_ref[:, sl], do_s[hh]
                s = _dot(kh, qh, NT) + cq_ref[hh] - ck_ref[hh]
                p = jnp.exp(s - lse_ref[hh])
                if diagonal:
                    p = jnp.where(_iota2((T, T), 0) <= _iota2((T, T), 1), p, 0.0)
                ds = p * (_dot(vh, dob, NT) - delta_s[hh])
                dsb = ds.astype(bf16)
                dq_s[hh] += _dot(kh, dsb, TN)
                dk_ref[rows, sl] += _dot(dsb, qh)
                dv_ref[rows, sl] += _dot(p.astype(bf16), dob)
                dcq_s[hh] += jnp.sum(ds, axis=0, keepdims=True)
                dck_ref[hh, rows, :] += jnp.sum(ds, axis=1, keepdims=True)

        @pl.when(kb < qb)
        def _():
            tile(False)

        @pl.when(kb == qb)
        def _():
            tile(True)
            for hh in range(HP):
                dq_ref[:, _hsl(hh)] = (jnp.transpose(dq_s[hh]) * scale).astype(dq_ref.dtype)
                dcq_ref[hh] = dcq_s[hh]

    q, k, v, z = _pair_specs(S, BR, T, kmap)
    rowq = pl.BlockSpec((HP, 1,T), lambda p, i, j: (p, 0, i))
    colk = pl.BlockSpec((HP, T,1), lambda p, i, j: (p, kmap(i, j), 0))
    ospec = pl.BlockSpec((T, HW), lambda p, i, j: (i, p))
    full = pl.BlockSpec((S, HW), lambda p, i, j: (0, p))
    return _pallas(
        body, (P, P, P, P, o, du, lse, cq, ck), name=name, grid=(H // HP, nq, nq),
        in_specs=[q, k, v, z, ospec, ospec, rowq, rowq, colk],
        out_specs=[ospec, ospec, full, full, rowq, pl.BlockSpec((HP, S,1), lambda p, i, j: (p, 0, 0))],
        out_shape=[jax.ShapeDtypeStruct((S, BR), bf16)] * 2 + [jax.ShapeDtypeStruct((S, BR), f32)] * 2
        + [jax.ShapeDtypeStruct((H, 1, S), f32), jax.ShapeDtypeStruct((H, S, 1), f32)],
        scratch_shapes=[pltpu.VMEM((HP, T,HEAD_DIM), bf16), pltpu.VMEM((HP, 1,T), f32),
                        pltpu.VMEM((HP, HEAD_DIM,T), f32), pltpu.VMEM((HP, 1,T), f32)],
        semantics=("parallel", "arbitrary", "arbitrary"), comm=comm)


def _sbT_logs(kc, qh, before):
    l = _dot(kc, qh, NT)
    minus_abs = lax.bitcast_convert_type(lax.bitcast_convert_type(l, jnp.int32) | jnp.int32(-2 ** 31), f32)
    lsig = jnp.minimum(l, 0.0) - jnp.log(1.0 + jnp.exp(minus_abs))
    lf = lsig - l
    if before is not None:
        lf = jnp.where(before, lf, 0.0)
    return lsig, lf


def _chunk_rows(c):
    return slice(c * BLOCK, (c + 1) * BLOCK)


def _sbT_suffix_tile(lf, tri, nc, two_pass):
    if not two_pass:
        hi = lf.astype(bf16)
        return jnp.concatenate([_dot(tri, hi[_chunk_rows(c)]) for c in range(nc)], axis=0)
    hi, lo = _split2(lf)
    tri2 = jnp.concatenate([tri, tri], axis=1)
    return jnp.concatenate(
        [_dot(tri2, jnp.concatenate([hi[_chunk_rows(c)], lo[_chunk_rows(c)]], axis=0)) for c in range(nc)], axis=0)


def _sbT_fwd(P, BR, name, comm=None):
    S = P.shape[0]
    H = BR // HEAD_DIM
    T = _attn_tile(S)
    nq = S // T
    nc = T // BLOCK
    scale = HEAD_DIM ** -0.5
    kmap = lambda i, j: jnp.maximum(i - j, 0)

    def body(q_ref, k_ref, v_ref, z_ref, o_ref, u_ref, rc_ref, r_s, acc_s):
        qb, j = pl.program_id(1), pl.program_id(2)

        @pl.when(j == 0)
        def _():
            r_s[...] = jnp.zeros_like(r_s)
            acc_s[...] = jnp.zeros_like(acc_s)

        def tile(diagonal):
            ii, jj = _iota2((BLOCK, BLOCK), 0), _iota2((BLOCK, BLOCK), 1)
            tri = (jj > ii).astype(bf16)
            before = (_iota2((T, T), 0) < _iota2((T, T), 1)) if diagonal else None
            for hh in range(HP):
                sl = _hsl(hh)
                lsig, lf = _sbT_logs(k_ref[:, sl], _scaled(q_ref[:, sl], scale), before)
                x = lsig + _sbT_suffix_tile(lf, tri, nc, two_pass=True)
                r = r_s[hh]
                rc_ref[hh, 0] = r
                parts = [None] * nc
                for c in reversed(range(nc)):
                    parts[c] = jnp.exp(x[_chunk_rows(c)] + r)
                    r = r + jnp.sum(lf[_chunk_rows(c)], axis=0, keepdims=True)
                r_s[hh] = r
                a = jnp.concatenate(parts, axis=0)
                if diagonal:
                    a = jnp.where(before, a, 0.0)
                acc_s[hh] += _dot(v_ref[:, sl], a.astype(bf16), TN)

        @pl.when(j > 0)
        def _():
            @pl.when(j <= qb)
            def _():
                tile(False)

        @pl.when(j == 0)
        def _():
            tile(True)

        @pl.when(j == qb)
        def _():
            for hh in range(HP):
                sl = _hsl(hh)
                oh = jnp.transpose(acc_s[hh])
                o_ref[:, sl] = oh.astype(o_ref.dtype)
                u_ref[:, sl] = _gate_fwd(oh, z_ref[:, sl]).astype(u_ref.dtype)

    q, k, v, z = _pair_specs(S, BR, T, kmap)
    ospec = pl.BlockSpec((T, HW), lambda p, i, j: (i, p))
    rc = pl.BlockSpec((HP, 1,1, T), lambda p, i, j: (p, kmap(i, j), 0, i))
    return _pallas(
        body, (P, P, P, P), name=name, grid=(H // HP, nq, nq),
        in_specs=[q, k, v, z],
        out_specs=[ospec, ospec, rc],
        out_shape=[jax.ShapeDtypeStruct((S, BR), bf16)] * 2 + [jax.ShapeDtypeStruct((H, nq, 1, S), f32)],
        scratch_shapes=[pltpu.VMEM((HP, 1,T), f32), pltpu.VMEM((HP, HEAD_DIM,T), f32)],
        semantics=("parallel", "arbitrary", "arbitrary"), comm=comm)


def _sbT_bwd(P, o, du, rc, BR, name, comm=None):
    S = P.shape[0]
    H = BR // HEAD_DIM
    T = _attn_tile(S)
    nq = S // T
    nc = T // BLOCK
    scale = HEAD_DIM ** -0.5
    kmap = lambda i, j: jnp.minimum(i, j)

    def body(q_ref, k_ref, v_ref, z_ref, o_ref, du_ref, rc_ref,
             dq_ref, dz_ref, dk_ref, dv_ref, do_s, dq_s, g_s):
        qb, kb = pl.program_id(1), pl.program_id(2)

        @pl.when(jnp.logical_and(qb == 0, kb == 0))
        def _():
            dk_ref[...] = jnp.zeros_like(dk_ref)
            dv_ref[...] = jnp.zeros_like(dv_ref)

        @pl.when(kb == 0)
        def _():
            for hh in range(HP):
                sl = _hsl(hh)
                doh, dzh = _gate_bwd(du_ref[:, sl].astype(f32), o_ref[:, sl].astype(f32), z_ref[:, sl])
                dz_ref[:, sl] = dzh.astype(dz_ref.dtype)
                do_s[hh] = doh.astype(bf16)
            dq_s[...] = jnp.zeros_like(dq_s)
            g_s[...] = jnp.zeros_like(g_s)

        def tile(diagonal):
            ii, jj = _iota2((BLOCK, BLOCK), 0), _iota2((BLOCK, BLOCK), 1)
            tri_suffix = (jj > ii).astype(bf16)
            tri_prefix = (jj < ii).astype(bf16)
            before = (_iota2((T, T), 0) < _iota2((T, T), 1)) if diagonal else None
            out_rows = pl.ds(pl.multiple_of(kb * T, T), T)
            for hh in range(HP):
                sl = _hsl(hh)
                qh, kh, dob = _scaled(q_ref[:, sl], scale), k_ref[:, sl], do_s[hh]
                lsig, lf = _sbT_logs(kh, qh, before)
                x = lsig + _sbT_suffix_tile(lf, tri_suffix, nc, two_pass=False)
                r = rc_ref[hh, 0]
                parts = [None] * nc
                for c in reversed(range(nc)):
                    parts[c] = jnp.exp(x[_chunk_rows(c)] + r)
                    r = r + jnp.sum(lf[_chunk_rows(c)], axis=0, keepdims=True)
                a = jnp.concatenate(parts, axis=0)
                if diagonal:
                    a = jnp.where(before, a, 0.0)
                g = a * _dot(v_ref[:, sl], dob, NT)
                gb = g.astype(bf16)
                gsum = g_s[hh]
                for c in range(nc):
                    parts[c] = _dot(tri_prefix, gb[_chunk_rows(c)]) + gsum
                    gsum = gsum + jnp.sum(g[_chunk_rows(c)], axis=0, keepdims=True)
                g_s[hh] = gsum
                dl = g - (g + jnp.concatenate(parts, axis=0)) * jnp.exp(lsig)
                if diagonal:
                    dl = jnp.where(before, dl, 0.0)
                dl = dl.astype(bf16)
                dq_s[hh] += _dot(kh, dl, TN)
                dk_ref[out_rows, sl] += _dot(dl, qh)
                dv_ref[out_rows, sl] += _dot(a.astype(bf16), dob)

        @pl.when(kb < qb)
        def _():
            tile(False)

        @pl.when(kb == qb)
        def _():
            tile(True)
            for hh in range(HP):
                dq_ref[:, _hsl(hh)] = (jnp.transpose(dq_s[hh]) * scale).astype(dq_ref.dtype)

    q, k, v, z = _pair_specs(S, BR, T, kmap)
    ospec = pl.BlockSpec((T, HW), lambda p, i, j: (i, p))
    full = pl.BlockSpec((S, HW), lambda p, i, j: (0, p))
    rcs = pl.BlockSpec((HP, 1,1, T), lambda p, i, j: (p, kmap(i, j), 0, i))
    return _pallas(
        body, (P, P, P, P, o, du, rc), name=name, grid=(H // HP, nq, nq),
        in_specs=[q, k, v, z, ospec, ospec, rcs],
        out_specs=[ospec, ospec, full, full],
        out_shape=[jax.ShapeDtypeStruct((S, BR), bf16)] * 2 + [jax.ShapeDtypeStruct((S, BR), f32)] * 2,
        scratch_shapes=[pltpu.VMEM((HP, T,HEAD_DIM), bf16), pltpu.VMEM((HP, HEAD_DIM,T), f32),
                        pltpu.VMEM((HP, 1,T), f32)],
        semantics=("parallel", "arbitrary", "arbitrary"), comm=comm)


def _slab(R, C):
    tr = _pick(R, (256, 192, 128))
    if tr != R or R % 8 == 0:
        return tr, C
    return R, _pick(C, (256, 128))


def _adam_math(w, g, m, v):
    m = ADAM_B1 * m + (1.0 - ADAM_B1) * g
    v = ADAM_B2 * v + (1.0 - ADAM_B2) * (g * g)
    m_hat = m / (1.0 - ADAM_B1 ** ADAM_STEP)
    v_hat = v / (1.0 - ADAM_B2 ** ADAM_STEP)
    delta = -ADAM_LR * (m_hat / (jnp.sqrt(v_hat) + ADAM_EPS) + ADAM_WD * w)
    return delta, m, v


def _adamw_sum(parts, w, m, v, name):
    L, R, C = w.shape
    assert len(parts) == L
    n_parts = parts[0].shape[0]
    tr, tc = _slab(R, C)

    def body(*refs):
        p_refs = refs[:L]
        w_ref, m_ref, v_ref, g_ref, d_ref, nm_ref, nv_ref = refs[L:]
        layer = pl.program_id(0)
        g = None
        for l in range(L):
            gl = p_refs[l][0].astype(f32)
            for i in range(1, n_parts):
                gl = gl + p_refs[l][i].astype(f32)
            g = gl if g is None else jnp.where(layer == l, gl, g)
        g_ref[0] = g
        d_ref[0], nm_ref[0], nv_ref[0] = _adam_math(w_ref[0], g, m_ref[0], v_ref[0])

    blk = pl.BlockSpec((1, tr, tc), lambda l, i, j: (l, i, j))
    return _pallas(
        body, (*parts, w, m, v), name=name, grid=(L, R // tr, C // tc),
        in_specs=[pl.BlockSpec((n_parts, tr, tc), lambda l, i, j: (0, i, j))] * L + [blk, blk, blk],
        out_specs=[blk] * 4,
        out_shape=[jax.ShapeDtypeStruct((L, R, C), f32)] * 4,
        semantics=("parallel", "parallel", "parallel"))


def _coords():
    return lax.axis_index("x"), lax.axis_index("y"), lax.axis_index("c")


class _Gather:
    def __init__(self, shards):
        self.args = list(shards)
        self.n = len(shards)
        self.out_shape = [jax.ShapeDtypeStruct((N_DEV,) + s.shape, s.dtype) for s in shards]
        self.scratch = [pltpu.SemaphoreType.DMA((self.n, 7)), pltpu.SemaphoreType.DMA((self.n, 7)),
                        pltpu.SemaphoreType.DMA((self.n,))]

    def _ctx(self, ins, outs, sems):
        send_sems, recv_sems, local_sems = sems
        x, y, c = _coords()
        me, sibling = (x, y, c), (x, y, 1 - c)
        chips = [(1 - x, y), (x, 1 - y), (1 - x, 1 - y)]

        def slot(out, dev):
            return out.at[4 * dev[0] + 2 * dev[1] + dev[2]]

        def copy(a, k, block, to, src=None):
            return pltpu.make_async_remote_copy(
                src_ref=slot(outs[a], block) if src is None else src, dst_ref=slot(outs[a], block),
                send_sem=send_sems.at[a, k], recv_sem=recv_sems.at[a, k], device_id=to, device_id_type=MESH)

        mine = [pltpu.make_async_copy(ins[a], slot(outs[a], me), local_sems.at[a]) for a in range(self.n)]
        first = []
        for a in range(self.n):
            first.append(copy(a, 0, me, sibling, src=ins[a]))
            first += [copy(a, 1 + j, me, (*chip, c), src=ins[a]) for j, chip in enumerate(chips)]
        passed = [copy(a, 4 + j, (*chip, c), sibling) for j, chip in enumerate(chips) for a in range(self.n)]
        return c, me, sibling, chips, copy, mine, first, passed

    def start(self, ins, outs, sems):
        *_, mine, first, _ = self._ctx(ins, outs, sems)
        for cp in mine + first:
            cp.start()

    def mid(self, ins, outs, sems):
        c, me, _, chips, copy, _, _, passed = self._ctx(ins, outs, sems)
        i = 0
        for j, chip in enumerate(chips):
            for a in range(self.n):
                copy(a, 1 + j, (*chip, c), me).wait_recv()
                passed[i].start()
                i += 1

    def finish(self, ins, outs, sems):
        c, me, sibling, chips, copy, mine, first, passed = self._ctx(ins, outs, sems)
        for a in range(self.n):
            copy(a, 0, sibling, me).wait_recv()
            for j, chip in enumerate(chips):
                copy(a, 4 + j, (*chip, 1 - c), me).wait_recv()
        for cp in first + passed:
            cp.wait_send()
        for cp in mine:
            cp.wait()


N_CHIPS = 4


class _SiblingSwap:
    mid = None

    def __init__(self, blocks):
        self.args = list(blocks)
        self.n = len(blocks)
        self.out_shape = [jax.ShapeDtypeStruct((N_CHIPS,) + b.shape[1:], b.dtype) for b in blocks]
        self.scratch = [pltpu.SemaphoreType.DMA((self.n, N_CHIPS)), pltpu.SemaphoreType.DMA((self.n, N_CHIPS))]

    def _copies(self, ins, outs, sems):
        send_sems, recv_sems = sems
        x, y, c = _coords()
        return [pltpu.make_async_remote_copy(
            src_ref=ins[a].at[2 * q + (1 - c)], dst_ref=outs[a].at[q],
            send_sem=send_sems.at[a, q], recv_sem=recv_sems.at[a, q],
            device_id=(x, y, 1 - c), device_id_type=MESH) for a in range(self.n) for q in range(N_CHIPS)]

    def start(self, ins, outs, sems):
        for cp in self._copies(ins, outs, sems):
            cp.start()

    def finish(self, ins, outs, sems):
        for cp in self._copies(ins, outs, sems):
            cp.wait()


class _ChipScatter:
    mid = None

    def __init__(self, blocks):
        self.args = list(blocks)
        self.n = len(blocks)
        self.out_shape = [jax.ShapeDtypeStruct(b.shape, b.dtype) for b in blocks]
        self.scratch = [pltpu.SemaphoreType.DMA((self.n, 3)), pltpu.SemaphoreType.DMA((self.n, 3)),
                        pltpu.SemaphoreType.DMA((self.n,))]

    def _ctx(self, ins, outs, sems):
        send_sems, recv_sems, local_sems = sems
        x, y, c = _coords()
        me = 2 * x + y
        mine = [pltpu.make_async_copy(ins[a].at[me], outs[a].at[me], local_sems.at[a]) for a in range(self.n)]

        def copy(a, k, landing_here):
            px, py = x ^ ((k >> 1) & 1), y ^ (k & 1)
            them = 2 * px + py
            return pltpu.make_async_remote_copy(
                src_ref=ins[a].at[them], dst_ref=outs[a].at[them if landing_here else me],
                send_sem=send_sems.at[a, k - 1], recv_sem=recv_sems.at[a, k - 1],
                device_id=(px, py, c), device_id_type=MESH)

        sent = [copy(a, k, False) for k in range(1, N_CHIPS) for a in range(self.n)]
        arrivals = [copy(a, k, True) for k in range(1, N_CHIPS) for a in range(self.n)]
        return mine, sent, arrivals

    def start(self, ins, outs, sems):
        mine, sent, _ = self._ctx(ins, outs, sems)
        for cp in mine + sent:
            cp.start()

    def finish(self, ins, outs, sems):
        mine, sent, arrivals = self._ctx(ins, outs, sems)
        for cp in arrivals:
            cp.wait_recv()
        for cp in sent:
            cp.wait_send()
        for cp in mine:
            cp.wait()


def _pair_sum(blocks, got, core, name):
    _, R, C = blocks.shape
    tr, tc = _slab(R, C)

    def body(core_ref, mine_ref, got_ref, o_ref):
        o_ref[...] = (mine_ref[...].astype(f32) + got_ref[...].astype(f32)).astype(o_ref.dtype)

    return pl.pallas_call(
        body, name=name,
        grid_spec=pltpu.PrefetchScalarGridSpec(
            num_scalar_prefetch=1, grid=(N_CHIPS, R // tr, C // tc),
            in_specs=[pl.BlockSpec((1, tr, tc), lambda q, i, j, core_ref: (2 * q + core_ref[0], i, j)),
                      pl.BlockSpec((1, tr, tc), lambda q, i, j, core_ref: (q, i, j))],
            out_specs=pl.BlockSpec((1, tr, tc), lambda q, i, j, core_ref: (q, i, j))),
        out_shape=jax.ShapeDtypeStruct((N_CHIPS, R, C), blocks.dtype),
        compiler_params=_params(("parallel", "parallel", "parallel")),
    )(core, blocks, got)


def _run_comm(comm, name):
    n = comm.n

    def body(*refs):
        ins, outs, sems = refs[:n], refs[n:2 * n], refs[2 * n:]
        comm.start(ins, outs, sems)
        if comm.mid is not None:
            comm.mid(ins, outs, sems)
        comm.finish(ins, outs, sems)

    return pl.pallas_call(
        body, name=name, in_specs=[ANY] * n, out_specs=[ANY] * n,
        out_shape=comm.out_shape, scratch_shapes=comm.scratch,
    )(*comm.args)


def _pallas(body, args, *, name, grid, in_specs, out_specs, out_shape, scratch_shapes=(), semantics, comm=None):
    in_specs, out_specs, out_shape, scratch_shapes = list(in_specs), list(out_specs), list(out_shape), list(scratch_shapes)
    if comm is None:
        return pl.pallas_call(
            body, name=name, grid=grid, in_specs=in_specs, out_specs=out_specs, out_shape=out_shape,
            scratch_shapes=scratch_shapes, compiler_params=_params(semantics))(*args)
    a = len(in_specs)
    b = a + comm.n
    c = b + len(out_specs)
    d = c + comm.n
    e = d + len(scratch_shapes)
    total = math.prod(grid)
    mid_step = (7 * total) // 8

    def hosted(*refs):
        step = pl.program_id(0)
        for axis in range(1, len(grid)):
            step = step * grid[axis] + pl.program_id(axis)
        ins, outs, sems = refs[a:b], refs[c:d], refs[e:]

        @pl.when(step == 0)
        def _():
            comm.start(ins, outs, sems)

        body(*refs[:a], *refs[b:c], *refs[d:e])

        if comm.mid is not None:
            @pl.when(step == mid_step)
            def _():
                comm.mid(ins, outs, sems)

        @pl.when(step == total - 1)
        def _():
            comm.finish(ins, outs, sems)

    res = pl.pallas_call(
        hosted, name=name, grid=grid, in_specs=in_specs + [ANY] * comm.n, out_specs=out_specs + [ANY] * comm.n,
        out_shape=out_shape + comm.out_shape, scratch_shapes=scratch_shapes + comm.scratch,
        compiler_params=_params(("arbitrary",) * len(grid)))(*args, *comm.args)
    return list(res[:len(out_specs)]) + [list(res[len(out_specs):])]


def _all_reduce_small(vec, name):
    R, C = vec.shape

    def body(v_ref, o_ref, gath, send_sems, recv_sems):
        x, y, c = _coords()
        me = 4 * x + 2 * y + c
        gath[me] = v_ref[...]

        def copy(k):
            px, py, pc = x ^ ((k >> 2) & 1), y ^ ((k >> 1) & 1), c ^ (k & 1)
            return pltpu.make_async_remote_copy(
                src_ref=v_ref, dst_ref=gath.at[me], send_sem=send_sems.at[k - 1], recv_sem=recv_sems.at[k - 1],
                device_id=(px, py, pc), device_id_type=MESH)

        sent = [copy(k) for k in range(1, N_DEV)]
        for cp in sent:
            cp.start()
        for cp in sent:
            cp.wait()
        total = gath[0]
        for i in range(1, N_DEV):
            total = total + gath[i]
        o_ref[...] = total

    return pl.pallas_call(
        body, name=name,
        in_specs=[pl.BlockSpec(memory_space=pltpu.VMEM)], out_specs=pl.BlockSpec(memory_space=pltpu.VMEM),
        out_shape=jax.ShapeDtypeStruct((R, C), f32),
        scratch_shapes=[pltpu.VMEM((N_DEV, R, C), f32), pltpu.SemaphoreType.DMA((7,)), pltpu.SemaphoreType.DMA((7,))],
    )(vec)


def _gathered_in(g):
    return jnp.transpose(g, (1, 0, 2)).reshape(g.shape[1], -1)


def _col_blocks(dw):
    D, N = dw.shape
    return jnp.transpose(dw.reshape(D, N_DEV, N // N_DEV), (1, 0, 2)).astype(bf16)


def _heads_major(a, n):
    return jnp.transpose(a.reshape(a.shape[0], n, HEAD_DIM), (1, 0, 2))


def _heads_minor(a):
    return jnp.transpose(a, (1, 0, 2)).reshape(a.shape[1], -1)


def kernel(x, g_pre, g_post, w_in_a, w_out_a, sinks_a, w_in_b, w_out_b, w_in_c, b_f_c, w_out_c, loss_target, m_g_pre, m_g_post, m_w_in_a, m_w_out_a, m_sinks_a, m_w_in_b, m_w_out_b, m_w_in_c, m_b_f_c, m_w_out_c, v_g_pre, v_g_post, v_w_in_a, v_w_out_a, v_sinks_a, v_w_in_b, v_w_out_b, v_w_in_c, v_b_f_c, v_w_out_c):
    S, D = x.shape[1], x.shape[2]
    H = D // HEAD_DIM
    BR = H * HEAD_DIM
    n_kv = H // 8
    KV = n_kv * HEAD_DIM
    x0 = x[0]
    target = loss_target[0]
    swa_bias = _swa_bias(H, n_kv)
    w_in = {0: w_in_a, 1: w_in_b, 2: w_in_c}
    w_out = {0: w_out_a, 1: w_out_b, 2: w_out_c}

    saved = []
    xi = x0
    riders = {"pre_norm0": [("in", 0)], "in_proj0": [("out", 0)], "mixer0": [("in", 1)],
              "mixer1": [("out", 1), ("in", 2)], "mixer2": [("out", 2), ("in", 3), ("out", 3)]}
    full = {}

    def held_transposed(kind):
        return w_in[kind].shape[2] % BLOCK != 0

    def shard_of(which, i):
        kind, j = i % 3, i // 3
        if which == "out":
            return w_out[kind][j].astype(bf16)
        if held_transposed(kind):
            return jnp.swapaxes(w_in[kind], 1, 2)[j].astype(bf16)
        return w_in[kind][j].astype(bf16)

    def rider(slot):
        keys = riders.get(slot)
        if not keys:
            return None
        return _Gather([shard_of(which, i) for which, i in keys])

    def landed(slot, arrays):
        full.update(zip(riders[slot], arrays))

    for i in range(DEPTH):
        kind, j = i % 3, i // 3
        nxt = rider(f"mixer{i}")
        with_norm = rider(f"pre_norm{i}")
        h = _rmsnorm_fwd(xi, g_pre[i:i + 1], name=f"pre_norm{i}", comm=with_norm)
        if with_norm is not None:
            h, arrived = h
            landed(f"pre_norm{i}", arrived)
        w_t = held_transposed(kind)
        W_in = full["in", i].reshape(-1, D) if w_t else _gathered_in(full["in", i])
        st = dict(x=xi, h=h)
        if kind == 2:
            n_main = 4 * BR
            if w_t:
                W_in = jnp.pad(W_in, ((0, (-W_in.shape[0]) % BLOCK), (0, 0)))
                fl = _matmul(h, W_in[n_main:n_main + BLOCK], "nt", f32, name=f"f_proj{i}")
            else:
                W_in = jnp.pad(W_in, ((0, 0), (0, (-W_in.shape[1]) % BLOCK)))
                fl = _matmul(h, W_in[:, n_main:n_main + BLOCK], "nn", f32, name=f"f_proj{i}")
            flT = jnp.transpose(fl[:, :H])
            bcol = b_f_c[j].reshape(H, 1)
            cumT = _fox_cum(flT, bcol, name=f"fox_cum{i}")
            cq, ck = cumT[:, None, :], cumT[:, :, None]
            st.update(flT=flT, bcol=bcol, cq=cq, ck=ck)
        st["W_in"] = W_in
        with_proj = rider(f"in_proj{i}")
        P = _matmul(h, W_in, "nt" if w_t else "nn", bf16, name=f"in_proj{i}", comm=with_proj)
        if with_proj is not None:
            P, arrived = P
            landed(f"in_proj{i}", arrived)
        st["P"] = P
        if kind == 0:
            kh = _heads_major(P[:, BR:BR + KV], n_kv)
            vh = _heads_major(P[:, BR + KV:BR + 2 * KV], n_kv)
            o, u, *rest = _swaT_fwd(P, kh, vh, sinks_a[j], swa_bias, name=f"swa_fwd{i}", comm=nxt)
            st.update(kh=kh, vh=vh)
        elif kind == 1:
            o, u, rc, *rest = _sbT_fwd(P, BR, name=f"sb_fwd{i}", comm=nxt)
            st.update(rc=rc)
        else:
            o, u, lse, *rest = _foxT_fwd(P, cq, ck, BR, name=f"fox_fwd{i}", comm=nxt)
            st.update(lse=lse)
        if nxt is not None:
            landed(f"mixer{i}", rest[0])
        W_out = full["out", i].reshape(BR, D)
        st.update(o=o, u=u, W_out=W_out)
        y = _matmul(u, W_out, "nn", f32, name=f"out_proj{i}")
        st["y"] = y
        xi = _post_fwd(xi, y, g_post[i:i + 1], name=f"post_norm{i}")
        saved.append(st)

    loss_part, dx = _loss_fwd_bwd(xi, target, name="loss")

    dg_pre, dg_post = [None] * DEPTH, [None] * DEPTH
    dsinks = [None, None]
    db_f = None
    recv = [None] * DEPTH
    core = lax.axis_index("c").astype(jnp.int32).reshape(1)
    pending = None
    for i in reversed(range(DEPTH)):
        kind, j = i % 3, i // 3
        st = saved[i]
        dy, dg_post[i] = _post_bwd(dx, st["y"], g_post[i:i + 1], name=f"post_bwd{i}")
        du = _matmul(dy, st["W_out"], "nt", bf16, name=f"du{i}")
        dW_out = _matmul(st["u"], dy, "tn", bf16, name=f"dw_out{i}")
        P = st["P"]
        if kind == 0:
            dq, dz, dkh, dvh, dsk, *rest = _swaT_bwd(P, st["kh"], st["vh"], st["o"], du, sinks_a[j], swa_bias,
                                                     name=f"swa_bwd{i}", comm=pending)
            dsinks[j] = jnp.sum(dsk.reshape(H, BLOCK), axis=1)
            dP = jnp.concatenate([dq, _heads_minor(dkh).astype(bf16), _heads_minor(dvh).astype(bf16), dz], axis=1)
        elif kind == 1:
            dq, dz, dk, dv, *rest = _sbT_bwd(P, st["o"], du, st["rc"], BR, name=f"sb_bwd{i}", comm=pending)
            dP = jnp.concatenate([dq, dk.astype(bf16), dv.astype(bf16), dz], axis=1)
        else:
            dq, dz, dk, dv, dcq, dck, *rest = _foxT_bwd(P, st["o"], du, st["lse"], st["cq"], st["ck"], BR,
                                                       name=f"fox_bwd{i}", comm=pending)
        if pending is not None:
            recv[i + 1] = rest[0]
        if kind == 2:
            dflT, db_col = _fox_cum_bwd(dcq.reshape(H, S), dck.reshape(H, S), st["flT"], st["bcol"],
                                        name=f"fox_cum_bwd{i}")
            db_f = db_col.reshape(H)
            dfl = jnp.pad(jnp.transpose(dflT), ((0, 0), (0, 128 - H))).astype(bf16)
            dP = jnp.concatenate([dq, dk.astype(bf16), dv.astype(bf16), dz, dfl], axis=1)
        w_t = held_transposed(kind)
        c_shard = w_in[kind].shape[2]
        if w_t:
            dW_in = _matmul(dP, st["h"], "tn", bf16, name=f"dw_in{i}")[:c_shard * N_DEV].reshape(N_DEV, c_shard, D)
        else:
            dW_in = _col_blocks(_matmul(st["h"], dP, "tn", bf16, name=f"dw_in{i}")[:, :c_shard * N_DEV])
        blocks = [dW_in, dW_out.reshape(N_DEV, BR // N_DEV, D)]
        swap = _SiblingSwap(blocks)
        dh_mode = "nn" if w_t else "nt"
        if i > 0:
            dh = _matmul(dP, st["W_in"], dh_mode, f32, name=f"dh{i}")
            dx, dg_pre[i], got = _pre_bwd(dh, st["x"], g_pre[i:i + 1], dx, name=f"pre_bwd{i}", comm=swap)
        else:
            dh, got = _matmul(dP, st["W_in"], dh_mode, f32, name=f"dh{i}", comm=swap)
            dx, dg_pre[i] = _pre_bwd(dh, st["x"], g_pre[i:i + 1], dx, name=f"pre_bwd{i}")
        sums = [_pair_sum(b, g, core, name=f"pair_sum_{t}{i}") for t, b, g in zip(("in", "out"), blocks, got)]
        pending = _ChipScatter(sums)
    recv[0] = _run_comm(pending, name="scatter_dw0")

    small = jnp.concatenate(
        [jnp.concatenate(dg_pre, axis=0).reshape(-1), jnp.concatenate(dg_post, axis=0).reshape(-1),
         jnp.concatenate(dsinks), db_f, loss_part[0, :1]])
    n_small = small.shape[0]
    rows = -(-n_small // 128)
    rows = -(-rows // 8) * 8
    small = jnp.pad(small, (0, rows * 128 - n_small)).reshape(rows, 128)
    total = _all_reduce_small(small, name="reduce_small").reshape(-1)
    o0 = DEPTH * D
    grad_g_pre = total[:o0].reshape(DEPTH, D)
    grad_g_post = total[o0:2 * o0].reshape(DEPTH, D)
    grad_sinks = total[2 * o0:2 * o0 + 2 * H].reshape(2, H)
    grad_b_f = total[2 * o0 + 2 * H:2 * o0 + 3 * H].reshape(1, H)
    loss = total[2 * o0 + 3 * H]

    def small_adam(w, g, m, v, name):
        def body(w_ref, g_ref, m_ref, v_ref, d_ref, nm_ref, nv_ref):
            d_ref[...], nm_ref[...], nv_ref[...] = _adam_math(w_ref[...], g_ref[...], m_ref[...], v_ref[...])
        vm = pl.BlockSpec(memory_space=pltpu.VMEM)
        return pl.pallas_call(body, name=name, in_specs=[vm] * 4, out_specs=[vm] * 3,
                              out_shape=[jax.ShapeDtypeStruct(w.shape, f32)] * 3)(w, g, m, v)

    upd = {}
    upd["g_pre"] = (grad_g_pre,) + tuple(small_adam(g_pre, grad_g_pre, m_g_pre, v_g_pre, "adam_g_pre"))
    upd["g_post"] = (grad_g_post,) + tuple(small_adam(g_post, grad_g_post, m_g_post, v_g_post, "adam_g_post"))
    upd["sinks_a"] = (grad_sinks,) + tuple(small_adam(sinks_a, grad_sinks, m_sinks_a, v_sinks_a, "adam_sinks"))
    upd["b_f_c"] = (grad_b_f,) + tuple(small_adam(b_f_c, grad_b_f, m_b_f_c, v_b_f_c, "adam_b_f"))

    def big(layers, which, w, m, v, tag):
        return _adamw_sum([recv[i][which] for i in layers], w, m, v, name=f"adam_{tag}")

    def big_in(kind, layers, w, m, v, tag):
        if not held_transposed(kind):
            return big(layers, 0, w, m, v, tag)
        flip = lambda a: jnp.swapaxes(a, 1, 2)
        return [flip(t) for t in big(layers, 0, flip(w), flip(m), flip(v), tag)]

    upd["w_in_c"] = big_in(2, (2,), w_in_c, m_w_in_c, v_w_in_c, "in_c")
    upd["w_in_b"] = big_in(1, (1,), w_in_b, m_w_in_b, v_w_in_b, "in_b")
    upd["w_out_b"] = big((1,), 1, w_out_b, m_w_out_b, v_w_out_b, "out_b")
    upd["w_out_c"] = big((2,), 1, w_out_c, m_w_out_c, v_w_out_c, "out_c")
    upd["w_in_a"] = big_in(0, (0, 3), w_in_a, m_w_in_a, v_w_in_a, "in_a")
    upd["w_out_a"] = big((0, 3), 1, w_out_a, m_w_out_a, v_w_out_a, "out_a")

    names = ["g_pre", "g_post", "w_in_a", "w_out_a", "sinks_a", "w_in_b", "w_out_b", "w_in_c", "b_f_c", "w_out_c"]
    return (loss, dx[None], *[upd[k][0] for k in names], *[upd[k][1] for k in names],
            *[upd[k][2] for k in names], *[upd[k][3] for k in names])
```

```python
import functools
import math

import numpy as np
import jax
import jax.numpy as jnp
from jax import lax
from jax.experimental import pallas as pl
from jax.experimental.pallas import tpu as pltpu

HEAD_DIM = 64
HP = 8
HW = HP * HEAD_DIM
BLOCK = 128
NORM_EPS = 1e-6
NEG = -1e30
N_DEV = 8
DEPTH = 4
ADAM_LR, ADAM_B1, ADAM_B2, ADAM_EPS, ADAM_WD, ADAM_STEP = 0.001, 0.9, 0.999, 1e-8, 0.01, 10
VMEM_LIMIT = 56 * 1024 * 1024

bf16 = jnp.bfloat16
f32 = jnp.float32
MESH = pl.DeviceIdType.MESH
ANY = pl.BlockSpec(memory_space=pl.ANY)
SMEM = pl.BlockSpec(memory_space=pltpu.SMEM)

NN = (((1,), (0,)), ((), ()))
NT = (((1,), (1,)), ((), ()))
TN = (((0,), (0,)), ((), ()))


def _dot(a, b, dims=NN):
    return lax.dot_general(a, b, dims, preferred_element_type=f32)


def _params(sem):
    return pltpu.CompilerParams(dimension_semantics=sem, vmem_limit_bytes=VMEM_LIMIT)


def _attn_tile(S):
    return 512 if S % 512 == 0 and S >= 1024 else 128


def _pick(n, pref):
    for t in pref:
        if n % t == 0:
            return t
    return n


def _matmul(a, b, mode, out_dtype, name, comm=None, col_blocks=None):
    if mode == "nn":
        (M, K), (K2, N) = a.shape, b.shape
    elif mode == "nt":
        (M, K), (N, K2) = a.shape, b.shape
    else:
        (K, M), (K2, N) = a.shape, b.shape
    assert K == K2, (a.shape, b.shape, mode)
    tm = _pick(M, (1024, 640, 512, 256, 128))
    tn = _pick(N, (1024, 768, 640, 512, 256, 128))
    if col_blocks is not None:
        tn = N // col_blocks
        assert N % col_blocks == 0 and tn % BLOCK == 0, (N, col_blocks)
    tk = _pick(K, (2048, 1664, 1536, 1024, 512, 640, 256, 128))
    nk = K // tk
    dims = {"nn": NN, "nt": NT, "tn": TN}[mode]

    def body(a_ref, b_ref, o_ref, acc_ref):
        if nk == 1:
            o_ref[...] = _dot(a_ref[...], b_ref[...], dims).astype(o_ref.dtype)
            return
        k = pl.program_id(2)

        @pl.when(k == 0)
        def _():
            acc_ref[...] = jnp.zeros_like(acc_ref)

        acc_ref[...] += _dot(a_ref[...], b_ref[...], dims)

        @pl.when(k == nk - 1)
        def _():
            o_ref[...] = acc_ref[...].astype(o_ref.dtype)

    if mode == "tn":
        a_spec = pl.BlockSpec((tk, tm), lambda i, j, k: (k, i))
    else:
        a_spec = pl.BlockSpec((tm, tk), lambda i, j, k: (i, k))
    if mode == "nt":
        b_spec = pl.BlockSpec((tn, tk), lambda i, j, k: (j, k))
    else:
        b_spec = pl.BlockSpec((tk, tn), lambda i, j, k: (k, j))
    if col_blocks is None:
        o_spec, o_shape = pl.BlockSpec((tm, tn), lambda i, j, k: (i, j)), (M, N)
    else:
        o_spec, o_shape = pl.BlockSpec((None, tm, tn), lambda i, j, k: (j, i, 0)), (col_blocks, M, tn)
    res = _pallas(
        body, (a, b), name=name,
        grid=(M // tm, N // tn, nk),
        in_specs=[a_spec, b_spec],
        out_specs=[o_spec],
        out_shape=[jax.ShapeDtypeStruct(o_shape, out_dtype)],
        scratch_shapes=[pltpu.VMEM((tm, tn) if nk > 1 else (8, 128), f32)],
        semantics=("parallel", "parallel", "arbitrary"), comm=comm)
    return res[0] if comm is None else (res[0], res[1])


ROWS = 256


def _rows(S):
    return ROWS if S % ROWS == 0 else S


def _rmsnorm_fwd(x, g, name, comm=None):
    S, D = x.shape
    tr = _rows(S)

    def body(x_ref, g_ref, h_ref):
        xv = x_ref[...]
        r = lax.rsqrt(jnp.mean(xv * xv, axis=-1, keepdims=True) + NORM_EPS)
        h_ref[...] = (xv * r * g_ref[...]).astype(h_ref.dtype)

    res = _pallas(
        body, (x, g), name=name, grid=(S // tr,),
        in_specs=[pl.BlockSpec((tr, D), lambda i: (i, 0)), pl.BlockSpec((1, D), lambda i: (0, 0))],
        out_specs=[pl.BlockSpec((tr, D), lambda i: (i, 0))],
        out_shape=[jax.ShapeDtypeStruct((S, D), bf16)],
        semantics=("parallel",), comm=comm)
    return res[0] if comm is None else (res[0], res[1])


def _post_fwd(x, y, g, name):
    S, D = x.shape
    tr = _rows(S)

    def body(x_ref, y_ref, g_ref, o_ref):
        yv = y_ref[...]
        r = lax.rsqrt(jnp.mean(yv * yv, axis=-1, keepdims=True) + NORM_EPS)
        o_ref[...] = x_ref[...] + yv * r * g_ref[...]

    row = pl.BlockSpec((tr, D), lambda i: (i, 0))
    return pl.pallas_call(
        body, name=name, grid=(S // tr,),
        in_specs=[row, row, pl.BlockSpec((1, D), lambda i: (0, 0))],
        out_specs=row,
        out_shape=jax.ShapeDtypeStruct((S, D), f32),
        compiler_params=_params(("parallel",)),
    )(x, y, g)


def _loss_fwd_bwd(y, t, name):
    S, D = y.shape
    tr = _rows(S)

    def body(y_ref, t_ref, l_ref, d_ref):
        @pl.when(pl.program_id(0) == 0)
        def _():
            l_ref[...] = jnp.zeros_like(l_ref)

        e = y_ref[...] - t_ref[...]
        d_ref[...] = e * (1.0 / D)
        part = 0.5 * jnp.sum(jnp.sum(e * e, axis=-1, keepdims=True) * (1.0 / D), axis=0, keepdims=True)
        l_ref[...] += jnp.broadcast_to(part, l_ref.shape)

    row = pl.BlockSpec((tr, D), lambda i: (i, 0))
    return pl.pallas_call(
        body, name=name, grid=(S // tr,),
        in_specs=[row, row],
        out_specs=[pl.BlockSpec((8, 128), lambda i: (0, 0)), row],
        out_shape=[jax.ShapeDtypeStruct((8, 128), f32), jax.ShapeDtypeStruct((S, D), f32)],
        compiler_params=_params(("arbitrary",)),
    )(y, t)


def _post_bwd(dxn, y, g, name):
    S, D = y.shape
    tr = _rows(S)

    def body(d_ref, y_ref, g_ref, dy_ref, dg_ref):
        @pl.when(pl.program_id(0) == 0)
        def _():
            dg_ref[...] = jnp.zeros_like(dg_ref)

        yv = y_ref[...]
        d = d_ref[...]
        r = lax.rsqrt(jnp.mean(yv * yv, axis=-1, keepdims=True) + NORM_EPS)
        n = yv * r
        dn = d * g_ref[...]
        dg_ref[...] += jnp.sum(d * n, axis=0, keepdims=True)
        dy_ref[...] = (r * (dn - n * jnp.mean(dn * n, axis=-1, keepdims=True))).astype(dy_ref.dtype)

    row = pl.BlockSpec((tr, D), lambda i: (i, 0))
    vec = pl.BlockSpec((1, D), lambda i: (0, 0))
    return pl.pallas_call(
        body, name=name, grid=(S // tr,),
        in_specs=[row, row, vec],
        out_specs=[row, vec],
        out_shape=[jax.ShapeDtypeStruct((S, D), bf16), jax.ShapeDtypeStruct((1, D), f32)],
        compiler_params=_params(("arbitrary",)),
    )(dxn, y, g)


def _pre_bwd(dh, x, g, dres, name, comm=None):
    S, D = x.shape
    tr = _rows(S)

    def body(dh_ref, x_ref, g_ref, dres_ref, dx_ref, dg_ref):
        @pl.when(pl.program_id(0) == 0)
        def _():
            dg_ref[...] = jnp.zeros_like(dg_ref)

        xv = x_ref[...]
        d = dh_ref[...]
        r = lax.rsqrt(jnp.mean(xv * xv, axis=-1, keepdims=True) + NORM_EPS)
        n = xv * r
        dn = d * g_ref[...]
        dg_ref[...] += jnp.sum(d * n, axis=0, keepdims=True)
        dx_ref[...] = dres_ref[...] + r * (dn - n * jnp.mean(dn * n, axis=-1, keepdims=True))

    row = pl.BlockSpec((tr, D), lambda i: (i, 0))
    vec = pl.BlockSpec((1, D), lambda i: (0, 0))
    return _pallas(
        body, (dh, x, g, dres), name=name, grid=(S // tr,),
        in_specs=[row, row, vec, row],
        out_specs=[row, vec],
        out_shape=[jax.ShapeDtypeStruct((S, D), f32), jax.ShapeDtypeStruct((1, D), f32)],
        semantics=("arbitrary",), comm=comm)


def _sigmoid(x):
    return 1.0 / (1.0 + jnp.exp(-x))


def _gate_fwd(o, z):
    zf = z.astype(f32)
    return o * (zf * _sigmoid(zf))


def _gate_bwd(du, o, z):
    zf = z.astype(f32)
    sig = _sigmoid(zf)
    do = du * (zf * sig)
    dz = du * o * (sig * (1.0 + zf * (1.0 - sig)))
    return do, dz


def _iota2(shape, dim):
    return lax.broadcasted_iota(jnp.int32, shape, dim)


def _swa_specs(S, BR, KV, G):
    gw = G * HEAD_DIM
    qspec = pl.BlockSpec((BLOCK, gw), lambda h, n: (n, h))
    zoff = (BR + 2 * KV) // gw
    zspec = pl.BlockSpec((BLOCK, gw), lambda h, n: (n, zoff + h))
    cur = pl.BlockSpec((1, BLOCK, HEAD_DIM), lambda h, n: (h, n, 0))
    prev = pl.BlockSpec((1, BLOCK, HEAD_DIM), lambda h, n: (h, jnp.maximum(n - 1, 0), 0))
    return qspec, zspec, cur, prev


def _split3(x):
    x1 = x.astype(bf16)
    r1 = x - x1.astype(f32)
    x2 = r1.astype(bf16)
    x3 = (r1 - x2.astype(f32)).astype(bf16)
    return x1, x2, x3


def _split2(x):
    x1 = x.astype(bf16)
    return x1, (x - x1.astype(f32)).astype(bf16)


def _pair_specs(S, BR, T, kmap):
    nb = BR // HW
    q = pl.BlockSpec((T, HW), lambda p, i, j: (i, p))
    k = pl.BlockSpec((T, HW), lambda p, i, j: (kmap(i, j), nb + p))
    v = pl.BlockSpec((T, HW), lambda p, i, j: (kmap(i, j), 2 * nb + p))
    z = pl.BlockSpec((T, HW), lambda p, i, j: (i, 3 * nb + p))
    return q, k, v, z


def _fox_cum(flT, b, name):
    H, S = flT.shape
    tj = _pick(S, (256, 128))

    def body(fl_ref, b_ref, c_ref):
        j = pl.program_id(0)
        x = fl_ref[...] + b_ref[...]
        logf = jnp.minimum(x, 0.0) - jnp.log(1.0 + jnp.exp(-jnp.abs(x)))
        tri = (_iota2((S, tj), 0) <= j * tj + _iota2((S, tj), 1)).astype(bf16)
        c_ref[...] = sum(_dot(part, tri) for part in _split3(logf))

    return pl.pallas_call(
        body, name=name, grid=(S // tj,),
        in_specs=[pl.BlockSpec((H, S), lambda j: (0, 0)), pl.BlockSpec((H, 1), lambda j: (0, 0))],
        out_specs=pl.BlockSpec((H, tj), lambda j: (0, j)),
        out_shape=jax.ShapeDtypeStruct((H, S), f32),
        compiler_params=_params(("parallel",)),
    )(flT, b)


def _fox_cum_bwd(dcq, dck, flT, b, name):
    H, S = flT.shape
    tj = _pick(S, (256, 128))

    def body(dcq_ref, dck_ref, fl_ref, b_ref, o_ref, db_ref):
        j = pl.program_id(0)

        @pl.when(j == 0)
        def _():
            db_ref[...] = jnp.zeros_like(db_ref)

        tri = (_iota2((S, tj), 0) >= j * tj + _iota2((S, tj), 1)).astype(bf16)
        dlogf = sum(_dot(part, tri) for part in _split3(dcq_ref[...] - dck_ref[...]))
        x = fl_ref[...] + b_ref[...]
        dfl = dlogf * _sigmoid(-x)
        o_ref[...] = dfl
        db_ref[...] += jnp.sum(dfl, axis=-1, keepdims=True)

    blk = pl.BlockSpec((H, tj), lambda j: (0, j))
    whole = pl.BlockSpec((H, S), lambda j: (0, 0))
    col = pl.BlockSpec((H, 1), lambda j: (0, 0))
    return pl.pallas_call(
        body, name=name, grid=(S // tj,),
        in_specs=[whole, whole, blk, col],
        out_specs=[blk, col],
        out_shape=[jax.ShapeDtypeStruct((H, S), f32), jax.ShapeDtypeStruct((H, 1), f32)],
        compiler_params=_params(("arbitrary",)),
    )(dcq, dck, flT, b)


def _hsl(hh):
    return slice(hh * HEAD_DIM, (hh + 1) * HEAD_DIM)


def _scaled(q, scale):
    assert math.log2(scale).is_integer()
    return (q.astype(f32) * scale).astype(bf16)


def _swa_bias(H, n_kv):
    G = H // n_kv
    slopes = 2.0 ** (-8.0 * np.arange(1, H + 1, dtype=np.float32) / H)
    dist = (np.arange(BLOCK)[None, :] + BLOCK - np.arange(2 * BLOCK)[:, None]).astype(np.float32)
    valid = (dist >= 0) & (dist < BLOCK)
    out = np.empty((n_kv, 2 * BLOCK, G * BLOCK), np.float32)
    for h in range(H):
        out[h // G][:, (h % G) * BLOCK:(h % G + 1) * BLOCK] = np.where(valid, -(slopes[h] * dist), np.float32(NEG))
    return jnp.asarray(out)


def _swaT_parts(q_ref, kc_ref, kp_ref, vc_ref, vp_ref, bias_ref, sink_ref, kvh, n, G, scale):
    qg = jnp.concatenate([q_ref[:, _hsl(g)] for g in range(G)], axis=0)
    kband = jnp.concatenate([kp_ref[0], kc_ref[0]], axis=0)
    vband = jnp.concatenate([vp_ref[0], vc_ref[0]], axis=0)
    s = _dot(kband, qg, NT) * scale + bias_ref[0]
    s = jnp.where(_iota2(s.shape, 0) + jnp.where(n > 0, BLOCK, 0) >= BLOCK, s, NEG)
    sink = jnp.concatenate([jnp.full((1, BLOCK), sink_ref[kvh * G + g], f32) for g in range(G)], axis=1)
    m = jnp.maximum(jnp.max(s, axis=0, keepdims=True), sink)
    p = jnp.exp(s - m)
    ps = jnp.exp(sink - m)
    den = jnp.sum(p, axis=0, keepdims=True) + ps
    return qg, kband, vband, p, ps, den


def _swaT_specs(S, BR, KV, G):
    qspec, zspec, cur, prev = _swa_specs(S, BR, KV, G)
    bias = pl.BlockSpec((1, 2 * BLOCK, G * BLOCK), lambda h, n: (h, 0, 0))
    return qspec, zspec, cur, prev, bias


def _swaT_fwd(P, kh, vh, sinks, bias, name, comm=None):
    S = P.shape[0]
    n_kv = kh.shape[0]
    H = sinks.shape[0]
    G = H // n_kv
    BR, KV = H * HEAD_DIM, n_kv * HEAD_DIM
    scale = HEAD_DIM ** -0.5
    assert (BR + 2 * KV) % (G * HEAD_DIM) == 0

    def body(q_ref, z_ref, kc_ref, kp_ref, vc_ref, vp_ref, bias_ref, sink_ref, o_ref, u_ref):
        kvh, n = pl.program_id(0), pl.program_id(1)
        _, _, vband, p, _, den = _swaT_parts(q_ref, kc_ref, kp_ref, vc_ref, vp_ref, bias_ref, sink_ref, kvh, n, G, scale)
        o = jnp.transpose(_dot(vband, p.astype(bf16), TN) / den)
        for g in range(G):
            oh = o[g * BLOCK:(g + 1) * BLOCK]
            o_ref[:, _hsl(g)] = oh.astype(o_ref.dtype)
            u_ref[:, _hsl(g)] = _gate_fwd(oh, z_ref[:, _hsl(g)]).astype(u_ref.dtype)

    qspec, zspec, cur, prev, bspec = _swaT_specs(S, BR, KV, G)
    ospec = pl.BlockSpec((BLOCK, G * HEAD_DIM), lambda h, n: (n, h))
    return _pallas(
        body, (P, P, kh, kh, vh, vh, bias, sinks), name=name, grid=(n_kv, S // BLOCK),
        in_specs=[qspec, zspec, cur, prev, cur, prev, bspec, SMEM],
        out_specs=[ospec, ospec],
        out_shape=[jax.ShapeDtypeStruct((S, BR), bf16)] * 2,
        semantics=("parallel", "parallel"), comm=comm)


def _swaT_bwd(P, kh, vh, o, du, sinks, bias, name, comm=None):
    S = P.shape[0]
    n_kv = kh.shape[0]
    H = sinks.shape[0]
    G = H // n_kv
    BR, KV = H * HEAD_DIM, n_kv * HEAD_DIM
    scale = HEAD_DIM ** -0.5

    def body(q_ref, z_ref, kc_ref, kp_ref, vc_ref, vp_ref, o_ref, du_ref, bias_ref, sink_ref,
             dq_ref, dz_ref, dk_ref, dv_ref, ds_ref):
        kvh, n = pl.program_id(0), pl.program_id(1)

        @pl.when(n == 0)
        def _():
            dk_ref[...] = jnp.zeros_like(dk_ref)
            dv_ref[...] = jnp.zeros_like(dv_ref)
            ds_ref[...] = jnp.zeros_like(ds_ref)

        dos, prods = [], []
        for g in range(G):
            oh = o_ref[:, _hsl(g)].astype(f32)
            doh, dzh = _gate_bwd(du_ref[:, _hsl(g)].astype(f32), oh, z_ref[:, _hsl(g)])
            dz_ref[:, _hsl(g)] = dzh.astype(dz_ref.dtype)
            dos.append(doh.astype(bf16))
            prods.append(doh * oh)
        dog = jnp.concatenate(dos, axis=0)
        delta = jnp.sum(jnp.transpose(jnp.concatenate(prods, axis=0)), axis=0, keepdims=True)
        qg, kband, vband, p, ps, den = _swaT_parts(q_ref, kc_ref, kp_ref, vc_ref, vp_ref, bias_ref, sink_ref,
                                                   kvh, n, G, scale)
        inv = 1.0 / den
        pn = p * inv
        ds = (pn * (_dot(vband, dog, NT) - delta)).astype(bf16)
        dq = jnp.transpose(_dot(kband, ds, TN)) * scale
        for g in range(G):
            dq_ref[:, _hsl(g)] = dq[g * BLOCK:(g + 1) * BLOCK].astype(dq_ref.dtype)
        dkb = _dot(ds, qg) * scale
        dvb = _dot(pn.astype(bf16), dog)
        ds_ref[0] -= ps * inv * delta
        row_c = pl.multiple_of(n * BLOCK, BLOCK)
        dk_ref[0, pl.ds(row_c, BLOCK), :] += dkb[BLOCK:]
        dv_ref[0, pl.ds(row_c, BLOCK), :] += dvb[BLOCK:]

        @pl.when(n > 0)
        def _():
            row_p = pl.multiple_of((n - 1) * BLOCK, BLOCK)
            dk_ref[0, pl.ds(row_p, BLOCK), :] += dkb[:BLOCK]
            dv_ref[0, pl.ds(row_p, BLOCK), :] += dvb[:BLOCK]

    qspec, zspec, cur, prev, bspec = _swaT_specs(S, BR, KV, G)
    ospec = pl.BlockSpec((BLOCK, G * HEAD_DIM), lambda h, n: (n, h))
    full = pl.BlockSpec((1, S, HEAD_DIM), lambda h, n: (h, 0, 0))
    return _pallas(
        body, (P, P, kh, kh, vh, vh, o, du, bias, sinks), name=name, grid=(n_kv, S // BLOCK),
        in_specs=[qspec, zspec, cur, prev, cur, prev, ospec, ospec, bspec, SMEM],
        out_specs=[ospec, ospec, full, full, pl.BlockSpec((1, 1, G * BLOCK), lambda h, n: (h, 0, 0))],
        out_shape=[jax.ShapeDtypeStruct((S, BR), bf16)] * 2
        + [jax.ShapeDtypeStruct((n_kv, S, HEAD_DIM), f32)] * 2
        + [jax.ShapeDtypeStruct((n_kv, 1, G * BLOCK), f32)],
        semantics=("parallel", "arbitrary"), comm=comm)


def _foxT_fwd(P, cq, ck, BR, name, comm=None):
    S = P.shape[0]
    H = BR // HEAD_DIM
    T = _attn_tile(S)
    nq = S // T
    scale = HEAD_DIM ** -0.5
    kmap = lambda i, j: jnp.minimum(i, j)

    def body(q_ref, k_ref, v_ref, z_ref, cq_ref, ck_ref, o_ref, u_ref, lse_ref, m_s, l_s, acc_s):
        qb, kb = pl.program_id(1), pl.program_id(2)

        @pl.when(kb == 0)
        def _():
            m_s[...] = jnp.full_like(m_s, NEG)
            l_s[...] = jnp.zeros_like(l_s)
            acc_s[...] = jnp.zeros_like(acc_s)

        def tile(diagonal):
            for hh in range(HP):
                sl = _hsl(hh)
                s = _dot(k_ref[:, sl], _scaled(q_ref[:, sl], scale), NT) + cq_ref[hh] - ck_ref[hh]
                if diagonal:
                    s = jnp.where(_iota2((T, T), 0) <= _iota2((T, T), 1), s, NEG)
                m_old = m_s[hh]
                m_new = jnp.maximum(m_old, jnp.max(s, axis=0, keepdims=True))
                alpha = jnp.exp(m_old - m_new)
                p = jnp.exp(s - m_new)
                l_s[hh] = alpha * l_s[hh] + jnp.sum(p, axis=0, keepdims=True)
                acc_s[hh] = alpha * acc_s[hh] + _dot(v_ref[:, sl], p.astype(bf16), TN)
                m_s[hh] = m_new

        @pl.when(kb < qb)
        def _():
            tile(False)

        @pl.when(kb == qb)
        def _():
            tile(True)
            for hh in range(HP):
                sl = _hsl(hh)
                oh = jnp.transpose(acc_s[hh] / l_s[hh])
                o_ref[:, sl] = oh.astype(o_ref.dtype)
                u_ref[:, sl] = _gate_fwd(oh, z_ref[:, sl]).astype(u_ref.dtype)
                lse_ref[hh] = m_s[hh] + jnp.log(l_s[hh])

    q, k, v, z = _pair_specs(S, BR, T, kmap)
    rowq = pl.BlockSpec((HP, 1,T), lambda p, i, j: (p, 0, i))
    colk = pl.BlockSpec((HP, T,1), lambda p, i, j: (p, kmap(i, j), 0))
    ospec = pl.BlockSpec((T, HW), lambda p, i, j: (i, p))
    return _pallas(
        body, (P, P, P, P, cq, ck), name=name, grid=(H // HP, nq, nq),
        in_specs=[q, k, v, z, rowq, colk],
        out_specs=[ospec, ospec, rowq],
        out_shape=[jax.ShapeDtypeStruct((S, BR), bf16)] * 2 + [jax.ShapeDtypeStruct((H, 1, S), f32)],
        scratch_shapes=[pltpu.VMEM((HP, 1,T), f32), pltpu.VMEM((HP, 1,T), f32), pltpu.VMEM((HP, HEAD_DIM,T), f32)],
        semantics=("parallel", "arbitrary", "arbitrary"), comm=comm)


def _foxT_bwd(P, o, du, lse, cq, ck, BR, name, comm=None):
    S = P.shape[0]
    H = BR // HEAD_DIM
    T = _attn_tile(S)
    nq = S // T
    scale = HEAD_DIM ** -0.5
    kmap = lambda i, j: jnp.minimum(i, j)

    def body(q_ref, k_ref, v_ref, z_ref, o_ref, du_ref, lse_ref, cq_ref, ck_ref,
             dq_ref, dz_ref, dk_ref, dv_ref, dcq_ref, dck_ref, do_s, delta_s, dq_s, dcq_s):
        qb, kb = pl.program_id(1), pl.program_id(2)

        @pl.when(jnp.logical_and(qb == 0, kb == 0))
        def _():
            dk_ref[...] = jnp.zeros_like(dk_ref)
            dv_ref[...] = jnp.zeros_like(dv_ref)
            dck_ref[...] = jnp.zeros_like(dck_ref)

        @pl.when(kb == 0)
        def _():
            for hh in range(HP):
                sl = _hsl(hh)
                oh = o_ref[:, sl].astype(f32)
                doh, dzh = _gate_bwd(du_ref[:, sl].astype(f32), oh, z_ref[:, sl])
                dz_ref[:, sl] = dzh.astype(dz_ref.dtype)
                do_s[hh] = doh.astype(bf16)
                delta_s[hh] = jnp.sum(jnp.transpose(doh * oh), axis=0, keepdims=True)
            dq_s[...] = jnp.zeros_like(dq_s)
            dcq_s[...] = jnp.zeros_like(dcq_s)

        def tile(diagonal):
            rows = pl.ds(pl.multiple_of(kb * T, T), T)
            for hh in range(HP):
                sl = _hsl(hh)
                qh, kh, vh, dob = _scaled(q_ref[:, sl], scale), k_ref[:, sl], v_ref[:, sl], do_s[hh]
                s = _dot(kh, qh, NT) + cq_ref[hh] - ck_ref[hh]
                p = jnp.exp(s - lse_ref[hh])
                if diagonal:
                    p = jnp.where(_iota2((T, T), 0) <= _iota2((T, T), 1), p, 0.0)
                ds = p * (_dot(vh, dob, NT) - delta_s[hh])
                dsb = ds.astype(bf16)
                dq_s[hh] += _dot(kh, dsb, TN)
                dk_ref[rows, sl] += _dot(dsb, qh)
                dv_ref[rows, sl] += _dot(p.astype(bf16), dob)
                dcq_s[hh] += jnp.sum(ds, axis=0, keepdims=True)
                dck_ref[hh, rows, :] += jnp.sum(ds, axis=1, keepdims=True)

        @pl.when(kb < qb)
        def _():
            tile(False)

        @pl.when(kb == qb)
        def _():
            tile(True)
            for hh in range(HP):
                dq_ref[:, _hsl(hh)] = (jnp.transpose(dq_s[hh]) * scale).astype(dq_ref.dtype)
                dcq_ref[hh] = dcq_s[hh]

    q, k, v, z = _pair_specs(S, BR, T, kmap)
    rowq = pl.BlockSpec((HP, 1,T), lambda p, i, j: (p, 0, i))
    colk = pl.BlockSpec((HP, T,1), lambda p, i, j: (p, kmap(i, j), 0))
    ospec = pl.BlockSpec((T, HW), lambda p, i, j: (i, p))
    full = pl.BlockSpec((S, HW), lambda p, i, j: (0, p))
    return _pallas(
        body, (P, P, P, P, o, du, lse, cq, ck), name=name, grid=(H // HP, nq, nq),
        in_specs=[q, k, v, z, ospec, ospec, rowq, rowq, colk],
        out_specs=[ospec, ospec, full, full, rowq, pl.BlockSpec((HP, S,1), lambda p, i, j: (p, 0, 0))],
        out_shape=[jax.ShapeDtypeStruct((S, BR), bf16)] * 2 + [jax.ShapeDtypeStruct((S, BR), f32)] * 2
        + [jax.ShapeDtypeStruct((H, 1, S), f32), jax.ShapeDtypeStruct((H, S, 1), f32)],
        scratch_shapes=[pltpu.VMEM((HP, T,HEAD_DIM), bf16), pltpu.VMEM((HP, 1,T), f32),
                        pltpu.VMEM((HP, HEAD_DIM,T), f32), pltpu.VMEM((HP, 1,T), f32)],
        semantics=("parallel", "arbitrary", "arbitrary"), comm=comm)


def _sbT_logs(kc, qh, before):
    l = _dot(kc, qh, NT)
    minus_abs = lax.bitcast_convert_type(lax.bitcast_convert_type(l, jnp.int32) | jnp.int32(-2 ** 31), f32)
    lsig = jnp.minimum(l, 0.0) - jnp.log(1.0 + jnp.exp(minus_abs))
    lf = lsig - l
    if before is not None:
        lf = jnp.where(before, lf, 0.0)
    return lsig, lf


def _chunk_rows(c):
    return slice(c * BLOCK, (c + 1) * BLOCK)


def _sbT_suffix_tile(lf, tri, nc, two_pass):
    if not two_pass:
        hi = lf.astype(bf16)
        return jnp.concatenate([_dot(tri, hi[_chunk_rows(c)]) for c in range(nc)], axis=0)
    hi, lo = _split2(lf)
    tri2 = jnp.concatenate([tri, tri], axis=1)
    return jnp.concatenate(
        [_dot(tri2, jnp.concatenate([hi[_chunk_rows(c)], lo[_chunk_rows(c)]], axis=0)) for c in range(nc)], axis=0)


def _sbT_fwd(P, BR, name, comm=None):
    S = P.shape[0]
    H = BR // HEAD_DIM
    T = _attn_tile(S)
    nq = S // T
    nc = T // BLOCK
    scale = HEAD_DIM ** -0.5
    kmap = lambda i, j: jnp.maximum(i - j, 0)

    def body(q_ref, k_ref, v_ref, z_ref, o_ref, u_ref, rc_ref, r_s, acc_s):
        qb, j = pl.program_id(1), pl.program_id(2)

        @pl.when(j == 0)
        def _():
            r_s[...] = jnp.zeros_like(r_s)
            acc_s[...] = jnp.zeros_like(acc_s)

        def tile(diagonal):
            ii, jj = _iota2((BLOCK, BLOCK), 0), _iota2((BLOCK, BLOCK), 1)
            tri = (jj > ii).astype(bf16)
            before = (_iota2((T, T), 0) < _iota2((T, T), 1)) if diagonal else None
            for hh in range(HP):
                sl = _hsl(hh)
                lsig, lf = _sbT_logs(k_ref[:, sl], _scaled(q_ref[:, sl], scale), before)
                x = lsig + _sbT_suffix_tile(lf, tri, nc, two_pass=True)
                r = r_s[hh]
                rc_ref[hh, 0] = r
                parts = [None] * nc
                for c in reversed(range(nc)):
                    parts[c] = jnp.exp(x[_chunk_rows(c)] + r)
                    r = r + jnp.sum(lf[_chunk_rows(c)], axis=0, keepdims=True)
                r_s[hh] = r
                a = jnp.concatenate(parts, axis=0)
                if diagonal:
                    a = jnp.where(before, a, 0.0)
                acc_s[hh] += _dot(v_ref[:, sl], a.astype(bf16), TN)

        @pl.when(j > 0)
        def _():
            @pl.when(j <= qb)
            def _():
                tile(False)

        @pl.when(j == 0)
        def _():
            tile(True)

        @pl.when(j == qb)
        def _():
            for hh in range(HP):
                sl = _hsl(hh)
                oh = jnp.transpose(acc_s[hh])
                o_ref[:, sl] = oh.astype(o_ref.dtype)
                u_ref[:, sl] = _gate_fwd(oh, z_ref[:, sl]).astype(u_ref.dtype)

    q, k, v, z = _pair_specs(S, BR, T, kmap)
    ospec = pl.BlockSpec((T, HW), lambda p, i, j: (i, p))
    rc = pl.BlockSpec((HP, 1,1, T), lambda p, i, j: (p, kmap(i, j), 0, i))
    return _pallas(
        body, (P, P, P, P), name=name, grid=(H // HP, nq, nq),
        in_specs=[q, k, v, z],
        out_specs=[ospec, ospec, rc],
        out_shape=[jax.ShapeDtypeStruct((S, BR), bf16)] * 2 + [jax.ShapeDtypeStruct((H, nq, 1, S), f32)],
        scratch_shapes=[pltpu.VMEM((HP, 1,T), f32), pltpu.VMEM((HP, HEAD_DIM,T), f32)],
        semantics=("parallel", "arbitrary", "arbitrary"), comm=comm)


def _sbT_bwd(P, o, du, rc, BR, name, comm=None):
    S = P.shape[0]
    H = BR // HEAD_DIM
    T = _attn_tile(S)
    nq = S // T
    nc = T // BLOCK
    scale = HEAD_DIM ** -0.5
    kmap = lambda i, j: jnp.minimum(i, j)

    def body(q_ref, k_ref, v_ref, z_ref, o_ref, du_ref, rc_ref,
             dq_ref, dz_ref, dk_ref, dv_ref, do_s, dq_s, g_s):
        qb, kb = pl.program_id(1), pl.program_id(2)

        @pl.when(jnp.logical_and(qb == 0, kb == 0))
        def _():
            dk_ref[...] = jnp.zeros_like(dk_ref)
            dv_ref[...] = jnp.zeros_like(dv_ref)

        @pl.when(kb == 0)
        def _():
            for hh in range(HP):
                sl = _hsl(hh)
                doh, dzh = _gate_bwd(du_ref[:, sl].astype(f32), o_ref[:, sl].astype(f32), z_ref[:, sl])
                dz_ref[:, sl] = dzh.astype(dz_ref.dtype)
                do_s[hh] = doh.astype(bf16)
            dq_s[...] = jnp.zeros_like(dq_s)
            g_s[...] = jnp.zeros_like(g_s)

        def tile(diagonal):
            ii, jj = _iota2((BLOCK, BLOCK), 0), _iota2((BLOCK, BLOCK), 1)
            tri_suffix = (jj > ii).astype(bf16)
            tri_prefix = (jj < ii).astype(bf16)
            before = (_iota2((T, T), 0) < _iota2((T, T), 1)) if diagonal else None
            out_rows = pl.ds(pl.multiple_of(kb * T, T), T)
            for hh in range(HP):
                sl = _hsl(hh)
                qh, kh, dob = _scaled(q_ref[:, sl], scale), k_ref[:, sl], do_s[hh]
                lsig, lf = _sbT_logs(kh, qh, before)
                x = lsig + _sbT_suffix_tile(lf, tri_suffix, nc, two_pass=False)
                r = rc_ref[hh, 0]
                parts = [None] * nc
                for c in reversed(range(nc)):
                    parts[c] = jnp.exp(x[_chunk_rows(c)] + r)
                    r = r + jnp.sum(lf[_chunk_rows(c)], axis=0, keepdims=True)
                a = jnp.concatenate(parts, axis=0)
                if diagonal:
                    a = jnp.where(before, a, 0.0)
                g = a * _dot(v_ref[:, sl], dob, NT)
                gb = g.astype(bf16)
                gsum = g_s[hh]
                for c in range(nc):
                    parts[c] = _dot(tri_prefix, gb[_chunk_rows(c)]) + gsum
                    gsum = gsum + jnp.sum(g[_chunk_rows(c)], axis=0, keepdims=True)
                g_s[hh] = gsum
                dl = g - (g + jnp.concatenate(parts, axis=0)) * jnp.exp(lsig)
                if diagonal:
                    dl = jnp.where(before, dl, 0.0)
                dl = dl.astype(bf16)
                dq_s[hh] += _dot(kh, dl, TN)
                dk_ref[out_rows, sl] += _dot(dl, qh)
                dv_ref[out_rows, sl] += _dot(a.astype(bf16), dob)

        @pl.when(kb < qb)
        def _():
            tile(False)

        @pl.when(kb == qb)
        def _():
            tile(True)
            for hh in range(HP):
                dq_ref[:, _hsl(hh)] = (jnp.transpose(dq_s[hh]) * scale).astype(dq_ref.dtype)

    q, k, v, z = _pair_specs(S, BR, T, kmap)
    ospec = pl.BlockSpec((T, HW), lambda p, i, j: (i, p))
    full = pl.BlockSpec((S, HW), lambda p, i, j: (0, p))
    rcs = pl.BlockSpec((HP, 1,1, T), lambda p, i, j: (p, kmap(i, j), 0, i))
    return _pallas(
        body, (P, P, P, P, o, du, rc), name=name, grid=(H // HP, nq, nq),
        in_specs=[q, k, v, z, ospec, ospec, rcs],
        out_specs=[ospec, ospec, full, full],
        out_shape=[jax.ShapeDtypeStruct((S, BR), bf16)] * 2 + [jax.ShapeDtypeStruct((S, BR), f32)] * 2,
        scratch_shapes=[pltpu.VMEM((HP, T,HEAD_DIM), bf16), pltpu.VMEM((HP, HEAD_DIM,T), f32),
                        pltpu.VMEM((HP, 1,T), f32)],
        semantics=("parallel", "arbitrary", "arbitrary"), comm=comm)


def _slab(R, C):
    tr = _pick(R, (256, 192, 128))
    if tr != R or R % 8 == 0:
        return tr, C
    return R, _pick(C, (256, 128))


def _adam_math(w, g, m, v):
    m = ADAM_B1 * m + (1.0 - ADAM_B1) * g
    v = ADAM_B2 * v + (1.0 - ADAM_B2) * (g * g)
    m_hat = m / (1.0 - ADAM_B1 ** ADAM_STEP)
    v_hat = v / (1.0 - ADAM_B2 ** ADAM_STEP)
    delta = -ADAM_LR * (m_hat / (jnp.sqrt(v_hat) + ADAM_EPS) + ADAM_WD * w)
    return delta, m, v


def _adamw_sum(parts, w, m, v, name):
    L, R, C = w.shape
    assert len(parts) == L
    n_parts = parts[0].shape[0]
    tr, tc = _slab(R, C)

    def body(*refs):
        p_refs = refs[:L]
        w_ref, m_ref, v_ref, g_ref, d_ref, nm_ref, nv_ref = refs[L:]
        layer = pl.program_id(0)
        g = None
        for l in range(L):
            gl = p_refs[l][0].astype(f32)
            for i in range(1, n_parts):
                gl = gl + p_refs[l][i].astype(f32)
            g = gl if g is None else jnp.where(layer == l, gl, g)
        g_ref[0] = g
        d_ref[0], nm_ref[0], nv_ref[0] = _adam_math(w_ref[0], g, m_ref[0], v_ref[0])

    blk = pl.BlockSpec((1, tr, tc), lambda l, i, j: (l, i, j))
    return _pallas(
        body, (*parts, w, m, v), name=name, grid=(L, R // tr, C // tc),
        in_specs=[pl.BlockSpec((n_parts, tr, tc), lambda l, i, j: (0, i, j))] * L + [blk, blk, blk],
        out_specs=[blk] * 4,
        out_shape=[jax.ShapeDtypeStruct((L, R, C), f32)] * 4,
        semantics=("parallel", "parallel", "parallel"))


def _coords():
    return lax.axis_index("x"), lax.axis_index("y"), lax.axis_index("c")


class _Gather:
    def __init__(self, shards):
        self.args = list(shards)
        self.n = len(shards)
        self.out_shape = [jax.ShapeDtypeStruct((N_DEV,) + s.shape, s.dtype) for s in shards]
        self.scratch = [pltpu.SemaphoreType.DMA((self.n, 7)), pltpu.SemaphoreType.DMA((self.n, 7)),
                        pltpu.SemaphoreType.DMA((self.n,))]

    def _ctx(self, ins, outs, sems):
        send_sems, recv_sems, local_sems = sems
        x, y, c = _coords()
        me, sibling = (x, y, c), (x, y, 1 - c)
        chips = [(1 - x, y), (x, 1 - y), (1 - x, 1 - y)]

        def slot(out, dev):
            return out.at[4 * dev[0] + 2 * dev[1] + dev[2]]

        def copy(a, k, block, to, src=None):
            return pltpu.make_async_remote_copy(
                src_ref=slot(outs[a], block) if src is None else src, dst_ref=slot(outs[a], block),
                send_sem=send_sems.at[a, k], recv_sem=recv_sems.at[a, k], device_id=to, device_id_type=MESH)

        mine = [pltpu.make_async_copy(ins[a], slot(outs[a], me), local_sems.at[a]) for a in range(self.n)]
        first = []
        for a in range(self.n):
            first.append(copy(a, 0, me, sibling, src=ins[a]))
            first += [copy(a, 1 + j, me, (*chip, c), src=ins[a]) for j, chip in enumerate(chips)]
        passed = [copy(a, 4 + j, (*chip, c), sibling) for j, chip in enumerate(chips) for a in range(self.n)]
        return c, me, sibling, chips, copy, mine, first, passed

    def start(self, ins, outs, sems):
        *_, mine, first, _ = self._ctx(ins, outs, sems)
        for cp in mine + first:
            cp.start()

    def mid(self, ins, outs, sems):
        c, me, _, chips, copy, _, _, passed = self._ctx(ins, outs, sems)
        i = 0
        for j, chip in enumerate(chips):
            for a in range(self.n):
                copy(a, 1 + j, (*chip, c), me).wait_recv()
                passed[i].start()
                i += 1

    def finish(self, ins, outs, sems):
        c, me, sibling, chips, copy, mine, first, passed = self._ctx(ins, outs, sems)
        for a in range(self.n):
            copy(a, 0, sibling, me).wait_recv()
            for j, chip in enumerate(chips):
                copy(a, 4 + j, (*chip, 1 - c), me).wait_recv()
        for cp in first + passed:
            cp.wait_send()
        for cp in mine:
            cp.wait()


N_CHIPS = 4


class _SiblingSwap:
    mid = None

    def __init__(self, blocks):
        self.args = list(blocks)
        self.n = len(blocks)
        self.out_shape = [jax.ShapeDtypeStruct((N_CHIPS,) + b.shape[1:], b.dtype) for b in blocks]
        self.scratch = [pltpu.SemaphoreType.DMA((self.n, N_CHIPS)), pltpu.SemaphoreType.DMA((self.n, N_CHIPS))]

    def _copies(self, ins, outs, sems):
        send_sems, recv_sems = sems
        x, y, c = _coords()
        return [pltpu.make_async_remote_copy(
            src_ref=ins[a].at[2 * q + (1 - c)], dst_ref=outs[a].at[q],
            send_sem=send_sems.at[a, q], recv_sem=recv_sems.at[a, q],
            device_id=(x, y, 1 - c), device_id_type=MESH) for a in range(self.n) for q in range(N_CHIPS)]

    def start(self, ins, outs, sems):
        for cp in self._copies(ins, outs, sems):
            cp.start()

    def finish(self, ins, outs, sems):
        for cp in self._copies(ins, outs, sems):
            cp.wait()


class _ChipScatter:
    mid = None

    def __init__(self, blocks):
        self.args = list(blocks)
        self.n = len(blocks)
        self.out_shape = [jax.ShapeDtypeStruct(b.shape, b.dtype) for b in blocks]
        self.scratch = [pltpu.SemaphoreType.DMA((self.n, 3)), pltpu.SemaphoreType.DMA((self.n, 3)),
                        pltpu.SemaphoreType.DMA((self.n,))]

    def _ctx(self, ins, outs, sems):
        send_sems, recv_sems, local_sems = sems
        x, y, c = _coords()
        me = 2 * x + y
        mine = [pltpu.make_async_copy(ins[a].at[me], outs[a].at[me], local_sems.at[a]) for a in range(self.n)]

        def copy(a, k, landing_here):
            px, py = x ^ ((k >> 1) & 1), y ^ (k & 1)
            them = 2 * px + py
            return pltpu.make_async_remote_copy(
                src_ref=ins[a].at[them], dst_ref=outs[a].at[them if landing_here else me],
                send_sem=send_sems.at[a, k - 1], recv_sem=recv_sems.at[a, k - 1],
                device_id=(px, py, c), device_id_type=MESH)

        sent = [copy(a, k, False) for k in range(1, N_CHIPS) for a in range(self.n)]
        arrivals = [copy(a, k, True) for k in range(1, N_CHIPS) for a in range(self.n)]
        return mine, sent, arrivals

    def start(self, ins, outs, sems):
        mine, sent, _ = self._ctx(ins, outs, sems)
        for cp in mine + sent:
            cp.start()

    def finish(self, ins, outs, sems):
        mine, sent, arrivals = self._ctx(ins, outs, sems)
        for cp in arrivals:
            cp.wait_recv()
        for cp in sent:
            cp.wait_send()
        for cp in mine:
            cp.wait()


def _pair_sum(blocks, got, core, name):
    _, R, C = blocks.shape
    tr, tc = _slab(R, C)

    def body(core_ref, mine_ref, got_ref, o_ref):
        o_ref[...] = (mine_ref[...].astype(f32) + got_ref[...].astype(f32)).astype(o_ref.dtype)

    return pl.pallas_call(
        body, name=name,
        grid_spec=pltpu.PrefetchScalarGridSpec(
            num_scalar_prefetch=1, grid=(N_CHIPS, R // tr, C // tc),
            in_specs=[pl.BlockSpec((1, tr, tc), lambda q, i, j, core_ref: (2 * q + core_ref[0], i, j)),
                      pl.BlockSpec((1, tr, tc), lambda q, i, j, core_ref: (q, i, j))],
            out_specs=pl.BlockSpec((1, tr, tc), lambda q, i, j, core_ref: (q, i, j))),
        out_shape=jax.ShapeDtypeStruct((N_CHIPS, R, C), blocks.dtype),
        compiler_params=_params(("parallel", "parallel", "parallel")),
    )(core, blocks, got)


def _run_comm(comm, name):
    n = comm.n

    def body(*refs):
        ins, outs, sems = refs[:n], refs[n:2 * n], refs[2 * n:]
        comm.start(ins, outs, sems)
        if comm.mid is not None:
            comm.mid(ins, outs, sems)
        comm.finish(ins, outs, sems)

    return pl.pallas_call(
        body, name=name, in_specs=[ANY] * n, out_specs=[ANY] * n,
        out_shape=comm.out_shape, scratch_shapes=comm.scratch,
    )(*comm.args)


def _pallas(body, args, *, name, grid, in_specs, out_specs, out_shape, scratch_shapes=(), semantics, comm=None):
    in_specs, out_specs, out_shape, scratch_shapes = list(in_specs), list(out_specs), list(out_shape), list(scratch_shapes)
    if comm is None:
        return pl.pallas_call(
            body, name=name, grid=grid, in_specs=in_specs, out_specs=out_specs, out_shape=out_shape,
            scratch_shapes=scratch_shapes, compiler_params=_params(semantics))(*args)
    a = len(in_specs)
    b = a + comm.n
    c = b + len(out_specs)
    d = c + comm.n
    e = d + len(scratch_shapes)
    total = math.prod(grid)
    mid_step = (7 * total) // 8

    def hosted(*refs):
        step = pl.program_id(0)
        for axis in range(1, len(grid)):
            step = step * grid[axis] + pl.program_id(axis)
        ins, outs, sems = refs[a:b], refs[c:d], refs[e:]

        @pl.when(step == 0)
        def _():
            comm.start(ins, outs, sems)

        body(*refs[:a], *refs[b:c], *refs[d:e])

        if comm.mid is not None:
            @pl.when(step == mid_step)
            def _():
                comm.mid(ins, outs, sems)

        @pl.when(step == total - 1)
        def _():
            comm.finish(ins, outs, sems)

    res = pl.pallas_call(
        hosted, name=name, grid=grid, in_specs=in_specs + [ANY] * comm.n, out_specs=out_specs + [ANY] * comm.n,
        out_shape=out_shape + comm.out_shape, scratch_shapes=scratch_shapes + comm.scratch,
        compiler_params=_params(("arbitrary",) * len(grid)))(*args, *comm.args)
    return list(res[:len(out_specs)]) + [list(res[len(out_specs):])]


def _all_reduce_small(vec, name):
    R, C = vec.shape

    def body(v_ref, o_ref, gath, send_sems, recv_sems):
        x, y, c = _coords()
        me = 4 * x + 2 * y + c
        gath[me] = v_ref[...]

        def copy(k):
            px, py, pc = x ^ ((k >> 2) & 1), y ^ ((k >> 1) & 1), c ^ (k & 1)
            return pltpu.make_async_remote_copy(
                src_ref=v_ref, dst_ref=gath.at[me], send_sem=send_sems.at[k - 1], recv_sem=recv_sems.at[k - 1],
                device_id=(px, py, pc), device_id_type=MESH)

        sent = [copy(k) for k in range(1, N_DEV)]
        for cp in sent:
            cp.start()
        for cp in sent:
            cp.wait()
        total = gath[0]
        for i in range(1, N_DEV):
            total = total + gath[i]
        o_ref[...] = total

    return pl.pallas_call(
        body, name=name,
        in_specs=[pl.BlockSpec(memory_space=pltpu.VMEM)], out_specs=pl.BlockSpec(memory_space=pltpu.VMEM),
        out_shape=jax.ShapeDtypeStruct((R, C), f32),
        scratch_shapes=[pltpu.VMEM((N_DEV, R, C), f32), pltpu.SemaphoreType.DMA((7,)), pltpu.SemaphoreType.DMA((7,))],
    )(vec)


def _gathered_in(g):
    return jnp.transpose(g, (1, 0, 2)).reshape(g.shape[1], -1)


def _heads_major(a, n):
    return jnp.transpose(a.reshape(a.shape[0], n, HEAD_DIM), (1, 0, 2))


def _heads_minor(a):
    return jnp.transpose(a, (1, 0, 2)).reshape(a.shape[1], -1)


def kernel(x, g_pre, g_post, w_in_a, w_out_a, sinks_a, w_in_b, w_out_b, w_in_c, b_f_c, w_out_c, loss_target, m_g_pre, m_g_post, m_w_in_a, m_w_out_a, m_sinks_a, m_w_in_b, m_w_out_b, m_w_in_c, m_b_f_c, m_w_out_c, v_g_pre, v_g_post, v_w_in_a, v_w_out_a, v_sinks_a, v_w_in_b, v_w_out_b, v_w_in_c, v_b_f_c, v_w_out_c):
    S, D = x.shape[1], x.shape[2]
    H = D // HEAD_DIM
    BR = H * HEAD_DIM
    n_kv = H // 8
    KV = n_kv * HEAD_DIM
    x0 = x[0]
    target = loss_target[0]
    swa_bias = _swa_bias(H, n_kv)
    w_in = {0: w_in_a, 1: w_in_b, 2: w_in_c}
    w_out = {0: w_out_a, 1: w_out_b, 2: w_out_c}

    saved = []
    xi = x0
    riders = {"pre_norm0": [("in", 0)], "in_proj0": [("out", 0)], "mixer0": [("in", 1)],
              "mixer1": [("out", 1), ("in", 2)], "mixer2": [("out", 2), ("in", 3), ("out", 3)]}
    full = {}

    def held_transposed(kind):
        return w_in[kind].shape[2] % BLOCK != 0

    def shard_of(which, i):
        kind, j = i % 3, i // 3
        if which == "out":
            return w_out[kind][j].astype(bf16)
        if held_transposed(kind):
            return jnp.swapaxes(w_in[kind], 1, 2)[j].astype(bf16)
        return w_in[kind][j].astype(bf16)

    def rider(slot):
        keys = riders.get(slot)
        if not keys:
            return None
        return _Gather([shard_of(which, i) for which, i in keys])

    def landed(slot, arrays):
        full.update(zip(riders[slot], arrays))

    for i in range(DEPTH):
        kind, j = i % 3, i // 3
        nxt = rider(f"mixer{i}")
        with_norm = rider(f"pre_norm{i}")
        h = _rmsnorm_fwd(xi, g_pre[i:i + 1], name=f"pre_norm{i}", comm=with_norm)
        if with_norm is not None:
            h, arrived = h
            landed(f"pre_norm{i}", arrived)
        w_t = held_transposed(kind)
        W_in = full["in", i].reshape(-1, D) if w_t else _gathered_in(full["in", i])
        st = dict(x=xi, h=h)
        if kind == 2:
            n_main = 4 * BR
            if w_t:
                W_in = jnp.pad(W_in, ((0, (-W_in.shape[0]) % BLOCK), (0, 0)))
                fl = _matmul(h, W_in[n_main:n_main + BLOCK], "nt", f32, name=f"f_proj{i}")
            else:
                W_in = jnp.pad(W_in, ((0, 0), (0, (-W_in.shape[1]) % BLOCK)))
                fl = _matmul(h, W_in[:, n_main:n_main + BLOCK], "nn", f32, name=f"f_proj{i}")
            flT = jnp.transpose(fl[:, :H])
            bcol = b_f_c[j].reshape(H, 1)
            cumT = _fox_cum(flT, bcol, name=f"fox_cum{i}")
            cq, ck = cumT[:, None, :], cumT[:, :, None]
            st.update(flT=flT, bcol=bcol, cq=cq, ck=ck)
        st["W_in"] = W_in
        with_proj = rider(f"in_proj{i}")
        P = _matmul(h, W_in, "nt" if w_t else "nn", bf16, name=f"in_proj{i}", comm=with_proj)
        if with_proj is not None:
            P, arrived = P
            landed(f"in_proj{i}", arrived)
        st["P"] = P
        if kind == 0:
            kh = _heads_major(P[:, BR:BR + KV], n_kv)
            vh = _heads_major(P[:, BR + KV:BR + 2 * KV], n_kv)
            o, u, *rest = _swaT_fwd(P, kh, vh, sinks_a[j], swa_bias, name=f"swa_fwd{i}", comm=nxt)
            st.update(kh=kh, vh=vh)
        elif kind == 1:
            o, u, rc, *rest = _sbT_fwd(P, BR, name=f"sb_fwd{i}", comm=nxt)
            st.update(rc=rc)
        else:
            o, u, lse, *rest = _foxT_fwd(P, cq, ck, BR, name=f"fox_fwd{i}", comm=nxt)
            st.update(lse=lse)
        if nxt is not None:
            landed(f"mixer{i}", rest[0])
        W_out = full["out", i].reshape(BR, D)
        st.update(o=o, u=u, W_out=W_out)
        y = _matmul(u, W_out, "nn", f32, name=f"out_proj{i}")
        st["y"] = y
        xi = _post_fwd(xi, y, g_post[i:i + 1], name=f"post_norm{i}")
        saved.append(st)

    loss_part, dx = _loss_fwd_bwd(xi, target, name="loss")

    dg_pre, dg_post = [None] * DEPTH, [None] * DEPTH
    dsinks = [None, None]
    db_f = None
    recv = [None] * DEPTH
    core = lax.axis_index("c").astype(jnp.int32).reshape(1)
    pending = None
    for i in reversed(range(DEPTH)):
        kind, j = i % 3, i // 3
        st = saved[i]
        dy, dg_post[i] = _post_bwd(dx, st["y"], g_post[i:i + 1], name=f"post_bwd{i}")
        du = _matmul(dy, st["W_out"], "nt", bf16, name=f"du{i}")
        dW_out = _matmul(st["u"], dy, "tn", bf16, name=f"dw_out{i}")
        P = st["P"]
        if kind == 0:
            dq, dz, dkh, dvh, dsk, *rest = _swaT_bwd(P, st["kh"], st["vh"], st["o"], du, sinks_a[j], swa_bias,
                                                     name=f"swa_bwd{i}", comm=pending)
            dsinks[j] = jnp.sum(dsk.reshape(H, BLOCK), axis=1)
            dP = jnp.concatenate([dq, _heads_minor(dkh).astype(bf16), _heads_minor(dvh).astype(bf16), dz], axis=1)
        elif kind == 1:
            dq, dz, dk, dv, *rest = _sbT_bwd(P, st["o"], du, st["rc"], BR, name=f"sb_bwd{i}", comm=pending)
            dP = jnp.concatenate([dq, dk.astype(bf16), dv.astype(bf16), dz], axis=1)
        else:
            dq, dz, dk, dv, dcq, dck, *rest = _foxT_bwd(P, st["o"], du, st["lse"], st["cq"], st["ck"], BR,
                                                       name=f"fox_bwd{i}", comm=pending)
        if pending is not None:
            recv[i + 1] = rest[0]
        if kind == 2:
            dflT, db_col = _fox_cum_bwd(dcq.reshape(H, S), dck.reshape(H, S), st["flT"], st["bcol"],
                                        name=f"fox_cum_bwd{i}")
            db_f = db_col.reshape(H)
            dfl = jnp.pad(jnp.transpose(dflT), ((0, 0), (0, 128 - H))).astype(bf16)
            dP = jnp.concatenate([dq, dk.astype(bf16), dv.astype(bf16), dz, dfl], axis=1)
        w_t = held_transposed(kind)
        c_shard = w_in[kind].shape[2]
        if w_t:
            dW_in = _matmul(dP, st["h"], "tn", bf16, name=f"dw_in{i}")[:c_shard * N_DEV].reshape(N_DEV, c_shard, D)
        else:
            assert dP.shape[1] == c_shard * N_DEV
            dW_in = _matmul(st["h"], dP, "tn", bf16, name=f"dw_in{i}", col_blocks=N_DEV)
        blocks = [dW_in, dW_out.reshape(N_DEV, BR // N_DEV, D)]
        swap = _SiblingSwap(blocks)
        dh_mode = "nn" if w_t else "nt"
        if i > 0:
            dh = _matmul(dP, st["W_in"], dh_mode, f32, name=f"dh{i}")
            dx, dg_pre[i], got = _pre_bwd(dh, st["x"], g_pre[i:i + 1], dx, name=f"pre_bwd{i}", comm=swap)
        else:
            dh, got = _matmul(dP, st["W_in"], dh_mode, f32, name=f"dh{i}", comm=swap)
            dx, dg_pre[i] = _pre_bwd(dh, st["x"], g_pre[i:i + 1], dx, name=f"pre_bwd{i}")
        sums = [_pair_sum(b, g, core, name=f"pair_sum_{t}{i}") for t, b, g in zip(("in", "out"), blocks, got)]
        pending = _ChipScatter(sums)
    recv[0] = _run_comm(pending, name="scatter_dw0")

    small = jnp.concatenate(
        [jnp.concatenate(dg_pre, axis=0).reshape(-1), jnp.concatenate(dg_post, axis=0).reshape(-1),
         jnp.concatenate(dsinks), db_f, loss_part[0, :1]])
    n_small = small.shape[0]
    rows = -(-n_small // 128)
    rows = -(-rows // 8) * 8
    small = jnp.pad(small, (0, rows * 128 - n_small)).reshape(rows, 128)
    total = _all_reduce_small(small, name="reduce_small").reshape(-1)
    o0 = DEPTH * D
    grad_g_pre = total[:o0].reshape(DEPTH, D)
    grad_g_post = total[o0:2 * o0].reshape(DEPTH, D)
    grad_sinks = total[2 * o0:2 * o0 + 2 * H].reshape(2, H)
    grad_b_f = total[2 * o0 + 2 * H:2 * o0 + 3 * H].reshape(1, H)
    loss = total[2 * o0 + 3 * H]

    def small_adam(w, g, m, v, name):
        def body(w_ref, g_ref, m_ref, v_ref, d_ref, nm_ref, nv_ref):
            d_ref[...], nm_ref[...], nv_ref[...] = _adam_math(w_ref[...], g_ref[...], m_ref[...], v_ref[...])
        vm = pl.BlockSpec(memory_space=pltpu.VMEM)
        return pl.pallas_call(body, name=name, in_specs=[vm] * 4, out_specs=[vm] * 3,
                              out_shape=[jax.ShapeDtypeStruct(w.shape, f32)] * 3)(w, g, m, v)

    upd = {}
    upd["g_pre"] = (grad_g_pre,) + tuple(small_adam(g_pre, grad_g_pre, m_g_pre, v_g_pre, "adam_g_pre"))
    upd["g_post"] = (grad_g_post,) + tuple(small_adam(g_post, grad_g_post, m_g_post, v_g_post, "adam_g_post"))
    upd["sinks_a"] = (grad_sinks,) + tuple(small_adam(sinks_a, grad_sinks, m_sinks_a, v_sinks_a, "adam_sinks"))
    upd["b_f_c"] = (grad_b_f,) + tuple(small_adam(b_f_c, grad_b_f, m_b_f_c, v_b_f_c, "adam_b_f"))

    def big(layers, which, w, m, v, tag):
        return _adamw_sum([recv[i][which] for i in layers], w, m, v, name=f"adam_{tag}")

    def big_in(kind, layers, w, m, v, tag):
        if not held_transposed(kind):
            return big(layers, 0, w, m, v, tag)
        flip = lambda a: jnp.swapaxes(a, 1, 2)
        return [flip(t) for t in big(layers, 0, flip(w), flip(m), flip(v), tag)]

    upd["w_in_c"] = big_in(2, (2,), w_in_c, m_w_in_c, v_w_in_c, "in_c")
    upd["w_in_b"] = big_in(1, (1,), w_in_b, m_w_in_b, v_w_in_b, "in_b")
    upd["w_out_b"] = big((1,), 1, w_out_b, m_w_out_b, v_w_out_b, "out_b")
    upd["w_out_c"] = big((2,), 1, w_out_c, m_w_out_c, v_w_out_c, "out_c")
    upd["w_in_a"] = big_in(0, (0, 3), w_in_a, m_w_in_a, v_w_in_a, "in_a")
    upd["w_out_a"] = big((0, 3), 1, w_out_a, m_w_out_a, v_w_out_a, "out_a")

    names = ["g_pre", "g_post", "w_in_a", "w_out_a", "sinks_a", "w_in_b", "w_out_b", "w_in_c", "b_f_c", "w_out_c"]
    return (loss, dx[None], *[upd[k][0] for k in names], *[upd[k][1] for k in names],
            *[upd[k][2] for k in names], *[upd[k][3] for k in names])
```

```python
import functools
import math

import numpy as np
import jax
import jax.numpy as jnp
from jax import lax
from jax.experimental import pallas as pl
from jax.experimental.pallas import tpu as pltpu

HEAD_DIM = 64
HP = 8
HW = HP * HEAD_DIM
BLOCK = 128
NORM_EPS = 1e-6
NEG = -1e30
N_DEV = 8
DEPTH = 4
ADAM_LR, ADAM_B1, ADAM_B2, ADAM_EPS, ADAM_WD, ADAM_STEP = 0.001, 0.9, 0.999, 1e-8, 0.01, 10
VMEM_LIMIT = 56 * 1024 * 1024

bf16 = jnp.bfloat16
f32 = jnp.float32
MESH = pl.DeviceIdType.MESH
ANY = pl.BlockSpec(memory_space=pl.ANY)
SMEM = pl.BlockSpec(memory_space=pltpu.SMEM)

NN = (((1,), (0,)), ((), ()))
NT = (((1,), (1,)), ((), ()))
TN = (((0,), (0,)), ((), ()))


def _dot(a, b, dims=NN):
    return lax.dot_general(a, b, dims, preferred_element_type=f32)


def _params(sem):
    return pltpu.CompilerParams(dimension_semantics=sem, vmem_limit_bytes=VMEM_LIMIT)


def _attn_tile(S):
    return 512 if S % 512 == 0 and S >= 1024 else 128


def _pick(n, pref):
    for t in pref:
        if n % t == 0:
            return t
    return n


def _matmul(a, b, mode, out_dtype, name, comm=None, col_blocks=None):
    if mode == "nn":
        (M, K), (K2, N) = a.shape, b.shape
    elif mode == "nt":
        (M, K), (N, K2) = a.shape, b.shape
    else:
        (K, M), (K2, N) = a.shape, b.shape
    assert K == K2, (a.shape, b.shape, mode)
    tm = _pick(M, (1024, 640, 512, 256, 128))
    tn = _pick(N, (1024, 768, 640, 512, 256, 128))
    if col_blocks is not None:
        tn = N // col_blocks
        assert N % col_blocks == 0 and tn % BLOCK == 0, (N, col_blocks)
    tk = _pick(K, (2048, 1664, 1536, 1024, 512, 640, 256, 128))
    nk = K // tk
    dims = {"nn": NN, "nt": NT, "tn": TN}[mode]

    def body(a_ref, b_ref, o_ref, acc_ref):
        if nk == 1:
            o_ref[...] = _dot(a_ref[...], b_ref[...], dims).astype(o_ref.dtype)
            return
        k = pl.program_id(2)

        @pl.when(k == 0)
        def _():
            acc_ref[...] = jnp.zeros_like(acc_ref)

        acc_ref[...] += _dot(a_ref[...], b_ref[...], dims)

        @pl.when(k == nk - 1)
        def _():
            o_ref[...] = acc_ref[...].astype(o_ref.dtype)

    if mode == "tn":
        a_spec = pl.BlockSpec((tk, tm), lambda i, j, k: (k, i))
    else:
        a_spec = pl.BlockSpec((tm, tk), lambda i, j, k: (i, k))
    if mode == "nt":
        b_spec = pl.BlockSpec((tn, tk), lambda i, j, k: (j, k))
    else:
        b_spec = pl.BlockSpec((tk, tn), lambda i, j, k: (k, j))
    if col_blocks is None:
        o_spec, o_shape = pl.BlockSpec((tm, tn), lambda i, j, k: (i, j)), (M, N)
    else:
        o_spec, o_shape = pl.BlockSpec((None, tm, tn), lambda i, j, k: (j, i, 0)), (col_blocks, M, tn)
    res = _pallas(
        body, (a, b), name=name,
        grid=(M // tm, N // tn, nk),
        in_specs=[a_spec, b_spec],
        out_specs=[o_spec],
        out_shape=[jax.ShapeDtypeStruct(o_shape, out_dtype)],
        scratch_shapes=[pltpu.VMEM((tm, tn) if nk > 1 else (8, 128), f32)],
        semantics=("parallel", "parallel", "arbitrary"), comm=comm)
    return res[0] if comm is None else (res[0], res[1])


ROWS = 256


def _rows(S):
    return ROWS if S % ROWS == 0 else S


def _rmsnorm_fwd(x, g, name, comm=None):
    S, D = x.shape
    tr = _rows(S)

    def body(x_ref, g_ref, h_ref):
        xv = x_ref[...]
        r = lax.rsqrt(jnp.mean(xv * xv, axis=-1, keepdims=True) + NORM_EPS)
        h_ref[...] = (xv * r * g_ref[...]).astype(h_ref.dtype)

    res = _pallas(
        body, (x, g), name=name, grid=(S // tr,),
        in_specs=[pl.BlockSpec((tr, D), lambda i: (i, 0)), pl.BlockSpec((1, D), lambda i: (0, 0))],
        out_specs=[pl.BlockSpec((tr, D), lambda i: (i, 0))],
        out_shape=[jax.ShapeDtypeStruct((S, D), bf16)],
        semantics=("parallel",), comm=comm)
    return res[0] if comm is None else (res[0], res[1])


def _post_fwd(x, y, g, g_next, name):
    S, D = x.shape
    tr = _rows(S)
    follows = g_next is not None

    def body(x_ref, y_ref, g_ref, *rest):
        yv = y_ref[...]
        r = lax.rsqrt(jnp.mean(yv * yv, axis=-1, keepdims=True) + NORM_EPS)
        xn = x_ref[...] + yv * r * g_ref[...]
        if not follows:
            rest[0][...] = xn
            return
        gn_ref, o_ref, h_ref = rest
        o_ref[...] = xn
        rn = lax.rsqrt(jnp.mean(xn * xn, axis=-1, keepdims=True) + NORM_EPS)
        h_ref[...] = (xn * rn * gn_ref[...]).astype(h_ref.dtype)

    row = pl.BlockSpec((tr, D), lambda i: (i, 0))
    vec = pl.BlockSpec((1, D), lambda i: (0, 0))
    return pl.pallas_call(
        body, name=name, grid=(S // tr,),
        in_specs=[row, row, vec] + [vec] * follows,
        out_specs=[row] + [row] * follows,
        out_shape=[jax.ShapeDtypeStruct((S, D), f32)] + [jax.ShapeDtypeStruct((S, D), bf16)] * follows,
        compiler_params=_params(("parallel",)),
    )(x, y, g, *([g_next] if follows else []))


def _loss_fwd_bwd(y, t, name):
    S, D = y.shape
    tr = _rows(S)

    def body(y_ref, t_ref, l_ref, d_ref):
        @pl.when(pl.program_id(0) == 0)
        def _():
            l_ref[...] = jnp.zeros_like(l_ref)

        e = y_ref[...] - t_ref[...]
        d_ref[...] = e * (1.0 / D)
        part = 0.5 * jnp.sum(jnp.sum(e * e, axis=-1, keepdims=True) * (1.0 / D), axis=0, keepdims=True)
        l_ref[...] += jnp.broadcast_to(part, l_ref.shape)

    row = pl.BlockSpec((tr, D), lambda i: (i, 0))
    return pl.pallas_call(
        body, name=name, grid=(S // tr,),
        in_specs=[row, row],
        out_specs=[pl.BlockSpec((8, 128), lambda i: (0, 0)), row],
        out_shape=[jax.ShapeDtypeStruct((8, 128), f32), jax.ShapeDtypeStruct((S, D), f32)],
        compiler_params=_params(("arbitrary",)),
    )(y, t)


def _post_bwd(dxn, y, g, name):
    S, D = y.shape
    tr = _rows(S)

    def body(d_ref, y_ref, g_ref, dy_ref, dg_ref):
        @pl.when(pl.program_id(0) == 0)
        def _():
            dg_ref[...] = jnp.zeros_like(dg_ref)

        yv = y_ref[...]
        d = d_ref[...]
        r = lax.rsqrt(jnp.mean(yv * yv, axis=-1, keepdims=True) + NORM_EPS)
        n = yv * r
        dn = d * g_ref[...]
        dg_ref[...] += jnp.sum(d * n, axis=0, keepdims=True)
        dy_ref[...] = (r * (dn - n * jnp.mean(dn * n, axis=-1, keepdims=True))).astype(dy_ref.dtype)

    row = pl.BlockSpec((tr, D), lambda i: (i, 0))
    vec = pl.BlockSpec((1, D), lambda i: (0, 0))
    return pl.pallas_call(
        body, name=name, grid=(S // tr,),
        in_specs=[row, row, vec],
        out_specs=[row, vec],
        out_shape=[jax.ShapeDtypeStruct((S, D), bf16), jax.ShapeDtypeStruct((1, D), f32)],
        compiler_params=_params(("arbitrary",)),
    )(dxn, y, g)


def _pre_bwd(dh, x, g, dres, name, comm=None):
    S, D = x.shape
    tr = _rows(S)

    def body(dh_ref, x_ref, g_ref, dres_ref, dx_ref, dg_ref):
        @pl.when(pl.program_id(0) == 0)
        def _():
            dg_ref[...] = jnp.zeros_like(dg_ref)

        xv = x_ref[...]
        d = dh_ref[...]
        r = lax.rsqrt(jnp.mean(xv * xv, axis=-1, keepdims=True) + NORM_EPS)
        n = xv * r
        dn = d * g_ref[...]
        dg_ref[...] += jnp.sum(d * n, axis=0, keepdims=True)
        dx_ref[...] = dres_ref[...] + r * (dn - n * jnp.mean(dn * n, axis=-1, keepdims=True))

    row = pl.BlockSpec((tr, D), lambda i: (i, 0))
    vec = pl.BlockSpec((1, D), lambda i: (0, 0))
    return _pallas(
        body, (dh, x, g, dres), name=name, grid=(S // tr,),
        in_specs=[row, row, vec, row],
        out_specs=[row, vec],
        out_shape=[jax.ShapeDtypeStruct((S, D), f32), jax.ShapeDtypeStruct((1, D), f32)],
        semantics=("arbitrary",), comm=comm)


def _sigmoid(x):
    return 1.0 / (1.0 + jnp.exp(-x))


def _gate_fwd(o, z):
    zf = z.astype(f32)
    return o * (zf * _sigmoid(zf))


def _gate_bwd(du, o, z):
    zf = z.astype(f32)
    sig = _sigmoid(zf)
    do = du * (zf * sig)
    dz = du * o * (sig * (1.0 + zf * (1.0 - sig)))
    return do, dz


def _iota2(shape, dim):
    return lax.broadcasted_iota(jnp.int32, shape, dim)


def _swa_specs(S, BR, KV, G):
    gw = G * HEAD_DIM
    qspec = pl.BlockSpec((BLOCK, gw), lambda h, n: (n, h))
    zoff = (BR + 2 * KV) // gw
    zspec = pl.BlockSpec((BLOCK, gw), lambda h, n: (n, zoff + h))
    cur = pl.BlockSpec((1, BLOCK, HEAD_DIM), lambda h, n: (h, n, 0))
    prev = pl.BlockSpec((1, BLOCK, HEAD_DIM), lambda h, n: (h, jnp.maximum(n - 1, 0), 0))
    return qspec, zspec, cur, prev


def _split3(x):
    x1 = x.astype(bf16)
    r1 = x - x1.astype(f32)
    x2 = r1.astype(bf16)
    x3 = (r1 - x2.astype(f32)).astype(bf16)
    return x1, x2, x3


def _split2(x):
    x1 = x.astype(bf16)
    return x1, (x - x1.astype(f32)).astype(bf16)


def _pair_specs(S, BR, T, kmap):
    nb = BR // HW
    q = pl.BlockSpec((T, HW), lambda p, i, j: (i, p))
    k = pl.BlockSpec((T, HW), lambda p, i, j: (kmap(i, j), nb + p))
    v = pl.BlockSpec((T, HW), lambda p, i, j: (kmap(i, j), 2 * nb + p))
    z = pl.BlockSpec((T, HW), lambda p, i, j: (i, 3 * nb + p))
    return q, k, v, z


def _fox_cum(flT, b, name):
    H, S = flT.shape
    tj = _pick(S, (256, 128))

    def body(fl_ref, b_ref, c_ref):
        j = pl.program_id(0)
        x = fl_ref[...] + b_ref[...]
        logf = jnp.minimum(x, 0.0) - jnp.log(1.0 + jnp.exp(-jnp.abs(x)))
        tri = (_iota2((S, tj), 0) <= j * tj + _iota2((S, tj), 1)).astype(bf16)
        c_ref[...] = sum(_dot(part, tri) for part in _split3(logf))

    return pl.pallas_call(
        body, name=name, grid=(S // tj,),
        in_specs=[pl.BlockSpec((H, S), lambda j: (0, 0)), pl.BlockSpec((H, 1), lambda j: (0, 0))],
        out_specs=pl.BlockSpec((H, tj), lambda j: (0, j)),
        out_shape=jax.ShapeDtypeStruct((H, S), f32),
        compiler_params=_params(("parallel",)),
    )(flT, b)


def _fox_cum_bwd(dcq, dck, flT, b, name):
    H, S = flT.shape
    tj = _pick(S, (256, 128))

    def body(dcq_ref, dck_ref, fl_ref, b_ref, o_ref, db_ref):
        j = pl.program_id(0)

        @pl.when(j == 0)
        def _():
            db_ref[...] = jnp.zeros_like(db_ref)

        tri = (_iota2((S, tj), 0) >= j * tj + _iota2((S, tj), 1)).astype(bf16)
        dlogf = sum(_dot(part, tri) for part in _split3(dcq_ref[...] - dck_ref[...]))
        x = fl_ref[...] + b_ref[...]
        dfl = dlogf * _sigmoid(-x)
        o_ref[...] = dfl
        db_ref[...] += jnp.sum(dfl, axis=-1, keepdims=True)

    blk = pl.BlockSpec((H, tj), lambda j: (0, j))
    whole = pl.BlockSpec((H, S), lambda j: (0, 0))
    col = pl.BlockSpec((H, 1), lambda j: (0, 0))
    return pl.pallas_call(
        body, name=name, grid=(S // tj,),
        in_specs=[whole, whole, blk, col],
        out_specs=[blk, col],
        out_shape=[jax.ShapeDtypeStruct((H, S), f32), jax.ShapeDtypeStruct((H, 1), f32)],
        compiler_params=_params(("arbitrary",)),
    )(dcq, dck, flT, b)


def _hsl(hh):
    return slice(hh * HEAD_DIM, (hh + 1) * HEAD_DIM)


def _scaled(q, scale):
    assert math.log2(scale).is_integer()
    return (q.astype(f32) * scale).astype(bf16)


def _swa_bias(H, n_kv):
    G = H // n_kv
    slopes = 2.0 ** (-8.0 * np.arange(1, H + 1, dtype=np.float32) / H)
    dist = (np.arange(BLOCK)[None, :] + BLOCK - np.arange(2 * BLOCK)[:, None]).astype(np.float32)
    valid = (dist >= 0) & (dist < BLOCK)
    out = np.empty((n_kv, 2 * BLOCK, G * BLOCK), np.float32)
    for h in range(H):
        out[h // G][:, (h % G) * BLOCK:(h % G + 1) * BLOCK] = np.where(valid, -(slopes[h] * dist), np.float32(NEG))
    return jnp.asarray(out)


def _swaT_parts(q_ref, kc_ref, kp_ref, vc_ref, vp_ref, bias_ref, sink_ref, kvh, n, G, scale):
    qg = jnp.concatenate([q_ref[:, _hsl(g)] for g in range(G)], axis=0)
    kband = jnp.concatenate([kp_ref[0], kc_ref[0]], axis=0)
    vband = jnp.concatenate([vp_ref[0], vc_ref[0]], axis=0)
    s = _dot(kband, qg, NT) * scale + bias_ref[0]
    s = jnp.where(_iota2(s.shape, 0) + jnp.where(n > 0, BLOCK, 0) >= BLOCK, s, NEG)
    sink = jnp.concatenate([jnp.full((1, BLOCK), sink_ref[kvh * G + g], f32) for g in range(G)], axis=1)
    m = jnp.maximum(jnp.max(s, axis=0, keepdims=True), sink)
    p = jnp.exp(s - m)
    ps = jnp.exp(sink - m)
    den = jnp.sum(p, axis=0, keepdims=True) + ps
    return qg, kband, vband, p, ps, den


def _swaT_specs(S, BR, KV, G):
    qspec, zspec, cur, prev = _swa_specs(S, BR, KV, G)
    bias = pl.BlockSpec((1, 2 * BLOCK, G * BLOCK), lambda h, n: (h, 0, 0))
    return qspec, zspec, cur, prev, bias


def _swaT_fwd(P, kh, vh, sinks, bias, name, comm=None):
    S = P.shape[0]
    n_kv = kh.shape[0]
    H = sinks.shape[0]
    G = H // n_kv
    BR, KV = H * HEAD_DIM, n_kv * HEAD_DIM
    scale = HEAD_DIM ** -0.5
    assert (BR + 2 * KV) % (G * HEAD_DIM) == 0

    def body(q_ref, z_ref, kc_ref, kp_ref, vc_ref, vp_ref, bias_ref, sink_ref, o_ref, u_ref):
        kvh, n = pl.program_id(0), pl.program_id(1)
        _, _, vband, p, _, den = _swaT_parts(q_ref, kc_ref, kp_ref, vc_ref, vp_ref, bias_ref, sink_ref, kvh, n, G, scale)
        o = jnp.transpose(_dot(vband, p.astype(bf16), TN) / den)
        for g in range(G):
            oh = o[g * BLOCK:(g + 1) * BLOCK]
            o_ref[:, _hsl(g)] = oh.astype(o_ref.dtype)
            u_ref[:, _hsl(g)] = _gate_fwd(oh, z_ref[:, _hsl(g)]).astype(u_ref.dtype)

    qspec, zspec, cur, prev, bspec = _swaT_specs(S, BR, KV, G)
    ospec = pl.BlockSpec((BLOCK, G * HEAD_DIM), lambda h, n: (n, h))
    return _pallas(
        body, (P, P, kh, kh, vh, vh, bias, sinks), name=name, grid=(n_kv, S // BLOCK),
        in_specs=[qspec, zspec, cur, prev, cur, prev, bspec, SMEM],
        out_specs=[ospec, ospec],
        out_shape=[jax.ShapeDtypeStruct((S, BR), bf16)] * 2,
        semantics=("parallel", "parallel"), comm=comm)


def _swaT_bwd(P, kh, vh, o, du, sinks, bias, name, comm=None):
    S = P.shape[0]
    n_kv = kh.shape[0]
    H = sinks.shape[0]
    G = H // n_kv
    BR, KV = H * HEAD_DIM, n_kv * HEAD_DIM
    scale = HEAD_DIM ** -0.5

    def body(q_ref, z_ref, kc_ref, kp_ref, vc_ref, vp_ref, o_ref, du_ref, bias_ref, sink_ref,
             dq_ref, dz_ref, dk_ref, dv_ref, ds_ref):
        kvh, n = pl.program_id(0), pl.program_id(1)

        @pl.when(n == 0)
        def _():
            dk_ref[...] = jnp.zeros_like(dk_ref)
            dv_ref[...] = jnp.zeros_like(dv_ref)
            ds_ref[...] = jnp.zeros_like(ds_ref)

        dos, prods = [], []
        for g in range(G):
            oh = o_ref[:, _hsl(g)].astype(f32)
            doh, dzh = _gate_bwd(du_ref[:, _hsl(g)].astype(f32), oh, z_ref[:, _hsl(g)])
            dz_ref[:, _hsl(g)] = dzh.astype(dz_ref.dtype)
            dos.append(doh.astype(bf16))
            prods.append(doh * oh)
        dog = jnp.concatenate(dos, axis=0)
        delta = jnp.sum(jnp.transpose(jnp.concatenate(prods, axis=0)), axis=0, keepdims=True)
        qg, kband, vband, p, ps, den = _swaT_parts(q_ref, kc_ref, kp_ref, vc_ref, vp_ref, bias_ref, sink_ref,
                                                   kvh, n, G, scale)
        inv = 1.0 / den
        pn = p * inv
        ds = (pn * (_dot(vband, dog, NT) - delta)).astype(bf16)
        dq = jnp.transpose(_dot(kband, ds, TN)) * scale
        for g in range(G):
            dq_ref[:, _hsl(g)] = dq[g * BLOCK:(g + 1) * BLOCK].astype(dq_ref.dtype)
        dkb = _dot(ds, qg) * scale
        dvb = _dot(pn.astype(bf16), dog)
        ds_ref[0] -= ps * inv * delta
        row_c = pl.multiple_of(n * BLOCK, BLOCK)
        dk_ref[0, pl.ds(row_c, BLOCK), :] += dkb[BLOCK:]
        dv_ref[0, pl.ds(row_c, BLOCK), :] += dvb[BLOCK:]

        @pl.when(n > 0)
        def _():
            row_p = pl.multiple_of((n - 1) * BLOCK, BLOCK)
            dk_ref[0, pl.ds(row_p, BLOCK), :] += dkb[:BLOCK]
            dv_ref[0, pl.ds(row_p, BLOCK), :] += dvb[:BLOCK]

    qspec, zspec, cur, prev, bspec = _swaT_specs(S, BR, KV, G)
    ospec = pl.BlockSpec((BLOCK, G * HEAD_DIM), lambda h, n: (n, h))
    full = pl.BlockSpec((1, S, HEAD_DIM), lambda h, n: (h, 0, 0))
    return _pallas(
        body, (P, P, kh, kh, vh, vh, o, du, bias, sinks), name=name, grid=(n_kv, S // BLOCK),
        in_specs=[qspec, zspec, cur, prev, cur, prev, ospec, ospec, bspec, SMEM],
        out_specs=[ospec, ospec, full, full, pl.BlockSpec((1, 1, G * BLOCK), lambda h, n: (h, 0, 0))],
        out_shape=[jax.ShapeDtypeStruct((S, BR), bf16)] * 2
        + [jax.ShapeDtypeStruct((n_kv, S, HEAD_DIM), f32)] * 2
        + [jax.ShapeDtypeStruct((n_kv, 1, G * BLOCK), f32)],
        semantics=("parallel", "arbitrary"), comm=comm)


def _foxT_fwd(P, cq, ck, BR, name, comm=None):
    S = P.shape[0]
    H = BR // HEAD_DIM
    T = _attn_tile(S)
    nq = S // T
    scale = HEAD_DIM ** -0.5
    kmap = lambda i, j: jnp.minimum(i, j)

    def body(q_ref, k_ref, v_ref, z_ref, cq_ref, ck_ref, o_ref, u_ref, lse_ref, m_s, l_s, acc_s):
        qb, kb = pl.program_id(1), pl.program_id(2)

        @pl.when(kb == 0)
        def _():
            m_s[...] = jnp.full_like(m_s, NEG)
            l_s[...] = jnp.zeros_like(l_s)
            acc_s[...] = jnp.zeros_like(acc_s)

        def tile(diagonal):
            for hh in range(HP):
                sl = _hsl(hh)
                s = _dot(k_ref[:, sl], _scaled(q_ref[:, sl], scale), NT) + cq_ref[hh] - ck_ref[hh]
                if diagonal:
                    s = jnp.where(_iota2((T, T), 0) <= _iota2((T, T), 1), s, NEG)
                m_old = m_s[hh]
                m_new = jnp.maximum(m_old, jnp.max(s, axis=0, keepdims=True))
                alpha = jnp.exp(m_old - m_new)
                p = jnp.exp(s - m_new)
                l_s[hh] = alpha * l_s[hh] + jnp.sum(p, axis=0, keepdims=True)
                acc_s[hh] = alpha * acc_s[hh] + _dot(v_ref[:, sl], p.astype(bf16), TN)
                m_s[hh] = m_new

        @pl.when(kb < qb)
        def _():
            tile(False)

        @pl.when(kb == qb)
        def _():
            tile(True)
            for hh in range(HP):
                sl = _hsl(hh)
                oh = jnp.transpose(acc_s[hh] / l_s[hh])
                o_ref[:, sl] = oh.astype(o_ref.dtype)
                u_ref[:, sl] = _gate_fwd(oh, z_ref[:, sl]).astype(u_ref.dtype)
                lse_ref[hh] = m_s[hh] + jnp.log(l_s[hh])

    q, k, v, z = _pair_specs(S, BR, T, kmap)
    rowq = pl.BlockSpec((HP, 1,T), lambda p, i, j: (p, 0, i))
    colk = pl.BlockSpec((HP, T,1), lambda p, i, j: (p, kmap(i, j), 0))
    ospec = pl.BlockSpec((T, HW), lambda p, i, j: (i, p))
    return _pallas(
        body, (P, P, P, P, cq, ck), name=name, grid=(H // HP, nq, nq),
        in_specs=[q, k, v, z, rowq, colk],
        out_specs=[ospec, ospec, rowq],
        out_shape=[jax.ShapeDtypeStruct((S, BR), bf16)] * 2 + [jax.ShapeDtypeStruct((H, 1, S), f32)],
        scratch_shapes=[pltpu.VMEM((HP, 1,T), f32), pltpu.VMEM((HP, 1,T), f32), pltpu.VMEM((HP, HEAD_DIM,T), f32)],
        semantics=("parallel", "arbitrary", "arbitrary"), comm=comm)


def _foxT_bwd(P, o, du, lse, cq, ck, BR, name, comm=None):
    S = P.shape[0]
    H = BR // HEAD_DIM
    T = _attn_tile(S)
    nq = S // T
    scale = HEAD_DIM ** -0.5
    kmap = lambda i, j: jnp.minimum(i, j)

    def body(q_ref, k_ref, v_ref, z_ref, o_ref, du_ref, lse_ref, cq_ref, ck_ref,
             dq_ref, dz_ref, dk_ref, dv_ref, dcq_ref, dck_ref, do_s, delta_s, dq_s, dcq_s):
        qb, kb = pl.program_id(1), pl.program_id(2)

        @pl.when(jnp.logical_and(qb == 0, kb == 0))
        def _():
            dk_ref[...] = jnp.zeros_like(dk_ref)
            dv_ref[...] = jnp.zeros_like(dv_ref)
            dck_ref[...] = jnp.zeros_like(dck_ref)

        @pl.when(kb == 0)
        def _():
            for hh in range(HP):
                sl = _hsl(hh)
                oh = o_ref[:, sl].astype(f32)
                doh, dzh = _gate_bwd(du_ref[:, sl].astype(f32), oh, z_ref[:, sl])
                dz_ref[:, sl] = dzh.astype(dz_ref.dtype)
                do_s[hh] = doh.astype(bf16)
                delta_s[hh] = jnp.sum(jnp.transpose(doh * oh), axis=0, keepdims=True)
            dq_s[...] = jnp.zeros_like(dq_s)
            dcq_s[...] = jnp.zeros_like(dcq_s)

        def tile(diagonal):
            rows = pl.ds(pl.multiple_of(kb * T, T), T)
            for hh in range(HP):
                sl = _hsl(hh)
                qh, kh, vh, dob = _scaled(q_ref[:, sl], scale), k_ref[:, sl], v_ref[:, sl], do_s[hh]
                s = _dot(kh, qh, NT) + cq_ref[hh] - ck_ref[hh]
                p = jnp.exp(s - lse_ref[hh])
                if diagonal:
                    p = jnp.where(_iota2((T, T), 0) <= _iota2((T, T), 1), p, 0.0)
                ds = p * (_dot(vh, dob, NT) - delta_s[hh])
                dsb = ds.astype(bf16)
                dq_s[hh] += _dot(kh, dsb, TN)
                dk_ref[rows, sl] += _dot(dsb, qh)
                dv_ref[rows, sl] += _dot(p.astype(bf16), dob)
                dcq_s[hh] += jnp.sum(ds, axis=0, keepdims=True)
                dck_ref[hh, rows, :] += jnp.sum(ds, axis=1, keepdims=True)

        @pl.when(kb < qb)
        def _():
            tile(False)

        @pl.when(kb == qb)
        def _():
            tile(True)
            for hh in range(HP):
                dq_ref[:, _hsl(hh)] = (jnp.transpose(dq_s[hh]) * scale).astype(dq_ref.dtype)
                dcq_ref[hh] = dcq_s[hh]

    q, k, v, z = _pair_specs(S, BR, T, kmap)
    rowq = pl.BlockSpec((HP, 1,T), lambda p, i, j: (p, 0, i))
    colk = pl.BlockSpec((HP, T,1), lambda p, i, j: (p, kmap(i, j), 0))
    ospec = pl.BlockSpec((T, HW), lambda p, i, j: (i, p))
    full = pl.BlockSpec((S, HW), lambda p, i, j: (0, p))
    return _pallas(
        body, (P, P, P, P, o, du, lse, cq, ck), name=name, grid=(H // HP, nq, nq),
        in_specs=[q, k, v, z, ospec, ospec, rowq, rowq, colk],
        out_specs=[ospec, ospec, full, full, rowq, pl.BlockSpec((HP, S,1), lambda p, i, j: (p, 0, 0))],
        out_shape=[jax.ShapeDtypeStruct((S, BR), bf16)] * 2 + [jax.ShapeDtypeStruct((S, BR), f32)] * 2
        + [jax.ShapeDtypeStruct((H, 1, S), f32), jax.ShapeDtypeStruct((H, S, 1), f32)],
        scratch_shapes=[pltpu.VMEM((HP, T,HEAD_DIM), bf16), pltpu.VMEM((HP, 1,T), f32),
                        pltpu.VMEM((HP, HEAD_DIM,T), f32), pltpu.VMEM((HP, 1,T), f32)],
        semantics=("parallel", "arbitrary", "arbitrary"), comm=comm)


def _sbT_logs(kc, qh, before):
    l = _dot(kc, qh, NT)
    minus_abs = lax.bitcast_convert_type(lax.bitcast_convert_type(l, jnp.int32) | jnp.int32(-2 ** 31), f32)
    lsig = jnp.minimum(l, 0.0) - jnp.log(1.0 + jnp.exp(minus_abs))
    lf = lsig - l
    if before is not None:
        lf = jnp.where(before, lf, 0.0)
    return lsig, lf


def _chunk_rows(c):
    return slice(c * BLOCK, (c + 1) * BLOCK)


def _sbT_suffix_tile(lf, tri, nc, two_pass):
    if not two_pass:
        hi = lf.astype(bf16)
        return jnp.concatenate([_dot(tri, hi[_chunk_rows(c)]) for c in range(nc)], axis=0)
    hi, lo = _split2(lf)
    tri2 = jnp.concatenate([tri, tri], axis=1)
    return jnp.concatenate(
        [_dot(tri2, jnp.concatenate([hi[_chunk_rows(c)], lo[_chunk_rows(c)]], axis=0)) for c in range(nc)], axis=0)


def _sbT_fwd(P, BR, name, comm=None):
    S = P.shape[0]
    H = BR // HEAD_DIM
    T = _attn_tile(S)
    nq = S // T
    nc = T // BLOCK
    scale = HEAD_DIM ** -0.5
    kmap = lambda i, j: jnp.maximum(i - j, 0)

    def body(q_ref, k_ref, v_ref, z_ref, o_ref, u_ref, rc_ref, r_s, acc_s):
        qb, j = pl.program_id(1), pl.program_id(2)

        @pl.when(j == 0)
        def _():
            r_s[...] = jnp.zeros_like(r_s)
            acc_s[...] = jnp.zeros_like(acc_s)

        def tile(diagonal):
            ii, jj = _iota2((BLOCK, BLOCK), 0), _iota2((BLOCK, BLOCK), 1)
            tri = (jj > ii).astype(bf16)
            before = (_iota2((T, T), 0) < _iota2((T, T), 1)) if diagonal else None
            for hh in range(HP):
                sl = _hsl(hh)
                lsig, lf = _sbT_logs(k_ref[:, sl], _scaled(q_ref[:, sl], scale), before)
                x = lsig + _sbT_suffix_tile(lf, tri, nc, two_pass=True)
                r = r_s[hh]
                rc_ref[hh, 0] = r
                parts = [None] * nc
                for c in reversed(range(nc)):
                    parts[c] = jnp.exp(x[_chunk_rows(c)] + r)
                    r = r + jnp.sum(lf[_chunk_rows(c)], axis=0, keepdims=True)
                r_s[hh] = r
                a = jnp.concatenate(parts, axis=0)
                if diagonal:
                    a = jnp.where(before, a, 0.0)
                acc_s[hh] += _dot(v_ref[:, sl], a.astype(bf16), TN)

        @pl.when(j > 0)
        def _():
            @pl.when(j <= qb)
            def _():
                tile(False)

        @pl.when(j == 0)
        def _():
            tile(True)

        @pl.when(j == qb)
        def _():
            for hh in range(HP):
                sl = _hsl(hh)
                oh = jnp.transpose(acc_s[hh])
                o_ref[:, sl] = oh.astype(o_ref.dtype)
                u_ref[:, sl] = _gate_fwd(oh, z_ref[:, sl]).astype(u_ref.dtype)

    q, k, v, z = _pair_specs(S, BR, T, kmap)
    ospec = pl.BlockSpec((T, HW), lambda p, i, j: (i, p))
    rc = pl.BlockSpec((HP, 1,1, T), lambda p, i, j: (p, kmap(i, j), 0, i))
    return _pallas(
        body, (P, P, P, P), name=name, grid=(H // HP, nq, nq),
        in_specs=[q, k, v, z],
        out_specs=[ospec, ospec, rc],
        out_shape=[jax.ShapeDtypeStruct((S, BR), bf16)] * 2 + [jax.ShapeDtypeStruct((H, nq, 1, S), f32)],
        scratch_shapes=[pltpu.VMEM((HP, 1,T), f32), pltpu.VMEM((HP, HEAD_DIM,T), f32)],
        semantics=("parallel", "arbitrary", "arbitrary"), comm=comm)


def _sbT_bwd(P, o, du, rc, BR, name, comm=None):
    S = P.shape[0]
    H = BR // HEAD_DIM
    T = _attn_tile(S)
    nq = S // T
    nc = T // BLOCK
    scale = HEAD_DIM ** -0.5
    kmap = lambda i, j: jnp.minimum(i, j)

    def body(q_ref, k_ref, v_ref, z_ref, o_ref, du_ref, rc_ref,
             dq_ref, dz_ref, dk_ref, dv_ref, do_s, dq_s, g_s):
        qb, kb = pl.program_id(1), pl.program_id(2)

        @pl.when(jnp.logical_and(qb == 0, kb == 0))
        def _():
            dk_ref[...] = jnp.zeros_like(dk_ref)
            dv_ref[...] = jnp.zeros_like(dv_ref)

        @pl.when(kb == 0)
        def _():
            for hh in range(HP):
                sl = _hsl(hh)
                doh, dzh = _gate_bwd(du_ref[:, sl].astype(f32), o_ref[:, sl].astype(f32), z_ref[:, sl])
                dz_ref[:, sl] = dzh.astype(dz_ref.dtype)
                do_s[hh] = doh.astype(bf16)
            dq_s[...] = jnp.zeros_like(dq_s)
            g_s[...] = jnp.zeros_like(g_s)

        def tile(diagonal):
            ii, jj = _iota2((BLOCK, BLOCK), 0), _iota2((BLOCK, BLOCK), 1)
            tri_suffix = (jj > ii).astype(bf16)
            tri_prefix = (jj < ii).astype(bf16)
            before = (_iota2((T, T), 0) < _iota2((T, T), 1)) if diagonal else None
            out_rows = pl.ds(pl.multiple_of(kb * T, T), T)
            for hh in range(HP):
                sl = _hsl(hh)
                qh, kh, dob = _scaled(q_ref[:, sl], scale), k_ref[:, sl], do_s[hh]
                lsig, lf = _sbT_logs(kh, qh, before)
                x = lsig + _sbT_suffix_tile(lf, tri_suffix, nc, two_pass=False)
                r = rc_ref[hh, 0]
                parts = [None] * nc
                for c in reversed(range(nc)):
                    parts[c] = jnp.exp(x[_chunk_rows(c)] + r)
                    r = r + jnp.sum(lf[_chunk_rows(c)], axis=0, keepdims=True)
                a = jnp.concatenate(parts, axis=0)
                if diagonal:
                    a = jnp.where(before, a, 0.0)
                g = a * _dot(v_ref[:, sl], dob, NT)
                gb = g.astype(bf16)
                gsum = g_s[hh]
                for c in range(nc):
                    parts[c] = _dot(tri_prefix, gb[_chunk_rows(c)]) + gsum
                    gsum = gsum + jnp.sum(g[_chunk_rows(c)], axis=0, keepdims=True)
                g_s[hh] = gsum
                dl = g - (g + jnp.concatenate(parts, axis=0)) * jnp.exp(lsig)
                if diagonal:
                    dl = jnp.where(before, dl, 0.0)
                dl = dl.astype(bf16)
                dq_s[hh] += _dot(kh, dl, TN)
                dk_ref[out_rows, sl] += _dot(dl, qh)
                dv_ref[out_rows, sl] += _dot(a.astype(bf16), dob)

        @pl.when(kb < qb)
        def _():
            tile(False)

        @pl.when(kb == qb)
        def _():
            tile(True)
            for hh in range(HP):
                dq_ref[:, _hsl(hh)] = (jnp.transpose(dq_s[hh]) * scale).astype(dq_ref.dtype)

    q, k, v, z = _pair_specs(S, BR, T, kmap)
    ospec = pl.BlockSpec((T, HW), lambda p, i, j: (i, p))
    full = pl.BlockSpec((S, HW), lambda p, i, j: (0, p))
    rcs = pl.BlockSpec((HP, 1,1, T), lambda p, i, j: (p, kmap(i, j), 0, i))
    return _pallas(
        body, (P, P, P, P, o, du, rc), name=name, grid=(H // HP, nq, nq),
        in_specs=[q, k, v, z, ospec, ospec, rcs],
        out_specs=[ospec, ospec, full, full],
        out_shape=[jax.ShapeDtypeStruct((S, BR), bf16)] * 2 + [jax.ShapeDtypeStruct((S, BR), f32)] * 2,
        scratch_shapes=[pltpu.VMEM((HP, T,HEAD_DIM), bf16), pltpu.VMEM((HP, HEAD_DIM,T), f32),
                        pltpu.VMEM((HP, 1,T), f32)],
        semantics=("parallel", "arbitrary", "arbitrary"), comm=comm)


def _slab(R, C):
    tr = _pick(R, (256, 192, 128))
    if tr != R or R % 8 == 0:
        return tr, C
    return R, _pick(C, (256, 128))


def _adam_math(w, g, m, v):
    m = ADAM_B1 * m + (1.0 - ADAM_B1) * g
    v = ADAM_B2 * v + (1.0 - ADAM_B2) * (g * g)
    m_hat = m / (1.0 - ADAM_B1 ** ADAM_STEP)
    v_hat = v / (1.0 - ADAM_B2 ** ADAM_STEP)
    delta = -ADAM_LR * (m_hat / (jnp.sqrt(v_hat) + ADAM_EPS) + ADAM_WD * w)
    return delta, m, v


def _adamw_sum(parts, w, m, v, name):
    L, R, C = w.shape
    assert len(parts) == L
    n_parts = parts[0].shape[0]
    tr, tc = _slab(R, C)

    def body(*refs):
        p_refs = refs[:L]
        w_ref, m_ref, v_ref, g_ref, d_ref, nm_ref, nv_ref = refs[L:]
        layer = pl.program_id(0)
        g = None
        for l in range(L):
            gl = p_refs[l][0].astype(f32)
            for i in range(1, n_parts):
                gl = gl + p_refs[l][i].astype(f32)
            g = gl if g is None else jnp.where(layer == l, gl, g)
        g_ref[0] = g
        d_ref[0], nm_ref[0], nv_ref[0] = _adam_math(w_ref[0], g, m_ref[0], v_ref[0])

    blk = pl.BlockSpec((1, tr, tc), lambda l, i, j: (l, i, j))
    return _pallas(
        body, (*parts, w, m, v), name=name, grid=(L, R // tr, C // tc),
        in_specs=[pl.BlockSpec((n_parts, tr, tc), lambda l, i, j: (0, i, j))] * L + [blk, blk, blk],
        out_specs=[blk] * 4,
        out_shape=[jax.ShapeDtypeStruct((L, R, C), f32)] * 4,
        semantics=("parallel", "parallel", "parallel"))


def _coords():
    return lax.axis_index("x"), lax.axis_index("y"), lax.axis_index("c")


class _Gather:
    def __init__(self, shards):
        self.args = list(shards)
        self.n = len(shards)
        self.out_shape = [jax.ShapeDtypeStruct((N_DEV,) + s.shape, s.dtype) for s in shards]
        self.scratch = [pltpu.SemaphoreType.DMA((self.n, 7)), pltpu.SemaphoreType.DMA((self.n, 7)),
                        pltpu.SemaphoreType.DMA((self.n,))]

    def _ctx(self, ins, outs, sems):
        send_sems, recv_sems, local_sems = sems
        x, y, c = _coords()
        me, sibling = (x, y, c), (x, y, 1 - c)
        chips = [(1 - x, y), (x, 1 - y), (1 - x, 1 - y)]

        def slot(out, dev):
            return out.at[4 * dev[0] + 2 * dev[1] + dev[2]]

        def copy(a, k, block, to, src=None):
            return pltpu.make_async_remote_copy(
                src_ref=slot(outs[a], block) if src is None else src, dst_ref=slot(outs[a], block),
                send_sem=send_sems.at[a, k], recv_sem=recv_sems.at[a, k], device_id=to, device_id_type=MESH)

        mine = [pltpu.make_async_copy(ins[a], slot(outs[a], me), local_sems.at[a]) for a in range(self.n)]
        first = []
        for a in range(self.n):
            first.append(copy(a, 0, me, sibling, src=ins[a]))
            first += [copy(a, 1 + j, me, (*chip, c), src=ins[a]) for j, chip in enumerate(chips)]
        passed = [copy(a, 4 + j, (*chip, c), sibling) for j, chip in enumerate(chips) for a in range(self.n)]
        return c, me, sibling, chips, copy, mine, first, passed

    def start(self, ins, outs, sems):
        *_, mine, first, _ = self._ctx(ins, outs, sems)
        for cp in mine + first:
            cp.start()

    def mid(self, ins, outs, sems):
        c, me, _, chips, copy, _, _, passed = self._ctx(ins, outs, sems)
        i = 0
        for j, chip in enumerate(chips):
            for a in range(self.n):
                copy(a, 1 + j, (*chip, c), me).wait_recv()
                passed[i].start()
                i += 1

    def finish(self, ins, outs, sems):
        c, me, sibling, chips, copy, mine, first, passed = self._ctx(ins, outs, sems)
        for a in range(self.n):
            copy(a, 0, sibling, me).wait_recv()
            for j, chip in enumerate(chips):
                copy(a, 4 + j, (*chip, 1 - c), me).wait_recv()
        for cp in first + passed:
            cp.wait_send()
        for cp in mine:
            cp.wait()


N_CHIPS = 4


class _SiblingSwap:
    mid = None

    def __init__(self, blocks):
        self.args = list(blocks)
        self.n = len(blocks)
        self.out_shape = [jax.ShapeDtypeStruct((N_CHIPS,) + b.shape[1:], b.dtype) for b in blocks]
        self.scratch = [pltpu.SemaphoreType.DMA((self.n, N_CHIPS)), pltpu.SemaphoreType.DMA((self.n, N_CHIPS))]

    def _copies(self, ins, outs, sems):
        send_sems, recv_sems = sems
        x, y, c = _coords()
        return [pltpu.make_async_remote_copy(
            src_ref=ins[a].at[2 * q + (1 - c)], dst_ref=outs[a].at[q],
            send_sem=send_sems.at[a, q], recv_sem=recv_sems.at[a, q],
            device_id=(x, y, 1 - c), device_id_type=MESH) for a in range(self.n) for q in range(N_CHIPS)]

    def start(self, ins, outs, sems):
        for cp in self._copies(ins, outs, sems):
            cp.start()

    def finish(self, ins, outs, sems):
        for cp in self._copies(ins, outs, sems):
            cp.wait()


class _ChipScatter:
    mid = None

    def __init__(self, blocks):
        self.args = list(blocks)
        self.n = len(blocks)
        self.out_shape = [jax.ShapeDtypeStruct(b.shape, b.dtype) for b in blocks]
        self.scratch = [pltpu.SemaphoreType.DMA((self.n, 3)), pltpu.SemaphoreType.DMA((self.n, 3)),
                        pltpu.SemaphoreType.DMA((self.n,))]

    def _ctx(self, ins, outs, sems):
        send_sems, recv_sems, local_sems = sems
        x, y, c = _coords()
        me = 2 * x + y
        mine = [pltpu.make_async_copy(ins[a].at[me], outs[a].at[me], local_sems.at[a]) for a in range(self.n)]

        def copy(a, k, landing_here):
            px, py = x ^ ((k >> 1) & 1), y ^ (k & 1)
            them = 2 * px + py
            return pltpu.make_async_remote_copy(
                src_ref=ins[a].at[them], dst_ref=outs[a].at[them if landing_here else me],
                send_sem=send_sems.at[a, k - 1], recv_sem=recv_sems.at[a, k - 1],
                device_id=(px, py, c), device_id_type=MESH)

        sent = [copy(a, k, False) for k in range(1, N_CHIPS) for a in range(self.n)]
        arrivals = [copy(a, k, True) for k in range(1, N_CHIPS) for a in range(self.n)]
        return mine, sent, arrivals

    def start(self, ins, outs, sems):
        mine, sent, _ = self._ctx(ins, outs, sems)
        for cp in mine + sent:
            cp.start()

    def finish(self, ins, outs, sems):
        mine, sent, arrivals = self._ctx(ins, outs, sems)
        for cp in arrivals:
            cp.wait_recv()
        for cp in sent:
            cp.wait_send()
        for cp in mine:
            cp.wait()


def _pair_sum(blocks, got, core, name):
    _, R, C = blocks.shape
    tr, tc = _slab(R, C)

    def body(core_ref, mine_ref, got_ref, o_ref):
        o_ref[...] = (mine_ref[...].astype(f32) + got_ref[...].astype(f32)).astype(o_ref.dtype)

    return pl.pallas_call(
        body, name=name,
        grid_spec=pltpu.PrefetchScalarGridSpec(
            num_scalar_prefetch=1, grid=(N_CHIPS, R // tr, C // tc),
            in_specs=[pl.BlockSpec((1, tr, tc), lambda q, i, j, core_ref: (2 * q + core_ref[0], i, j)),
                      pl.BlockSpec((1, tr, tc), lambda q, i, j, core_ref: (q, i, j))],
            out_specs=pl.BlockSpec((1, tr, tc), lambda q, i, j, core_ref: (q, i, j))),
        out_shape=jax.ShapeDtypeStruct((N_CHIPS, R, C), blocks.dtype),
        compiler_params=_params(("parallel", "parallel", "parallel")),
    )(core, blocks, got)


def _run_comm(comm, name):
    n = comm.n

    def body(*refs):
        ins, outs, sems = refs[:n], refs[n:2 * n], refs[2 * n:]
        comm.start(ins, outs, sems)
        if comm.mid is not None:
            comm.mid(ins, outs, sems)
        comm.finish(ins, outs, sems)

    return pl.pallas_call(
        body, name=name, in_specs=[ANY] * n, out_specs=[ANY] * n,
        out_shape=comm.out_shape, scratch_shapes=comm.scratch,
    )(*comm.args)


def _pallas(body, args, *, name, grid, in_specs, out_specs, out_shape, scratch_shapes=(), semantics, comm=None):
    in_specs, out_specs, out_shape, scratch_shapes = list(in_specs), list(out_specs), list(out_shape), list(scratch_shapes)
    if comm is None:
        return pl.pallas_call(
            body, name=name, grid=grid, in_specs=in_specs, out_specs=out_specs, out_shape=out_shape,
            scratch_shapes=scratch_shapes, compiler_params=_params(semantics))(*args)
    a = len(in_specs)
    b = a + comm.n
    c = b + len(out_specs)
    d = c + comm.n
    e = d + len(scratch_shapes)
    total = math.prod(grid)
    mid_step = (7 * total) // 8

    def hosted(*refs):
        step = pl.program_id(0)
        for axis in range(1, len(grid)):
            step = step * grid[axis] + pl.program_id(axis)
        ins, outs, sems = refs[a:b], refs[c:d], refs[e:]

        @pl.when(step == 0)
        def _():
            comm.start(ins, outs, sems)

        body(*refs[:a], *refs[b:c], *refs[d:e])

        if comm.mid is not None:
            @pl.when(step == mid_step)
            def _():
                comm.mid(ins, outs, sems)

        @pl.when(step == total - 1)
        def _():
            comm.finish(ins, outs, sems)

    res = pl.pallas_call(
        hosted, name=name, grid=grid, in_specs=in_specs + [ANY] * comm.n, out_specs=out_specs + [ANY] * comm.n,
        out_shape=out_shape + comm.out_shape, scratch_shapes=scratch_shapes + comm.scratch,
        compiler_params=_params(("arbitrary",) * len(grid)))(*args, *comm.args)
    return list(res[:len(out_specs)]) + [list(res[len(out_specs):])]


def _all_reduce_small(vec, name):
    R, C = vec.shape

    def body(v_ref, o_ref, gath, send_sems, recv_sems):
        x, y, c = _coords()
        me = 4 * x + 2 * y + c
        gath[me] = v_ref[...]

        def copy(k):
            px, py, pc = x ^ ((k >> 2) & 1), y ^ ((k >> 1) & 1), c ^ (k & 1)
            return pltpu.make_async_remote_copy(
                src_ref=v_ref, dst_ref=gath.at[me], send_sem=send_sems.at[k - 1], recv_sem=recv_sems.at[k - 1],
                device_id=(px, py, pc), device_id_type=MESH)

        sent = [copy(k) for k in range(1, N_DEV)]
        for cp in sent:
            cp.start()
        for cp in sent:
            cp.wait()
        total = gath[0]
        for i in range(1, N_DEV):
            total = total + gath[i]
        o_ref[...] = total

    return pl.pallas_call(
        body, name=name,
        in_specs=[pl.BlockSpec(memory_space=pltpu.VMEM)], out_specs=pl.BlockSpec(memory_space=pltpu.VMEM),
        out_shape=jax.ShapeDtypeStruct((R, C), f32),
        scratch_shapes=[pltpu.VMEM((N_DEV, R, C), f32), pltpu.SemaphoreType.DMA((7,)), pltpu.SemaphoreType.DMA((7,))],
    )(vec)


def _gathered_in(g):
    return jnp.transpose(g, (1, 0, 2)).reshape(g.shape[1], -1)


def _heads_major(a, n):
    return jnp.transpose(a.reshape(a.shape[0], n, HEAD_DIM), (1, 0, 2))


def _heads_minor(a):
    return jnp.transpose(a, (1, 0, 2)).reshape(a.shape[1], -1)


def kernel(x, g_pre, g_post, w_in_a, w_out_a, sinks_a, w_in_b, w_out_b, w_in_c, b_f_c, w_out_c, loss_target, m_g_pre, m_g_post, m_w_in_a, m_w_out_a, m_sinks_a, m_w_in_b, m_w_out_b, m_w_in_c, m_b_f_c, m_w_out_c, v_g_pre, v_g_post, v_w_in_a, v_w_out_a, v_sinks_a, v_w_in_b, v_w_out_b, v_w_in_c, v_b_f_c, v_w_out_c):
    S, D = x.shape[1], x.shape[2]
    H = D // HEAD_DIM
    BR = H * HEAD_DIM
    n_kv = H // 8
    KV = n_kv * HEAD_DIM
    x0 = x[0]
    target = loss_target[0]
    swa_bias = _swa_bias(H, n_kv)
    w_in = {0: w_in_a, 1: w_in_b, 2: w_in_c}
    w_out = {0: w_out_a, 1: w_out_b, 2: w_out_c}

    saved = []
    xi = x0
    riders = {"pre_norm0": [("in", 0)], "in_proj0": [("out", 0)], "mixer0": [("in", 1)],
              "mixer1": [("out", 1), ("in", 2)], "mixer2": [("out", 2), ("in", 3), ("out", 3)]}
    full = {}

    def held_transposed(kind):
        return w_in[kind].shape[2] % BLOCK != 0

    def shard_of(which, i):
        kind, j = i % 3, i // 3
        if which == "out":
            return w_out[kind][j].astype(bf16)
        if held_transposed(kind):
            return jnp.swapaxes(w_in[kind], 1, 2)[j].astype(bf16)
        return w_in[kind][j].astype(bf16)

    def rider(slot):
        keys = riders.get(slot)
        if not keys:
            return None
        return _Gather([shard_of(which, i) for which, i in keys])

    def landed(slot, arrays):
        full.update(zip(riders[slot], arrays))

    for i in range(DEPTH):
        kind, j = i % 3, i // 3
        nxt = rider(f"mixer{i}")
        if i == 0:
            h, arrived = _rmsnorm_fwd(xi, g_pre[:1], name="pre_norm0", comm=rider("pre_norm0"))
            landed("pre_norm0", arrived)
        w_t = held_transposed(kind)
        W_in = full["in", i].reshape(-1, D) if w_t else _gathered_in(full["in", i])
        st = dict(x=xi, h=h)
        if kind == 2:
            n_main = 4 * BR
            if w_t:
                W_in = jnp.pad(W_in, ((0, (-W_in.shape[0]) % BLOCK), (0, 0)))
                fl = _matmul(h, W_in[n_main:n_main + BLOCK], "nt", f32, name=f"f_proj{i}")
            else:
                W_in = jnp.pad(W_in, ((0, 0), (0, (-W_in.shape[1]) % BLOCK)))
                fl = _matmul(h, W_in[:, n_main:n_main + BLOCK], "nn", f32, name=f"f_proj{i}")
            flT = jnp.transpose(fl[:, :H])
            bcol = b_f_c[j].reshape(H, 1)
            cumT = _fox_cum(flT, bcol, name=f"fox_cum{i}")
            cq, ck = cumT[:, None, :], cumT[:, :, None]
            st.update(flT=flT, bcol=bcol, cq=cq, ck=ck)
        st["W_in"] = W_in
        with_proj = rider(f"in_proj{i}")
        P = _matmul(h, W_in, "nt" if w_t else "nn", bf16, name=f"in_proj{i}", comm=with_proj)
        if with_proj is not None:
            P, arrived = P
            landed(f"in_proj{i}", arrived)
        st["P"] = P
        if kind == 0:
            kh = _heads_major(P[:, BR:BR + KV], n_kv)
            vh = _heads_major(P[:, BR + KV:BR + 2 * KV], n_kv)
            o, u, *rest = _swaT_fwd(P, kh, vh, sinks_a[j], swa_bias, name=f"swa_fwd{i}", comm=nxt)
            st.update(kh=kh, vh=vh)
        elif kind == 1:
            o, u, rc, *rest = _sbT_fwd(P, BR, name=f"sb_fwd{i}", comm=nxt)
            st.update(rc=rc)
        else:
            o, u, lse, *rest = _foxT_fwd(P, cq, ck, BR, name=f"fox_fwd{i}", comm=nxt)
            st.update(lse=lse)
        if nxt is not None:
            landed(f"mixer{i}", rest[0])
        W_out = full["out", i].reshape(BR, D)
        st.update(o=o, u=u, W_out=W_out)
        y = _matmul(u, W_out, "nn", f32, name=f"out_proj{i}")
        st["y"] = y
        xi, *h_next = _post_fwd(xi, y, g_post[i:i + 1], g_pre[i + 1:i + 2] if i + 1 < DEPTH else None,
                                name=f"post_norm{i}")
        h = h_next[0] if h_next else None
        saved.append(st)

    loss_part, dx = _loss_fwd_bwd(xi, target, name="loss")

    dg_pre, dg_post = [None] * DEPTH, [None] * DEPTH
    dsinks = [None, None]
    db_f = None
    recv = [None] * DEPTH
    core = lax.axis_index("c").astype(jnp.int32).reshape(1)
    pending = None
    for i in reversed(range(DEPTH)):
        kind, j = i % 3, i // 3
        st = saved[i]
        dy, dg_post[i] = _post_bwd(dx, st["y"], g_post[i:i + 1], name=f"post_bwd{i}")
        du = _matmul(dy, st["W_out"], "nt", bf16, name=f"du{i}")
        dW_out = _matmul(st["u"], dy, "tn", bf16, name=f"dw_out{i}")
        P = st["P"]
        if kind == 0:
            dq, dz, dkh, dvh, dsk, *rest = _swaT_bwd(P, st["kh"], st["vh"], st["o"], du, sinks_a[j], swa_bias,
                                                     name=f"swa_bwd{i}", comm=pending)
            dsinks[j] = jnp.sum(dsk.reshape(H, BLOCK), axis=1)
            dP = jnp.concatenate([dq, _heads_minor(dkh).astype(bf16), _heads_minor(dvh).astype(bf16), dz], axis=1)
        elif kind == 1:
            dq, dz, dk, dv, *rest = _sbT_bwd(P, st["o"], du, st["rc"], BR, name=f"sb_bwd{i}", comm=pending)
            dP = jnp.concatenate([dq, dk.astype(bf16), dv.astype(bf16), dz], axis=1)
        else:
            dq, dz, dk, dv, dcq, dck, *rest = _foxT_bwd(P, st["o"], du, st["lse"], st["cq"], st["ck"], BR,
                                                       name=f"fox_bwd{i}", comm=pending)
        if pending is not None:
            recv[i + 1] = rest[0]
        if kind == 2:
            dflT, db_col = _fox_cum_bwd(dcq.reshape(H, S), dck.reshape(H, S), st["flT"], st["bcol"],
                                        name=f"fox_cum_bwd{i}")
            db_f = db_col.reshape(H)
            dfl = jnp.pad(jnp.transpose(dflT), ((0, 0), (0, 128 - H))).astype(bf16)
            dP = jnp.concatenate([dq, dk.astype(bf16), dv.astype(bf16), dz, dfl], axis=1)
        w_t = held_transposed(kind)
        c_shard = w_in[kind].shape[2]
        if w_t:
            dW_in = _matmul(dP, st["h"], "tn", bf16, name=f"dw_in{i}")[:c_shard * N_DEV].reshape(N_DEV, c_shard, D)
        else:
            assert dP.shape[1] == c_shard * N_DEV
            dW_in = _matmul(st["h"], dP, "tn", bf16, name=f"dw_in{i}", col_blocks=N_DEV)
        blocks = [dW_in, dW_out.reshape(N_DEV, BR // N_DEV, D)]
        swap = _SiblingSwap(blocks)
        dh_mode = "nn" if w_t else "nt"
        if i > 0:
            dh = _matmul(dP, st["W_in"], dh_mode, f32, name=f"dh{i}")
            dx, dg_pre[i], got = _pre_bwd(dh, st["x"], g_pre[i:i + 1], dx, name=f"pre_bwd{i}", comm=swap)
        else:
            dh, got = _matmul(dP, st["W_in"], dh_mode, f32, name=f"dh{i}", comm=swap)
            dx, dg_pre[i] = _pre_bwd(dh, st["x"], g_pre[i:i + 1], dx, name=f"pre_bwd{i}")
        sums = [_pair_sum(b, g, core, name=f"pair_sum_{t}{i}") for t, b, g in zip(("in", "out"), blocks, got)]
        pending = _ChipScatter(sums)
    recv[0] = _run_comm(pending, name="scatter_dw0")

    small = jnp.concatenate(
        [jnp.concatenate(dg_pre, axis=0).reshape(-1), jnp.concatenate(dg_post, axis=0).reshape(-1),
         jnp.concatenate(dsinks), db_f, loss_part[0, :1]])
    n_small = small.shape[0]
    rows = -(-n_small // 128)
    rows = -(-rows // 8) * 8
    small = jnp.pad(small, (0, rows * 128 - n_small)).reshape(rows, 128)
    total = _all_reduce_small(small, name="reduce_small").reshape(-1)
    o0 = DEPTH * D
    grad_g_pre = total[:o0].reshape(DEPTH, D)
    grad_g_post = total[o0:2 * o0].reshape(DEPTH, D)
    grad_sinks = total[2 * o0:2 * o0 + 2 * H].reshape(2, H)
    grad_b_f = total[2 * o0 + 2 * H:2 * o0 + 3 * H].reshape(1, H)
    loss = total[2 * o0 + 3 * H]

    def small_adam(w, g, m, v, name):
        def body(w_ref, g_ref, m_ref, v_ref, d_ref, nm_ref, nv_ref):
            d_ref[...], nm_ref[...], nv_ref[...] = _adam_math(w_ref[...], g_ref[...], m_ref[...], v_ref[...])
        vm = pl.BlockSpec(memory_space=pltpu.VMEM)
        return pl.pallas_call(body, name=name, in_specs=[vm] * 4, out_specs=[vm] * 3,
                              out_shape=[jax.ShapeDtypeStruct(w.shape, f32)] * 3)(w, g, m, v)

    upd = {}
    upd["g_pre"] = (grad_g_pre,) + tuple(small_adam(g_pre, grad_g_pre, m_g_pre, v_g_pre, "adam_g_pre"))
    upd["g_post"] = (grad_g_post,) + tuple(small_adam(g_post, grad_g_post, m_g_post, v_g_post, "adam_g_post"))
    upd["sinks_a"] = (grad_sinks,) + tuple(small_adam(sinks_a, grad_sinks, m_sinks_a, v_sinks_a, "adam_sinks"))
    upd["b_f_c"] = (grad_b_f,) + tuple(small_adam(b_f_c, grad_b_f, m_b_f_c, v_b_f_c, "adam_b_f"))

    def big(layers, which, w, m, v, tag):
        return _adamw_sum([recv[i][which] for i in layers], w, m, v, name=f"adam_{tag}")

    def big_in(kind, layers, w, m, v, tag):
        if not held_transposed(kind):
            return big(layers, 0, w, m, v, tag)
        flip = lambda a: jnp.swapaxes(a, 1, 2)
        return [flip(t) for t in big(layers, 0, flip(w), flip(m), flip(v), tag)]

    upd["w_in_c"] = big_in(2, (2,), w_in_c, m_w_in_c, v_w_in_c, "in_c")
    upd["w_in_b"] = big_in(1, (1,), w_in_b, m_w_in_b, v_w_in_b, "in_b")
    upd["w_out_b"] = big((1,), 1, w_out_b, m_w_out_b, v_w_out_b, "out_b")
    upd["w_out_c"] = big((2,), 1, w_out_c, m_w_out_c, v_w_out_c, "out_c")
    upd["w_in_a"] = big_in(0, (0, 3), w_in_a, m_w_in_a, v_w_in_a, "in_a")
    upd["w_out_a"] = big((0, 3), 1, w_out_a, m_w_out_a, v_w_out_a, "out_a")

    names = ["g_pre", "g_post", "w_in_a", "w_out_a", "sinks_a", "w_in_b", "w_out_b", "w_in_c", "b_f_c", "w_out_c"]
    return (loss, dx[None], *[upd[k][0] for k in names], *[upd[k][1] for k in names],
            *[upd[k][2] for k in names], *[upd[k][3] for k in names])
```
